```python
import math
import jax
import jax.numpy as jnp
from jax import lax
import numpy as np

D_MODEL = 2048
BATCH = 8
SEQ = 2048
DEPTH = 2

GRID_W = 64
Q_BLOCK = 128
NORM_EPS = 1e-6

A_HEADS = 8
A_KV_HEADS = 2
A_HEAD_DIM = 128
A_WIDTH = A_HEADS * A_HEAD_DIM
ROPE_THETA = 10000.0

B_WIDTH = 512
B_EMB_DIM = 33
B_FILTER_HIDDEN = 64
B_DECAY_TARGET = 1e-2
B_FAST_DECAY_PCT = 0.3
B_SLOW_DECAY_PCT = 1.5

C_HEADS = 8
C_HEAD_DIM = 64
C_WIDTH = C_HEADS * C_HEAD_DIM
C_DECAY_LORA = 96
C_AAA_LORA = 96
C_GATE_LORA = 256
C_SHIFT = 3 * C_WIDTH + C_DECAY_LORA + C_AAA_LORA
C_GN_EPS = 64e-5

D_HEADS = 4
D_HEAD_DIM = 64
D_V_DIM = 2 * D_HEAD_DIM
D_WIDTH = D_HEADS * D_V_DIM

N_BRANCHES = 4
BRANCH_WIDTHS = (A_WIDTH, B_WIDTH, C_WIDTH, D_WIDTH)
D_MIX = A_WIDTH + B_WIDTH + C_WIDTH + D_WIDTH
IN_SPLITS = (A_WIDTH, A_KV_HEADS * A_HEAD_DIM, A_KV_HEADS * A_HEAD_DIM, 3 * B_WIDTH, C_SHIFT, C_GATE_LORA, 2 * D_HEADS * D_HEAD_DIM, 2 * D_HEADS * D_HEAD_DIM, D_WIDTH)
D_IN = sum(IN_SPLITS)
FFN_HIDDEN = -(-8 * D_MODEL // (3 * 256)) * 256

kernel_name = 'hybrid_gated_parallel_encoder'


def _offsets(sizes):
    return [int(v) for v in np.cumsum(sizes)[:-1]]


def rms_norm(x, g):
    xf = x.astype(jnp.float32)
    y = xf * lax.rsqrt(jnp.mean(xf * xf, axis=-1, keepdims=True) + NORM_EPS)
    return (y * g.astype(jnp.float32)).astype(x.dtype)


def axial_rope_tables(L):
    rows = L // GRID_W
    row_idx = jnp.repeat(jnp.arange(rows, dtype=jnp.float32), GRID_W)
    col_idx = jnp.tile(jnp.arange(GRID_W, dtype=jnp.float32), rows)
    axis_dim = A_HEAD_DIM // 2
    inv_freq = ROPE_THETA ** (-jnp.arange(0, axis_dim, 2, dtype=jnp.float32) / axis_dim)
    ang_r = row_idx[:, None] * inv_freq[None, :]
    ang_c = col_idx[:, None] * inv_freq[None, :]
    ang = jnp.concatenate([ang_r, ang_r, ang_c, ang_c], axis=-1)
    return jnp.cos(ang), jnp.sin(ang)


def apply_axial_rope(x, cos, sin):
    xr = x.reshape(x.shape[:-1] + (2, 2, A_HEAD_DIM // 4))
    rot = jnp.stack([-xr[..., 1, :], xr[..., 0, :]], axis=-2).reshape(x.shape)
    return x * cos[:, None, :] + rot * sin[:, None, :]


def gqa_axial_attention(q, k, v, q_norm, k_norm, cos, sin):
    B, T = q.shape[:2]
    G = A_HEADS // A_KV_HEADS
    nb = T // Q_BLOCK
    q = apply_axial_rope(rms_norm(q.reshape(B, T, A_HEADS, A_HEAD_DIM), q_norm), cos, sin)
    k = apply_axial_rope(rms_norm(k.reshape(B, T, A_KV_HEADS, A_HEAD_DIM), k_norm), cos, sin)
    v = v.reshape(B, T, A_KV_HEADS, A_HEAD_DIM)
    scale = A_HEAD_DIM ** -0.5
    q_blocks = q.reshape(B, nb, Q_BLOCK, A_KV_HEADS, G, A_HEAD_DIM).transpose(1, 0, 3, 4, 2, 5)

    def block(qb):
        s = jnp.einsum('bhgqd,bkhd->bhgqk', qb, k).astype(jnp.float32) * scale
        p = jax.nn.softmax(s, axis=-1).astype(v.dtype)
        return jnp.einsum('bhgqk,bkhd->bhgqd', p, v)

    o = lax.map(block, q_blocks)
    return o.transpose(1, 0, 4, 2, 3, 5).reshape(B, T, A_WIDTH)


def hyena_two_sided_filter(L, w1, b1, w2, b2, w3, b3, w4, freq):
    f32 = jnp.float32
    w1, b1, w2, b2, w3, b3, w4, freq = (a.astype(f32) for a in (w1, b1, w2, b2, w3, b3, w4, freq))
    t = jnp.linspace(0.0, 1.0, L, dtype=f32)[:, None]
    n_bands = (B_EMB_DIM - 1) // 2
    bands = jnp.linspace(1e-4, n_bands - 1, n_bands, dtype=f32)[None, :]
    ang = (2.0 * math.pi / L) * jnp.arange(L, dtype=f32)[:, None] * bands
    z = jnp.concatenate([t, jnp.cos(ang), -jnp.sin(ang)], axis=-1)
    hid = jnp.sin(freq * (z @ w1 + b1))
    hid = jnp.sin(freq * (hid @ w2 + b2))
    hid = jnp.sin(freq * (hid @ w3 + b3))
    h = hid @ w4
    max_decay = math.log(B_DECAY_TARGET) / B_FAST_DECAY_PCT
    min_decay = math.log(B_DECAY_TARGET) / B_SLOW_DECAY_PCT
    deltas = jnp.abs(jnp.linspace(min_decay, max_decay, B_WIDTH, dtype=f32))
    window = jnp.exp(-t * deltas[None, :])
    h_fwd = h[:, :B_WIDTH] * window
    h_bwd = h[:, B_WIDTH:] * window
    kbuf = jnp.concatenate([h_fwd, jnp.zeros((1, B_WIDTH), f32), h_bwd[1:][::-1]], axis=0)
    return kbuf / jnp.sum(jnp.abs(kbuf), axis=0, keepdims=True)


def hyena_mixer(u, conv_w, conv_b, w1, b1, w2, b2, w3, b3, w4, freq, skip):
    B, T = u.shape[:2]
    up = jnp.pad(u, ((0, 0), (1, 1), (0, 0)))
    uc = conv_w[0] * up[:, :-2] + conv_w[1] * up[:, 1:-1] + conv_w[2] * up[:, 2:] + conv_b
    v, x1, x0 = jnp.split(uc, 3, axis=-1)
    z = (v * x1).astype(jnp.float32)
    kbuf = hyena_two_sided_filter(T, w1, b1, w2, b2, w3, b3, w4, freq)
    n_fft = 2 * T
    zf = jnp.fft.rfft(z, n=n_fft, axis=1)
    kf = jnp.fft.rfft(kbuf, n=n_fft, axis=0)
    y = jnp.fft.irfft(zf * kf[None], n=n_fft, axis=1)[:, :T]
    y = y + z * skip.astype(jnp.float32)
    return y.astype(u.dtype) * x0


def rwkv7_scan(r, w, k, v, a_vec, b_vec, reverse):
    B, T, H, N = r.shape

    def step(S, inp):
        r_t, w_t, k_t, v_t, a_t, b_t = inp
        sa = jnp.einsum('bhvk,bhk->bhv', S, a_t)
        S = S * w_t[:, :, None, :] + sa[..., None] * b_t[:, :, None, :] + v_t[..., None] * k_t[:, :, None, :]
        return S, jnp.einsum('bhvk,bhk->bhv', S, r_t)

    xs = tuple(jnp.moveaxis(a, 1, 0) for a in (r, w, k, v, a_vec, b_vec))
    S0 = jnp.zeros((B, H, N, N), jnp.float32)
    _, o = lax.scan(step, S0, xs, reverse=reverse)
    return jnp.moveaxis(o, 0, 1)


def rwkv7_mixer(feats, g_lo, mu, w0, w_up, a0, a_up, g_up, k_k, k_a, r_k, ln_w, ln_b):
    f32 = jnp.float32
    B, T = feats.shape[:2]
    prev = jnp.pad(feats, ((0, 0), (1, 0), (0, 0)))[:, :-1]
    nxt = jnp.pad(feats, ((0, 0), (0, 1), (0, 0)))[:, 1:]
    g = jax.nn.sigmoid(g_lo) @ g_up
    split_at = [C_WIDTH, 2 * C_WIDTH, 3 * C_WIDTH, 3 * C_WIDTH + C_DECAY_LORA]
    heads = lambda a: a.reshape(B, T, C_HEADS, C_HEAD_DIM).astype(f32)
    ln_w_h = ln_w.reshape(C_HEADS, C_HEAD_DIM).astype(f32)
    ln_b_h = ln_b.reshape(C_HEADS, C_HEAD_DIM).astype(f32)
    r_k_h = r_k.astype(f32)
    outs = []
    for d, shifted in enumerate((prev, nxt)):
        f = feats + (shifted - feats) * mu[d]
        r, k, v, w_lo, a_lo = jnp.split(f, split_at, axis=-1)
        w = -jax.nn.softplus(-(w0[d] + jnp.tanh(w_lo) @ w_up[d])) - 0.5
        decay = jnp.exp(-jnp.exp(w.astype(f32)))
        a = jax.nn.sigmoid(a0[d] + a_lo @ a_up[d])
        kk = heads(k * k_k)
        kk = kk / jnp.maximum(jnp.sqrt(jnp.sum(kk * kk, axis=-1, keepdims=True)), 1e-12)
        k = k * (1.0 + (a - 1.0) * k_a)
        r_h, k_h, v_h, a_h, w_h = heads(r), heads(k), heads(v), heads(a), heads(decay)
        o = rwkv7_scan(r_h, w_h, k_h, v_h, -kk, kk * a_h, reverse=(d == 1))
        mean = jnp.mean(o, axis=-1, keepdims=True)
        var = jnp.mean(jnp.square(o - mean), axis=-1, keepdims=True)
        o = (o - mean) * lax.rsqrt(var + C_GN_EPS) * ln_w_h + ln_b_h
        o = o + jnp.sum(r_h * k_h * r_k_h, axis=-1, keepdims=True) * v_h
        outs.append(o)
    y = (outs[0] + outs[1]).reshape(B, T, C_WIDTH)
    return y.astype(g.dtype) * g


def diff_attention(q, k, v, lq1, lk1, lq2, lk2, subln, lam_init):
    f32 = jnp.float32
    B, T = q.shape[:2]
    nb = T // Q_BLOCK
    q = q.reshape(B, T, D_HEADS, 2, D_HEAD_DIM)
    k = k.reshape(B, T, D_HEADS, 2, D_HEAD_DIM)
    v = v.reshape(B, T, D_HEADS, D_V_DIM)
    lam = (jnp.exp(jnp.sum(lq1.astype(f32) * lk1.astype(f32))) - jnp.exp(jnp.sum(lq2.astype(f32) * lk2.astype(f32))) + lam_init)
    slopes = 2.0 ** (-8.0 * jnp.arange(1, D_HEADS + 1, dtype=f32) / D_HEADS)
    scale = D_HEAD_DIM ** -0.5
    q_blocks = q.reshape(B, nb, Q_BLOCK, D_HEADS, 2, D_HEAD_DIM).transpose(1, 0, 2, 3, 4, 5)
    q_pos = jnp.arange(T).reshape(nb, Q_BLOCK)
    k_pos = jnp.arange(T)

    def block(args):
        qb, qp = args
        s = jnp.einsum('bqhcd,bkhcd->bhcqk', qb, k).astype(f32) * scale
        dist = jnp.abs(qp[:, None] - k_pos[None, :]).astype(f32)
        s = s - slopes[:, None, None, None] * dist
        p = jax.nn.softmax(s, axis=-1)
        attn = p[:, :, 0] - lam * p[:, :, 1]
        return jnp.einsum('bhqk,bkhe->bqhe', attn.astype(v.dtype), v)

    o = lax.map(block, (q_blocks, q_pos))
    o = o.transpose(1, 0, 2, 3, 4).reshape(B, T, D_HEADS, D_V_DIM)
    o = rms_norm(o, subln) * (1.0 - lam_init)
    return o.reshape(B, T, D_WIDTH)


def setup_inputs(seed: int = 0) -> dict:
    key = jax.random.key(seed)
    keys = iter(jax.random.split(key, 48))
    f32 = jnp.float32

    def nrm(shape, scale):
        return scale * jax.random.normal(next(keys), shape, f32)

    def gain(shape):
        return 1.0 + nrm(shape, 0.02)

    br_scale = jnp.concatenate([jnp.full((w,), w ** -0.5, f32) for w in BRANCH_WIDTHS])
    return {
        'x': nrm((BATCH, SEQ, D_MODEL), 1.0),
        'norm_mix': gain((DEPTH, D_MODEL)),
        'w_in': nrm((DEPTH, D_MODEL, D_IN), D_MODEL ** -0.5),
        'a_q_norm': gain((DEPTH, A_HEAD_DIM)),
        'a_k_norm': gain((DEPTH, A_HEAD_DIM)),
        'b_conv_w': nrm((DEPTH, 3, 3 * B_WIDTH), 0.5),
        'b_conv_b': nrm((DEPTH, 3 * B_WIDTH), 0.02),
        'b_filt_w1': nrm((DEPTH, B_EMB_DIM, B_FILTER_HIDDEN), B_EMB_DIM ** -0.5),
        'b_filt_b1': nrm((DEPTH, B_FILTER_HIDDEN), 0.02),
        'b_filt_w2': nrm((DEPTH, B_FILTER_HIDDEN, B_FILTER_HIDDEN), B_FILTER_HIDDEN ** -0.5),
        'b_filt_b2': nrm((DEPTH, B_FILTER_HIDDEN), 0.02),
        'b_filt_w3': nrm((DEPTH, B_FILTER_HIDDEN, B_FILTER_HIDDEN), B_FILTER_HIDDEN ** -0.5),
        'b_filt_b3': nrm((DEPTH, B_FILTER_HIDDEN), 0.02),
        'b_filt_w4': nrm((DEPTH, B_FILTER_HIDDEN, 2 * B_WIDTH), B_FILTER_HIDDEN ** -0.5),
        'b_filt_freq': gain((DEPTH, B_FILTER_HIDDEN)),
        'b_skip': nrm((DEPTH, B_WIDTH), 0.5),
        'c_mu': jax.random.uniform(next(keys), (DEPTH, 2, C_SHIFT), f32),
        'c_w0': jnp.linspace(-6.0, -1.0, C_WIDTH, dtype=f32) + nrm((DEPTH, 2, C_WIDTH), 0.1),
        'c_w_up': nrm((DEPTH, 2, C_DECAY_LORA, C_WIDTH), 0.1 * C_DECAY_LORA ** -0.5),
        'c_a0': nrm((DEPTH, 2, C_WIDTH), 0.1),
        'c_a_up': nrm((DEPTH, 2, C_AAA_LORA, C_WIDTH), C_AAA_LORA ** -0.5),
        'c_g_up': nrm((DEPTH, C_GATE_LORA, C_WIDTH), C_GATE_LORA ** -0.5),
        'c_k_k': 0.85 + nrm((DEPTH, C_WIDTH), 0.02),
        'c_k_a': gain((DEPTH, C_WIDTH)),
        'c_r_k': nrm((DEPTH, C_HEADS, C_HEAD_DIM), 0.1),
        'c_ln_w': gain((DEPTH, C_WIDTH)),
        'c_ln_b': nrm((DEPTH, C_WIDTH), 0.02),
        'd_lq1': nrm((DEPTH, D_HEAD_DIM), 0.1),
        'd_lk1': nrm((DEPTH, D_HEAD_DIM), 0.1),
        'd_lq2': nrm((DEPTH, D_HEAD_DIM), 0.1),
        'd_lk2': nrm((DEPTH, D_HEAD_DIM), 0.1),
        'd_subln': gain((DEPTH, D_V_DIM)),
        'w_gate': nrm((DEPTH, N_BRANCHES, D_MODEL, D_MODEL), D_MODEL ** -0.5),
        'w_branch': nrm((DEPTH, D_MIX, D_MODEL), 1.0) * br_scale[None, :, None],
        'w_out': nrm((DEPTH, D_MODEL, D_MODEL), D_MODEL ** -0.5),
        'norm_ffn': gain((DEPTH, D_MODEL)),
        'w_ff_gate': nrm((DEPTH, D_MODEL, FFN_HIDDEN), D_MODEL ** -0.5),
        'w_ff_up': nrm((DEPTH, D_MODEL, FFN_HIDDEN), D_MODEL ** -0.5),
        'w_ff_down': nrm((DEPTH, FFN_HIDDEN, D_MODEL), FFN_HIDDEN ** -0.5),
        'norm_final': gain((D_MODEL,)),
    }


def reference(x, norm_mix, w_in, a_q_norm, a_k_norm, b_conv_w, b_conv_b, b_filt_w1, b_filt_b1, b_filt_w2, b_filt_b2, b_filt_w3, b_filt_b3, b_filt_w4, b_filt_freq, b_skip, c_mu, c_w0, c_w_up, c_a0, c_a_up, c_g_up, c_k_k, c_k_a, c_r_k, c_ln_w, c_ln_b, d_lq1, d_lk1, d_lq2, d_lk2, d_subln, w_gate, w_branch, w_out, norm_ffn, w_ff_gate, w_ff_up, w_ff_down, norm_final):
    L = x.shape[1]
    cos, sin = axial_rope_tables(L)
    in_off = _offsets(IN_SPLITS)
    br_off = _offsets(BRANCH_WIDTHS)
    for l in range(DEPTH):
        h = rms_norm(x, norm_mix[l])
        aq, ak, av, bu, cf, cg, dq, dk, dv = jnp.split(h @ w_in[l], in_off, axis=-1)
        o_a = gqa_axial_attention(aq, ak, av, a_q_norm[l], a_k_norm[l], cos, sin)
        o_b = hyena_mixer(bu, b_conv_w[l], b_conv_b[l], b_filt_w1[l], b_filt_b1[l], b_filt_w2[l], b_filt_b2[l], b_filt_w3[l], b_filt_b3[l], b_filt_w4[l], b_filt_freq[l], b_skip[l])
        o_c = rwkv7_mixer(cf, cg, c_mu[l], c_w0[l], c_w_up[l], c_a0[l], c_a_up[l], c_g_up[l], c_k_k[l], c_k_a[l], c_r_k[l], c_ln_w[l], c_ln_b[l])
        lam_init = 0.8 - 0.6 * math.exp(-0.3 * l)
        o_d = diff_attention(dq, dk, dv, d_lq1[l], d_lk1[l], d_lq2[l], d_lk2[l], d_subln[l], lam_init)
        branches = (o_a, o_b, o_c, o_d)
        w_br = jnp.split(w_branch[l], br_off, axis=0)
        merged = jax.nn.sigmoid(h @ w_gate[l, 0]) * (branches[0] @ w_br[0])
        for i in range(1, N_BRANCHES):
            merged = merged + jax.nn.sigmoid(h @ w_gate[l, i]) * (branches[i] @ w_br[i])
        x = x + merged @ w_out[l]
        h2 = rms_norm(x, norm_ffn[l])
        x = x + (jax.nn.silu(h2 @ w_ff_gate[l]) * (h2 @ w_ff_up[l])) @ w_ff_down[l]
    return rms_norm(x, norm_final)
```

```python
import functools
import math

import jax
import jax.numpy as jnp
import numpy as np
from jax import lax
from jax.experimental import pallas as pl
from jax.experimental.pallas import tpu as pltpu

D_MODEL = 2048
DEPTH = 2
GRID_W = 64
NORM_EPS = 1e-6

A_HEADS = 8
A_KV_HEADS = 2
A_HEAD_DIM = 128
A_WIDTH = A_HEADS * A_HEAD_DIM
ROPE_THETA = 10000.0

B_WIDTH = 512
B_EMB_DIM = 33
B_FILTER_HIDDEN = 64
B_DECAY_TARGET = 1e-2
B_FAST_DECAY_PCT = 0.3
B_SLOW_DECAY_PCT = 1.5

C_HEADS = 8
C_HEAD_DIM = 64
C_WIDTH = C_HEADS * C_HEAD_DIM
C_LORA = 96
C_LORA_PAD = 128
C_GATE_LORA = 256
C_GN_EPS = 64e-5

D_HEADS = 4
D_HEAD_DIM = 64
D_V_DIM = 2 * D_HEAD_DIM
D_WIDTH = D_HEADS * D_V_DIM

FFN_HIDDEN = -(-8 * D_MODEL // (3 * 256)) * 256

COL_AQ = 0
COL_AK = COL_AQ + A_WIDTH
COL_AV = COL_AK + A_KV_HEADS * A_HEAD_DIM
COL_BU = COL_AV + A_KV_HEADS * A_HEAD_DIM
COL_CF = COL_BU + 3 * B_WIDTH
COL_CLO = COL_CF + 3 * C_WIDTH
COL_CG = COL_CLO + 2 * C_LORA_PAD
COL_DQ = COL_CG + C_GATE_LORA
COL_DK = COL_DQ + 2 * D_HEADS * D_HEAD_DIM
COL_DV = COL_DK + 2 * D_HEADS * D_HEAD_DIM
D_IN_PAD = COL_DV + D_WIDTH

VMEM_LIMIT_V7X = 56 * 1024 * 1024
F32 = jnp.float32
BF16 = jnp.bfloat16
HIGHEST = lax.Precision.HIGHEST
NT_DIMS = (((1,), (1,)), ((), ()))


def _cp(*sem):
    return pltpu.CompilerParams(dimension_semantics=sem, vmem_limit_bytes=VMEM_LIMIT_V7X)


def _const_spec(shape):
    return pl.BlockSpec(shape, lambda *_: (0,) * len(shape), pipeline_mode=pl.Buffered(1))


def _rmsnorm_kernel(x_ref, g_ref, o_ref):
    x = x_ref[...]
    ms = jnp.mean(x * x, axis=-1, keepdims=True)
    o_ref[...] = (x * lax.rsqrt(ms + NORM_EPS) * g_ref[...]).astype(o_ref.dtype)


def _rmsnorm(x, g, out_dtype, tm=512):
    m, d = x.shape
    tm = min(tm, m)
    return pl.pallas_call(
        _rmsnorm_kernel,
        out_shape=jax.ShapeDtypeStruct((m, d), out_dtype),
        grid=(m // tm,),
        in_specs=[pl.BlockSpec((tm, d), lambda i: (i, 0)), pl.BlockSpec((1, d), lambda i: (0, 0))],
        out_specs=pl.BlockSpec((tm, d), lambda i: (i, 0)),
        compiler_params=_cp("parallel"),
        name="rmsnorm",
    )(x, g.reshape(1, d))


def _mm_kernel(a_ref, b_ref, o_ref, *, precision):
    o_ref[...] = jnp.dot(a_ref[...], b_ref[...], preferred_element_type=F32, precision=precision).astype(o_ref.dtype)


def _mm_res_kernel(a_ref, b_ref, r_ref, o_ref):
    o_ref[...] = r_ref[...] + jnp.dot(a_ref[...], b_ref[...], preferred_element_type=F32)


def _matmul(a, b, out_dtype, tm, tn, residual=None, precision=None, name="matmul"):
    m, k = a.shape
    n = b.shape[1]
    tm, tn = min(tm, m), min(tn, n)
    in_specs = [pl.BlockSpec((tm, k), lambda i, j: (i, 0)), pl.BlockSpec((k, tn), lambda i, j: (0, j))]
    args = [a, b]
    if residual is None:
        body = functools.partial(_mm_kernel, precision=precision)
    else:
        body = _mm_res_kernel
        in_specs.append(pl.BlockSpec((tm, tn), lambda i, j: (i, j)))
        args.append(residual)
    return pl.pallas_call(
        body,
        out_shape=jax.ShapeDtypeStruct((m, n), out_dtype),
        grid=(m // tm, n // tn),
        in_specs=in_specs,
        out_specs=pl.BlockSpec((tm, tn), lambda i, j: (i, j)),
        compiler_params=_cp("parallel", "arbitrary"),
        name=name,
    )(*args)


def _rope_tables(seq):
    rows = seq // GRID_W
    row_idx = jnp.repeat(jnp.arange(rows, dtype=F32), GRID_W)
    col_idx = jnp.tile(jnp.arange(GRID_W, dtype=F32), rows)
    axis_dim = A_HEAD_DIM // 2
    inv_freq = ROPE_THETA ** (-jnp.arange(0, axis_dim, 2, dtype=F32) / axis_dim)
    ang_r = row_idx[:, None] * inv_freq[None, :]
    ang_c = col_idx[:, None] * inv_freq[None, :]
    ang = jnp.concatenate([ang_r, ang_r, ang_c, ang_c], axis=-1)
    return jnp.cos(ang), jnp.sin(ang)


def _qk_prep_kernel(x_ref, g_ref, cos_ref, sin_ref, o_ref):
    x = x_ref[...]
    xn = x * lax.rsqrt(jnp.mean(x * x, axis=-1, keepdims=True) + NORM_EPS) * g_ref[0]
    lane = lax.broadcasted_iota(jnp.int32, xn.shape, 1)
    quarter = A_HEAD_DIM // 4
    first = (lane % (2 * quarter)) < quarter
    rot = jnp.where(first, -pltpu.roll(xn, A_HEAD_DIM - quarter, 1), pltpu.roll(xn, quarter, 1))
    y = xn * cos_ref[...] + rot * sin_ref[...]
    y = y * jnp.where(pl.program_id(1) < A_HEADS, A_HEAD_DIM**-0.5, 1.0)
    o_ref[...] = y.astype(o_ref.dtype)


def _qk_prep(proj, gains, cos, sin, seq, tm=512):
    m = proj.shape[0]
    tm = min(tm, seq)
    n_heads = A_HEADS + A_KV_HEADS
    nt = seq // tm
    return pl.pallas_call(
        _qk_prep_kernel,
        out_shape=jax.ShapeDtypeStruct((m, n_heads * A_HEAD_DIM), BF16),
        grid=(m // tm, n_heads),
        in_specs=[
            pl.BlockSpec((tm, A_HEAD_DIM), lambda i, j: (i, j)),
            pl.BlockSpec((1, 1, A_HEAD_DIM), lambda i, j: (j, 0, 0)),
            pl.BlockSpec((tm, A_HEAD_DIM), lambda i, j: (i % nt, 0)),
            pl.BlockSpec((tm, A_HEAD_DIM), lambda i, j: (i % nt, 0)),
        ],
        out_specs=pl.BlockSpec((tm, A_HEAD_DIM), lambda i, j: (i, j)),
        compiler_params=_cp("parallel", "arbitrary"),
        name="a_qk_prep",
    )(proj, gains, cos, sin)


def _attn_a_kernel(q_ref, k_ref, v_ref, o_ref):
    s = lax.dot_general(q_ref[...], k_ref[...], NT_DIMS, preferred_element_type=F32)
    p = jnp.exp(s - jnp.max(s, axis=-1, keepdims=True))
    l = jnp.sum(p, axis=-1, keepdims=True)
    o = jnp.dot(p.astype(BF16), v_ref[...].astype(BF16), preferred_element_type=F32)
    o_ref[...] = (o / l).astype(o_ref.dtype)


def _attn_a(qk, proj, batch, seq, tq=512):
    tq = min(tq, seq)
    nq = seq // tq
    group = A_HEADS // A_KV_HEADS
    k_blk = COL_AK // A_HEAD_DIM
    v_blk = COL_AV // A_HEAD_DIM
    return pl.pallas_call(
        _attn_a_kernel,
        out_shape=jax.ShapeDtypeStruct((batch * seq, A_WIDTH), BF16),
        grid=(batch, A_HEADS, nq),
        in_specs=[
            pl.BlockSpec((tq, A_HEAD_DIM), lambda b, h, i: (b * nq + i, h)),
            pl.BlockSpec((seq, A_HEAD_DIM), lambda b, h, i: (b, k_blk + h // group)),
            pl.BlockSpec((seq, A_HEAD_DIM), lambda b, h, i: (b, v_blk + h // group)),
        ],
        out_specs=pl.BlockSpec((tq, A_HEAD_DIM), lambda b, h, i: (b * nq + i, h)),
        compiler_params=_cp("parallel", "arbitrary", "arbitrary"),
        name="a_attention",
    )(qk, qk, proj)


def _attn_d_kernel(q_ref, k_ref, v_ref, slope_ref, lam_ref, g_ref, o_ref, *, tq, lam_init):
    q = q_ref[...]
    k = k_ref[...].astype(BF16)
    seq = k.shape[0]
    lane = lax.broadcasted_iota(jnp.int32, q.shape, 1)
    q0 = jnp.where(lane < D_HEAD_DIM, q, 0.0).astype(BF16)
    q1 = jnp.where(lane >= D_HEAD_DIM, q, 0.0).astype(BF16)
    q_pos = pl.program_id(2) * tq + lax.broadcasted_iota(jnp.int32, (tq, seq), 0)
    k_pos = lax.broadcasted_iota(jnp.int32, (tq, seq), 1)
    bias = slope_ref[0][:, 0:1] * jnp.abs(q_pos - k_pos).astype(F32)
    scale = D_HEAD_DIM**-0.5

    def softmax_parts(qc):
        s = lax.dot_general(qc, k, NT_DIMS, preferred_element_type=F32) * scale - bias
        p = jnp.exp(s - jnp.max(s, axis=-1, keepdims=True))
        return p, jnp.sum(p, axis=-1, keepdims=True)

    p0, l0 = softmax_parts(q0)
    p1, l1 = softmax_parts(q1)
    lam_v = lam_ref[...]
    lam = (
        jnp.exp(jnp.sum(lam_v[0:1] * lam_v[1:2], axis=-1, keepdims=True))
        - jnp.exp(jnp.sum(lam_v[2:3] * lam_v[3:4], axis=-1, keepdims=True))
        + lam_init
    )
    attn = p0 * (1.0 / l0) - p1 * (lam / l1)
    o = jnp.dot(attn.astype(BF16), v_ref[...].astype(BF16), preferred_element_type=F32)
    o = o * lax.rsqrt(jnp.mean(o * o, axis=-1, keepdims=True) + NORM_EPS) * g_ref[...]
    o_ref[...] = (o * (1.0 - lam_init)).astype(o_ref.dtype)


def _attn_d(proj, lam_vecs, subln, lam_init, batch, seq, tq=256):
    tq = min(tq, seq)
    nq = seq // tq
    slopes = 2.0 ** (-8.0 * np.arange(1, D_HEADS + 1, dtype=np.float32) / D_HEADS)
    slopes = jnp.asarray(np.broadcast_to(slopes[:, None, None], (D_HEADS, 1, 128)).astype(np.float32))
    q_blk, k_blk, v_blk = COL_DQ // D_V_DIM, COL_DK // D_V_DIM, COL_DV // D_V_DIM
    return pl.pallas_call(
        functools.partial(_attn_d_kernel, tq=tq, lam_init=lam_init),
        out_shape=jax.ShapeDtypeStruct((batch * seq, D_WIDTH), BF16),
        grid=(batch, D_HEADS, nq),
        in_specs=[
            pl.BlockSpec((tq, D_V_DIM), lambda b, h, i: (b * nq + i, q_blk + h)),
            pl.BlockSpec((seq, D_V_DIM), lambda b, h, i: (b, k_blk + h)),
            pl.BlockSpec((seq, D_V_DIM), lambda b, h, i: (b, v_blk + h)),
            pl.BlockSpec((1, 1, 128), lambda b, h, i: (h, 0, 0)),
            pl.BlockSpec((4, D_HEAD_DIM), lambda b, h, i: (0, 0)),
            pl.BlockSpec((1, D_V_DIM), lambda b, h, i: (0, 0)),
        ],
        out_specs=pl.BlockSpec((tq, D_V_DIM), lambda b, h, i: (b * nq + i, h)),
        compiler_params=_cp("parallel", "arbitrary", "arbitrary"),
        name="d_attention",
    )(proj, proj, proj, slopes, lam_vecs, subln.reshape(1, D_V_DIM))


def _dft_tables(seq):
    n = 2 * seq
    f = jnp.arange(seq, dtype=jnp.int32)
    odd = 2 * f + 1
    m_half = (odd[:, None] * odd[None, :]) % (4 * n)
    ang_half = m_half.astype(F32) * (2.0 * math.pi / (4 * n))
    m_int = (odd[:, None] * f[None, :]) % (2 * n)
    ang_int = m_int.astype(F32) * (2.0 * math.pi / (2 * n))
    return jnp.cos(ang_half).astype(BF16), jnp.sin(ang_half).astype(BF16), jnp.cos(ang_int), jnp.sin(ang_int)


def _filter_features(seq):
    t = jnp.linspace(0.0, 1.0, seq, dtype=F32)[:, None]
    n_bands = (B_EMB_DIM - 1) // 2
    bands = jnp.linspace(1e-4, n_bands - 1, n_bands, dtype=F32)[None, :]
    ang = (2.0 * math.pi / seq) * jnp.arange(seq, dtype=F32)[:, None] * bands
    z = jnp.concatenate([t, jnp.cos(ang), -jnp.sin(ang)], axis=-1)
    z = jnp.pad(z, ((0, 0), (0, B_FILTER_HIDDEN - B_EMB_DIM)))
    max_decay = math.log(B_DECAY_TARGET) / B_FAST_DECAY_PCT
    min_decay = math.log(B_DECAY_TARGET) / B_SLOW_DECAY_PCT
    deltas = jnp.abs(jnp.linspace(min_decay, max_decay, B_WIDTH, dtype=F32))[None, :]
    return z, t, deltas


def _filter_kernel(z_ref, t_ref, dl_ref, w1_ref, b1_ref, w2_ref, b2_ref, w3_ref, b3_ref, w4_ref, fr_ref, hs_ref, hd_ref):
    fr = fr_ref[...]
    hid = jnp.sin(fr * (jnp.dot(z_ref[...], w1_ref[...], preferred_element_type=F32, precision=HIGHEST) + b1_ref[...]))
    hid = jnp.sin(fr * (jnp.dot(hid, w2_ref[...], preferred_element_type=F32, precision=HIGHEST) + b2_ref[...]))
    hid = jnp.sin(fr * (jnp.dot(hid, w3_ref[...], preferred_element_type=F32, precision=HIGHEST) + b3_ref[...]))
    h = jnp.dot(hid, w4_ref[...], preferred_element_type=F32, precision=HIGHEST)
    window = jnp.exp(-t_ref[...] * dl_ref[...])
    h_fwd = h[:, :B_WIDTH] * window
    h_bwd = h[:, B_WIDTH:] * window
    row = lax.broadcasted_iota(jnp.int32, h_bwd.shape, 0)
    h_bwd = jnp.where(row == 0, 0.0, h_bwd)
    norm = jnp.sum(jnp.abs(h_fwd), axis=0, keepdims=True) + jnp.sum(jnp.abs(h_bwd), axis=0, keepdims=True)
    seq = h.shape[0]
    inv_n = 1.0 / seq
    hs_ref[...] = (h_fwd + h_bwd) / norm * inv_n
    hd_ref[...] = (h_bwd - h_fwd) / norm * inv_n


def _hyena_filter_spectrum(seq, feats, cos_int, sin_int, w1, b1, w2, b2, w3, b3, w4, freq):
    z, t, deltas = feats
    w1p = jnp.pad(w1, ((0, B_FILTER_HIDDEN - B_EMB_DIM), (0, 0)))
    row = lambda v: v.reshape(1, -1)
    args = (z, t, deltas, w1p, row(b1), w2, row(b2), w3, row(b3), w4, row(freq))
    h_sum, h_diff = pl.pallas_call(
        _filter_kernel,
        out_shape=[jax.ShapeDtypeStruct((seq, B_WIDTH), F32)] * 2,
        in_specs=[pl.BlockSpec(a.shape, lambda: (0, 0)) for a in args],
        out_specs=[pl.BlockSpec((seq, B_WIDTH), lambda: (0, 0))] * 2,
        compiler_params=pltpu.CompilerParams(vmem_limit_bytes=VMEM_LIMIT_V7X),
        name="b_filter",
    )(*args)
    k_re = _matmul(cos_int, h_sum, F32, 256, B_WIDTH, precision=HIGHEST, name="b_filter_dft_re")
    k_im = _matmul(sin_int, h_diff, F32, 256, B_WIDTH, precision=HIGHEST, name="b_filter_dft_im")
    return k_re, k_im


def _hyena_kernel(v_ref, x1_ref, x0_ref, wv_ref, w1_ref, w0_ref, bv_ref, b1_ref, b0_ref, skip_ref, c_ref, s_ref, kre_ref, kim_ref, o_ref):
    seq = v_ref.shape[0]
    row = lax.broadcasted_iota(jnp.int32, v_ref.shape, 0)

    def conv3(u_ref, w_ref, b_ref):
        u = u_ref[...]
        w = w_ref[...]
        u_prev = jnp.where(row == 0, 0.0, pltpu.roll(u, 1, 0))
        u_next = jnp.where(row == seq - 1, 0.0, pltpu.roll(u, seq - 1, 0))
        return w[0:1] * u_prev + w[1:2] * u + w[2:3] * u_next + b_ref[...]

    z = conv3(v_ref, wv_ref, bv_ref) * conv3(x1_ref, w1_ref, b1_ref)
    zb = z.astype(BF16)
    cz = jnp.dot(c_ref[...], zb, preferred_element_type=F32)
    sz = jnp.dot(s_ref[...], zb, preferred_element_type=F32)
    k_re = kre_ref[...]
    k_im = kim_ref[...]
    y_re = (cz * k_re + sz * k_im).astype(BF16)
    y_im = (cz * k_im - sz * k_re).astype(BF16)
    y = jnp.dot(c_ref[...], y_re, preferred_element_type=F32) - jnp.dot(s_ref[...], y_im, preferred_element_type=F32)
    y = y + z * skip_ref[...]
    o_ref[...] = (y * conv3(x0_ref, w0_ref, b0_ref)).astype(o_ref.dtype)


def _hyena(proj, conv_w, conv_b, skip, cos_half, sin_half, k_re, k_im, batch, seq, cb=128):
    ncb = B_WIDTH // cb
    u_blk = COL_BU // cb

    def u_spec(part):
        return pl.BlockSpec((seq, cb), lambda b, j: (b, u_blk + part * ncb + j))

    def w_spec(rows, part):
        return pl.BlockSpec((rows, cb), lambda b, j: (0, part * ncb + j))

    return pl.pallas_call(
        _hyena_kernel,
        out_shape=jax.ShapeDtypeStruct((batch * seq, B_WIDTH), BF16),
        grid=(batch, ncb),
        in_specs=[
            u_spec(0), u_spec(1), u_spec(2),
            w_spec(3, 0), w_spec(3, 1), w_spec(3, 2),
            w_spec(1, 0), w_spec(1, 1), w_spec(1, 2),
            pl.BlockSpec((1, cb), lambda b, j: (0, j)),
            _const_spec((seq, seq)), _const_spec((seq, seq)),
            pl.BlockSpec((seq, cb), lambda b, j: (0, j)),
            pl.BlockSpec((seq, cb), lambda b, j: (0, j)),
        ],
        out_specs=pl.BlockSpec((seq, cb), lambda b, j: (b, j)),
        compiler_params=_cp("parallel", "arbitrary"),
        name="b_hyena",
    )(proj, proj, proj, conv_w, conv_w, conv_w, conv_b.reshape(1, -1), conv_b.reshape(1, -1), conv_b.reshape(1, -1),
      skip.reshape(1, -1), cos_half, sin_half, k_re, k_im)


def _rwkv_prep_kernel(cur_ref, lo_ref, pcur_ref, plo_ref, ncur_ref, nlo_ref, mu_ref, mulo_ref, w0_ref, wup_ref, a0_ref, aup_ref,
                      kk_ref, ka_ref, ones_ref, o_ref, *, blocks_per_seq):
    i = pl.program_id(0)
    first = (i % blocks_per_seq) == 0
    last = (i % blocks_per_seq) == blocks_per_seq - 1
    halo = pcur_ref.shape[0]

    def neighbours(x_ref, p_ref, n_ref):
        x = x_ref[...]
        tm = x.shape[0]
        row = lax.broadcasted_iota(jnp.int32, x.shape, 0)
        p_row = jnp.where(first, 0.0, p_ref[halo - 1:halo, :])
        n_row = jnp.where(last, 0.0, n_ref[0:1, :])
        prev = jnp.where(row == 0, p_row, pltpu.roll(x, 1, 0))
        nxt = jnp.where(row == tm - 1, n_row, pltpu.roll(x, tm - 1, 0))
        return x, (prev, nxt)

    cur, cur_sh = neighbours(cur_ref, pcur_ref, ncur_ref)
    lo, lo_sh = neighbours(lo_ref, plo_ref, nlo_ref)
    k_k = kk_ref[...]
    k_a = ka_ref[...]
    for d in range(2):
        f = cur + (cur_sh[d] - cur) * mu_ref[d]
        f_lo = lo + (lo_sh[d] - lo) * mulo_ref[d]
        r = f[:, :C_WIDTH]
        k = f[:, C_WIDTH:2 * C_WIDTH]
        v = f[:, 2 * C_WIDTH:]
        w_lo = f_lo[:, :C_LORA_PAD]
        a_lo = f_lo[:, C_LORA_PAD:]
        x = w0_ref[d] + jnp.dot(jnp.tanh(w_lo), wup_ref[d], preferred_element_type=F32, precision=HIGHEST)
        w = -(jnp.maximum(-x, 0.0) + jnp.log(1.0 + jnp.exp(-jnp.abs(x)))) - 0.5
        decay = jnp.exp(-jnp.exp(w))
        a = jax.nn.sigmoid(a0_ref[d] + jnp.dot(a_lo, aup_ref[d], preferred_element_type=F32, precision=HIGHEST))
        kk = k * k_k
        ss = jnp.dot(kk * kk, ones_ref[...], preferred_element_type=F32, precision=HIGHEST)
        kk = kk / jnp.maximum(jnp.sqrt(ss), 1e-12)
        o_ref[d, 0] = r
        o_ref[d, 1] = decay
        o_ref[d, 2] = k * (1.0 + (a - 1.0) * k_a)
        o_ref[d, 3] = v
        o_ref[d, 4] = -kk
        o_ref[d, 5] = kk * a


def _rwkv_prep(proj, mu, mu_lo, w0, w_up, a0, a_up, k_k, k_a, seq, tm=256):
    m = proj.shape[0]
    tm = min(tm, seq)
    halo = 8
    hb = tm // halo
    n_hblk = m // halo
    cur_blk = COL_CF // (3 * C_WIDTH)
    lo_blk = COL_CLO // (2 * C_LORA_PAD)
    head_of = np.arange(C_WIDTH) // C_HEAD_DIM
    ones = jnp.asarray((head_of[:, None] == head_of[None, :]).astype(np.float32))
    wide, narrow = 3 * C_WIDTH, 2 * C_LORA_PAD

    def prev_map(blk):
        return lambda i: (jnp.maximum(i * hb - 1, 0), blk)

    def next_map(blk):
        return lambda i: (jnp.minimum((i + 1) * hb, n_hblk - 1), blk)

    small = lambda a: pl.BlockSpec(a.shape, lambda i: (0,) * a.ndim)
    consts = (mu, mu_lo, w0, w_up, a0, a_up, k_k, k_a, ones)
    return pl.pallas_call(
        functools.partial(_rwkv_prep_kernel, blocks_per_seq=seq // tm),
        out_shape=jax.ShapeDtypeStruct((2, 6, m, C_WIDTH), F32),
        grid=(m // tm,),
        in_specs=[
            pl.BlockSpec((tm, wide), lambda i: (i, cur_blk)),
            pl.BlockSpec((tm, narrow), lambda i: (i, lo_blk)),
            pl.BlockSpec((halo, wide), prev_map(cur_blk)),
            pl.BlockSpec((halo, narrow), prev_map(lo_blk)),
            pl.BlockSpec((halo, wide), next_map(cur_blk)),
            pl.BlockSpec((halo, narrow), next_map(lo_blk)),
        ] + [small(a) for a in consts],
        out_specs=pl.BlockSpec((2, 6, tm, C_WIDTH), lambda i: (0, 0, i, 0)),
        compiler_params=_cp("parallel"),
        name="c_prep",
    )(proj, proj, proj, proj, proj, proj, *consts)


def _rwkv_scan_kernel(x_ref, rk_ref, lnw_ref, lnb_ref, o_ref, s_ref, *, tc):
    n = s_ref.shape[0]

    @pl.when(pl.program_id(0) == 0)
    def _():
        s_ref[...] = jnp.zeros_like(s_ref)

    def step(t, carry):
        vv = x_ref[t, 3]
        sa = s_ref[0] * x_ref[t, 4, 0:1, :]
        for k in range(1, n):
            sa = sa + s_ref[k] * x_ref[t, 4, k:k + 1, :]
        o = None
        for k in range(n):
            sk = s_ref[k] * x_ref[t, 1, k:k + 1, :] + sa * x_ref[t, 5, k:k + 1, :] + vv * x_ref[t, 2, k:k + 1, :]
            s_ref[k] = sk
            term = sk * x_ref[t, 0, k:k + 1, :]
            o = term if o is None else o + term
        o_ref[t] = o
        return carry

    lax.fori_loop(0, tc, step, 0)

    o = o_ref[...]
    mean = jnp.mean(o, axis=1, keepdims=True)
    var = jnp.mean(jnp.square(o - mean), axis=1, keepdims=True)
    o = (o - mean) * lax.rsqrt(var + C_GN_EPS) * lnw_ref[...] + lnb_ref[...]
    bonus = jnp.sum(x_ref[:, 0] * x_ref[:, 2] * rk_ref[...], axis=1, keepdims=True)
    o_ref[...] = o + bonus * x_ref[:, 3]


def _rwkv_scan(xs, rk, lnw, lnb, tc=16):
    seq, _, n, lanes = xs.shape
    tc = min(tc, seq)
    tile = lambda: pl.BlockSpec((n, lanes), lambda i: (0, 0))
    return pl.pallas_call(
        functools.partial(_rwkv_scan_kernel, tc=tc),
        out_shape=jax.ShapeDtypeStruct((seq, n, lanes), F32),
        grid=(seq // tc,),
        in_specs=[pl.BlockSpec((tc, 6, n, lanes), lambda i: (i, 0, 0, 0)), tile(), tile(), tile()],
        out_specs=pl.BlockSpec((tc, n, lanes), lambda i: (i, 0, 0)),
        scratch_shapes=[pltpu.VMEM((n, n, lanes), F32)],
        compiler_params=_cp("arbitrary"),
        name="c_scan",
    )(xs, rk, lnw, lnb)


def _rwkv_out_kernel(o_ref, glo_ref, gup_ref, y_ref):
    g = jnp.dot(jax.nn.sigmoid(glo_ref[...]).astype(BF16), gup_ref[...], preferred_element_type=F32)
    y_ref[...] = ((o_ref[0] + o_ref[1]) * g).astype(y_ref.dtype)


def _rwkv_out(o_dirs, proj, g_up, tm=512):
    m = proj.shape[0]
    tm = min(tm, m)
    g_blk = COL_CG // C_GATE_LORA
    return pl.pallas_call(
        _rwkv_out_kernel,
        out_shape=jax.ShapeDtypeStruct((m, C_WIDTH), BF16),
        grid=(m // tm,),
        in_specs=[
            pl.BlockSpec((2, tm, C_WIDTH), lambda i: (0, i, 0)),
            pl.BlockSpec((tm, C_GATE_LORA), lambda i: (i, g_blk)),
            pl.BlockSpec((C_GATE_LORA, C_WIDTH), lambda i: (0, 0)),
        ],
        out_specs=pl.BlockSpec((tm, C_WIDTH), lambda i: (i, 0)),
        compiler_params=_cp("parallel"),
        name="c_out",
    )(o_dirs, proj, g_up)


def _rwkv(proj, mu, w0, w_up, a0, a_up, g_up, k_k, k_a, r_k, ln_w, ln_b, batch, seq):
    pad_lo = C_LORA_PAD - C_LORA
    split = 3 * C_WIDTH
    mu_wide = mu[:, None, :split]
    mu_lo = jnp.concatenate([
        jnp.pad(mu[:, split:split + C_LORA], ((0, 0), (0, pad_lo))),
        jnp.pad(mu[:, split + C_LORA:], ((0, 0), (0, pad_lo))),
    ], axis=-1)[:, None, :]
    w_up_p = jnp.pad(w_up, ((0, 0), (0, pad_lo), (0, 0)))
    a_up_p = jnp.pad(a_up, ((0, 0), (0, pad_lo), (0, 0)))
    feats = _rwkv_prep(proj, mu_wide, mu_lo, w0[:, None, :], w_up_p, a0[:, None, :], a_up_p,
                       k_k.reshape(1, C_WIDTH), k_a.reshape(1, C_WIDTH), seq)
    feats = feats.reshape(2, 6, batch, seq, C_HEADS, C_HEAD_DIM)
    feats = jnp.stack([feats[0], feats[1, :, :, ::-1]], axis=0)
    lanes = 2 * batch * C_HEADS
    xs = feats.transpose(3, 1, 5, 0, 2, 4).reshape(seq, 6, C_HEAD_DIM, lanes)
    per_lane = lambda p: jnp.tile(p.reshape(C_HEADS, C_HEAD_DIM).T, (1, 2 * batch))
    o = _rwkv_scan(xs, per_lane(r_k), per_lane(ln_w), per_lane(ln_b))
    o = o.reshape(seq, C_HEAD_DIM, 2, batch, C_HEADS).transpose(2, 3, 0, 4, 1)
    o = jnp.stack([o[0], o[1, :, ::-1]], axis=0).reshape(2, batch * seq, C_WIDTH)
    return _rwkv_out(o, proj, g_up.astype(BF16))


def _merge_kernel(h_ref, wg_ref, oa_ref, ob_ref, oc_ref, od_ref, wa_ref, wb_ref, wc_ref, wd_ref, o_ref):
    h = h_ref[...]
    acc = None
    for i, (b_ref, w_ref) in enumerate(((oa_ref, wa_ref), (ob_ref, wb_ref), (oc_ref, wc_ref), (od_ref, wd_ref))):
        gate = jax.nn.sigmoid(jnp.dot(h, wg_ref[i], preferred_element_type=F32))
        term = gate * jnp.dot(b_ref[...], w_ref[...], preferred_element_type=F32)
        acc = term if acc is None else acc + term
    o_ref[...] = acc.astype(o_ref.dtype)


def _merge(h, w_gate, branches, w_branches, tm=512, tn=512):
    m = h.shape[0]
    tm = min(tm, m)
    row = lambda a: pl.BlockSpec((tm, a.shape[1]), lambda i, j: (i, 0))
    col = lambda a: pl.BlockSpec((a.shape[0], tn), lambda i, j: (0, j))
    return pl.pallas_call(
        _merge_kernel,
        out_shape=jax.ShapeDtypeStruct((m, D_MODEL), BF16),
        grid=(m // tm, D_MODEL // tn),
        in_specs=[row(h), pl.BlockSpec((4, D_MODEL, tn), lambda i, j: (0, 0, j))]
        + [row(b) for b in branches] + [col(w) for w in w_branches],
        out_specs=pl.BlockSpec((tm, tn), lambda i, j: (i, j)),
        compiler_params=_cp("parallel", "arbitrary"),
        name="gated_merge",
    )(h, w_gate, *branches, *w_branches)


def _ffn_up_kernel(h_ref, wg_ref, wu_ref, o_ref):
    h = h_ref[...]
    g = jnp.dot(h, wg_ref[...], preferred_element_type=F32)
    u = jnp.dot(h, wu_ref[...], preferred_element_type=F32)
    o_ref[...] = (g * jax.nn.sigmoid(g) * u).astype(o_ref.dtype)


def _ffn_up(h, w_gate, w_up, tm=1024, tn=512):
    m, k = h.shape
    n = w_gate.shape[1]
    tm = min(tm, m)
    return pl.pallas_call(
        _ffn_up_kernel,
        out_shape=jax.ShapeDtypeStruct((m, n), BF16),
        grid=(m // tm, n // tn),
        in_specs=[
            pl.BlockSpec((tm, k), lambda i, j: (i, 0)),
            pl.BlockSpec((k, tn), lambda i, j: (0, j)),
            pl.BlockSpec((k, tn), lambda i, j: (0, j)),
        ],
        out_specs=pl.BlockSpec((tm, tn), lambda i, j: (i, j)),
        compiler_params=_cp("parallel", "arbitrary"),
        name="ffn_up",
    )(h, w_gate, w_up)


def _pad_w_in(w):
    pad = ((0, 0), (0, C_LORA_PAD - C_LORA))
    lo = COL_CLO
    return jnp.concatenate([
        w[:, :lo],
        jnp.pad(w[:, lo:lo + C_LORA], pad),
        jnp.pad(w[:, lo + C_LORA:lo + 2 * C_LORA], pad),
        w[:, lo + 2 * C_LORA:],
    ], axis=1).astype(BF16)


def kernel(x, norm_mix, w_in, a_q_norm, a_k_norm, b_conv_w, b_conv_b, b_filt_w1, b_filt_b1, b_filt_w2, b_filt_b2, b_filt_w3, b_filt_b3, b_filt_w4, b_filt_freq, b_skip, c_mu, c_w0, c_w_up, c_a0, c_a_up, c_g_up, c_k_k, c_k_a, c_r_k, c_ln_w, c_ln_b, d_lq1, d_lk1, d_lq2, d_lk2, d_subln, w_gate, w_branch, w_out, norm_ffn, w_ff_gate, w_ff_up, w_ff_down, norm_final):
    batch, seq, _ = x.shape
    m = batch * seq
    cos, sin = _rope_tables(seq)
    cos_half, sin_half, cos_int, sin_int = _dft_tables(seq)
    filt_feats = _filter_features(seq)
    br_off = np.cumsum((A_WIDTH, B_WIDTH, C_WIDTH))
    x = x.reshape(m, D_MODEL)
    for l in range(DEPTH):
        h = _rmsnorm(x, norm_mix[l], BF16)
        proj = _matmul(h, _pad_w_in(w_in[l]), F32, 1024, 512, name="in_proj")

        gains = jnp.concatenate([
            jnp.broadcast_to(a_q_norm[l], (A_HEADS, A_HEAD_DIM)),
            jnp.broadcast_to(a_k_norm[l], (A_KV_HEADS, A_HEAD_DIM)),
        ])[:, None, :]
        o_a = _attn_a(_qk_prep(proj, gains, cos, sin, seq), proj, batch, seq)

        k_re, k_im = _hyena_filter_spectrum(seq, filt_feats, cos_int, sin_int, b_filt_w1[l], b_filt_b1[l], b_filt_w2[l],
                                            b_filt_b2[l], b_filt_w3[l], b_filt_b3[l], b_filt_w4[l], b_filt_freq[l])
        o_b = _hyena(proj, b_conv_w[l], b_conv_b[l], b_skip[l], cos_half, sin_half, k_re, k_im, batch, seq)

        o_c = _rwkv(proj, c_mu[l], c_w0[l], c_w_up[l], c_a0[l], c_a_up[l], c_g_up[l], c_k_k[l], c_k_a[l], c_r_k[l],
                    c_ln_w[l], c_ln_b[l], batch, seq)

        lam_init = 0.8 - 0.6 * math.exp(-0.3 * l)
        lam_vecs = jnp.stack([d_lq1[l], d_lk1[l], d_lq2[l], d_lk2[l]])
        o_d = _attn_d(proj, lam_vecs, d_subln[l], lam_init, batch, seq)

        w_br = [w.astype(BF16) for w in jnp.split(w_branch[l], br_off, axis=0)]
        merged = _merge(h, w_gate[l].astype(BF16), (o_a, o_b, o_c, o_d), w_br)
        x = _matmul(merged, w_out[l].astype(BF16), F32, 1024, 512, residual=x, name="out_proj")

        h2 = _rmsnorm(x, norm_ffn[l], BF16)
        mid = _ffn_up(h2, w_ff_gate[l].astype(BF16), w_ff_up[l].astype(BF16))
        x = _matmul(mid, w_ff_down[l].astype(BF16), F32, 512, 512, residual=x, name="ffn_down")
    return _rmsnorm(x, norm_final, F32).reshape(batch, seq, D_MODEL)
```

```python
import functools
import math

import jax
import jax.numpy as jnp
import numpy as np
from jax import lax
from jax.experimental import pallas as pl
from jax.experimental.pallas import tpu as pltpu

D_MODEL = 2048
DEPTH = 2
GRID_W = 64
NORM_EPS = 1e-6

A_HEADS = 8
A_KV_HEADS = 2
A_HEAD_DIM = 128
A_WIDTH = A_HEADS * A_HEAD_DIM
ROPE_THETA = 10000.0

B_WIDTH = 512
B_EMB_DIM = 33
B_FILTER_HIDDEN = 64
B_DECAY_TARGET = 1e-2
B_FAST_DECAY_PCT = 0.3
B_SLOW_DECAY_PCT = 1.5

C_HEADS = 8
C_HEAD_DIM = 64
C_WIDTH = C_HEADS * C_HEAD_DIM
C_LORA = 96
C_LORA_PAD = 128
C_GATE_LORA = 256
C_GN_EPS = 64e-5

D_HEADS = 4
D_HEAD_DIM = 64
D_V_DIM = 2 * D_HEAD_DIM
D_WIDTH = D_HEADS * D_V_DIM

FFN_HIDDEN = -(-8 * D_MODEL // (3 * 256)) * 256

COL_AQ = 0
COL_AK = COL_AQ + A_WIDTH
COL_AV = COL_AK + A_KV_HEADS * A_HEAD_DIM
COL_BU = COL_AV + A_KV_HEADS * A_HEAD_DIM
COL_CF = COL_BU + 3 * B_WIDTH
COL_CLO = COL_CF + 3 * C_WIDTH
COL_CG = COL_CLO + 2 * C_LORA_PAD
COL_DQ = COL_CG + C_GATE_LORA
COL_DK = COL_DQ + 2 * D_HEADS * D_HEAD_DIM
COL_DV = COL_DK + 2 * D_HEADS * D_HEAD_DIM
D_IN_PAD = COL_DV + D_WIDTH

VMEM_LIMIT_V7X = 56 * 1024 * 1024
F32 = jnp.float32
BF16 = jnp.bfloat16
HIGHEST = lax.Precision.HIGHEST
NT_DIMS = (((1,), (1,)), ((), ()))


def _cp(*sem):
    return pltpu.CompilerParams(dimension_semantics=sem, vmem_limit_bytes=VMEM_LIMIT_V7X)


def _const_spec(shape):
    return pl.BlockSpec(shape, lambda *_: (0,) * len(shape), pipeline_mode=pl.Buffered(1))


def _rmsnorm_kernel(x_ref, g_ref, o_ref):
    x = x_ref[...]
    ms = jnp.mean(x * x, axis=-1, keepdims=True)
    o_ref[...] = (x * lax.rsqrt(ms + NORM_EPS) * g_ref[...]).astype(o_ref.dtype)


def _rmsnorm(x, g, out_dtype, tm=512):
    m, d = x.shape
    tm = min(tm, m)
    return pl.pallas_call(
        _rmsnorm_kernel,
        out_shape=jax.ShapeDtypeStruct((m, d), out_dtype),
        grid=(m // tm,),
        in_specs=[pl.BlockSpec((tm, d), lambda i: (i, 0)), pl.BlockSpec((1, d), lambda i: (0, 0))],
        out_specs=pl.BlockSpec((tm, d), lambda i: (i, 0)),
        compiler_params=_cp("parallel"),
        name="rmsnorm",
    )(x, g.reshape(1, d))


def _mm_f32_kernel(a_ref, b_ref, o_ref):
    o_ref[...] = jnp.dot(a_ref[...], b_ref[...], preferred_element_type=F32, precision=HIGHEST)


def _matmul_f32(a, b, tm, tn, name):
    m, k = a.shape
    n = b.shape[1]
    tm, tn = min(tm, m), min(tn, n)
    return pl.pallas_call(
        _mm_f32_kernel,
        out_shape=jax.ShapeDtypeStruct((m, n), F32),
        grid=(m // tm, n // tn),
        in_specs=[pl.BlockSpec((tm, k), lambda i, j: (i, 0)), pl.BlockSpec((k, tn), lambda i, j: (0, j))],
        out_specs=pl.BlockSpec((tm, tn), lambda i, j: (i, j)),
        compiler_params=_cp("parallel", "arbitrary"),
        name=name,
    )(a, b)


def _cast_weights_once(w_refs, wb_refs):
    @pl.when(pl.program_id(1) == 0)
    def _():
        for w_ref, wb_ref in zip(w_refs, wb_refs):
            wb_ref[...] = w_ref[...].astype(BF16)


def _mm_kernel(a_ref, w_ref, o_ref, wb_ref):
    _cast_weights_once((w_ref,), (wb_ref,))
    o_ref[...] = jnp.dot(a_ref[...], wb_ref[...], preferred_element_type=F32).astype(o_ref.dtype)


def _mm_res_kernel(a_ref, w_ref, r_ref, o_ref, wb_ref):
    _cast_weights_once((w_ref,), (wb_ref,))
    o_ref[...] = r_ref[...] + jnp.dot(a_ref[...], wb_ref[...], preferred_element_type=F32)


def _matmul(a, w, out_dtype, tm, tn, residual=None, name="matmul"):
    m, k = a.shape
    n = w.shape[1]
    tm, tn = min(tm, m), min(tn, n)
    in_specs = [pl.BlockSpec((tm, k), lambda j, i: (i, 0)), pl.BlockSpec((k, tn), lambda j, i: (0, j))]
    args = [a, w]
    body = _mm_kernel
    if residual is not None:
        body = _mm_res_kernel
        in_specs.append(pl.BlockSpec((tm, tn), lambda j, i: (i, j)))
        args.append(residual)
    return pl.pallas_call(
        body,
        out_shape=jax.ShapeDtypeStruct((m, n), out_dtype),
        grid=(n // tn, m // tm),
        in_specs=in_specs,
        out_specs=pl.BlockSpec((tm, tn), lambda j, i: (i, j)),
        scratch_shapes=[pltpu.VMEM((k, tn), BF16)],
        compiler_params=_cp("parallel", "arbitrary"),
        name=name,
    )(*args)


def _rope_tables(seq):
    rows = seq // GRID_W
    row_idx = jnp.repeat(jnp.arange(rows, dtype=F32), GRID_W)
    col_idx = jnp.tile(jnp.arange(GRID_W, dtype=F32), rows)
    axis_dim = A_HEAD_DIM // 2
    inv_freq = ROPE_THETA ** (-jnp.arange(0, axis_dim, 2, dtype=F32) / axis_dim)
    ang_r = row_idx[:, None] * inv_freq[None, :]
    ang_c = col_idx[:, None] * inv_freq[None, :]
    ang = jnp.concatenate([ang_r, ang_r, ang_c, ang_c], axis=-1)
    return jnp.cos(ang), jnp.sin(ang)


def _qk_prep_kernel(x_ref, g_ref, cos_ref, sin_ref, o_ref):
    cos = cos_ref[...]
    sin = sin_ref[...]
    lane = lax.broadcasted_iota(jnp.int32, cos.shape, 1)
    quarter = A_HEAD_DIM // 4
    first = (lane % (2 * quarter)) < quarter
    for h in range(A_HEADS + A_KV_HEADS):
        cols = slice(h * A_HEAD_DIM, (h + 1) * A_HEAD_DIM)
        x = x_ref[:, cols]
        xn = x * lax.rsqrt(jnp.mean(x * x, axis=-1, keepdims=True) + NORM_EPS) * g_ref[h]
        rot = jnp.where(first, -pltpu.roll(xn, A_HEAD_DIM - quarter, 1), pltpu.roll(xn, quarter, 1))
        y = xn * cos + rot * sin
        if h < A_HEADS:
            y = y * A_HEAD_DIM**-0.5
        o_ref[:, cols] = y.astype(o_ref.dtype)


def _qk_prep(proj, gains, cos, sin, seq, tm=256):
    m = proj.shape[0]
    tm = min(tm, seq)
    width = (A_HEADS + A_KV_HEADS) * A_HEAD_DIM
    nt = seq // tm
    return pl.pallas_call(
        _qk_prep_kernel,
        out_shape=jax.ShapeDtypeStruct((m, width), BF16),
        grid=(m // tm,),
        in_specs=[
            pl.BlockSpec((tm, width), lambda i: (i, 0)),
            pl.BlockSpec(gains.shape, lambda i: (0, 0, 0)),
            pl.BlockSpec((tm, A_HEAD_DIM), lambda i: (i % nt, 0)),
            pl.BlockSpec((tm, A_HEAD_DIM), lambda i: (i % nt, 0)),
        ],
        out_specs=pl.BlockSpec((tm, width), lambda i: (i, 0)),
        compiler_params=_cp("parallel"),
        name="a_qk_prep",
    )(proj, gains, cos, sin)


def _attn_a_kernel(q_ref, k_ref, v_ref, o_ref):
    s = lax.dot_general(q_ref[...], k_ref[...], NT_DIMS, preferred_element_type=F32)
    p = jnp.exp(s - jnp.max(s, axis=-1, keepdims=True))
    l = jnp.sum(p, axis=-1, keepdims=True)
    o = jnp.dot(p.astype(BF16), v_ref[...].astype(BF16), preferred_element_type=F32)
    o_ref[...] = (o / l).astype(o_ref.dtype)


def _attn_a(qk, proj, batch, seq, tq=512):
    tq = min(tq, seq)
    nq = seq // tq
    group = A_HEADS // A_KV_HEADS
    k_blk = COL_AK // A_HEAD_DIM
    v_blk = COL_AV // A_HEAD_DIM
    return pl.pallas_call(
        _attn_a_kernel,
        out_shape=jax.ShapeDtypeStruct((batch * seq, A_WIDTH), BF16),
        grid=(batch, A_HEADS, nq),
        in_specs=[
            pl.BlockSpec((tq, A_HEAD_DIM), lambda b, h, i: (b * nq + i, h)),
            pl.BlockSpec((seq, A_HEAD_DIM), lambda b, h, i: (b, k_blk + h // group)),
            pl.BlockSpec((seq, A_HEAD_DIM), lambda b, h, i: (b, v_blk + h // group)),
        ],
        out_specs=pl.BlockSpec((tq, A_HEAD_DIM), lambda b, h, i: (b * nq + i, h)),
        compiler_params=_cp("parallel", "arbitrary", "arbitrary"),
        name="a_attention",
    )(qk, qk, proj)


def _attn_d_kernel(q_ref, k_ref, v_ref, slope_ref, lam_ref, g_ref, o_ref, *, tq, lam_init):
    q = q_ref[...]
    k = k_ref[...].astype(BF16)
    seq = k.shape[0]
    lane = lax.broadcasted_iota(jnp.int32, q.shape, 1)
    q0 = jnp.where(lane < D_HEAD_DIM, q, 0.0).astype(BF16)
    q1 = jnp.where(lane >= D_HEAD_DIM, q, 0.0).astype(BF16)
    q_pos = pl.program_id(2) * tq + lax.broadcasted_iota(jnp.int32, (tq, seq), 0)
    k_pos = lax.broadcasted_iota(jnp.int32, (tq, seq), 1)
    bias = slope_ref[0][:, 0:1] * jnp.abs(q_pos - k_pos).astype(F32)
    scale = D_HEAD_DIM**-0.5

    def softmax_parts(qc):
        s = lax.dot_general(qc, k, NT_DIMS, preferred_element_type=F32) * scale - bias
        p = jnp.exp(s - jnp.max(s, axis=-1, keepdims=True))
        return p, jnp.sum(p, axis=-1, keepdims=True)

    p0, l0 = softmax_parts(q0)
    p1, l1 = softmax_parts(q1)
    lam_v = lam_ref[...]
    lam = (
        jnp.exp(jnp.sum(lam_v[0:1] * lam_v[1:2], axis=-1, keepdims=True))
        - jnp.exp(jnp.sum(lam_v[2:3] * lam_v[3:4], axis=-1, keepdims=True))
        + lam_init
    )
    attn = p0 * (1.0 / l0) - p1 * (lam / l1)
    o = jnp.dot(attn.astype(BF16), v_ref[...].astype(BF16), preferred_element_type=F32)
    o = o * lax.rsqrt(jnp.mean(o * o, axis=-1, keepdims=True) + NORM_EPS) * g_ref[...]
    o_ref[...] = (o * (1.0 - lam_init)).astype(o_ref.dtype)


def _attn_d(proj, lam_vecs, subln, lam_init, batch, seq, tq=256):
    tq = min(tq, seq)
    nq = seq // tq
    slopes = 2.0 ** (-8.0 * np.arange(1, D_HEADS + 1, dtype=np.float32) / D_HEADS)
    slopes = jnp.asarray(np.broadcast_to(slopes[:, None, None], (D_HEADS, 1, 128)).astype(np.float32))
    q_blk, k_blk, v_blk = COL_DQ // D_V_DIM, COL_DK // D_V_DIM, COL_DV // D_V_DIM
    return pl.pallas_call(
        functools.partial(_attn_d_kernel, tq=tq, lam_init=lam_init),
        out_shape=jax.ShapeDtypeStruct((batch * seq, D_WIDTH), BF16),
        grid=(batch, D_HEADS, nq),
        in_specs=[
            pl.BlockSpec((tq, D_V_DIM), lambda b, h, i: (b * nq + i, q_blk + h)),
            pl.BlockSpec((seq, D_V_DIM), lambda b, h, i: (b, k_blk + h)),
            pl.BlockSpec((seq, D_V_DIM), lambda b, h, i: (b, v_blk + h)),
            pl.BlockSpec((1, 1, 128), lambda b, h, i: (h, 0, 0)),
            pl.BlockSpec((4, D_HEAD_DIM), lambda b, h, i: (0, 0)),
            pl.BlockSpec((1, D_V_DIM), lambda b, h, i: (0, 0)),
        ],
        out_specs=pl.BlockSpec((tq, D_V_DIM), lambda b, h, i: (b * nq + i, h)),
        compiler_params=_cp("parallel", "arbitrary", "arbitrary"),
        name="d_attention",
    )(proj, proj, proj, slopes, lam_vecs, subln.reshape(1, D_V_DIM))


def _dft_tables(seq):
    n = 2 * seq
    f = jnp.arange(seq, dtype=jnp.int32)
    odd = 2 * f + 1
    m_half = (odd[:, None] * odd[None, :]) % (4 * n)
    ang_half = m_half.astype(F32) * (2.0 * math.pi / (4 * n))
    m_int = (odd[:, None] * f[None, :]) % (2 * n)
    ang_int = m_int.astype(F32) * (2.0 * math.pi / (2 * n))
    return jnp.cos(ang_half).astype(BF16), jnp.sin(ang_half).astype(BF16), jnp.cos(ang_int), jnp.sin(ang_int)


def _filter_features(seq):
    t = jnp.linspace(0.0, 1.0, seq, dtype=F32)[:, None]
    n_bands = (B_EMB_DIM - 1) // 2
    bands = jnp.linspace(1e-4, n_bands - 1, n_bands, dtype=F32)[None, :]
    ang = (2.0 * math.pi / seq) * jnp.arange(seq, dtype=F32)[:, None] * bands
    z = jnp.concatenate([t, jnp.cos(ang), -jnp.sin(ang)], axis=-1)
    z = jnp.pad(z, ((0, 0), (0, B_FILTER_HIDDEN - B_EMB_DIM)))
    max_decay = math.log(B_DECAY_TARGET) / B_FAST_DECAY_PCT
    min_decay = math.log(B_DECAY_TARGET) / B_SLOW_DECAY_PCT
    deltas = jnp.abs(jnp.linspace(min_decay, max_decay, B_WIDTH, dtype=F32))[None, :]
    return z, t, deltas


def _filter_kernel(z_ref, t_ref, dl_ref, w1_ref, b1_ref, w2_ref, b2_ref, w3_ref, b3_ref, w4_ref, fr_ref, hs_ref, hd_ref):
    fr = fr_ref[...]
    hid = jnp.sin(fr * (jnp.dot(z_ref[...], w1_ref[...], preferred_element_type=F32, precision=HIGHEST) + b1_ref[...]))
    hid = jnp.sin(fr * (jnp.dot(hid, w2_ref[...], preferred_element_type=F32, precision=HIGHEST) + b2_ref[...]))
    hid = jnp.sin(fr * (jnp.dot(hid, w3_ref[...], preferred_element_type=F32, precision=HIGHEST) + b3_ref[...]))
    h = jnp.dot(hid, w4_ref[...], preferred_element_type=F32, precision=HIGHEST)
    window = jnp.exp(-t_ref[...] * dl_ref[...])
    h_fwd = h[:, :B_WIDTH] * window
    h_bwd = h[:, B_WIDTH:] * window
    row = lax.broadcasted_iota(jnp.int32, h_bwd.shape, 0)
    h_bwd = jnp.where(row == 0, 0.0, h_bwd)
    norm = jnp.sum(jnp.abs(h_fwd), axis=0, keepdims=True) + jnp.sum(jnp.abs(h_bwd), axis=0, keepdims=True)
    seq = h.shape[0]
    inv_n = 1.0 / seq
    hs_ref[...] = (h_fwd + h_bwd) / norm * inv_n
    hd_ref[...] = (h_bwd - h_fwd) / norm * inv_n


def _hyena_filter_spectrum(seq, feats, cos_int, sin_int, w1, b1, w2, b2, w3, b3, w4, freq):
    z, t, deltas = feats
    w1p = jnp.pad(w1, ((0, B_FILTER_HIDDEN - B_EMB_DIM), (0, 0)))
    row = lambda v: v.reshape(1, -1)
    args = (z, t, deltas, w1p, row(b1), w2, row(b2), w3, row(b3), w4, row(freq))
    h_sum, h_diff = pl.pallas_call(
        _filter_kernel,
        out_shape=[jax.ShapeDtypeStruct((seq, B_WIDTH), F32)] * 2,
        in_specs=[pl.BlockSpec(a.shape, lambda: (0, 0)) for a in args],
        out_specs=[pl.BlockSpec((seq, B_WIDTH), lambda: (0, 0))] * 2,
        compiler_params=pltpu.CompilerParams(vmem_limit_bytes=VMEM_LIMIT_V7X),
        name="b_filter",
    )(*args)
    k_re = _matmul_f32(cos_int, h_sum, 256, B_WIDTH, name="b_filter_dft_re")
    k_im = _matmul_f32(sin_int, h_diff, 256, B_WIDTH, name="b_filter_dft_im")
    return k_re, k_im


def _hyena_kernel(v_ref, x1_ref, x0_ref, wv_ref, w1_ref, w0_ref, bv_ref, b1_ref, b0_ref, skip_ref, c_ref, s_ref, kre_ref, kim_ref, o_ref):
    seq = v_ref.shape[0]
    row = lax.broadcasted_iota(jnp.int32, v_ref.shape, 0)

    def conv3(u_ref, w_ref, b_ref):
        u = u_ref[...]
        w = w_ref[...]
        u_prev = jnp.where(row == 0, 0.0, pltpu.roll(u, 1, 0))
        u_next = jnp.where(row == seq - 1, 0.0, pltpu.roll(u, seq - 1, 0))
        return w[0:1] * u_prev + w[1:2] * u + w[2:3] * u_next + b_ref[...]

    z = conv3(v_ref, wv_ref, bv_ref) * conv3(x1_ref, w1_ref, b1_ref)
    zb = z.astype(BF16)
    cz = jnp.dot(c_ref[...], zb, preferred_element_type=F32)
    sz = jnp.dot(s_ref[...], zb, preferred_element_type=F32)
    k_re = kre_ref[...]
    k_im = kim_ref[...]
    y_re = (cz * k_re + sz * k_im).astype(BF16)
    y_im = (cz * k_im - sz * k_re).astype(BF16)
    y = jnp.dot(c_ref[...], y_re, preferred_element_type=F32) - jnp.dot(s_ref[...], y_im, preferred_element_type=F32)
    y = y + z * skip_ref[...]
    o_ref[...] = (y * conv3(x0_ref, w0_ref, b0_ref)).astype(o_ref.dtype)


def _hyena(proj, conv_w, conv_b, skip, cos_half, sin_half, k_re, k_im, batch, seq, cb=128):
    ncb = B_WIDTH // cb
    u_blk = COL_BU // cb

    def u_spec(part):
        return pl.BlockSpec((seq, cb), lambda b, j: (b, u_blk + part * ncb + j))

    def w_spec(rows, part):
        return pl.BlockSpec((rows, cb), lambda b, j: (0, part * ncb + j))

    return pl.pallas_call(
        _hyena_kernel,
        out_shape=jax.ShapeDtypeStruct((batch * seq, B_WIDTH), BF16),
        grid=(batch, ncb),
        in_specs=[
            u_spec(0), u_spec(1), u_spec(2),
            w_spec(3, 0), w_spec(3, 1), w_spec(3, 2),
            w_spec(1, 0), w_spec(1, 1), w_spec(1, 2),
            pl.BlockSpec((1, cb), lambda b, j: (0, j)),
            _const_spec((seq, seq)), _const_spec((seq, seq)),
            pl.BlockSpec((seq, cb), lambda b, j: (0, j)),
            pl.BlockSpec((seq, cb), lambda b, j: (0, j)),
        ],
        out_specs=pl.BlockSpec((seq, cb), lambda b, j: (b, j)),
        compiler_params=_cp("parallel", "arbitrary"),
        name="b_hyena",
    )(proj, proj, proj, conv_w, conv_w, conv_w, conv_b.reshape(1, -1), conv_b.reshape(1, -1), conv_b.reshape(1, -1),
      skip.reshape(1, -1), cos_half, sin_half, k_re, k_im)


def _rwkv_prep_kernel(cur_ref, lo_ref, pcur_ref, plo_ref, ncur_ref, nlo_ref, mu_ref, mulo_ref, w0_ref, wup_ref, a0_ref, aup_ref,
                      kk_ref, ka_ref, ones_ref, o_ref, *, blocks_per_seq):
    i = pl.program_id(0)
    first = (i % blocks_per_seq) == 0
    last = (i % blocks_per_seq) == blocks_per_seq - 1
    halo = pcur_ref.shape[0]

    def neighbours(x_ref, p_ref, n_ref):
        x = x_ref[...]
        tm = x.shape[0]
        row = lax.broadcasted_iota(jnp.int32, x.shape, 0)
        p_row = jnp.where(first, 0.0, p_ref[halo - 1:halo, :])
        n_row = jnp.where(last, 0.0, n_ref[0:1, :])
        prev = jnp.where(row == 0, p_row, pltpu.roll(x, 1, 0))
        nxt = jnp.where(row == tm - 1, n_row, pltpu.roll(x, tm - 1, 0))
        return x, (prev, nxt)

    cur, cur_sh = neighbours(cur_ref, pcur_ref, ncur_ref)
    lo, lo_sh = neighbours(lo_ref, plo_ref, nlo_ref)
    k_k = kk_ref[...]
    k_a = ka_ref[...]
    for d in range(2):
        f = cur + (cur_sh[d] - cur) * mu_ref[d]
        f_lo = lo + (lo_sh[d] - lo) * mulo_ref[d]
        r = f[:, :C_WIDTH]
        k = f[:, C_WIDTH:2 * C_WIDTH]
        v = f[:, 2 * C_WIDTH:]
        w_lo = f_lo[:, :C_LORA_PAD]
        a_lo = f_lo[:, C_LORA_PAD:]
        x = w0_ref[d] + jnp.dot(jnp.tanh(w_lo), wup_ref[d], preferred_element_type=F32, precision=HIGHEST)
        w = -(jnp.maximum(-x, 0.0) + jnp.log(1.0 + jnp.exp(-jnp.abs(x)))) - 0.5
        decay = jnp.exp(-jnp.exp(w))
        a = jax.nn.sigmoid(a0_ref[d] + jnp.dot(a_lo, aup_ref[d], preferred_element_type=F32, precision=HIGHEST))
        kk = k * k_k
        ss = jnp.dot(kk * kk, ones_ref[...], preferred_element_type=F32, precision=HIGHEST)
        kk = kk / jnp.maximum(jnp.sqrt(ss), 1e-12)
        o_ref[d, 0] = r
        o_ref[d, 1] = decay
        o_ref[d, 2] = k * (1.0 + (a - 1.0) * k_a)
        o_ref[d, 3] = v
        o_ref[d, 4] = -kk
        o_ref[d, 5] = kk * a


def _rwkv_prep(proj, mu, mu_lo, w0, w_up, a0, a_up, k_k, k_a, seq, tm=256):
    m = proj.shape[0]
    tm = min(tm, seq)
    halo = 8
    hb = tm // halo
    n_hblk = m // halo
    cur_blk = COL_CF // (3 * C_WIDTH)
    lo_blk = COL_CLO // (2 * C_LORA_PAD)
    head_of = np.arange(C_WIDTH) // C_HEAD_DIM
    ones = jnp.asarray((head_of[:, None] == head_of[None, :]).astype(np.float32))
    wide, narrow = 3 * C_WIDTH, 2 * C_LORA_PAD

    def prev_map(blk):
        return lambda i: (jnp.maximum(i * hb - 1, 0), blk)

    def next_map(blk):
        return lambda i: (jnp.minimum((i + 1) * hb, n_hblk - 1), blk)

    small = lambda a: pl.BlockSpec(a.shape, lambda i: (0,) * a.ndim)
    consts = (mu, mu_lo, w0, w_up, a0, a_up, k_k, k_a, ones)
    return pl.pallas_call(
        functools.partial(_rwkv_prep_kernel, blocks_per_seq=seq // tm),
        out_shape=jax.ShapeDtypeStruct((2, 6, m, C_WIDTH), F32),
        grid=(m // tm,),
        in_specs=[
            pl.BlockSpec((tm, wide), lambda i: (i, cur_blk)),
            pl.BlockSpec((tm, narrow), lambda i: (i, lo_blk)),
            pl.BlockSpec((halo, wide), prev_map(cur_blk)),
            pl.BlockSpec((halo, narrow), prev_map(lo_blk)),
            pl.BlockSpec((halo, wide), next_map(cur_blk)),
            pl.BlockSpec((halo, narrow), next_map(lo_blk)),
        ] + [small(a) for a in consts],
        out_specs=pl.BlockSpec((2, 6, tm, C_WIDTH), lambda i: (0, 0, i, 0)),
        compiler_params=_cp("parallel"),
        name="c_prep",
    )(proj, proj, proj, proj, proj, proj, *consts)


def _rwkv_scan_kernel(xf_ref, xb_ref, rk_ref, lnw_ref, lnb_ref, of_ref, ob_ref, s_ref, x_ref, o_ref, *, tc, k_chunk):
    n = s_ref.shape[0]
    lanes = s_ref.shape[2]

    @pl.when(pl.program_id(0) == 0)
    def _():
        s_ref[...] = jnp.zeros_like(s_ref)

    fwd = lax.broadcasted_iota(jnp.int32, xf_ref.shape[1:], 2) < lanes // 2
    for t in range(tc):
        x_ref[t] = jnp.where(fwd, xf_ref[t], xb_ref[tc - 1 - t])

    def step(t, carry):
        vv = x_ref[t, 3]

        def sa_chunk(c, sa):
            base = pl.multiple_of(c * k_chunk, k_chunk)
            for j in range(k_chunk):
                sa = sa + s_ref[base + j] * x_ref[t, 4, pl.ds(base + j, 1), :]
            return sa

        sa = lax.fori_loop(0, n // k_chunk, sa_chunk, jnp.zeros((n, lanes), F32))

        def update_chunk(c, o):
            base = pl.multiple_of(c * k_chunk, k_chunk)
            for j in range(k_chunk):
                k = base + j
                sk = s_ref[k] * x_ref[t, 1, pl.ds(k, 1), :] + sa * x_ref[t, 5, pl.ds(k, 1), :] + vv * x_ref[t, 2, pl.ds(k, 1), :]
                s_ref[k] = sk
                o = o + sk * x_ref[t, 0, pl.ds(k, 1), :]
            return o

        o_ref[t] = lax.fori_loop(0, n // k_chunk, update_chunk, jnp.zeros((n, lanes), F32))
        return carry

    lax.fori_loop(0, tc, step, 0)

    o = o_ref[...]
    mean = jnp.mean(o, axis=1, keepdims=True)
    var = jnp.mean(jnp.square(o - mean), axis=1, keepdims=True)
    o = (o - mean) * lax.rsqrt(var + C_GN_EPS) * lnw_ref[...] + lnb_ref[...]
    bonus = jnp.sum(x_ref[:, 0] * x_ref[:, 2] * rk_ref[...], axis=1, keepdims=True)
    o = o + bonus * x_ref[:, 3]
    of_ref[...] = o
    for t in range(tc):
        ob_ref[t] = o[tc - 1 - t]


def _rwkv_scan(xs, rk, lnw, lnb, tc=16, k_chunk=32):
    seq, _, n, lanes = xs.shape
    tc = min(tc, seq)
    nc = seq // tc
    tile = lambda: pl.BlockSpec((n, lanes), lambda i: (0, 0))
    out = jax.ShapeDtypeStruct((seq, n, lanes), F32)
    return pl.pallas_call(
        functools.partial(_rwkv_scan_kernel, tc=tc, k_chunk=min(k_chunk, n)),
        out_shape=[out, out],
        grid=(nc,),
        in_specs=[
            pl.BlockSpec((tc, 6, n, lanes), lambda i: (i, 0, 0, 0)),
            pl.BlockSpec((tc, 6, n, lanes), lambda i: (nc - 1 - i, 0, 0, 0)),
            tile(), tile(), tile(),
        ],
        out_specs=[
            pl.BlockSpec((tc, n, lanes), lambda i: (i, 0, 0)),
            pl.BlockSpec((tc, n, lanes), lambda i: (nc - 1 - i, 0, 0)),
        ],
        scratch_shapes=[pltpu.VMEM((n, n, lanes), F32), pltpu.VMEM((tc, 6, n, lanes), F32), pltpu.VMEM((tc, n, lanes), F32)],
        compiler_params=_cp("arbitrary"),
        name="c_scan",
    )(xs, xs, rk, lnw, lnb)


def _rwkv_out_kernel(o_ref, glo_ref, gup_ref, y_ref):
    g = jnp.dot(jax.nn.sigmoid(glo_ref[...]).astype(BF16), gup_ref[...], preferred_element_type=F32)
    y_ref[...] = ((o_ref[0] + o_ref[1]) * g).astype(y_ref.dtype)


def _rwkv_out(o_dirs, proj, g_up, tm=512):
    m = proj.shape[0]
    tm = min(tm, m)
    g_blk = COL_CG // C_GATE_LORA
    return pl.pallas_call(
        _rwkv_out_kernel,
        out_shape=jax.ShapeDtypeStruct((m, C_WIDTH), BF16),
        grid=(m // tm,),
        in_specs=[
            pl.BlockSpec((2, tm, C_WIDTH), lambda i: (0, i, 0)),
            pl.BlockSpec((tm, C_GATE_LORA), lambda i: (i, g_blk)),
            pl.BlockSpec((C_GATE_LORA, C_WIDTH), lambda i: (0, 0)),
        ],
        out_specs=pl.BlockSpec((tm, C_WIDTH), lambda i: (i, 0)),
        compiler_params=_cp("parallel"),
        name="c_out",
    )(o_dirs, proj, g_up)


def _rwkv(proj, mu, w0, w_up, a0, a_up, g_up, k_k, k_a, r_k, ln_w, ln_b, batch, seq):
    pad_lo = C_LORA_PAD - C_LORA
    split = 3 * C_WIDTH
    mu_wide = mu[:, None, :split]
    mu_lo = jnp.concatenate([
        jnp.pad(mu[:, split:split + C_LORA], ((0, 0), (0, pad_lo))),
        jnp.pad(mu[:, split + C_LORA:], ((0, 0), (0, pad_lo))),
    ], axis=-1)[:, None, :]
    w_up_p = jnp.pad(w_up, ((0, 0), (0, pad_lo), (0, 0)))
    a_up_p = jnp.pad(a_up, ((0, 0), (0, pad_lo), (0, 0)))
    feats = _rwkv_prep(proj, mu_wide, mu_lo, w0[:, None, :], w_up_p, a0[:, None, :], a_up_p,
                       k_k.reshape(1, C_WIDTH), k_a.reshape(1, C_WIDTH), seq)
    lanes = 2 * batch * C_HEADS
    xs = feats.reshape(2, 6, batch, seq, C_HEADS, C_HEAD_DIM).transpose(3, 1, 5, 0, 2, 4).reshape(seq, 6, C_HEAD_DIM, lanes)
    per_lane = lambda p: jnp.tile(p.reshape(C_HEADS, C_HEAD_DIM).T, (1, 2 * batch))
    o_f, o_b = _rwkv_scan(xs, per_lane(r_k), per_lane(ln_w), per_lane(ln_b))
    o = jnp.stack([o_f[:, :, :lanes // 2], o_b[:, :, lanes // 2:]])
    o = o.reshape(2, seq, C_HEAD_DIM, batch, C_HEADS).transpose(0, 3, 1, 4, 2).reshape(2, batch * seq, C_WIDTH)
    return _rwkv_out(o, proj, g_up.astype(BF16))


def _merge_kernel(h_ref, wg_ref, wbr_ref, oa_ref, ob_ref, oc_ref, od_ref, o_ref, wgb_ref, wbrb_ref):
    _cast_weights_once((wg_ref, wbr_ref), (wgb_ref, wbrb_ref))
    h = h_ref[...]
    acc = None
    row = 0
    for i, b_ref in enumerate((oa_ref, ob_ref, oc_ref, od_ref)):
        width = b_ref.shape[1]
        gate = jax.nn.sigmoid(jnp.dot(h, wgb_ref[i], preferred_element_type=F32))
        term = gate * jnp.dot(b_ref[...], wbrb_ref[row:row + width, :], preferred_element_type=F32)
        acc = term if acc is None else acc + term
        row += width
    o_ref[...] = acc.astype(o_ref.dtype)


def _merge(h, w_gate, w_branch, branches, tm=512, tn=256):
    m = h.shape[0]
    tm = min(tm, m)
    d_mix = w_branch.shape[0]
    row = lambda a: pl.BlockSpec((tm, a.shape[1]), lambda j, i: (i, 0))
    return pl.pallas_call(
        _merge_kernel,
        out_shape=jax.ShapeDtypeStruct((m, D_MODEL), BF16),
        grid=(D_MODEL // tn, m // tm),
        in_specs=[
            row(h),
            pl.BlockSpec((4, D_MODEL, tn), lambda j, i: (0, 0, j)),
            pl.BlockSpec((d_mix, tn), lambda j, i: (0, j)),
        ] + [row(b) for b in branches],
        out_specs=pl.BlockSpec((tm, tn), lambda j, i: (i, j)),
        scratch_shapes=[pltpu.VMEM((4, D_MODEL, tn), BF16), pltpu.VMEM((d_mix, tn), BF16)],
        compiler_params=_cp("parallel", "arbitrary"),
        name="gated_merge",
    )(h, w_gate, w_branch, *branches)


def _ffn_up_kernel(h_ref, wg_ref, wu_ref, o_ref, wgb_ref, wub_ref):
    _cast_weights_once((wg_ref, wu_ref), (wgb_ref, wub_ref))
    h = h_ref[...]
    g = jnp.dot(h, wgb_ref[...], preferred_element_type=F32)
    u = jnp.dot(h, wub_ref[...], preferred_element_type=F32)
    o_ref[...] = (g * jax.nn.sigmoid(g) * u).astype(o_ref.dtype)


def _ffn_up(h, w_gate, w_up, tm=1024, tn=512):
    m, k = h.shape
    n = w_gate.shape[1]
    tm = min(tm, m)
    return pl.pallas_call(
        _ffn_up_kernel,
        out_shape=jax.ShapeDtypeStruct((m, n), BF16),
        grid=(n // tn, m // tm),
        in_specs=[
            pl.BlockSpec((tm, k), lambda j, i: (i, 0)),
            pl.BlockSpec((k, tn), lambda j, i: (0, j)),
            pl.BlockSpec((k, tn), lambda j, i: (0, j)),
        ],
        out_specs=pl.BlockSpec((tm, tn), lambda j, i: (i, j)),
        scratch_shapes=[pltpu.VMEM((k, tn), BF16), pltpu.VMEM((k, tn), BF16)],
        compiler_params=_cp("parallel", "arbitrary"),
        name="ffn_up",
    )(h, w_gate, w_up)


def _pad_w_in(w):
    pad = ((0, 0), (0, C_LORA_PAD - C_LORA))
    lo = COL_CLO
    return jnp.concatenate([
        w[:, :lo],
        jnp.pad(w[:, lo:lo + C_LORA], pad),
        jnp.pad(w[:, lo + C_LORA:lo + 2 * C_LORA], pad),
        w[:, lo + 2 * C_LORA:],
    ], axis=1)


def kernel(x, norm_mix, w_in, a_q_norm, a_k_norm, b_conv_w, b_conv_b, b_filt_w1, b_filt_b1, b_filt_w2, b_filt_b2, b_filt_w3, b_filt_b3, b_filt_w4, b_filt_freq, b_skip, c_mu, c_w0, c_w_up, c_a0, c_a_up, c_g_up, c_k_k, c_k_a, c_r_k, c_ln_w, c_ln_b, d_lq1, d_lk1, d_lq2, d_lk2, d_subln, w_gate, w_branch, w_out, norm_ffn, w_ff_gate, w_ff_up, w_ff_down, norm_final):
    batch, seq, _ = x.shape
    m = batch * seq
    cos, sin = _rope_tables(seq)
    cos_half, sin_half, cos_int, sin_int = _dft_tables(seq)
    filt_feats = _filter_features(seq)
    x = x.reshape(m, D_MODEL)
    for l in range(DEPTH):
        h = _rmsnorm(x, norm_mix[l], BF16)
        proj = _matmul(h, _pad_w_in(w_in[l]), F32, 1024, 512, name="in_proj")

        gains = jnp.concatenate([
            jnp.broadcast_to(a_q_norm[l], (A_HEADS, A_HEAD_DIM)),
            jnp.broadcast_to(a_k_norm[l], (A_KV_HEADS, A_HEAD_DIM)),
        ])[:, None, :]
        o_a = _attn_a(_qk_prep(proj, gains, cos, sin, seq), proj, batch, seq)

        k_re, k_im = _hyena_filter_spectrum(seq, filt_feats, cos_int, sin_int, b_filt_w1[l], b_filt_b1[l], b_filt_w2[l],
                                            b_filt_b2[l], b_filt_w3[l], b_filt_b3[l], b_filt_w4[l], b_filt_freq[l])
        o_b = _hyena(proj, b_conv_w[l], b_conv_b[l], b_skip[l], cos_half, sin_half, k_re, k_im, batch, seq)

        o_c = _rwkv(proj, c_mu[l], c_w0[l], c_w_up[l], c_a0[l], c_a_up[l], c_g_up[l], c_k_k[l], c_k_a[l], c_r_k[l],
                    c_ln_w[l], c_ln_b[l], batch, seq)

        lam_init = 0.8 - 0.6 * math.exp(-0.3 * l)
        lam_vecs = jnp.stack([d_lq1[l], d_lk1[l], d_lq2[l], d_lk2[l]])
        o_d = _attn_d(proj, lam_vecs, d_subln[l], lam_init, batch, seq)

        merged = _merge(h, w_gate[l], w_branch[l], (o_a, o_b, o_c, o_d))
        x = _matmul(merged, w_out[l], F32, 1024, 512, residual=x, name="out_proj")

        h2 = _rmsnorm(x, norm_ffn[l], BF16)
        mid = _ffn_up(h2, w_ff_gate[l], w_ff_up[l])
        x = _matmul(mid, w_ff_down[l], F32, 512, 512, residual=x, name="ffn_down")
    return _rmsnorm(x, norm_final, F32).reshape(batch, seq, D_MODEL)
```

```python
import functools
import math

import jax
import jax.numpy as jnp
import numpy as np
from jax import lax
from jax.experimental import pallas as pl
from jax.experimental.pallas import tpu as pltpu

D_MODEL = 2048
DEPTH = 2
GRID_W = 64
NORM_EPS = 1e-6

A_HEADS = 8
A_KV_HEADS = 2
A_HEAD_DIM = 128
A_WIDTH = A_HEADS * A_HEAD_DIM
ROPE_THETA = 10000.0

B_WIDTH = 512
B_EMB_DIM = 33
B_FILTER_HIDDEN = 64
B_DECAY_TARGET = 1e-2
B_FAST_DECAY_PCT = 0.3
B_SLOW_DECAY_PCT = 1.5

C_HEADS = 8
C_HEAD_DIM = 64
C_WIDTH = C_HEADS * C_HEAD_DIM
C_LORA = 96
C_LORA_PAD = 128
C_GATE_LORA = 256
C_GN_EPS = 64e-5

D_HEADS = 4
D_HEAD_DIM = 64
D_V_DIM = 2 * D_HEAD_DIM
D_WIDTH = D_HEADS * D_V_DIM

FFN_HIDDEN = -(-8 * D_MODEL // (3 * 256)) * 256

COL_AQ = 0
COL_AK = COL_AQ + A_WIDTH
COL_AV = COL_AK + A_KV_HEADS * A_HEAD_DIM
COL_BU = COL_AV + A_KV_HEADS * A_HEAD_DIM
COL_CF = COL_BU + 3 * B_WIDTH
COL_CLO = COL_CF + 3 * C_WIDTH
COL_CG = COL_CLO + 2 * C_LORA_PAD
COL_DQ = COL_CG + C_GATE_LORA
COL_DK = COL_DQ + 2 * D_HEADS * D_HEAD_DIM
COL_DV = COL_DK + 2 * D_HEADS * D_HEAD_DIM
D_IN_PAD = COL_DV + D_WIDTH

VMEM_LIMIT_V7X = 56 * 1024 * 1024
F32 = jnp.float32
BF16 = jnp.bfloat16
HIGHEST = lax.Precision.HIGHEST
NT_DIMS = (((1,), (1,)), ((), ()))


def _cp(*sem):
    return pltpu.CompilerParams(dimension_semantics=sem, vmem_limit_bytes=VMEM_LIMIT_V7X)


def _const_spec(shape):
    return pl.BlockSpec(shape, lambda *_: (0,) * len(shape), pipeline_mode=pl.Buffered(1))


def _rmsnorm_kernel(x_ref, g_ref, o_ref):
    x = x_ref[...]
    ms = jnp.mean(x * x, axis=-1, keepdims=True)
    o_ref[...] = (x * lax.rsqrt(ms + NORM_EPS) * g_ref[...]).astype(o_ref.dtype)


def _rmsnorm(x, g, out_dtype, tm=512):
    m, d = x.shape
    tm = min(tm, m)
    return pl.pallas_call(
        _rmsnorm_kernel,
        out_shape=jax.ShapeDtypeStruct((m, d), out_dtype),
        grid=(m // tm,),
        in_specs=[pl.BlockSpec((tm, d), lambda i: (i, 0)), pl.BlockSpec((1, d), lambda i: (0, 0))],
        out_specs=pl.BlockSpec((tm, d), lambda i: (i, 0)),
        compiler_params=_cp("parallel"),
        name="rmsnorm",
    )(x, g.reshape(1, d))


def _mm_f32_kernel(a_ref, b_ref, o_ref):
    o_ref[...] = jnp.dot(a_ref[...], b_ref[...], preferred_element_type=F32, precision=HIGHEST)


def _matmul_f32(a, b, tm, tn, name):
    m, k = a.shape
    n = b.shape[1]
    tm, tn = min(tm, m), min(tn, n)
    return pl.pallas_call(
        _mm_f32_kernel,
        out_shape=jax.ShapeDtypeStruct((m, n), F32),
        grid=(m // tm, n // tn),
        in_specs=[pl.BlockSpec((tm, k), lambda i, j: (i, 0)), pl.BlockSpec((k, tn), lambda i, j: (0, j))],
        out_specs=pl.BlockSpec((tm, tn), lambda i, j: (i, j)),
        compiler_params=_cp("parallel", "arbitrary"),
        name=name,
    )(a, b)


def _cast_weights_once(w_refs, wb_refs):
    @pl.when(pl.program_id(1) == 0)
    def _():
        for w_ref, wb_ref in zip(w_refs, wb_refs):
            wb_ref[...] = w_ref[...].astype(BF16)


def _mm_kernel(a_ref, w_ref, o_ref, wb_ref):
    _cast_weights_once((w_ref,), (wb_ref,))
    o_ref[...] = jnp.dot(a_ref[...], wb_ref[...], preferred_element_type=F32).astype(o_ref.dtype)


def _mm_res_kernel(a_ref, w_ref, r_ref, o_ref, wb_ref):
    _cast_weights_once((w_ref,), (wb_ref,))
    o_ref[...] = r_ref[...] + jnp.dot(a_ref[...], wb_ref[...], preferred_element_type=F32)


def _matmul(a, w, out_dtype, tm, tn, residual=None, name="matmul"):
    m, k = a.shape
    n = w.shape[1]
    tm, tn = min(tm, m), min(tn, n)
    in_specs = [pl.BlockSpec((tm, k), lambda j, i: (i, 0)), pl.BlockSpec((k, tn), lambda j, i: (0, j))]
    args = [a, w]
    body = _mm_kernel
    if residual is not None:
        body = _mm_res_kernel
        in_specs.append(pl.BlockSpec((tm, tn), lambda j, i: (i, j)))
        args.append(residual)
    return pl.pallas_call(
        body,
        out_shape=jax.ShapeDtypeStruct((m, n), out_dtype),
        grid=(n // tn, m // tm),
        in_specs=in_specs,
        out_specs=pl.BlockSpec((tm, tn), lambda j, i: (i, j)),
        scratch_shapes=[pltpu.VMEM((k, tn), BF16)],
        compiler_params=_cp("parallel", "arbitrary"),
        name=name,
    )(*args)


def _rope_tables(seq):
    rows = seq // GRID_W
    row_idx = jnp.repeat(jnp.arange(rows, dtype=F32), GRID_W)
    col_idx = jnp.tile(jnp.arange(GRID_W, dtype=F32), rows)
    axis_dim = A_HEAD_DIM // 2
    inv_freq = ROPE_THETA ** (-jnp.arange(0, axis_dim, 2, dtype=F32) / axis_dim)
    ang_r = row_idx[:, None] * inv_freq[None, :]
    ang_c = col_idx[:, None] * inv_freq[None, :]
    ang = jnp.concatenate([ang_r, ang_r, ang_c, ang_c], axis=-1)
    return jnp.cos(ang), jnp.sin(ang)


def _qk_prep_kernel(x_ref, g_ref, cos_ref, sin_ref, o_ref):
    cos = cos_ref[...]
    sin = sin_ref[...]
    lane = lax.broadcasted_iota(jnp.int32, cos.shape, 1)
    quarter = A_HEAD_DIM // 4
    first = (lane % (2 * quarter)) < quarter
    for h in range(A_HEADS + A_KV_HEADS):
        cols = slice(h * A_HEAD_DIM, (h + 1) * A_HEAD_DIM)
        x = x_ref[:, cols]
        xn = x * lax.rsqrt(jnp.mean(x * x, axis=-1, keepdims=True) + NORM_EPS) * g_ref[h]
        rot = jnp.where(first, -pltpu.roll(xn, A_HEAD_DIM - quarter, 1), pltpu.roll(xn, quarter, 1))
        y = xn * cos + rot * sin
        if h < A_HEADS:
            y = y * A_HEAD_DIM**-0.5
        o_ref[:, cols] = y.astype(o_ref.dtype)


def _qk_prep(proj, gains, cos, sin, seq, tm=256):
    m = proj.shape[0]
    tm = min(tm, seq)
    width = (A_HEADS + A_KV_HEADS) * A_HEAD_DIM
    nt = seq // tm
    return pl.pallas_call(
        _qk_prep_kernel,
        out_shape=jax.ShapeDtypeStruct((m, width), BF16),
        grid=(m // tm,),
        in_specs=[
            pl.BlockSpec((tm, width), lambda i: (i, 0)),
            pl.BlockSpec(gains.shape, lambda i: (0, 0, 0)),
            pl.BlockSpec((tm, A_HEAD_DIM), lambda i: (i % nt, 0)),
            pl.BlockSpec((tm, A_HEAD_DIM), lambda i: (i % nt, 0)),
        ],
        out_specs=pl.BlockSpec((tm, width), lambda i: (i, 0)),
        compiler_params=_cp("parallel"),
        name="a_qk_prep",
    )(proj, gains, cos, sin)


def _attn_a_kernel(q_ref, k_ref, v_ref, o_ref):
    s = lax.dot_general(q_ref[...], k_ref[...], NT_DIMS, preferred_element_type=F32)
    p = jnp.exp(s - jnp.max(s, axis=-1, keepdims=True))
    l = jnp.sum(p, axis=-1, keepdims=True)
    o = jnp.dot(p.astype(BF16), v_ref[...].astype(BF16), preferred_element_type=F32)
    o_ref[...] = (o / l).astype(o_ref.dtype)


def _attn_a(qk, proj, batch, seq, tq=512):
    tq = min(tq, seq)
    nq = seq // tq
    group = A_HEADS // A_KV_HEADS
    k_blk = COL_AK // A_HEAD_DIM
    v_blk = COL_AV // A_HEAD_DIM
    return pl.pallas_call(
        _attn_a_kernel,
        out_shape=jax.ShapeDtypeStruct((batch * seq, A_WIDTH), BF16),
        grid=(batch, A_HEADS, nq),
        in_specs=[
            pl.BlockSpec((tq, A_HEAD_DIM), lambda b, h, i: (b * nq + i, h)),
            pl.BlockSpec((seq, A_HEAD_DIM), lambda b, h, i: (b, k_blk + h // group)),
            pl.BlockSpec((seq, A_HEAD_DIM), lambda b, h, i: (b, v_blk + h // group)),
        ],
        out_specs=pl.BlockSpec((tq, A_HEAD_DIM), lambda b, h, i: (b * nq + i, h)),
        compiler_params=_cp("parallel", "arbitrary", "arbitrary"),
        name="a_attention",
    )(qk, qk, proj)


def _attn_d_kernel(q_ref, k_ref, v_ref, slope_ref, lam_ref, g_ref, o_ref, *, tq, lam_init):
    q = q_ref[...]
    k = k_ref[...].astype(BF16)
    seq = k.shape[0]
    lane = lax.broadcasted_iota(jnp.int32, q.shape, 1)
    q0 = jnp.where(lane < D_HEAD_DIM, q, 0.0).astype(BF16)
    q1 = jnp.where(lane >= D_HEAD_DIM, q, 0.0).astype(BF16)
    q_pos = pl.program_id(2) * tq + lax.broadcasted_iota(jnp.int32, (tq, seq), 0)
    k_pos = lax.broadcasted_iota(jnp.int32, (tq, seq), 1)
    bias = slope_ref[0][:, 0:1] * jnp.abs(q_pos - k_pos).astype(F32)
    scale = D_HEAD_DIM**-0.5

    def softmax_parts(qc):
        s = lax.dot_general(qc, k, NT_DIMS, preferred_element_type=F32) * scale - bias
        p = jnp.exp(s - jnp.max(s, axis=-1, keepdims=True))
        return p, jnp.sum(p, axis=-1, keepdims=True)

    p0, l0 = softmax_parts(q0)
    p1, l1 = softmax_parts(q1)
    lam_v = lam_ref[...]
    lam = (
        jnp.exp(jnp.sum(lam_v[0:1] * lam_v[1:2], axis=-1, keepdims=True))
        - jnp.exp(jnp.sum(lam_v[2:3] * lam_v[3:4], axis=-1, keepdims=True))
        + lam_init
    )
    attn = p0 * (1.0 / l0) - p1 * (lam / l1)
    o = jnp.dot(attn.astype(BF16), v_ref[...].astype(BF16), preferred_element_type=F32)
    o = o * lax.rsqrt(jnp.mean(o * o, axis=-1, keepdims=True) + NORM_EPS) * g_ref[...]
    o_ref[...] = (o * (1.0 - lam_init)).astype(o_ref.dtype)


def _attn_d(proj, lam_vecs, subln, lam_init, batch, seq, tq=256):
    tq = min(tq, seq)
    nq = seq // tq
    slopes = 2.0 ** (-8.0 * np.arange(1, D_HEADS + 1, dtype=np.float32) / D_HEADS)
    slopes = jnp.asarray(np.broadcast_to(slopes[:, None, None], (D_HEADS, 1, 128)).astype(np.float32))
    q_blk, k_blk, v_blk = COL_DQ // D_V_DIM, COL_DK // D_V_DIM, COL_DV // D_V_DIM
    return pl.pallas_call(
        functools.partial(_attn_d_kernel, tq=tq, lam_init=lam_init),
        out_shape=jax.ShapeDtypeStruct((batch * seq, D_WIDTH), BF16),
        grid=(batch, D_HEADS, nq),
        in_specs=[
            pl.BlockSpec((tq, D_V_DIM), lambda b, h, i: (b * nq + i, q_blk + h)),
            pl.BlockSpec((seq, D_V_DIM), lambda b, h, i: (b, k_blk + h)),
            pl.BlockSpec((seq, D_V_DIM), lambda b, h, i: (b, v_blk + h)),
            pl.BlockSpec((1, 1, 128), lambda b, h, i: (h, 0, 0)),
            pl.BlockSpec((4, D_HEAD_DIM), lambda b, h, i: (0, 0)),
            pl.BlockSpec((1, D_V_DIM), lambda b, h, i: (0, 0)),
        ],
        out_specs=pl.BlockSpec((tq, D_V_DIM), lambda b, h, i: (b * nq + i, h)),
        compiler_params=_cp("parallel", "arbitrary", "arbitrary"),
        name="d_attention",
    )(proj, proj, proj, slopes, lam_vecs, subln.reshape(1, D_V_DIM))


def _dft_tables(seq):
    n = 2 * seq
    f = jnp.arange(seq, dtype=jnp.int32)
    odd = 2 * f + 1
    m_half = (odd[:, None] * odd[None, :]) % (4 * n)
    ang_half = m_half.astype(F32) * (2.0 * math.pi / (4 * n))
    m_int = (odd[:, None] * f[None, :]) % (2 * n)
    ang_int = m_int.astype(F32) * (2.0 * math.pi / (2 * n))
    return jnp.cos(ang_half).astype(BF16), jnp.sin(ang_half).astype(BF16), jnp.cos(ang_int), jnp.sin(ang_int)


def _filter_features(seq):
    t = jnp.linspace(0.0, 1.0, seq, dtype=F32)[:, None]
    n_bands = (B_EMB_DIM - 1) // 2
    bands = jnp.linspace(1e-4, n_bands - 1, n_bands, dtype=F32)[None, :]
    ang = (2.0 * math.pi / seq) * jnp.arange(seq, dtype=F32)[:, None] * bands
    z = jnp.concatenate([t, jnp.cos(ang), -jnp.sin(ang)], axis=-1)
    z = jnp.pad(z, ((0, 0), (0, B_FILTER_HIDDEN - B_EMB_DIM)))
    max_decay = math.log(B_DECAY_TARGET) / B_FAST_DECAY_PCT
    min_decay = math.log(B_DECAY_TARGET) / B_SLOW_DECAY_PCT
    deltas = jnp.abs(jnp.linspace(min_decay, max_decay, B_WIDTH, dtype=F32))[None, :]
    return z, t, deltas


def _filter_kernel(z_ref, t_ref, dl_ref, w1_ref, b1_ref, w2_ref, b2_ref, w3_ref, b3_ref, w4_ref, fr_ref, hs_ref, hd_ref):
    fr = fr_ref[...]
    hid = jnp.sin(fr * (jnp.dot(z_ref[...], w1_ref[...], preferred_element_type=F32, precision=HIGHEST) + b1_ref[...]))
    hid = jnp.sin(fr * (jnp.dot(hid, w2_ref[...], preferred_element_type=F32, precision=HIGHEST) + b2_ref[...]))
    hid = jnp.sin(fr * (jnp.dot(hid, w3_ref[...], preferred_element_type=F32, precision=HIGHEST) + b3_ref[...]))
    h = jnp.dot(hid, w4_ref[...], preferred_element_type=F32, precision=HIGHEST)
    window = jnp.exp(-t_ref[...] * dl_ref[...])
    h_fwd = h[:, :B_WIDTH] * window
    h_bwd = h[:, B_WIDTH:] * window
    row = lax.broadcasted_iota(jnp.int32, h_bwd.shape, 0)
    h_bwd = jnp.where(row == 0, 0.0, h_bwd)
    norm = jnp.sum(jnp.abs(h_fwd), axis=0, keepdims=True) + jnp.sum(jnp.abs(h_bwd), axis=0, keepdims=True)
    seq = h.shape[0]
    inv_n = 1.0 / seq
    hs_ref[...] = (h_fwd + h_bwd) / norm * inv_n
    hd_ref[...] = (h_bwd - h_fwd) / norm * inv_n


def _hyena_filter_spectrum(seq, feats, cos_int, sin_int, w1, b1, w2, b2, w3, b3, w4, freq):
    z, t, deltas = feats
    w1p = jnp.pad(w1, ((0, B_FILTER_HIDDEN - B_EMB_DIM), (0, 0)))
    row = lambda v: v.reshape(1, -1)
    args = (z, t, deltas, w1p, row(b1), w2, row(b2), w3, row(b3), w4, row(freq))
    h_sum, h_diff = pl.pallas_call(
        _filter_kernel,
        out_shape=[jax.ShapeDtypeStruct((seq, B_WIDTH), F32)] * 2,
        in_specs=[pl.BlockSpec(a.shape, lambda: (0, 0)) for a in args],
        out_specs=[pl.BlockSpec((seq, B_WIDTH), lambda: (0, 0))] * 2,
        compiler_params=pltpu.CompilerParams(vmem_limit_bytes=VMEM_LIMIT_V7X),
        name="b_filter",
    )(*args)
    k_re = _matmul_f32(cos_int, h_sum, 256, B_WIDTH, name="b_filter_dft_re")
    k_im = _matmul_f32(sin_int, h_diff, 256, B_WIDTH, name="b_filter_dft_im")
    return k_re, k_im


def _hyena_kernel(v_ref, x1_ref, x0_ref, wv_ref, w1_ref, w0_ref, bv_ref, b1_ref, b0_ref, skip_ref, c_ref, s_ref, kre_ref, kim_ref, o_ref):
    seq = v_ref.shape[0]
    row = lax.broadcasted_iota(jnp.int32, v_ref.shape, 0)

    def conv3(u_ref, w_ref, b_ref):
        u = u_ref[...]
        w = w_ref[...]
        u_prev = jnp.where(row == 0, 0.0, pltpu.roll(u, 1, 0))
        u_next = jnp.where(row == seq - 1, 0.0, pltpu.roll(u, seq - 1, 0))
        return w[0:1] * u_prev + w[1:2] * u + w[2:3] * u_next + b_ref[...]

    z = conv3(v_ref, wv_ref, bv_ref) * conv3(x1_ref, w1_ref, b1_ref)
    zb = z.astype(BF16)
    cz = jnp.dot(c_ref[...], zb, preferred_element_type=F32)
    sz = jnp.dot(s_ref[...], zb, preferred_element_type=F32)
    k_re = kre_ref[...]
    k_im = kim_ref[...]
    y_re = (cz * k_re + sz * k_im).astype(BF16)
    y_im = (cz * k_im - sz * k_re).astype(BF16)
    y = jnp.dot(c_ref[...], y_re, preferred_element_type=F32) - jnp.dot(s_ref[...], y_im, preferred_element_type=F32)
    y = y + z * skip_ref[...]
    o_ref[...] = (y * conv3(x0_ref, w0_ref, b0_ref)).astype(o_ref.dtype)


def _hyena(proj, conv_w, conv_b, skip, cos_half, sin_half, k_re, k_im, batch, seq, cb=128):
    ncb = B_WIDTH // cb
    u_blk = COL_BU // cb

    def u_spec(part):
        return pl.BlockSpec((seq, cb), lambda b, j: (b, u_blk + part * ncb + j))

    def w_spec(rows, part):
        return pl.BlockSpec((rows, cb), lambda b, j: (0, part * ncb + j))

    return pl.pallas_call(
        _hyena_kernel,
        out_shape=jax.ShapeDtypeStruct((batch * seq, B_WIDTH), BF16),
        grid=(batch, ncb),
        in_specs=[
            u_spec(0), u_spec(1), u_spec(2),
            w_spec(3, 0), w_spec(3, 1), w_spec(3, 2),
            w_spec(1, 0), w_spec(1, 1), w_spec(1, 2),
            pl.BlockSpec((1, cb), lambda b, j: (0, j)),
            _const_spec((seq, seq)), _const_spec((seq, seq)),
            pl.BlockSpec((seq, cb), lambda b, j: (0, j)),
            pl.BlockSpec((seq, cb), lambda b, j: (0, j)),
        ],
        out_specs=pl.BlockSpec((seq, cb), lambda b, j: (b, j)),
        compiler_params=_cp("parallel", "arbitrary"),
        name="b_hyena",
    )(proj, proj, proj, conv_w, conv_w, conv_w, conv_b.reshape(1, -1), conv_b.reshape(1, -1), conv_b.reshape(1, -1),
      skip.reshape(1, -1), cos_half, sin_half, k_re, k_im)


def _rwkv_prep_kernel(cf_ref, lf_ref, pcf_ref, plf_ref, cb_ref, lb_ref, pcb_ref, plb_ref, mu_ref, mulo_ref, w0_ref, wup_ref,
                      a0_ref, aup_ref, kk_ref, ka_ref, ones_ref, o_ref):
    first = pl.program_id(1) == 0
    halo = pcf_ref.shape[0]
    tm = cf_ref.shape[0]

    def with_prev(x_ref, p_ref):
        x = x_ref[...]
        row = lax.broadcasted_iota(jnp.int32, x.shape, 0)
        p_row = jnp.where(first, 0.0, p_ref[halo - 1:halo, :])
        return x, jnp.where(row == 0, p_row, pltpu.roll(x, 1, 0))

    k_k = kk_ref[...]
    k_a = ka_ref[...]
    for d, (x_ref, l_ref, px_ref, pl_ref) in enumerate(((cf_ref, lf_ref, pcf_ref, plf_ref), (cb_ref, lb_ref, pcb_ref, plb_ref))):
        cur, cur_prev = with_prev(x_ref, px_ref)
        lo, lo_prev = with_prev(l_ref, pl_ref)
        f = cur + (cur_prev - cur) * mu_ref[d]
        f_lo = lo + (lo_prev - lo) * mulo_ref[d]
        r = f[:, :C_WIDTH]
        k = f[:, C_WIDTH:2 * C_WIDTH]
        v = f[:, 2 * C_WIDTH:]
        w_lo = f_lo[:, :C_LORA_PAD]
        a_lo = f_lo[:, C_LORA_PAD:]
        x = w0_ref[d] + jnp.dot(jnp.tanh(w_lo), wup_ref[d], preferred_element_type=F32, precision=HIGHEST)
        w = -(jnp.maximum(-x, 0.0) + jnp.log(1.0 + jnp.exp(-jnp.abs(x)))) - 0.5
        decay = jnp.exp(-jnp.exp(w))
        a = jax.nn.sigmoid(a0_ref[d] + jnp.dot(a_lo, aup_ref[d], preferred_element_type=F32, precision=HIGHEST))
        kk = k * k_k
        ss = jnp.dot(kk * kk, ones_ref[...], preferred_element_type=F32, precision=HIGHEST)
        kk = kk / jnp.maximum(jnp.sqrt(ss), 1e-12)
        for q, val in enumerate((r, decay, k * (1.0 + (a - 1.0) * k_a), v, -kk, kk * a)):
            o_ref[d, q] = val.T.reshape(C_HEAD_DIM, C_HEADS, tm)


def _rwkv_prep(proj, cf_rev, mu, mu_lo, w0, w_up, a0, a_up, k_k, k_a, batch, seq, tm=256):
    tm = min(tm, seq)
    nt = seq // tm
    halo = 8
    hb = tm // halo
    wide, narrow = 3 * C_WIDTH, 2 * C_LORA_PAD
    head_of = np.arange(C_WIDTH) % C_HEADS
    ones = jnp.asarray((head_of[:, None] == head_of[None, :]).astype(np.float32))

    def cur_map(blk):
        return lambda b, j: (b * nt + j, blk)

    def prev_map(blk):
        return lambda b, j: (jnp.maximum((b * nt + j) * hb - 1, 0), blk)

    def specs(wide_blk, narrow_blk):
        return [
            pl.BlockSpec((tm, wide), cur_map(wide_blk)),
            pl.BlockSpec((tm, narrow), cur_map(narrow_blk)),
            pl.BlockSpec((halo, wide), prev_map(wide_blk)),
            pl.BlockSpec((halo, narrow), prev_map(narrow_blk)),
        ]

    small = lambda a: pl.BlockSpec(a.shape, lambda b, j: (0,) * a.ndim)
    consts = (mu, mu_lo, w0, w_up, a0, a_up, k_k, k_a, ones)
    return pl.pallas_call(
        _rwkv_prep_kernel,
        out_shape=jax.ShapeDtypeStruct((2, 6, C_HEAD_DIM, batch, C_HEADS, seq), F32),
        grid=(batch, nt),
        in_specs=specs(COL_CF // wide, COL_CLO // narrow) + specs(0, wide // narrow) + [small(a) for a in consts],
        out_specs=pl.BlockSpec((2, 6, C_HEAD_DIM, None, C_HEADS, tm), lambda b, j: (0, 0, 0, b, 0, j)),
        compiler_params=_cp("parallel", "arbitrary"),
        name="c_prep",
    )(proj, proj, proj, proj, cf_rev, cf_rev, cf_rev, cf_rev, *consts)


RELAYOUT_T = 128


def _rwkv_relayout_kernel(p_ref, o_ref, *, tc):
    n = p_ref.shape[1]
    t_blk = p_ref.shape[-1]
    for k in range(n):
        rows = jnp.concatenate([p_ref[0, k].reshape(-1, t_blk), p_ref[1, k].reshape(-1, t_blk)], axis=0)
        cols = rows.T
        for c in range(t_blk // tc):
            o_ref[c, k] = cols[c * tc:(c + 1) * tc]


def _rwkv_relayout(p, tc):
    _, nq, n, batch, heads, seq = p.shape
    t_blk = min(RELAYOUT_T, seq)
    lanes = 2 * batch * heads
    per_blk = t_blk // tc
    return pl.pallas_call(
        functools.partial(_rwkv_relayout_kernel, tc=tc),
        out_shape=jax.ShapeDtypeStruct((seq // tc, nq, n, tc, lanes), F32),
        grid=(seq // t_blk, nq),
        in_specs=[pl.BlockSpec((2, None, n, batch, heads, t_blk), lambda c, q: (0, q, 0, 0, 0, c))],
        out_specs=pl.BlockSpec((per_blk, None, n, tc, lanes), lambda c, q: (c, q, 0, 0, 0)),
        compiler_params=_cp("parallel", "arbitrary"),
        name="c_relayout",
    )(p)


def _rwkv_scan_kernel(x_ref, rk_ref, lnw_ref, lnb_ref, out_ref, s_ref, o_ref, *, tc, k_chunk):
    n = s_ref.shape[0]
    lanes = s_ref.shape[2]

    @pl.when(pl.program_id(0) == 0)
    def _():
        s_ref[...] = jnp.zeros_like(s_ref)

    def row(q, k, t):
        return x_ref[pl.ds((q * n + k) * tc + t, 1), :]

    def step(t, carry):
        vv = x_ref[pl.ds(3 * n * tc + t, n, stride=tc), :]

        def sa_chunk(c, sa):
            base = pl.multiple_of(c * k_chunk, k_chunk)
            for j in range(k_chunk):
                sa = sa + s_ref[base + j] * row(4, base + j, t)
            return sa

        sa = lax.fori_loop(0, n // k_chunk, sa_chunk, jnp.zeros((n, lanes), F32))

        def update_chunk(c, o):
            base = pl.multiple_of(c * k_chunk, k_chunk)
            for j in range(k_chunk):
                k = base + j
                sk = s_ref[k] * row(1, k, t) + sa * row(5, k, t) + vv * row(2, k, t)
                s_ref[k] = sk
                o = o + sk * row(0, k, t)
            return o

        o_ref[pl.ds(t, n, stride=tc), :] = lax.fori_loop(0, n // k_chunk, update_chunk, jnp.zeros((n, lanes), F32))
        return carry

    lax.fori_loop(0, tc, step, 0)

    quantity = lambda q: x_ref[q * n * tc:(q + 1) * n * tc, :].reshape(n, tc, lanes)
    o = o_ref[...].reshape(n, tc, lanes)
    mean = jnp.mean(o, axis=0, keepdims=True)
    var = jnp.mean(jnp.square(o - mean), axis=0, keepdims=True)
    o = (o - mean) * lax.rsqrt(var + C_GN_EPS) * lnw_ref[...] + lnb_ref[...]
    bonus = jnp.sum(quantity(0) * quantity(2) * rk_ref[...], axis=0, keepdims=True)
    out_ref[...] = o + bonus * quantity(3)


def _rwkv_scan(xs, rk, lnw, lnb, k_chunk=32):
    nc, nq, n, tc, lanes = xs.shape
    tile = lambda: pl.BlockSpec((n, 1, lanes), lambda i: (0, 0, 0))
    return pl.pallas_call(
        functools.partial(_rwkv_scan_kernel, tc=tc, k_chunk=min(k_chunk, n)),
        out_shape=jax.ShapeDtypeStruct((n, nc * tc, lanes), F32),
        grid=(nc,),
        in_specs=[pl.BlockSpec((nq * n * tc, lanes), lambda i: (i, 0)), tile(), tile(), tile()],
        out_specs=pl.BlockSpec((n, tc, lanes), lambda i: (0, i, 0)),
        scratch_shapes=[pltpu.VMEM((n, n, lanes), F32), pltpu.VMEM((n * tc, lanes), F32)],
        compiler_params=_cp("arbitrary"),
        name="c_scan",
    )(xs.reshape(nc * nq * n * tc, lanes), rk, lnw, lnb)


def _rwkv_out_kernel(o_ref, glo_ref, gup_ref, y_ref):
    g = jnp.dot(jax.nn.sigmoid(glo_ref[...]).astype(BF16), gup_ref[...], preferred_element_type=F32)
    y_ref[...] = ((o_ref[0] + o_ref[1]) * g).astype(y_ref.dtype)


def _rwkv_out(o_dirs, proj, g_up, tm=512):
    m = proj.shape[0]
    tm = min(tm, m)
    g_blk = COL_CG // C_GATE_LORA
    return pl.pallas_call(
        _rwkv_out_kernel,
        out_shape=jax.ShapeDtypeStruct((m, C_WIDTH), BF16),
        grid=(m // tm,),
        in_specs=[
            pl.BlockSpec((2, tm, C_WIDTH), lambda i: (0, i, 0)),
            pl.BlockSpec((tm, C_GATE_LORA), lambda i: (i, g_blk)),
            pl.BlockSpec((C_GATE_LORA, C_WIDTH), lambda i: (0, 0)),
        ],
        out_specs=pl.BlockSpec((tm, C_WIDTH), lambda i: (i, 0)),
        compiler_params=_cp("parallel"),
        name="c_out",
    )(o_dirs, proj, g_up)


SCAN_TC = 16


def _heads_minor(p):
    return p.reshape(p.shape[:-1] + (C_HEADS, C_HEAD_DIM)).swapaxes(-1, -2).reshape(p.shape)


def _rwkv(proj, mu, w0, w_up, a0, a_up, g_up, k_k, k_a, r_k, ln_w, ln_b, batch, seq):
    pad_lo = C_LORA_PAD - C_LORA
    split = 3 * C_WIDTH
    mu_wide = jnp.concatenate([_heads_minor(mu[:, i * C_WIDTH:(i + 1) * C_WIDTH]) for i in range(3)], axis=-1)[:, None, :]
    mu_lo = jnp.concatenate([
        jnp.pad(mu[:, split:split + C_LORA], ((0, 0), (0, pad_lo))),
        jnp.pad(mu[:, split + C_LORA:], ((0, 0), (0, pad_lo))),
    ], axis=-1)[:, None, :]
    w_up_p = jnp.pad(_heads_minor(w_up), ((0, 0), (0, pad_lo), (0, 0)))
    a_up_p = jnp.pad(_heads_minor(a_up), ((0, 0), (0, pad_lo), (0, 0)))
    cf_rev = proj.reshape(batch, seq, D_IN_PAD)[:, ::-1, COL_CF:COL_CG].reshape(batch * seq, COL_CG - COL_CF)
    feats = _rwkv_prep(proj, cf_rev, mu_wide, mu_lo, _heads_minor(w0)[:, None, :], w_up_p, _heads_minor(a0)[:, None, :], a_up_p,
                       _heads_minor(k_k).reshape(1, C_WIDTH), _heads_minor(k_a).reshape(1, C_WIDTH), batch, seq)
    xs = _rwkv_relayout(feats, min(SCAN_TC, seq))
    per_lane = lambda p: jnp.tile(p.reshape(C_HEADS, C_HEAD_DIM).T, (1, 2 * batch))[:, None, :]
    o = _rwkv_scan(xs, per_lane(r_k), per_lane(ln_w), per_lane(ln_b))
    o = o.reshape(C_HEAD_DIM, seq, 2, batch, C_HEADS).transpose(2, 3, 1, 4, 0)
    o = jnp.stack([o[0], o[1, :, ::-1]]).reshape(2, batch * seq, C_WIDTH)
    return _rwkv_out(o, proj, g_up.astype(BF16))


def _merge_kernel(h_ref, wg_ref, wbr_ref, oa_ref, ob_ref, oc_ref, od_ref, o_ref, wgb_ref, wbrb_ref):
    _cast_weights_once((wg_ref, wbr_ref), (wgb_ref, wbrb_ref))
    h = h_ref[...]
    acc = None
    row = 0
    for i, b_ref in enumerate((oa_ref, ob_ref, oc_ref, od_ref)):
        width = b_ref.shape[1]
        gate = jax.nn.sigmoid(jnp.dot(h, wgb_ref[i], preferred_element_type=F32))
        term = gate * jnp.dot(b_ref[...], wbrb_ref[row:row + width, :], preferred_element_type=F32)
        acc = term if acc is None else acc + term
        row += width
    o_ref[...] = acc.astype(o_ref.dtype)


def _merge(h, w_gate, w_branch, branches, tm=512, tn=256):
    m = h.shape[0]
    tm = min(tm, m)
    d_mix = w_branch.shape[0]
    row = lambda a: pl.BlockSpec((tm, a.shape[1]), lambda j, i: (i, 0))
    return pl.pallas_call(
        _merge_kernel,
        out_shape=jax.ShapeDtypeStruct((m, D_MODEL), BF16),
        grid=(D_MODEL // tn, m // tm),
        in_specs=[
            row(h),
            pl.BlockSpec((4, D_MODEL, tn), lambda j, i: (0, 0, j)),
            pl.BlockSpec((d_mix, tn), lambda j, i: (0, j)),
        ] + [row(b) for b in branches],
        out_specs=pl.BlockSpec((tm, tn), lambda j, i: (i, j)),
        scratch_shapes=[pltpu.VMEM((4, D_MODEL, tn), BF16), pltpu.VMEM((d_mix, tn), BF16)],
        compiler_params=_cp("parallel", "arbitrary"),
        name="gated_merge",
    )(h, w_gate, w_branch, *branches)


def _ffn_up_kernel(h_ref, wg_ref, wu_ref, o_ref, wgb_ref, wub_ref):
    _cast_weights_once((wg_ref, wu_ref), (wgb_ref, wub_ref))
    h = h_ref[...]
    g = jnp.dot(h, wgb_ref[...], preferred_element_type=F32)
    u = jnp.dot(h, wub_ref[...], preferred_element_type=F32)
    o_ref[...] = (g * jax.nn.sigmoid(g) * u).astype(o_ref.dtype)


def _ffn_up(h, w_gate, w_up, tm=1024, tn=512):
    m, k = h.shape
    n = w_gate.shape[1]
    tm = min(tm, m)
    return pl.pallas_call(
        _ffn_up_kernel,
        out_shape=jax.ShapeDtypeStruct((m, n), BF16),
        grid=(n // tn, m // tm),
        in_specs=[
            pl.BlockSpec((tm, k), lambda j, i: (i, 0)),
            pl.BlockSpec((k, tn), lambda j, i: (0, j)),
            pl.BlockSpec((k, tn), lambda j, i: (0, j)),
        ],
        out_specs=pl.BlockSpec((tm, tn), lambda j, i: (i, j)),
        scratch_shapes=[pltpu.VMEM((k, tn), BF16), pltpu.VMEM((k, tn), BF16)],
        compiler_params=_cp("parallel", "arbitrary"),
        name="ffn_up",
    )(h, w_gate, w_up)


def _pad_w_in(w):
    pad = ((0, 0), (0, C_LORA_PAD - C_LORA))
    lo = COL_CLO
    cf = [_heads_minor(w[:, COL_CF + i * C_WIDTH:COL_CF + (i + 1) * C_WIDTH]) for i in range(3)]
    return jnp.concatenate([
        w[:, :COL_CF],
        *cf,
        jnp.pad(w[:, lo:lo + C_LORA], pad),
        jnp.pad(w[:, lo + C_LORA:lo + 2 * C_LORA], pad),
        w[:, lo + 2 * C_LORA:],
    ], axis=1)


def kernel(x, norm_mix, w_in, a_q_norm, a_k_norm, b_conv_w, b_conv_b, b_filt_w1, b_filt_b1, b_filt_w2, b_filt_b2, b_filt_w3, b_filt_b3, b_filt_w4, b_filt_freq, b_skip, c_mu, c_w0, c_w_up, c_a0, c_a_up, c_g_up, c_k_k, c_k_a, c_r_k, c_ln_w, c_ln_b, d_lq1, d_lk1, d_lq2, d_lk2, d_subln, w_gate, w_branch, w_out, norm_ffn, w_ff_gate, w_ff_up, w_ff_down, norm_final):
    batch, seq, _ = x.shape
    m = batch * seq
    cos, sin = _rope_tables(seq)
    cos_half, sin_half, cos_int, sin_int = _dft_tables(seq)
    filt_feats = _filter_features(seq)
    x = x.reshape(m, D_MODEL)
    for l in range(DEPTH):
        h = _rmsnorm(x, norm_mix[l], BF16)
        proj = _matmul(h, _pad_w_in(w_in[l]), F32, 1024, 512, name="in_proj")

        gains = jnp.concatenate([
            jnp.broadcast_to(a_q_norm[l], (A_HEADS, A_HEAD_DIM)),
            jnp.broadcast_to(a_k_norm[l], (A_KV_HEADS, A_HEAD_DIM)),
        ])[:, None, :]
        o_a = _attn_a(_qk_prep(proj, gains, cos, sin, seq), proj, batch, seq)

        k_re, k_im = _hyena_filter_spectrum(seq, filt_feats, cos_int, sin_int, b_filt_w1[l], b_filt_b1[l], b_filt_w2[l],
                                            b_filt_b2[l], b_filt_w3[l], b_filt_b3[l], b_filt_w4[l], b_filt_freq[l])
        o_b = _hyena(proj, b_conv_w[l], b_conv_b[l], b_skip[l], cos_half, sin_half, k_re, k_im, batch, seq)

        o_c = _rwkv(proj, c_mu[l], c_w0[l], c_w_up[l], c_a0[l], c_a_up[l], c_g_up[l], c_k_k[l], c_k_a[l], c_r_k[l],
                    c_ln_w[l], c_ln_b[l], batch, seq)

        lam_init = 0.8 - 0.6 * math.exp(-0.3 * l)
        lam_vecs = jnp.stack([d_lq1[l], d_lk1[l], d_lq2[l], d_lk2[l]])
        o_d = _attn_d(proj, lam_vecs, d_subln[l], lam_init, batch, seq)

        merged = _merge(h, w_gate[l], w_branch[l], (o_a, o_b, o_c, o_d))
        x = _matmul(merged, w_out[l], F32, 1024, 512, residual=x, name="out_proj")

        h2 = _rmsnorm(x, norm_ffn[l], BF16)
        mid = _ffn_up(h2, w_ff_gate[l], w_ff_up[l])
        x = _matmul(mid, w_ff_down[l], F32, 512, 512, residual=x, name="ffn_down")
    return _rmsnorm(x, norm_final, F32).reshape(batch, seq, D_MODEL)
```

```python
import functools
import math

import jax
import jax.numpy as jnp
import numpy as np
from jax import lax
from jax.experimental import pallas as pl
from jax.experimental.pallas import tpu as pltpu

D_MODEL = 2048
DEPTH = 2
GRID_W = 64
NORM_EPS = 1e-6

A_HEADS = 8
A_KV_HEADS = 2
A_HEAD_DIM = 128
A_WIDTH = A_HEADS * A_HEAD_DIM
ROPE_THETA = 10000.0

B_WIDTH = 512
B_EMB_DIM = 33
B_FILTER_HIDDEN = 64
B_DECAY_TARGET = 1e-2
B_FAST_DECAY_PCT = 0.3
B_SLOW_DECAY_PCT = 1.5

C_HEADS = 8
C_HEAD_DIM = 64
C_WIDTH = C_HEADS * C_HEAD_DIM
C_LORA = 96
C_LORA_PAD = 128
C_GATE_LORA = 256
C_GN_EPS = 64e-5

D_HEADS = 4
D_HEAD_DIM = 64
D_V_DIM = 2 * D_HEAD_DIM
D_WIDTH = D_HEADS * D_V_DIM

FFN_HIDDEN = -(-8 * D_MODEL // (3 * 256)) * 256

COL_AQ = 0
COL_AK = COL_AQ + A_WIDTH
COL_AV = COL_AK + A_KV_HEADS * A_HEAD_DIM
COL_BU = COL_AV + A_KV_HEADS * A_HEAD_DIM
COL_CF = COL_BU + 3 * B_WIDTH
COL_CLO = COL_CF + 3 * C_WIDTH
COL_CG = COL_CLO + 2 * C_LORA_PAD
COL_DQ = COL_CG + C_GATE_LORA
COL_DK = COL_DQ + 2 * D_HEADS * D_HEAD_DIM
COL_DV = COL_DK + 2 * D_HEADS * D_HEAD_DIM
D_IN_PAD = COL_DV + D_WIDTH

VMEM_LIMIT_V7X = 56 * 1024 * 1024
F32 = jnp.float32
BF16 = jnp.bfloat16
HIGHEST = lax.Precision.HIGHEST
NT_DIMS = (((1,), (1,)), ((), ()))


def _cp(*sem):
    return pltpu.CompilerParams(dimension_semantics=sem, vmem_limit_bytes=VMEM_LIMIT_V7X)


def _const_spec(shape):
    return pl.BlockSpec(shape, lambda *_: (0,) * len(shape), pipeline_mode=pl.Buffered(1))


SINGLE_BUFFER_BYTES = 8 * 1024 * 1024


def _weight_spec(shape, index_map):
    if 4 * math.prod(shape) > SINGLE_BUFFER_BYTES:
        return pl.BlockSpec(shape, index_map, pipeline_mode=pl.Buffered(1))
    return pl.BlockSpec(shape, index_map)


def _rmsnorm_kernel(x_ref, g_ref, o_ref):
    x = x_ref[...]
    ms = jnp.mean(x * x, axis=-1, keepdims=True)
    o_ref[...] = (x * lax.rsqrt(ms + NORM_EPS) * g_ref[...]).astype(o_ref.dtype)


def _rmsnorm(x, g, out_dtype, tm=512):
    m, d = x.shape
    tm = min(tm, m)
    return pl.pallas_call(
        _rmsnorm_kernel,
        out_shape=jax.ShapeDtypeStruct((m, d), out_dtype),
        grid=(m // tm,),
        in_specs=[pl.BlockSpec((tm, d), lambda i: (i, 0)), pl.BlockSpec((1, d), lambda i: (0, 0))],
        out_specs=pl.BlockSpec((tm, d), lambda i: (i, 0)),
        compiler_params=_cp("parallel"),
        name="rmsnorm",
    )(x, g.reshape(1, d))


def _mm_f32_kernel(a_ref, b_ref, o_ref):
    o_ref[...] = jnp.dot(a_ref[...], b_ref[...], preferred_element_type=F32, precision=HIGHEST)


def _matmul_f32(a, b, tm, tn, name):
    m, k = a.shape
    n = b.shape[1]
    tm, tn = min(tm, m), min(tn, n)
    return pl.pallas_call(
        _mm_f32_kernel,
        out_shape=jax.ShapeDtypeStruct((m, n), F32),
        grid=(m // tm, n // tn),
        in_specs=[pl.BlockSpec((tm, k), lambda i, j: (i, 0)), pl.BlockSpec((k, tn), lambda i, j: (0, j))],
        out_specs=pl.BlockSpec((tm, tn), lambda i, j: (i, j)),
        compiler_params=_cp("parallel", "arbitrary"),
        name=name,
    )(a, b)


def _cast_weights_once(w_refs, wb_refs):
    @pl.when(pl.program_id(1) == 0)
    def _():
        for w_ref, wb_ref in zip(w_refs, wb_refs):
            wb_ref[...] = w_ref[...].astype(BF16)


def _mm_kernel(a_ref, w_ref, o_ref, wb_ref):
    _cast_weights_once((w_ref,), (wb_ref,))
    o_ref[...] = jnp.dot(a_ref[...], wb_ref[...], preferred_element_type=F32).astype(o_ref.dtype)


def _mm_res_kernel(a_ref, w_ref, r_ref, o_ref, wb_ref):
    _cast_weights_once((w_ref,), (wb_ref,))
    o_ref[...] = r_ref[...] + jnp.dot(a_ref[...], wb_ref[...], preferred_element_type=F32)


def _matmul(a, w, out_dtype, tm, tn, residual=None, name="matmul"):
    m, k = a.shape
    n = w.shape[1]
    tm, tn = min(tm, m), min(tn, n)
    in_specs = [pl.BlockSpec((tm, k), lambda j, i: (i, 0)), _weight_spec((k, tn), lambda j, i: (0, j))]
    args = [a, w]
    body = _mm_kernel
    if residual is not None:
        body = _mm_res_kernel
        in_specs.append(pl.BlockSpec((tm, tn), lambda j, i: (i, j)))
        args.append(residual)
    return pl.pallas_call(
        body,
        out_shape=jax.ShapeDtypeStruct((m, n), out_dtype),
        grid=(n // tn, m // tm),
        in_specs=in_specs,
        out_specs=pl.BlockSpec((tm, tn), lambda j, i: (i, j)),
        scratch_shapes=[pltpu.VMEM((k, tn), BF16)],
        compiler_params=_cp("parallel", "arbitrary"),
        name=name,
    )(*args)


def _rope_tables(seq):
    rows = seq // GRID_W
    row_idx = jnp.repeat(jnp.arange(rows, dtype=F32), GRID_W)
    col_idx = jnp.tile(jnp.arange(GRID_W, dtype=F32), rows)
    axis_dim = A_HEAD_DIM // 2
    inv_freq = ROPE_THETA ** (-jnp.arange(0, axis_dim, 2, dtype=F32) / axis_dim)
    ang_r = row_idx[:, None] * inv_freq[None, :]
    ang_c = col_idx[:, None] * inv_freq[None, :]
    ang = jnp.concatenate([ang_r, ang_r, ang_c, ang_c], axis=-1)
    return jnp.cos(ang), jnp.sin(ang)


def _qk_prep_kernel(x_ref, g_ref, cos_ref, sin_ref, o_ref):
    cos = cos_ref[...]
    sin = sin_ref[...]
    lane = lax.broadcasted_iota(jnp.int32, cos.shape, 1)
    quarter = A_HEAD_DIM // 4
    first = (lane % (2 * quarter)) < quarter
    for h in range(A_HEADS + A_KV_HEADS):
        cols = slice(h * A_HEAD_DIM, (h + 1) * A_HEAD_DIM)
        x = x_ref[:, cols]
        xn = x * lax.rsqrt(jnp.mean(x * x, axis=-1, keepdims=True) + NORM_EPS) * g_ref[h]
        rot = jnp.where(first, -pltpu.roll(xn, A_HEAD_DIM - quarter, 1), pltpu.roll(xn, quarter, 1))
        y = xn * cos + rot * sin
        if h < A_HEADS:
            y = y * A_HEAD_DIM**-0.5
        o_ref[:, cols] = y.astype(o_ref.dtype)


def _qk_prep(proj, gains, cos, sin, seq, tm=256):
    m = proj.shape[0]
    tm = min(tm, seq)
    width = (A_HEADS + A_KV_HEADS) * A_HEAD_DIM
    nt = seq // tm
    return pl.pallas_call(
        _qk_prep_kernel,
        out_shape=jax.ShapeDtypeStruct((m, width), BF16),
        grid=(m // tm,),
        in_specs=[
            pl.BlockSpec((tm, width), lambda i: (i, 0)),
            pl.BlockSpec(gains.shape, lambda i: (0, 0, 0)),
            pl.BlockSpec((tm, A_HEAD_DIM), lambda i: (i % nt, 0)),
            pl.BlockSpec((tm, A_HEAD_DIM), lambda i: (i % nt, 0)),
        ],
        out_specs=pl.BlockSpec((tm, width), lambda i: (i, 0)),
        compiler_params=_cp("parallel"),
        name="a_qk_prep",
    )(proj, gains, cos, sin)


def _attn_a_kernel(q_ref, k_ref, v_ref, o_ref):
    s = lax.dot_general(q_ref[...], k_ref[...], NT_DIMS, preferred_element_type=F32)
    p = jnp.exp(s - jnp.max(s, axis=-1, keepdims=True))
    l = jnp.sum(p, axis=-1, keepdims=True)
    o = jnp.dot(p.astype(BF16), v_ref[...].astype(BF16), preferred_element_type=F32)
    o_ref[...] = (o / l).astype(o_ref.dtype)


def _attn_a(qk, proj, batch, seq, tq=512):
    tq = min(tq, seq)
    nq = seq // tq
    group = A_HEADS // A_KV_HEADS
    k_blk = COL_AK // A_HEAD_DIM
    v_blk = COL_AV // A_HEAD_DIM
    return pl.pallas_call(
        _attn_a_kernel,
        out_shape=jax.ShapeDtypeStruct((batch * seq, A_WIDTH), BF16),
        grid=(batch, A_HEADS, nq),
        in_specs=[
            pl.BlockSpec((tq, A_HEAD_DIM), lambda b, h, i: (b * nq + i, h)),
            pl.BlockSpec((seq, A_HEAD_DIM), lambda b, h, i: (b, k_blk + h // group)),
            pl.BlockSpec((seq, A_HEAD_DIM), lambda b, h, i: (b, v_blk + h // group)),
        ],
        out_specs=pl.BlockSpec((tq, A_HEAD_DIM), lambda b, h, i: (b * nq + i, h)),
        compiler_params=_cp("parallel", "arbitrary", "arbitrary"),
        name="a_attention",
    )(qk, qk, proj)


def _attn_d_kernel(q_ref, k_ref, v_ref, slope_ref, lam_ref, g_ref, o_ref, *, tq, lam_init):
    q = q_ref[...]
    k = k_ref[...].astype(BF16)
    seq = k.shape[0]
    lane = lax.broadcasted_iota(jnp.int32, q.shape, 1)
    q0 = jnp.where(lane < D_HEAD_DIM, q, 0.0).astype(BF16)
    q1 = jnp.where(lane >= D_HEAD_DIM, q, 0.0).astype(BF16)
    q_pos = pl.program_id(2) * tq + lax.broadcasted_iota(jnp.int32, (tq, seq), 0)
    k_pos = lax.broadcasted_iota(jnp.int32, (tq, seq), 1)
    bias = slope_ref[0][:, 0:1] * jnp.abs(q_pos - k_pos).astype(F32)
    scale = D_HEAD_DIM**-0.5

    def softmax_parts(qc):
        s = lax.dot_general(qc, k, NT_DIMS, preferred_element_type=F32) * scale - bias
        p = jnp.exp(s - jnp.max(s, axis=-1, keepdims=True))
        return p, jnp.sum(p, axis=-1, keepdims=True)

    p0, l0 = softmax_parts(q0)
    p1, l1 = softmax_parts(q1)
    lam_v = lam_ref[...]
    lam = (
        jnp.exp(jnp.sum(lam_v[0:1] * lam_v[1:2], axis=-1, keepdims=True))
        - jnp.exp(jnp.sum(lam_v[2:3] * lam_v[3:4], axis=-1, keepdims=True))
        + lam_init
    )
    attn = p0 * (1.0 / l0) - p1 * (lam / l1)
    o = jnp.dot(attn.astype(BF16), v_ref[...].astype(BF16), preferred_element_type=F32)
    o = o * lax.rsqrt(jnp.mean(o * o, axis=-1, keepdims=True) + NORM_EPS) * g_ref[...]
    o_ref[...] = (o * (1.0 - lam_init)).astype(o_ref.dtype)


def _attn_d(proj, lam_vecs, subln, lam_init, batch, seq, tq=256):
    tq = min(tq, seq)
    nq = seq // tq
    slopes = 2.0 ** (-8.0 * np.arange(1, D_HEADS + 1, dtype=np.float32) / D_HEADS)
    slopes = jnp.asarray(np.broadcast_to(slopes[:, None, None], (D_HEADS, 1, 128)).astype(np.float32))
    q_blk, k_blk, v_blk = COL_DQ // D_V_DIM, COL_DK // D_V_DIM, COL_DV // D_V_DIM
    return pl.pallas_call(
        functools.partial(_attn_d_kernel, tq=tq, lam_init=lam_init),
        out_shape=jax.ShapeDtypeStruct((batch * seq, D_WIDTH), BF16),
        grid=(batch, D_HEADS, nq),
        in_specs=[
            pl.BlockSpec((tq, D_V_DIM), lambda b, h, i: (b * nq + i, q_blk + h)),
            pl.BlockSpec((seq, D_V_DIM), lambda b, h, i: (b, k_blk + h)),
            pl.BlockSpec((seq, D_V_DIM), lambda b, h, i: (b, v_blk + h)),
            pl.BlockSpec((1, 1, 128), lambda b, h, i: (h, 0, 0)),
            pl.BlockSpec((4, D_HEAD_DIM), lambda b, h, i: (0, 0)),
            pl.BlockSpec((1, D_V_DIM), lambda b, h, i: (0, 0)),
        ],
        out_specs=pl.BlockSpec((tq, D_V_DIM), lambda b, h, i: (b * nq + i, h)),
        compiler_params=_cp("parallel", "arbitrary", "arbitrary"),
        name="d_attention",
    )(proj, proj, proj, slopes, lam_vecs, subln.reshape(1, D_V_DIM))


def _dft_tables(seq):
    n = 2 * seq
    f = jnp.arange(seq, dtype=jnp.int32)
    odd = 2 * f + 1
    m_half = (odd[:, None] * odd[None, :]) % (4 * n)
    ang_half = m_half.astype(F32) * (2.0 * math.pi / (4 * n))
    m_int = (odd[:, None] * f[None, :]) % (2 * n)
    ang_int = m_int.astype(F32) * (2.0 * math.pi / (2 * n))
    return jnp.cos(ang_half).astype(BF16), jnp.sin(ang_half).astype(BF16), jnp.cos(ang_int), jnp.sin(ang_int)


def _filter_features(seq):
    t = jnp.linspace(0.0, 1.0, seq, dtype=F32)[:, None]
    n_bands = (B_EMB_DIM - 1) // 2
    bands = jnp.linspace(1e-4, n_bands - 1, n_bands, dtype=F32)[None, :]
    ang = (2.0 * math.pi / seq) * jnp.arange(seq, dtype=F32)[:, None] * bands
    z = jnp.concatenate([t, jnp.cos(ang), -jnp.sin(ang)], axis=-1)
    z = jnp.pad(z, ((0, 0), (0, B_FILTER_HIDDEN - B_EMB_DIM)))
    max_decay = math.log(B_DECAY_TARGET) / B_FAST_DECAY_PCT
    min_decay = math.log(B_DECAY_TARGET) / B_SLOW_DECAY_PCT
    deltas = jnp.abs(jnp.linspace(min_decay, max_decay, B_WIDTH, dtype=F32))[None, :]
    return z, t, deltas


def _filter_kernel(z_ref, t_ref, dl_ref, w1_ref, b1_ref, w2_ref, b2_ref, w3_ref, b3_ref, w4_ref, fr_ref, hs_ref, hd_ref):
    fr = fr_ref[...]
    hid = jnp.sin(fr * (jnp.dot(z_ref[...], w1_ref[...], preferred_element_type=F32, precision=HIGHEST) + b1_ref[...]))
    hid = jnp.sin(fr * (jnp.dot(hid, w2_ref[...], preferred_element_type=F32, precision=HIGHEST) + b2_ref[...]))
    hid = jnp.sin(fr * (jnp.dot(hid, w3_ref[...], preferred_element_type=F32, precision=HIGHEST) + b3_ref[...]))
    h = jnp.dot(hid, w4_ref[...], preferred_element_type=F32, precision=HIGHEST)
    window = jnp.exp(-t_ref[...] * dl_ref[...])
    h_fwd = h[:, :B_WIDTH] * window
    h_bwd = h[:, B_WIDTH:] * window
    row = lax.broadcasted_iota(jnp.int32, h_bwd.shape, 0)
    h_bwd = jnp.where(row == 0, 0.0, h_bwd)
    norm = jnp.sum(jnp.abs(h_fwd), axis=0, keepdims=True) + jnp.sum(jnp.abs(h_bwd), axis=0, keepdims=True)
    seq = h.shape[0]
    inv_n = 1.0 / seq
    hs_ref[...] = (h_fwd + h_bwd) / norm * inv_n
    hd_ref[...] = (h_bwd - h_fwd) / norm * inv_n


def _hyena_filter_spectrum(seq, feats, cos_int, sin_int, w1, b1, w2, b2, w3, b3, w4, freq):
    z, t, deltas = feats
    w1p = jnp.pad(w1, ((0, B_FILTER_HIDDEN - B_EMB_DIM), (0, 0)))
    row = lambda v: v.reshape(1, -1)
    args = (z, t, deltas, w1p, row(b1), w2, row(b2), w3, row(b3), w4, row(freq))
    h_sum, h_diff = pl.pallas_call(
        _filter_kernel,
        out_shape=[jax.ShapeDtypeStruct((seq, B_WIDTH), F32)] * 2,
        in_specs=[pl.BlockSpec(a.shape, lambda: (0, 0)) for a in args],
        out_specs=[pl.BlockSpec((seq, B_WIDTH), lambda: (0, 0))] * 2,
        compiler_params=pltpu.CompilerParams(vmem_limit_bytes=VMEM_LIMIT_V7X),
        name="b_filter",
    )(*args)
    k_re = _matmul_f32(cos_int, h_sum, 256, B_WIDTH, name="b_filter_dft_re")
    k_im = _matmul_f32(sin_int, h_diff, 256, B_WIDTH, name="b_filter_dft_im")
    return k_re, k_im


def _hyena_kernel(v_ref, x1_ref, x0_ref, wv_ref, w1_ref, w0_ref, bv_ref, b1_ref, b0_ref, skip_ref, c_ref, s_ref, kre_ref, kim_ref, o_ref):
    seq = v_ref.shape[0]
    row = lax.broadcasted_iota(jnp.int32, v_ref.shape, 0)

    def conv3(u_ref, w_ref, b_ref):
        u = u_ref[...]
        w = w_ref[...]
        u_prev = jnp.where(row == 0, 0.0, pltpu.roll(u, 1, 0))
        u_next = jnp.where(row == seq - 1, 0.0, pltpu.roll(u, seq - 1, 0))
        return w[0:1] * u_prev + w[1:2] * u + w[2:3] * u_next + b_ref[...]

    z = conv3(v_ref, wv_ref, bv_ref) * conv3(x1_ref, w1_ref, b1_ref)
    zb = z.astype(BF16)
    cz = jnp.dot(c_ref[...], zb, preferred_element_type=F32)
    sz = jnp.dot(s_ref[...], zb, preferred_element_type=F32)
    k_re = kre_ref[...]
    k_im = kim_ref[...]
    y_re = (cz * k_re + sz * k_im).astype(BF16)
    y_im = (cz * k_im - sz * k_re).astype(BF16)
    y = jnp.dot(c_ref[...], y_re, preferred_element_type=F32) - jnp.dot(s_ref[...], y_im, preferred_element_type=F32)
    y = y + z * skip_ref[...]
    o_ref[...] = (y * conv3(x0_ref, w0_ref, b0_ref)).astype(o_ref.dtype)


def _hyena(proj, conv_w, conv_b, skip, cos_half, sin_half, k_re, k_im, batch, seq, cb=128):
    ncb = B_WIDTH // cb
    u_blk = COL_BU // cb

    def u_spec(part):
        return pl.BlockSpec((seq, cb), lambda b, j: (b, u_blk + part * ncb + j))

    def w_spec(rows, part):
        return pl.BlockSpec((rows, cb), lambda b, j: (0, part * ncb + j))

    return pl.pallas_call(
        _hyena_kernel,
        out_shape=jax.ShapeDtypeStruct((batch * seq, B_WIDTH), BF16),
        grid=(batch, ncb),
        in_specs=[
            u_spec(0), u_spec(1), u_spec(2),
            w_spec(3, 0), w_spec(3, 1), w_spec(3, 2),
            w_spec(1, 0), w_spec(1, 1), w_spec(1, 2),
            pl.BlockSpec((1, cb), lambda b, j: (0, j)),
            _const_spec((seq, seq)), _const_spec((seq, seq)),
            pl.BlockSpec((seq, cb), lambda b, j: (0, j)),
            pl.BlockSpec((seq, cb), lambda b, j: (0, j)),
        ],
        out_specs=pl.BlockSpec((seq, cb), lambda b, j: (b, j)),
        compiler_params=_cp("parallel", "arbitrary"),
        name="b_hyena",
    )(proj, proj, proj, conv_w, conv_w, conv_w, conv_b.reshape(1, -1), conv_b.reshape(1, -1), conv_b.reshape(1, -1),
      skip.reshape(1, -1), cos_half, sin_half, k_re, k_im)


def _rwkv_prep_kernel(cur_ref, lo_ref, pcur_ref, plo_ref, ncur_ref, nlo_ref, mu_ref, mulo_ref, w0_ref, wup_ref, a0_ref, aup_ref,
                      kk_ref, ka_ref, ones_ref, o_ref):
    first = pl.program_id(1) == 0
    last = pl.program_id(1) == pl.num_programs(1) - 1
    halo = pcur_ref.shape[0]
    tm = cur_ref.shape[0]

    def neighbours(x_ref, p_ref, n_ref):
        x = x_ref[...]
        row = lax.broadcasted_iota(jnp.int32, x.shape, 0)
        p_row = jnp.where(first, 0.0, p_ref[halo - 1:halo, :])
        n_row = jnp.where(last, 0.0, n_ref[0:1, :])
        prev = jnp.where(row == 0, p_row, pltpu.roll(x, 1, 0))
        nxt = jnp.where(row == tm - 1, n_row, pltpu.roll(x, tm - 1, 0))
        return x, (prev, nxt)

    cur, cur_sh = neighbours(cur_ref, pcur_ref, ncur_ref)
    lo, lo_sh = neighbours(lo_ref, plo_ref, nlo_ref)
    k_k = kk_ref[...]
    k_a = ka_ref[...]
    for d in range(2):
        f = cur + (cur_sh[d] - cur) * mu_ref[d]
        f_lo = lo + (lo_sh[d] - lo) * mulo_ref[d]
        r = f[:, :C_WIDTH]
        k = f[:, C_WIDTH:2 * C_WIDTH]
        v = f[:, 2 * C_WIDTH:]
        w_lo = f_lo[:, :C_LORA_PAD]
        a_lo = f_lo[:, C_LORA_PAD:]
        x = w0_ref[d] + jnp.dot(jnp.tanh(w_lo), wup_ref[d], preferred_element_type=F32, precision=HIGHEST)
        w = -(jnp.maximum(-x, 0.0) + jnp.log(1.0 + jnp.exp(-jnp.abs(x)))) - 0.5
        decay = jnp.exp(-jnp.exp(w))
        a = jax.nn.sigmoid(a0_ref[d] + jnp.dot(a_lo, aup_ref[d], preferred_element_type=F32, precision=HIGHEST))
        kk = k * k_k
        ss = jnp.dot(kk * kk, ones_ref[...], preferred_element_type=F32, precision=HIGHEST)
        kk = kk / jnp.maximum(jnp.sqrt(ss), 1e-12)
        for q, val in enumerate((r, decay, k * (1.0 + (a - 1.0) * k_a), v, -kk, kk * a)):
            o_ref[d, q] = val.T.reshape(C_HEAD_DIM, C_HEADS, tm)


def _rwkv_prep(proj, mu, mu_lo, w0, w_up, a0, a_up, k_k, k_a, batch, seq, tm=256):
    tm = min(tm, seq)
    nt = seq // tm
    halo = 8
    hb = tm // halo
    n_hblk = batch * seq // halo
    wide, narrow = 3 * C_WIDTH, 2 * C_LORA_PAD
    cur_blk, lo_blk = COL_CF // wide, COL_CLO // narrow
    head_of = np.arange(C_WIDTH) % C_HEADS
    ones = jnp.asarray((head_of[:, None] == head_of[None, :]).astype(np.float32))

    def cur_map(blk):
        return lambda b, j: (b * nt + j, blk)

    def prev_map(blk):
        return lambda b, j: (jnp.maximum((b * nt + j) * hb - 1, 0), blk)

    def next_map(blk):
        return lambda b, j: (jnp.minimum((b * nt + j + 1) * hb, n_hblk - 1), blk)

    small = lambda a: pl.BlockSpec(a.shape, lambda b, j: (0,) * a.ndim)
    consts = (mu, mu_lo, w0, w_up, a0, a_up, k_k, k_a, ones)
    return pl.pallas_call(
        _rwkv_prep_kernel,
        out_shape=jax.ShapeDtypeStruct((2, 6, C_HEAD_DIM, batch, C_HEADS, seq), F32),
        grid=(batch, nt),
        in_specs=[
            pl.BlockSpec((tm, wide), cur_map(cur_blk)),
            pl.BlockSpec((tm, narrow), cur_map(lo_blk)),
            pl.BlockSpec((halo, wide), prev_map(cur_blk)),
            pl.BlockSpec((halo, narrow), prev_map(lo_blk)),
            pl.BlockSpec((halo, wide), next_map(cur_blk)),
            pl.BlockSpec((halo, narrow), next_map(lo_blk)),
        ] + [small(a) for a in consts],
        out_specs=pl.BlockSpec((2, 6, C_HEAD_DIM, None, C_HEADS, tm), lambda b, j: (0, 0, 0, b, 0, j)),
        compiler_params=_cp("parallel", "arbitrary"),
        name="c_prep",
    )(proj, proj, proj, proj, proj, proj, *consts)


RELAYOUT_T = 128


def _rwkv_relayout_kernel(pf_ref, pb_ref, o_ref, *, tc):
    n = pf_ref.shape[0]
    t_blk = pf_ref.shape[-1]
    mirror = t_blk - 1 - lax.broadcasted_iota(jnp.int32, (pb_ref.shape[1] * pb_ref.shape[2], t_blk), 1)
    for k in range(n):
        bwd = jnp.take_along_axis(pb_ref[k].reshape(-1, t_blk), mirror, axis=1)
        cols = jnp.concatenate([pf_ref[k].reshape(-1, t_blk), bwd], axis=0).T
        for c in range(t_blk // tc):
            o_ref[c, k] = cols[c * tc:(c + 1) * tc]


def _rwkv_relayout(p, tc):
    _, nq, n, batch, heads, seq = p.shape
    t_blk = min(RELAYOUT_T, seq)
    nb = seq // t_blk
    lanes = 2 * batch * heads
    per_blk = t_blk // tc
    return pl.pallas_call(
        functools.partial(_rwkv_relayout_kernel, tc=tc),
        out_shape=jax.ShapeDtypeStruct((seq // tc, nq, n, tc, lanes), F32),
        grid=(nb, nq),
        in_specs=[
            pl.BlockSpec((None, None, n, batch, heads, t_blk), lambda c, q: (0, q, 0, 0, 0, c)),
            pl.BlockSpec((None, None, n, batch, heads, t_blk), lambda c, q: (1, q, 0, 0, 0, nb - 1 - c)),
        ],
        out_specs=pl.BlockSpec((per_blk, None, n, tc, lanes), lambda c, q: (c, q, 0, 0, 0)),
        compiler_params=_cp("parallel", "arbitrary"),
        name="c_relayout",
    )(p, p)


def _rwkv_scan_kernel(x_ref, rk_ref, lnw_ref, lnb_ref, out_ref, s_ref, o_ref, *, tc, k_chunk):
    n = s_ref.shape[0]
    lanes = s_ref.shape[2]

    @pl.when(pl.program_id(0) == 0)
    def _():
        s_ref[...] = jnp.zeros_like(s_ref)

    def row(q, k, t):
        return x_ref[pl.ds((q * n + k) * tc + t, 1), :]

    def step(t, carry):
        vv = x_ref[pl.ds(3 * n * tc + t, n, stride=tc), :]

        def sa_chunk(c, sa):
            base = pl.multiple_of(c * k_chunk, k_chunk)
            for j in range(k_chunk):
                sa = sa + s_ref[base + j] * row(4, base + j, t)
            return sa

        sa = lax.fori_loop(0, n // k_chunk, sa_chunk, jnp.zeros((n, lanes), F32))

        def update_chunk(c, o):
            base = pl.multiple_of(c * k_chunk, k_chunk)
            for j in range(k_chunk):
                k = base + j
                sk = s_ref[k] * row(1, k, t) + sa * row(5, k, t) + vv * row(2, k, t)
                s_ref[k] = sk
                o = o + sk * row(0, k, t)
            return o

        o_ref[pl.ds(t, n, stride=tc), :] = lax.fori_loop(0, n // k_chunk, update_chunk, jnp.zeros((n, lanes), F32))
        return carry

    lax.fori_loop(0, tc, step, 0)

    quantity = lambda q: x_ref[q * n * tc:(q + 1) * n * tc, :].reshape(n, tc, lanes)
    o = o_ref[...].reshape(n, tc, lanes)
    mean = jnp.mean(o, axis=0, keepdims=True)
    var = jnp.mean(jnp.square(o - mean), axis=0, keepdims=True)
    o = (o - mean) * lax.rsqrt(var + C_GN_EPS) * lnw_ref[...] + lnb_ref[...]
    bonus = jnp.sum(quantity(0) * quantity(2) * rk_ref[...], axis=0, keepdims=True)
    out_ref[...] = o + bonus * quantity(3)


def _rwkv_scan(xs, rk, lnw, lnb, k_chunk=32):
    nc, nq, n, tc, lanes = xs.shape
    tile = lambda: pl.BlockSpec((n, 1, lanes), lambda i: (0, 0, 0))
    return pl.pallas_call(
        functools.partial(_rwkv_scan_kernel, tc=tc, k_chunk=min(k_chunk, n)),
        out_shape=jax.ShapeDtypeStruct((n, nc * tc, lanes), F32),
        grid=(nc,),
        in_specs=[pl.BlockSpec((nq * n * tc, lanes), lambda i: (i, 0)), tile(), tile(), tile()],
        out_specs=pl.BlockSpec((n, tc, lanes), lambda i: (0, i, 0)),
        scratch_shapes=[pltpu.VMEM((n, n, lanes), F32), pltpu.VMEM((n * tc, lanes), F32)],
        compiler_params=_cp("arbitrary"),
        name="c_scan",
    )(xs.reshape(nc * nq * n * tc, lanes), rk, lnw, lnb)


def _rwkv_out_kernel(o_ref, glo_ref, gup_ref, y_ref):
    g = jnp.dot(jax.nn.sigmoid(glo_ref[...]).astype(BF16), gup_ref[...], preferred_element_type=F32)
    y_ref[...] = ((o_ref[0] + o_ref[1]) * g).astype(y_ref.dtype)


def _rwkv_out(o_dirs, proj, g_up, tm=512):
    m = proj.shape[0]
    tm = min(tm, m)
    g_blk = COL_CG // C_GATE_LORA
    return pl.pallas_call(
        _rwkv_out_kernel,
        out_shape=jax.ShapeDtypeStruct((m, C_WIDTH), BF16),
        grid=(m // tm,),
        in_specs=[
            pl.BlockSpec((2, tm, C_WIDTH), lambda i: (0, i, 0)),
            pl.BlockSpec((tm, C_GATE_LORA), lambda i: (i, g_blk)),
            pl.BlockSpec((C_GATE_LORA, C_WIDTH), lambda i: (0, 0)),
        ],
        out_specs=pl.BlockSpec((tm, C_WIDTH), lambda i: (i, 0)),
        compiler_params=_cp("parallel"),
        name="c_out",
    )(o_dirs, proj, g_up)


SCAN_TC = 16


def _heads_minor(p):
    return p.reshape(p.shape[:-1] + (C_HEADS, C_HEAD_DIM)).swapaxes(-1, -2).reshape(p.shape)


def _rwkv(proj, mu, w0, w_up, a0, a_up, g_up, k_k, k_a, r_k, ln_w, ln_b, batch, seq):
    pad_lo = C_LORA_PAD - C_LORA
    split = 3 * C_WIDTH
    mu_wide = jnp.concatenate([_heads_minor(mu[:, i * C_WIDTH:(i + 1) * C_WIDTH]) for i in range(3)], axis=-1)[:, None, :]
    mu_lo = jnp.concatenate([
        jnp.pad(mu[:, split:split + C_LORA], ((0, 0), (0, pad_lo))),
        jnp.pad(mu[:, split + C_LORA:], ((0, 0), (0, pad_lo))),
    ], axis=-1)[:, None, :]
    w_up_p = jnp.pad(_heads_minor(w_up), ((0, 0), (0, pad_lo), (0, 0)))
    a_up_p = jnp.pad(_heads_minor(a_up), ((0, 0), (0, pad_lo), (0, 0)))
    feats = _rwkv_prep(proj, mu_wide, mu_lo, _heads_minor(w0)[:, None, :], w_up_p, _heads_minor(a0)[:, None, :], a_up_p,
                       _heads_minor(k_k).reshape(1, C_WIDTH), _heads_minor(k_a).reshape(1, C_WIDTH), batch, seq)
    xs = _rwkv_relayout(feats, min(SCAN_TC, seq))
    per_lane = lambda p: jnp.tile(p.reshape(C_HEADS, C_HEAD_DIM).T, (1, 2 * batch))[:, None, :]
    o = _rwkv_scan(xs, per_lane(r_k), per_lane(ln_w), per_lane(ln_b))
    o = o.reshape(C_HEAD_DIM, seq, 2, batch, C_HEADS).transpose(2, 3, 1, 4, 0)
    o = jnp.stack([o[0], o[1, :, ::-1]]).reshape(2, batch * seq, C_WIDTH)
    return _rwkv_out(o, proj, g_up.astype(BF16))


def _merge_kernel(h_ref, wg_ref, wbr_ref, oa_ref, ob_ref, oc_ref, od_ref, o_ref, wgb_ref, wbrb_ref):
    _cast_weights_once((wg_ref, wbr_ref), (wgb_ref, wbrb_ref))
    h = h_ref[...]
    acc = None
    row = 0
    for i, b_ref in enumerate((oa_ref, ob_ref, oc_ref, od_ref)):
        width = b_ref.shape[1]
        gate = jax.nn.sigmoid(jnp.dot(h, wgb_ref[i], preferred_element_type=F32))
        term = gate * jnp.dot(b_ref[...], wbrb_ref[row:row + width, :], preferred_element_type=F32)
        acc = term if acc is None else acc + term
        row += width
    o_ref[...] = acc.astype(o_ref.dtype)


def _merge(h, w_gate, w_branch, branches, tm=512, tn=512):
    m = h.shape[0]
    tm = min(tm, m)
    d_mix = w_branch.shape[0]
    row = lambda a: pl.BlockSpec((tm, a.shape[1]), lambda j, i: (i, 0))
    return pl.pallas_call(
        _merge_kernel,
        out_shape=jax.ShapeDtypeStruct((m, D_MODEL), BF16),
        grid=(D_MODEL // tn, m // tm),
        in_specs=[
            row(h),
            _weight_spec((4, D_MODEL, tn), lambda j, i: (0, 0, j)),
            _weight_spec((d_mix, tn), lambda j, i: (0, j)),
        ] + [row(b) for b in branches],
        out_specs=pl.BlockSpec((tm, tn), lambda j, i: (i, j)),
        scratch_shapes=[pltpu.VMEM((4, D_MODEL, tn), BF16), pltpu.VMEM((d_mix, tn), BF16)],
        compiler_params=_cp("parallel", "arbitrary"),
        name="gated_merge",
    )(h, w_gate, w_branch, *branches)


def _ffn_up_kernel(h_ref, wg_ref, wu_ref, o_ref, wgb_ref, wub_ref):
    _cast_weights_once((wg_ref, wu_ref), (wgb_ref, wub_ref))
    h = h_ref[...]
    g = jnp.dot(h, wgb_ref[...], preferred_element_type=F32)
    u = jnp.dot(h, wub_ref[...], preferred_element_type=F32)
    o_ref[...] = (g * jax.nn.sigmoid(g) * u).astype(o_ref.dtype)


def _ffn_up(h, w_gate, w_up, tm=1024, tn=512):
    m, k = h.shape
    n = w_gate.shape[1]
    tm = min(tm, m)
    return pl.pallas_call(
        _ffn_up_kernel,
        out_shape=jax.ShapeDtypeStruct((m, n), BF16),
        grid=(n // tn, m // tm),
        in_specs=[
            pl.BlockSpec((tm, k), lambda j, i: (i, 0)),
            _weight_spec((k, tn), lambda j, i: (0, j)),
            _weight_spec((k, tn), lambda j, i: (0, j)),
        ],
        out_specs=pl.BlockSpec((tm, tn), lambda j, i: (i, j)),
        scratch_shapes=[pltpu.VMEM((k, tn), BF16), pltpu.VMEM((k, tn), BF16)],
        compiler_params=_cp("parallel", "arbitrary"),
        name="ffn_up",
    )(h, w_gate, w_up)


def _pad_w_in(w):
    pad = ((0, 0), (0, C_LORA_PAD - C_LORA))
    lo = COL_CLO
    cf = [_heads_minor(w[:, COL_CF + i * C_WIDTH:COL_CF + (i + 1) * C_WIDTH]) for i in range(3)]
    return jnp.concatenate([
        w[:, :COL_CF],
        *cf,
        jnp.pad(w[:, lo:lo + C_LORA], pad),
        jnp.pad(w[:, lo + C_LORA:lo + 2 * C_LORA], pad),
        w[:, lo + 2 * C_LORA:],
    ], axis=1)


def kernel(x, norm_mix, w_in, a_q_norm, a_k_norm, b_conv_w, b_conv_b, b_filt_w1, b_filt_b1, b_filt_w2, b_filt_b2, b_filt_w3, b_filt_b3, b_filt_w4, b_filt_freq, b_skip, c_mu, c_w0, c_w_up, c_a0, c_a_up, c_g_up, c_k_k, c_k_a, c_r_k, c_ln_w, c_ln_b, d_lq1, d_lk1, d_lq2, d_lk2, d_subln, w_gate, w_branch, w_out, norm_ffn, w_ff_gate, w_ff_up, w_ff_down, norm_final):
    batch, seq, _ = x.shape
    m = batch * seq
    cos, sin = _rope_tables(seq)
    cos_half, sin_half, cos_int, sin_int = _dft_tables(seq)
    filt_feats = _filter_features(seq)
    x = x.reshape(m, D_MODEL)
    for l in range(DEPTH):
        h = _rmsnorm(x, norm_mix[l], BF16)
        proj = _matmul(h, _pad_w_in(w_in[l]), F32, 512, D_IN_PAD // 4, name="in_proj")

        gains = jnp.concatenate([
            jnp.broadcast_to(a_q_norm[l], (A_HEADS, A_HEAD_DIM)),
            jnp.broadcast_to(a_k_norm[l], (A_KV_HEADS, A_HEAD_DIM)),
        ])[:, None, :]
        o_a = _attn_a(_qk_prep(proj, gains, cos, sin, seq), proj, batch, seq)

        k_re, k_im = _hyena_filter_spectrum(seq, filt_feats, cos_int, sin_int, b_filt_w1[l], b_filt_b1[l], b_filt_w2[l],
                                            b_filt_b2[l], b_filt_w3[l], b_filt_b3[l], b_filt_w4[l], b_filt_freq[l])
        o_b = _hyena(proj, b_conv_w[l], b_conv_b[l], b_skip[l], cos_half, sin_half, k_re, k_im, batch, seq)

        o_c = _rwkv(proj, c_mu[l], c_w0[l], c_w_up[l], c_a0[l], c_a_up[l], c_g_up[l], c_k_k[l], c_k_a[l], c_r_k[l],
                    c_ln_w[l], c_ln_b[l], batch, seq)

        lam_init = 0.8 - 0.6 * math.exp(-0.3 * l)
        lam_vecs = jnp.stack([d_lq1[l], d_lk1[l], d_lq2[l], d_lk2[l]])
        o_d = _attn_d(proj, lam_vecs, d_subln[l], lam_init, batch, seq)

        merged = _merge(h, w_gate[l], w_branch[l], (o_a, o_b, o_c, o_d))
        x = _matmul(merged, w_out[l], F32, 1024, 1024, residual=x, name="out_proj")

        h2 = _rmsnorm(x, norm_ffn[l], BF16)
        mid = _ffn_up(h2, w_ff_gate[l], w_ff_up[l])
        x = _matmul(mid, w_ff_down[l], F32, 256, 1024, residual=x, name="ffn_down")
    return _rmsnorm(x, norm_final, F32).reshape(batch, seq, D_MODEL)
```

```python
import functools
import math

import jax
import jax.numpy as jnp
import numpy as np
from jax import lax
from jax.experimental import pallas as pl
from jax.experimental.pallas import tpu as pltpu

D_MODEL = 2048
DEPTH = 2
GRID_W = 64
NORM_EPS = 1e-6

A_HEADS = 8
A_KV_HEADS = 2
A_HEAD_DIM = 128
A_WIDTH = A_HEADS * A_HEAD_DIM
ROPE_THETA = 10000.0

B_WIDTH = 512
B_EMB_DIM = 33
B_FILTER_HIDDEN = 64
B_DECAY_TARGET = 1e-2
B_FAST_DECAY_PCT = 0.3
B_SLOW_DECAY_PCT = 1.5

C_HEADS = 8
C_HEAD_DIM = 64
C_WIDTH = C_HEADS * C_HEAD_DIM
C_LORA = 96
C_LORA_PAD = 128
C_GATE_LORA = 256
C_GN_EPS = 64e-5

D_HEADS = 4
D_HEAD_DIM = 64
D_V_DIM = 2 * D_HEAD_DIM
D_WIDTH = D_HEADS * D_V_DIM

FFN_HIDDEN = -(-8 * D_MODEL // (3 * 256)) * 256

COL_AQ = 0
COL_AK = COL_AQ + A_WIDTH
COL_AV = COL_AK + A_KV_HEADS * A_HEAD_DIM
COL_BU = COL_AV + A_KV_HEADS * A_HEAD_DIM
COL_CF = COL_BU + 3 * B_WIDTH
COL_CLO = COL_CF + 3 * C_WIDTH
COL_CG = COL_CLO + 2 * C_LORA_PAD
COL_DQ = COL_CG + C_GATE_LORA
COL_DK = COL_DQ + 2 * D_HEADS * D_HEAD_DIM
COL_DV = COL_DK + 2 * D_HEADS * D_HEAD_DIM
D_IN_PAD = COL_DV + D_WIDTH

VMEM_LIMIT_V7X = 56 * 1024 * 1024
F32 = jnp.float32
BF16 = jnp.bfloat16
HIGHEST = lax.Precision.HIGHEST
NT_DIMS = (((1,), (1,)), ((), ()))


def _cp(*sem):
    return pltpu.CompilerParams(dimension_semantics=sem, vmem_limit_bytes=VMEM_LIMIT_V7X)


def _const_spec(shape):
    return pl.BlockSpec(shape, lambda *_: (0,) * len(shape), pipeline_mode=pl.Buffered(1))


SINGLE_BUFFER_BYTES = 8 * 1024 * 1024


def _weight_spec(shape, index_map):
    if 4 * math.prod(shape) > SINGLE_BUFFER_BYTES:
        return pl.BlockSpec(shape, index_map, pipeline_mode=pl.Buffered(1))
    return pl.BlockSpec(shape, index_map)


def _rmsnorm_kernel(x_ref, g_ref, o_ref):
    x = x_ref[...]
    ms = jnp.mean(x * x, axis=-1, keepdims=True)
    o_ref[...] = (x * lax.rsqrt(ms + NORM_EPS) * g_ref[...]).astype(o_ref.dtype)


def _rmsnorm(x, g, out_dtype, tm=512):
    m, d = x.shape
    tm = min(tm, m)
    return pl.pallas_call(
        _rmsnorm_kernel,
        out_shape=jax.ShapeDtypeStruct((m, d), out_dtype),
        grid=(m // tm,),
        in_specs=[pl.BlockSpec((tm, d), lambda i: (i, 0)), pl.BlockSpec((1, d), lambda i: (0, 0))],
        out_specs=pl.BlockSpec((tm, d), lambda i: (i, 0)),
        compiler_params=_cp("parallel"),
        name="rmsnorm",
    )(x, g.reshape(1, d))


def _mm_f32_kernel(a_ref, b_ref, o_ref):
    o_ref[...] = jnp.dot(a_ref[...], b_ref[...], preferred_element_type=F32, precision=HIGHEST)


def _matmul_f32(a, b, tm, tn, name):
    m, k = a.shape
    n = b.shape[1]
    tm, tn = min(tm, m), min(tn, n)
    return pl.pallas_call(
        _mm_f32_kernel,
        out_shape=jax.ShapeDtypeStruct((m, n), F32),
        grid=(m // tm, n // tn),
        in_specs=[pl.BlockSpec((tm, k), lambda i, j: (i, 0)), pl.BlockSpec((k, tn), lambda i, j: (0, j))],
        out_specs=pl.BlockSpec((tm, tn), lambda i, j: (i, j)),
        compiler_params=_cp("parallel", "arbitrary"),
        name=name,
    )(a, b)


def _cast_weights_once(w_refs, wb_refs):
    @pl.when(pl.program_id(1) == 0)
    def _():
        for w_ref, wb_ref in zip(w_refs, wb_refs):
            wb_ref[...] = w_ref[...].astype(BF16)


def _mm_kernel(a_ref, w_ref, o_ref, wb_ref):
    _cast_weights_once((w_ref,), (wb_ref,))
    o_ref[...] = jnp.dot(a_ref[...], wb_ref[...], preferred_element_type=F32).astype(o_ref.dtype)


def _mm_res_kernel(a_ref, w_ref, r_ref, o_ref, wb_ref):
    _cast_weights_once((w_ref,), (wb_ref,))
    o_ref[...] = r_ref[...] + jnp.dot(a_ref[...], wb_ref[...], preferred_element_type=F32)


def _matmul(a, w, out_dtype, tm, tn, residual=None, name="matmul"):
    m, k = a.shape
    n = w.shape[1]
    tm, tn = min(tm, m), min(tn, n)
    in_specs = [pl.BlockSpec((tm, k), lambda j, i: (i, 0)), _weight_spec((k, tn), lambda j, i: (0, j))]
    args = [a, w]
    body = _mm_kernel
    if residual is not None:
        body = _mm_res_kernel
        in_specs.append(pl.BlockSpec((tm, tn), lambda j, i: (i, j)))
        args.append(residual)
    return pl.pallas_call(
        body,
        out_shape=jax.ShapeDtypeStruct((m, n), out_dtype),
        grid=(n // tn, m // tm),
        in_specs=in_specs,
        out_specs=pl.BlockSpec((tm, tn), lambda j, i: (i, j)),
        scratch_shapes=[pltpu.VMEM((k, tn), BF16)],
        compiler_params=_cp("parallel", "arbitrary"),
        name=name,
    )(*args)


def _rope_tables(seq):
    rows = seq // GRID_W
    row_idx = jnp.repeat(jnp.arange(rows, dtype=F32), GRID_W)
    col_idx = jnp.tile(jnp.arange(GRID_W, dtype=F32), rows)
    axis_dim = A_HEAD_DIM // 2
    inv_freq = ROPE_THETA ** (-jnp.arange(0, axis_dim, 2, dtype=F32) / axis_dim)
    ang_r = row_idx[:, None] * inv_freq[None, :]
    ang_c = col_idx[:, None] * inv_freq[None, :]
    ang = jnp.concatenate([ang_r, ang_r, ang_c, ang_c], axis=-1)
    return jnp.cos(ang), jnp.sin(ang)


def _qk_prep_kernel(x_ref, g_ref, cos_ref, sin_ref, o_ref):
    cos = cos_ref[...]
    sin = sin_ref[...]
    lane = lax.broadcasted_iota(jnp.int32, cos.shape, 1)
    quarter = A_HEAD_DIM // 4
    first = (lane % (2 * quarter)) < quarter
    for h in range(A_HEADS + A_KV_HEADS):
        cols = slice(h * A_HEAD_DIM, (h + 1) * A_HEAD_DIM)
        x = x_ref[:, cols]
        xn = x * lax.rsqrt(jnp.mean(x * x, axis=-1, keepdims=True) + NORM_EPS) * g_ref[h]
        rot = jnp.where(first, -pltpu.roll(xn, A_HEAD_DIM - quarter, 1), pltpu.roll(xn, quarter, 1))
        y = xn * cos + rot * sin
        if h < A_HEADS:
            y = y * A_HEAD_DIM**-0.5
        o_ref[:, cols] = y.astype(o_ref.dtype)


def _qk_prep(proj, gains, cos, sin, seq, tm=256):
    m = proj.shape[0]
    tm = min(tm, seq)
    width = (A_HEADS + A_KV_HEADS) * A_HEAD_DIM
    nt = seq // tm
    return pl.pallas_call(
        _qk_prep_kernel,
        out_shape=jax.ShapeDtypeStruct((m, width), BF16),
        grid=(m // tm,),
        in_specs=[
            pl.BlockSpec((tm, width), lambda i: (i, 0)),
            pl.BlockSpec(gains.shape, lambda i: (0, 0, 0)),
            pl.BlockSpec((tm, A_HEAD_DIM), lambda i: (i % nt, 0)),
            pl.BlockSpec((tm, A_HEAD_DIM), lambda i: (i % nt, 0)),
        ],
        out_specs=pl.BlockSpec((tm, width), lambda i: (i, 0)),
        compiler_params=_cp("parallel"),
        name="a_qk_prep",
    )(proj, gains, cos, sin)


class _OnlineSoftmax:
    def __init__(self, rows, width):
        self.m = jnp.full((rows, 1), -jnp.inf, F32)
        self.l = jnp.zeros((rows, 1), F32)
        self.acc = jnp.zeros((rows, width), F32)

    def add(self, s, v):
        m_new = jnp.maximum(self.m, jnp.max(s, axis=-1, keepdims=True))
        p = jnp.exp(s - m_new)
        alpha = jnp.exp(self.m - m_new)
        self.l = alpha * self.l + jnp.sum(p, axis=-1, keepdims=True)
        self.acc = alpha * self.acc + jnp.dot(p.astype(BF16), v, preferred_element_type=F32)
        self.m = m_new

    def result(self):
        return self.acc / self.l


def _attn_a_kernel(q_ref, k_ref, v_ref, o_ref, *, kc):
    q = q_ref[...]
    sm = _OnlineSoftmax(q.shape[0], A_HEAD_DIM)
    for c in range(k_ref.shape[0] // kc):
        keys = slice(c * kc, (c + 1) * kc)
        s = lax.dot_general(q, k_ref[keys, :], NT_DIMS, preferred_element_type=F32)
        sm.add(s, v_ref[keys, :].astype(BF16))
    o_ref[...] = sm.result().astype(o_ref.dtype)


def _attn_a(qk, proj, batch, seq, tq=1024, kc=512):
    tq = min(tq, seq)
    kc = min(kc, seq)
    nq = seq // tq
    group = A_HEADS // A_KV_HEADS
    k_blk = COL_AK // A_HEAD_DIM
    v_blk = COL_AV // A_HEAD_DIM
    return pl.pallas_call(
        functools.partial(_attn_a_kernel, kc=kc),
        out_shape=jax.ShapeDtypeStruct((batch * seq, A_WIDTH), BF16),
        grid=(batch, A_HEADS, nq),
        in_specs=[
            pl.BlockSpec((tq, A_HEAD_DIM), lambda b, h, i: (b * nq + i, h)),
            pl.BlockSpec((seq, A_HEAD_DIM), lambda b, h, i: (b, k_blk + h // group)),
            pl.BlockSpec((seq, A_HEAD_DIM), lambda b, h, i: (b, v_blk + h // group)),
        ],
        out_specs=pl.BlockSpec((tq, A_HEAD_DIM), lambda b, h, i: (b * nq + i, h)),
        compiler_params=_cp("parallel", "arbitrary", "arbitrary"),
        name="a_attention",
    )(qk, qk, proj)


def _attn_d_kernel(q_ref, k_ref, v_ref, slope_ref, lam_ref, g_ref, o_ref, *, tq, kc, lam_init):
    q = q_ref[...] * D_HEAD_DIM**-0.5
    lane = lax.broadcasted_iota(jnp.int32, q.shape, 1)
    q_maps = (jnp.where(lane < D_HEAD_DIM, q, 0.0).astype(BF16), jnp.where(lane >= D_HEAD_DIM, q, 0.0).astype(BF16))
    slope = slope_ref[0][:, 0:1]
    rel = pl.program_id(2) * tq + lax.broadcasted_iota(jnp.int32, (tq, kc), 0) - lax.broadcasted_iota(jnp.int32, (tq, kc), 1)
    rel = slope * rel.astype(F32)
    maps = (_OnlineSoftmax(tq, D_V_DIM), _OnlineSoftmax(tq, D_V_DIM))
    for c in range(k_ref.shape[0] // kc):
        keys = slice(c * kc, (c + 1) * kc)
        k = k_ref[keys, :].astype(BF16)
        v = v_ref[keys, :].astype(BF16)
        bias = jnp.abs(rel - slope * float(c * kc))
        for q_map, sm in zip(q_maps, maps):
            sm.add(lax.dot_general(q_map, k, NT_DIMS, preferred_element_type=F32) - bias, v)
    lam_v = lam_ref[...]
    lam = (
        jnp.exp(jnp.sum(lam_v[0:1] * lam_v[1:2], axis=-1, keepdims=True))
        - jnp.exp(jnp.sum(lam_v[2:3] * lam_v[3:4], axis=-1, keepdims=True))
        + lam_init
    )
    o = maps[0].result() - lam * maps[1].result()
    o = o * lax.rsqrt(jnp.mean(o * o, axis=-1, keepdims=True) + NORM_EPS) * g_ref[...]
    o_ref[...] = (o * (1.0 - lam_init)).astype(o_ref.dtype)


def _attn_d(proj, lam_vecs, subln, lam_init, batch, seq, tq=1024, kc=512):
    tq = min(tq, seq)
    kc = min(kc, seq)
    nq = seq // tq
    slopes = 2.0 ** (-8.0 * np.arange(1, D_HEADS + 1, dtype=np.float32) / D_HEADS)
    slopes = jnp.asarray(np.broadcast_to(slopes[:, None, None], (D_HEADS, 1, 128)).astype(np.float32))
    q_blk, k_blk, v_blk = COL_DQ // D_V_DIM, COL_DK // D_V_DIM, COL_DV // D_V_DIM
    return pl.pallas_call(
        functools.partial(_attn_d_kernel, tq=tq, kc=kc, lam_init=lam_init),
        out_shape=jax.ShapeDtypeStruct((batch * seq, D_WIDTH), BF16),
        grid=(batch, D_HEADS, nq),
        in_specs=[
            pl.BlockSpec((tq, D_V_DIM), lambda b, h, i: (b * nq + i, q_blk + h)),
            pl.BlockSpec((seq, D_V_DIM), lambda b, h, i: (b, k_blk + h)),
            pl.BlockSpec((seq, D_V_DIM), lambda b, h, i: (b, v_blk + h)),
            pl.BlockSpec((1, 1, 128), lambda b, h, i: (h, 0, 0)),
            pl.BlockSpec((4, D_HEAD_DIM), lambda b, h, i: (0, 0)),
            pl.BlockSpec((1, D_V_DIM), lambda b, h, i: (0, 0)),
        ],
        out_specs=pl.BlockSpec((tq, D_V_DIM), lambda b, h, i: (b * nq + i, h)),
        compiler_params=_cp("parallel", "arbitrary", "arbitrary"),
        name="d_attention",
    )(proj, proj, proj, slopes, lam_vecs, subln.reshape(1, D_V_DIM))


def _dft_tables(seq):
    n = 2 * seq
    f = jnp.arange(seq, dtype=jnp.int32)
    odd = 2 * f + 1
    m_half = (odd[:, None] * odd[None, :]) % (4 * n)
    ang_half = m_half.astype(F32) * (2.0 * math.pi / (4 * n))
    m_int = (odd[:, None] * f[None, :]) % (2 * n)
    ang_int = m_int.astype(F32) * (2.0 * math.pi / (2 * n))
    return jnp.cos(ang_half).astype(BF16), jnp.sin(ang_half).astype(BF16), jnp.cos(ang_int), jnp.sin(ang_int)


def _filter_features(seq):
    t = jnp.linspace(0.0, 1.0, seq, dtype=F32)[:, None]
    n_bands = (B_EMB_DIM - 1) // 2
    bands = jnp.linspace(1e-4, n_bands - 1, n_bands, dtype=F32)[None, :]
    ang = (2.0 * math.pi / seq) * jnp.arange(seq, dtype=F32)[:, None] * bands
    z = jnp.concatenate([t, jnp.cos(ang), -jnp.sin(ang)], axis=-1)
    z = jnp.pad(z, ((0, 0), (0, B_FILTER_HIDDEN - B_EMB_DIM)))
    max_decay = math.log(B_DECAY_TARGET) / B_FAST_DECAY_PCT
    min_decay = math.log(B_DECAY_TARGET) / B_SLOW_DECAY_PCT
    deltas = jnp.abs(jnp.linspace(min_decay, max_decay, B_WIDTH, dtype=F32))[None, :]
    return z, t, deltas


def _filter_kernel(z_ref, t_ref, dl_ref, w1_ref, b1_ref, w2_ref, b2_ref, w3_ref, b3_ref, w4_ref, fr_ref, hs_ref, hd_ref):
    fr = fr_ref[...]
    hid = jnp.sin(fr * (jnp.dot(z_ref[...], w1_ref[...], preferred_element_type=F32, precision=HIGHEST) + b1_ref[...]))
    hid = jnp.sin(fr * (jnp.dot(hid, w2_ref[...], preferred_element_type=F32, precision=HIGHEST) + b2_ref[...]))
    hid = jnp.sin(fr * (jnp.dot(hid, w3_ref[...], preferred_element_type=F32, precision=HIGHEST) + b3_ref[...]))
    h = jnp.dot(hid, w4_ref[...], preferred_element_type=F32, precision=HIGHEST)
    window = jnp.exp(-t_ref[...] * dl_ref[...])
    h_fwd = h[:, :B_WIDTH] * window
    h_bwd = h[:, B_WIDTH:] * window
    row = lax.broadcasted_iota(jnp.int32, h_bwd.shape, 0)
    h_bwd = jnp.where(row == 0, 0.0, h_bwd)
    norm = jnp.sum(jnp.abs(h_fwd), axis=0, keepdims=True) + jnp.sum(jnp.abs(h_bwd), axis=0, keepdims=True)
    seq = h.shape[0]
    inv_n = 1.0 / seq
    hs_ref[...] = (h_fwd + h_bwd) / norm * inv_n
    hd_ref[...] = (h_bwd - h_fwd) / norm * inv_n


def _hyena_filter_spectrum(seq, feats, cos_int, sin_int, w1, b1, w2, b2, w3, b3, w4, freq):
    z, t, deltas = feats
    w1p = jnp.pad(w1, ((0, B_FILTER_HIDDEN - B_EMB_DIM), (0, 0)))
    row = lambda v: v.reshape(1, -1)
    args = (z, t, deltas, w1p, row(b1), w2, row(b2), w3, row(b3), w4, row(freq))
    h_sum, h_diff = pl.pallas_call(
        _filter_kernel,
        out_shape=[jax.ShapeDtypeStruct((seq, B_WIDTH), F32)] * 2,
        in_specs=[pl.BlockSpec(a.shape, lambda: (0, 0)) for a in args],
        out_specs=[pl.BlockSpec((seq, B_WIDTH), lambda: (0, 0))] * 2,
        compiler_params=pltpu.CompilerParams(vmem_limit_bytes=VMEM_LIMIT_V7X),
        name="b_filter",
    )(*args)
    k_re = _matmul_f32(cos_int, h_sum, 256, B_WIDTH, name="b_filter_dft_re")
    k_im = _matmul_f32(sin_int, h_diff, 256, B_WIDTH, name="b_filter_dft_im")
    return k_re, k_im


def _hyena_kernel(v_ref, x1_ref, x0_ref, wv_ref, w1_ref, w0_ref, bv_ref, b1_ref, b0_ref, skip_ref, c_ref, s_ref, kre_ref, kim_ref, o_ref):
    seq = v_ref.shape[0]
    row = lax.broadcasted_iota(jnp.int32, v_ref.shape, 0)

    def conv3(u_ref, w_ref, b_ref):
        u = u_ref[...]
        w = w_ref[...]
        u_prev = jnp.where(row == 0, 0.0, pltpu.roll(u, 1, 0))
        u_next = jnp.where(row == seq - 1, 0.0, pltpu.roll(u, seq - 1, 0))
        return w[0:1] * u_prev + w[1:2] * u + w[2:3] * u_next + b_ref[...]

    z = conv3(v_ref, wv_ref, bv_ref) * conv3(x1_ref, w1_ref, b1_ref)
    zb = z.astype(BF16)
    cz = jnp.dot(c_ref[...], zb, preferred_element_type=F32)
    sz = jnp.dot(s_ref[...], zb, preferred_element_type=F32)
    k_re = kre_ref[...]
    k_im = kim_ref[...]
    y_re = (cz * k_re + sz * k_im).astype(BF16)
    y_im = (cz * k_im - sz * k_re).astype(BF16)
    y = jnp.dot(c_ref[...], y_re, preferred_element_type=F32) - jnp.dot(s_ref[...], y_im, preferred_element_type=F32)
    y = y + z * skip_ref[...]
    o_ref[...] = (y * conv3(x0_ref, w0_ref, b0_ref)).astype(o_ref.dtype)


def _hyena(proj, conv_w, conv_b, skip, cos_half, sin_half, k_re, k_im, batch, seq, cb=128):
    ncb = B_WIDTH // cb
    u_blk = COL_BU // cb

    def u_spec(part):
        return pl.BlockSpec((seq, cb), lambda b, j: (b, u_blk + part * ncb + j))

    def w_spec(rows, part):
        return pl.BlockSpec((rows, cb), lambda b, j: (0, part * ncb + j))

    return pl.pallas_call(
        _hyena_kernel,
        out_shape=jax.ShapeDtypeStruct((batch * seq, B_WIDTH), BF16),
        grid=(batch, ncb),
        in_specs=[
            u_spec(0), u_spec(1), u_spec(2),
            w_spec(3, 0), w_spec(3, 1), w_spec(3, 2),
            w_spec(1, 0), w_spec(1, 1), w_spec(1, 2),
            pl.BlockSpec((1, cb), lambda b, j: (0, j)),
            _const_spec((seq, seq)), _const_spec((seq, seq)),
            pl.BlockSpec((seq, cb), lambda b, j: (0, j)),
            pl.BlockSpec((seq, cb), lambda b, j: (0, j)),
        ],
        out_specs=pl.BlockSpec((seq, cb), lambda b, j: (b, j)),
        compiler_params=_cp("parallel", "arbitrary"),
        name="b_hyena",
    )(proj, proj, proj, conv_w, conv_w, conv_w, conv_b.reshape(1, -1), conv_b.reshape(1, -1), conv_b.reshape(1, -1),
      skip.reshape(1, -1), cos_half, sin_half, k_re, k_im)


def _rwkv_prep_kernel(cur_ref, lo_ref, pcur_ref, plo_ref, ncur_ref, nlo_ref, mu_ref, mulo_ref, w0_ref, wup_ref, a0_ref, aup_ref,
                      kk_ref, ka_ref, o_ref):
    first = pl.program_id(1) == 0
    last = pl.program_id(1) == pl.num_programs(1) - 1
    halo = pcur_ref.shape[0]
    tm = cur_ref.shape[0]

    def neighbours(x_ref, p_ref, n_ref):
        x = x_ref[...]
        row = lax.broadcasted_iota(jnp.int32, x.shape, 0)
        p_row = jnp.where(first, 0.0, p_ref[halo - 1:halo, :])
        n_row = jnp.where(last, 0.0, n_ref[0:1, :])
        prev = jnp.where(row == 0, p_row, pltpu.roll(x, 1, 0))
        nxt = jnp.where(row == tm - 1, n_row, pltpu.roll(x, tm - 1, 0))
        return x, (prev, nxt)

    cur, cur_sh = neighbours(cur_ref, pcur_ref, ncur_ref)
    lo, lo_sh = neighbours(lo_ref, plo_ref, nlo_ref)
    k_k = kk_ref[...]
    k_a = ka_ref[...]
    for d in range(2):
        f = cur + (cur_sh[d] - cur) * mu_ref[d]
        f_lo = lo + (lo_sh[d] - lo) * mulo_ref[d]
        r = f[:, :C_WIDTH]
        k = f[:, C_WIDTH:2 * C_WIDTH]
        v = f[:, 2 * C_WIDTH:]
        w_lo = f_lo[:, :C_LORA_PAD]
        a_lo = f_lo[:, C_LORA_PAD:]
        x = w0_ref[d] + jnp.dot(jnp.tanh(w_lo), wup_ref[d], preferred_element_type=F32, precision=HIGHEST)
        w = -(jnp.maximum(-x, 0.0) + jnp.log(1.0 + jnp.exp(-jnp.abs(x)))) - 0.5
        decay = jnp.exp(-jnp.exp(w))
        a = jax.nn.sigmoid(a0_ref[d] + jnp.dot(a_lo, aup_ref[d], preferred_element_type=F32, precision=HIGHEST))
        transposed = lambda val: val.T.reshape(C_HEAD_DIM, C_HEADS, tm)
        kk = transposed(k * k_k)
        kk = kk / jnp.maximum(jnp.sqrt(jnp.sum(kk * kk, axis=0, keepdims=True)), 1e-12)
        for q, val in enumerate((r, decay, k * (1.0 + (a - 1.0) * k_a), v)):
            o_ref[d, q] = transposed(val)
        o_ref[d, 4] = -kk
        o_ref[d, 5] = kk * transposed(a)


def _rwkv_prep(proj, mu, mu_lo, w0, w_up, a0, a_up, k_k, k_a, batch, seq, tm=256):
    tm = min(tm, seq)
    nt = seq // tm
    halo = 8
    hb = tm // halo
    n_hblk = batch * seq // halo
    wide, narrow = 3 * C_WIDTH, 2 * C_LORA_PAD
    cur_blk, lo_blk = COL_CF // wide, COL_CLO // narrow

    def cur_map(blk):
        return lambda b, j: (b * nt + j, blk)

    def prev_map(blk):
        return lambda b, j: (jnp.maximum((b * nt + j) * hb - 1, 0), blk)

    def next_map(blk):
        return lambda b, j: (jnp.minimum((b * nt + j + 1) * hb, n_hblk - 1), blk)

    small = lambda a: pl.BlockSpec(a.shape, lambda b, j: (0,) * a.ndim)
    consts = (mu, mu_lo, w0, w_up, a0, a_up, k_k, k_a)
    return pl.pallas_call(
        _rwkv_prep_kernel,
        out_shape=jax.ShapeDtypeStruct((2, 6, C_HEAD_DIM, batch, C_HEADS, seq), F32),
        grid=(batch, nt),
        in_specs=[
            pl.BlockSpec((tm, wide), cur_map(cur_blk)),
            pl.BlockSpec((tm, narrow), cur_map(lo_blk)),
            pl.BlockSpec((halo, wide), prev_map(cur_blk)),
            pl.BlockSpec((halo, narrow), prev_map(lo_blk)),
            pl.BlockSpec((halo, wide), next_map(cur_blk)),
            pl.BlockSpec((halo, narrow), next_map(lo_blk)),
        ] + [small(a) for a in consts],
        out_specs=pl.BlockSpec((2, 6, C_HEAD_DIM, None, C_HEADS, tm), lambda b, j: (0, 0, 0, b, 0, j)),
        compiler_params=_cp("parallel", "arbitrary"),
        name="c_prep",
    )(proj, proj, proj, proj, proj, proj, *consts)


RELAYOUT_T = 128


def _rwkv_relayout_kernel(pf_ref, pb_ref, o_ref, *, tc):
    n = pf_ref.shape[0]
    t_blk = pf_ref.shape[-1]
    mirror = t_blk - 1 - lax.broadcasted_iota(jnp.int32, (pb_ref.shape[1] * pb_ref.shape[2], t_blk), 1)
    for k in range(n):
        bwd = jnp.take_along_axis(pb_ref[k].reshape(-1, t_blk), mirror, axis=1)
        cols = jnp.concatenate([pf_ref[k].reshape(-1, t_blk), bwd], axis=0).T
        for c in range(t_blk // tc):
            o_ref[c, k] = cols[c * tc:(c + 1) * tc]


def _rwkv_relayout(p, tc):
    _, nq, n, batch, heads, seq = p.shape
    t_blk = min(RELAYOUT_T, seq)
    nb = seq // t_blk
    lanes = 2 * batch * heads
    per_blk = t_blk // tc
    return pl.pallas_call(
        functools.partial(_rwkv_relayout_kernel, tc=tc),
        out_shape=jax.ShapeDtypeStruct((seq // tc, nq, n, tc, lanes), F32),
        grid=(nb, nq),
        in_specs=[
            pl.BlockSpec((None, None, n, batch, heads, t_blk), lambda c, q: (0, q, 0, 0, 0, c)),
            pl.BlockSpec((None, None, n, batch, heads, t_blk), lambda c, q: (1, q, 0, 0, 0, nb - 1 - c)),
        ],
        out_specs=pl.BlockSpec((per_blk, None, n, tc, lanes), lambda c, q: (c, q, 0, 0, 0)),
        compiler_params=_cp("parallel", "arbitrary"),
        name="c_relayout",
    )(p, p)


def _rwkv_scan_kernel(x_ref, rk_ref, lnw_ref, lnb_ref, out_ref, s_ref, o_ref, *, tc, k_chunk):
    n = s_ref.shape[0]
    lanes = s_ref.shape[2]

    @pl.when(pl.program_id(0) == 0)
    def _():
        s_ref[...] = jnp.zeros_like(s_ref)

    def row(q, k, t):
        return x_ref[pl.ds((q * n + k) * tc + t, 1), :]

    def step(t, carry):
        vv = x_ref[pl.ds(3 * n * tc + t, n, stride=tc), :]

        def sa_chunk(c, sa):
            base = pl.multiple_of(c * k_chunk, k_chunk)
            for j in range(k_chunk):
                sa = sa + s_ref[base + j] * row(4, base + j, t)
            return sa

        sa = lax.fori_loop(0, n // k_chunk, sa_chunk, jnp.zeros((n, lanes), F32))

        def update_chunk(c, o):
            base = pl.multiple_of(c * k_chunk, k_chunk)
            for j in range(k_chunk):
                k = base + j
                sk = s_ref[k] * row(1, k, t) + sa * row(5, k, t) + vv * row(2, k, t)
                s_ref[k] = sk
                o = o + sk * row(0, k, t)
            return o

        o_ref[pl.ds(t, n, stride=tc), :] = lax.fori_loop(0, n // k_chunk, update_chunk, jnp.zeros((n, lanes), F32))
        return carry

    lax.fori_loop(0, tc, step, 0)

    quantity = lambda q: x_ref[q * n * tc:(q + 1) * n * tc, :].reshape(n, tc, lanes)
    o = o_ref[...].reshape(n, tc, lanes)
    mean = jnp.mean(o, axis=0, keepdims=True)
    var = jnp.mean(jnp.square(o - mean), axis=0, keepdims=True)
    o = (o - mean) * lax.rsqrt(var + C_GN_EPS) * lnw_ref[...] + lnb_ref[...]
    bonus = jnp.sum(quantity(0) * quantity(2) * rk_ref[...], axis=0, keepdims=True)
    out_ref[...] = o + bonus * quantity(3)


def _rwkv_scan(xs, rk, lnw, lnb, k_chunk=32):
    nc, nq, n, tc, lanes = xs.shape
    tile = lambda: pl.BlockSpec((n, 1, lanes), lambda i: (0, 0, 0))
    return pl.pallas_call(
        functools.partial(_rwkv_scan_kernel, tc=tc, k_chunk=min(k_chunk, n)),
        out_shape=jax.ShapeDtypeStruct((n, nc * tc, lanes), F32),
        grid=(nc,),
        in_specs=[pl.BlockSpec((nq * n * tc, lanes), lambda i: (i, 0)), tile(), tile(), tile()],
        out_specs=pl.BlockSpec((n, tc, lanes), lambda i: (0, i, 0)),
        scratch_shapes=[pltpu.VMEM((n, n, lanes), F32), pltpu.VMEM((n * tc, lanes), F32)],
        compiler_params=_cp("arbitrary"),
        name="c_scan",
    )(xs.reshape(nc * nq * n * tc, lanes), rk, lnw, lnb)


def _rwkv_unlayout_kernel(o_ref, q_ref):
    for v in range(o_ref.shape[0]):
        rows = o_ref[v].T
        q_ref[:, v] = rows.reshape(q_ref.shape[0], q_ref.shape[2], q_ref.shape[3])


def _rwkv_unlayout(o, batch):
    n, seq, lanes = o.shape
    t_blk = min(RELAYOUT_T, seq)
    heads = lanes // (2 * batch)
    return pl.pallas_call(
        _rwkv_unlayout_kernel,
        out_shape=jax.ShapeDtypeStruct((2 * batch, n, heads, seq), F32),
        grid=(seq // t_blk,),
        in_specs=[pl.BlockSpec((n, t_blk, lanes), lambda c: (0, c, 0))],
        out_specs=pl.BlockSpec((2 * batch, n, heads, t_blk), lambda c: (0, 0, 0, c)),
        compiler_params=_cp("parallel"),
        name="c_unlayout",
    )(o)


def _rwkv_out_kernel(qf_ref, qb_ref, glo_ref, gup_ref, y_ref):
    t_blk = qf_ref.shape[-1]
    fwd = qf_ref[...].reshape(-1, t_blk)
    bwd = qb_ref[...].reshape(-1, t_blk)
    mirror = t_blk - 1 - lax.broadcasted_iota(jnp.int32, bwd.shape, 1)
    o = (fwd + jnp.take_along_axis(bwd, mirror, axis=1)).T
    g = jnp.dot(jax.nn.sigmoid(glo_ref[...]).astype(BF16), gup_ref[...], preferred_element_type=F32)
    y_ref[...] = (o * g).astype(y_ref.dtype)


def _rwkv_out(q, proj, g_up, batch, seq):
    _, n, heads, _ = q.shape
    t_blk = min(RELAYOUT_T, seq)
    nt = seq // t_blk
    g_blk = COL_CG // C_GATE_LORA
    return pl.pallas_call(
        _rwkv_out_kernel,
        out_shape=jax.ShapeDtypeStruct((batch * seq, C_WIDTH), BF16),
        grid=(batch, nt),
        in_specs=[
            pl.BlockSpec((None, n, heads, t_blk), lambda b, j: (b, 0, 0, j)),
            pl.BlockSpec((None, n, heads, t_blk), lambda b, j: (batch + b, 0, 0, nt - 1 - j)),
            pl.BlockSpec((t_blk, C_GATE_LORA), lambda b, j: (b * nt + j, g_blk)),
            pl.BlockSpec((C_GATE_LORA, C_WIDTH), lambda b, j: (0, 0)),
        ],
        out_specs=pl.BlockSpec((t_blk, C_WIDTH), lambda b, j: (b * nt + j, 0)),
        compiler_params=_cp("parallel", "arbitrary"),
        name="c_out",
    )(q, q, proj, g_up)


SCAN_TC = 16


def _heads_minor(p):
    return p.reshape(p.shape[:-1] + (C_HEADS, C_HEAD_DIM)).swapaxes(-1, -2).reshape(p.shape)


def _rwkv(proj, mu, w0, w_up, a0, a_up, g_up, k_k, k_a, r_k, ln_w, ln_b, batch, seq):
    pad_lo = C_LORA_PAD - C_LORA
    split = 3 * C_WIDTH
    mu_wide = jnp.concatenate([_heads_minor(mu[:, i * C_WIDTH:(i + 1) * C_WIDTH]) for i in range(3)], axis=-1)[:, None, :]
    mu_lo = jnp.concatenate([
        jnp.pad(mu[:, split:split + C_LORA], ((0, 0), (0, pad_lo))),
        jnp.pad(mu[:, split + C_LORA:], ((0, 0), (0, pad_lo))),
    ], axis=-1)[:, None, :]
    w_up_p = jnp.pad(_heads_minor(w_up), ((0, 0), (0, pad_lo), (0, 0)))
    a_up_p = jnp.pad(_heads_minor(a_up), ((0, 0), (0, pad_lo), (0, 0)))
    feats = _rwkv_prep(proj, mu_wide, mu_lo, _heads_minor(w0)[:, None, :], w_up_p, _heads_minor(a0)[:, None, :], a_up_p,
                       _heads_minor(k_k).reshape(1, C_WIDTH), _heads_minor(k_a).reshape(1, C_WIDTH), batch, seq)
    xs = _rwkv_relayout(feats, min(SCAN_TC, seq))
    per_lane = lambda p: jnp.tile(p.reshape(C_HEADS, C_HEAD_DIM).T, (1, 2 * batch))[:, None, :]
    o = _rwkv_scan(xs, per_lane(r_k), per_lane(ln_w), per_lane(ln_b))
    return _rwkv_out(_rwkv_unlayout(o, batch), proj, _heads_minor(g_up).astype(BF16), batch, seq)


def _merge_kernel(h_ref, wg_ref, wbr_ref, oa_ref, ob_ref, oc_ref, od_ref, o_ref, wgb_ref, wbrb_ref):
    _cast_weights_once((wg_ref, wbr_ref), (wgb_ref, wbrb_ref))
    h = h_ref[...]
    acc = None
    row = 0
    for i, b_ref in enumerate((oa_ref, ob_ref, oc_ref, od_ref)):
        width = b_ref.shape[1]
        gate = jax.nn.sigmoid(jnp.dot(h, wgb_ref[i], preferred_element_type=F32))
        term = gate * jnp.dot(b_ref[...], wbrb_ref[row:row + width, :], preferred_element_type=F32)
        acc = term if acc is None else acc + term
        row += width
    o_ref[...] = acc.astype(o_ref.dtype)


def _merge(h, w_gate, w_branch, branches, tm=512, tn=512):
    m = h.shape[0]
    tm = min(tm, m)
    d_mix = w_branch.shape[0]
    row = lambda a: pl.BlockSpec((tm, a.shape[1]), lambda j, i: (i, 0))
    return pl.pallas_call(
        _merge_kernel,
        out_shape=jax.ShapeDtypeStruct((m, D_MODEL), BF16),
        grid=(D_MODEL // tn, m // tm),
        in_specs=[
            row(h),
            _weight_spec((4, D_MODEL, tn), lambda j, i: (0, 0, j)),
            _weight_spec((d_mix, tn), lambda j, i: (0, j)),
        ] + [row(b) for b in branches],
        out_specs=pl.BlockSpec((tm, tn), lambda j, i: (i, j)),
        scratch_shapes=[pltpu.VMEM((4, D_MODEL, tn), BF16), pltpu.VMEM((d_mix, tn), BF16)],
        compiler_params=_cp("parallel", "arbitrary"),
        name="gated_merge",
    )(h, w_gate, w_branch, *branches)


def _ffn_up_kernel(h_ref, wg_ref, wu_ref, o_ref, wgb_ref, wub_ref):
    _cast_weights_once((wg_ref, wu_ref), (wgb_ref, wub_ref))
    h = h_ref[...]
    g = jnp.dot(h, wgb_ref[...], preferred_element_type=F32)
    u = jnp.dot(h, wub_ref[...], preferred_element_type=F32)
    o_ref[...] = (g * jax.nn.sigmoid(g) * u).astype(o_ref.dtype)


def _ffn_up(h, w_gate, w_up, tm=1024, tn=512):
    m, k = h.shape
    n = w_gate.shape[1]
    tm = min(tm, m)
    return pl.pallas_call(
        _ffn_up_kernel,
        out_shape=jax.ShapeDtypeStruct((m, n), BF16),
        grid=(n // tn, m // tm),
        in_specs=[
            pl.BlockSpec((tm, k), lambda j, i: (i, 0)),
            _weight_spec((k, tn), lambda j, i: (0, j)),
            _weight_spec((k, tn), lambda j, i: (0, j)),
        ],
        out_specs=pl.BlockSpec((tm, tn), lambda j, i: (i, j)),
        scratch_shapes=[pltpu.VMEM((k, tn), BF16), pltpu.VMEM((k, tn), BF16)],
        compiler_params=_cp("parallel", "arbitrary"),
        name="ffn_up",
    )(h, w_gate, w_up)


def _pad_w_in(w):
    pad = ((0, 0), (0, C_LORA_PAD - C_LORA))
    lo = COL_CLO
    cf = [_heads_minor(w[:, COL_CF + i * C_WIDTH:COL_CF + (i + 1) * C_WIDTH]) for i in range(3)]
    return jnp.concatenate([
        w[:, :COL_CF],
        *cf,
        jnp.pad(w[:, lo:lo + C_LORA], pad),
        jnp.pad(w[:, lo + C_LORA:lo + 2 * C_LORA], pad),
        w[:, lo + 2 * C_LORA:],
    ], axis=1)


def kernel(x, norm_mix, w_in, a_q_norm, a_k_norm, b_conv_w, b_conv_b, b_filt_w1, b_filt_b1, b_filt_w2, b_filt_b2, b_filt_w3, b_filt_b3, b_filt_w4, b_filt_freq, b_skip, c_mu, c_w0, c_w_up, c_a0, c_a_up, c_g_up, c_k_k, c_k_a, c_r_k, c_ln_w, c_ln_b, d_lq1, d_lk1, d_lq2, d_lk2, d_subln, w_gate, w_branch, w_out, norm_ffn, w_ff_gate, w_ff_up, w_ff_down, norm_final):
    batch, seq, _ = x.shape
    m = batch * seq
    cos, sin = _rope_tables(seq)
    cos_half, sin_half, cos_int, sin_int = _dft_tables(seq)
    filt_feats = _filter_features(seq)
    x = x.reshape(m, D_MODEL)
    for l in range(DEPTH):
        h = _rmsnorm(x, norm_mix[l], BF16)
        proj = _matmul(h, _pad_w_in(w_in[l]), F32, 512, D_IN_PAD // 4, name="in_proj")

        gains = jnp.concatenate([
            jnp.broadcast_to(a_q_norm[l], (A_HEADS, A_HEAD_DIM)),
            jnp.broadcast_to(a_k_norm[l], (A_KV_HEADS, A_HEAD_DIM)),
        ])[:, None, :]
        o_a = _attn_a(_qk_prep(proj, gains, cos, sin, seq), proj, batch, seq)

        k_re, k_im = _hyena_filter_spectrum(seq, filt_feats, cos_int, sin_int, b_filt_w1[l], b_filt_b1[l], b_filt_w2[l],
                                            b_filt_b2[l], b_filt_w3[l], b_filt_b3[l], b_filt_w4[l], b_filt_freq[l])
        o_b = _hyena(proj, b_conv_w[l], b_conv_b[l], b_skip[l], cos_half, sin_half, k_re, k_im, batch, seq)

        o_c = _rwkv(proj, c_mu[l], c_w0[l], c_w_up[l], c_a0[l], c_a_up[l], c_g_up[l], c_k_k[l], c_k_a[l], c_r_k[l],
                    c_ln_w[l], c_ln_b[l], batch, seq)

        lam_init = 0.8 - 0.6 * math.exp(-0.3 * l)
        lam_vecs = jnp.stack([d_lq1[l], d_lk1[l], d_lq2[l], d_lk2[l]])
        o_d = _attn_d(proj, lam_vecs, d_subln[l], lam_init, batch, seq)

        c_lo, c_hi = A_WIDTH + B_WIDTH, A_WIDTH + B_WIDTH + C_WIDTH
        w_br = jnp.concatenate([w_branch[l, :c_lo], _heads_minor(w_branch[l, c_lo:c_hi].T).T, w_branch[l, c_hi:]])
        merged = _merge(h, w_gate[l], w_br, (o_a, o_b, o_c, o_d))
        x = _matmul(merged, w_out[l], F32, 1024, 1024, residual=x, name="out_proj")

        h2 = _rmsnorm(x, norm_ffn[l], BF16)
        mid = _ffn_up(h2, w_ff_gate[l], w_ff_up[l])
        x = _matmul(mid, w_ff_down[l], F32, 256, 1024, residual=x, name="ffn_down")
    return _rmsnorm(x, norm_final, F32).reshape(batch, seq, D_MODEL)
```

```python
import functools
import math

import jax
import jax.numpy as jnp
import numpy as np
from jax import lax
from jax.experimental import pallas as pl
from jax.experimental.pallas import tpu as pltpu

D_MODEL = 2048
DEPTH = 2
GRID_W = 64
NORM_EPS = 1e-6

A_HEADS = 8
A_KV_HEADS = 2
A_HEAD_DIM = 128
A_WIDTH = A_HEADS * A_HEAD_DIM
ROPE_THETA = 10000.0

B_WIDTH = 512
B_EMB_DIM = 33
B_FILTER_HIDDEN = 64
B_DECAY_TARGET = 1e-2
B_FAST_DECAY_PCT = 0.3
B_SLOW_DECAY_PCT = 1.5

C_HEADS = 8
C_HEAD_DIM = 64
C_WIDTH = C_HEADS * C_HEAD_DIM
C_LORA = 96
C_LORA_PAD = 128
C_GATE_LORA = 256
C_GN_EPS = 64e-5

D_HEADS = 4
D_HEAD_DIM = 64
D_V_DIM = 2 * D_HEAD_DIM
D_WIDTH = D_HEADS * D_V_DIM

FFN_HIDDEN = -(-8 * D_MODEL // (3 * 256)) * 256

COL_AQ = 0
COL_AK = COL_AQ + A_WIDTH
COL_AV = COL_AK + A_KV_HEADS * A_HEAD_DIM
COL_BU = COL_AV + A_KV_HEADS * A_HEAD_DIM
COL_CF = COL_BU + 3 * B_WIDTH
COL_CLO = COL_CF + 3 * C_WIDTH
COL_CG = COL_CLO + 2 * C_LORA_PAD
COL_DQ = COL_CG + C_GATE_LORA
COL_DK = COL_DQ + 2 * D_HEADS * D_HEAD_DIM
COL_DV = COL_DK + 2 * D_HEADS * D_HEAD_DIM
D_IN_PAD = COL_DV + D_WIDTH

VMEM_LIMIT_V7X = 56 * 1024 * 1024
F32 = jnp.float32
BF16 = jnp.bfloat16
HIGHEST = lax.Precision.HIGHEST
NT_DIMS = (((1,), (1,)), ((), ()))


def _cp(*sem):
    return pltpu.CompilerParams(dimension_semantics=sem, vmem_limit_bytes=VMEM_LIMIT_V7X)


def _const_spec(shape):
    return pl.BlockSpec(shape, lambda *_: (0,) * len(shape), pipeline_mode=pl.Buffered(1))


SINGLE_BUFFER_BYTES = 8 * 1024 * 1024


def _weight_spec(shape, index_map, layer=None):
    mode = {"pipeline_mode": pl.Buffered(1)} if 4 * math.prod(shape) > SINGLE_BUFFER_BYTES else {}
    if layer is None:
        return pl.BlockSpec(shape, index_map, **mode)
    return pl.BlockSpec((None,) + tuple(shape), lambda *g: (layer,) + tuple(index_map(*g)), **mode)


def _rmsnorm_kernel(x_ref, g_ref, o_ref):
    x = x_ref[...]
    ms = jnp.mean(x * x, axis=-1, keepdims=True)
    o_ref[...] = (x * lax.rsqrt(ms + NORM_EPS) * g_ref[...]).astype(o_ref.dtype)


def _rmsnorm(x, g, out_dtype, tm=512):
    m, d = x.shape
    tm = min(tm, m)
    return pl.pallas_call(
        _rmsnorm_kernel,
        out_shape=jax.ShapeDtypeStruct((m, d), out_dtype),
        grid=(m // tm,),
        in_specs=[pl.BlockSpec((tm, d), lambda i: (i, 0)), pl.BlockSpec((1, d), lambda i: (0, 0))],
        out_specs=pl.BlockSpec((tm, d), lambda i: (i, 0)),
        compiler_params=_cp("parallel"),
        name="rmsnorm",
    )(x, g.reshape(1, d))


def _mm_f32_kernel(a_ref, b_ref, o_ref):
    o_ref[...] = jnp.dot(a_ref[...], b_ref[...], preferred_element_type=F32, precision=HIGHEST)


def _matmul_f32(a, b, tm, tn, name):
    m, k = a.shape
    n = b.shape[1]
    tm, tn = min(tm, m), min(tn, n)
    return pl.pallas_call(
        _mm_f32_kernel,
        out_shape=jax.ShapeDtypeStruct((m, n), F32),
        grid=(m // tm, n // tn),
        in_specs=[pl.BlockSpec((tm, k), lambda i, j: (i, 0)), pl.BlockSpec((k, tn), lambda i, j: (0, j))],
        out_specs=pl.BlockSpec((tm, tn), lambda i, j: (i, j)),
        compiler_params=_cp("parallel", "arbitrary"),
        name=name,
    )(a, b)


def _cast_weights_once(w_refs, wb_refs):
    @pl.when(pl.program_id(1) == 0)
    def _():
        for w_ref, wb_ref in zip(w_refs, wb_refs):
            wb_ref[...] = w_ref[...].astype(BF16)


def _mm_kernel(a_ref, w_ref, o_ref, wb_ref):
    _cast_weights_once((w_ref,), (wb_ref,))
    o_ref[...] = jnp.dot(a_ref[...], wb_ref[...], preferred_element_type=F32).astype(o_ref.dtype)


def _mm_res_kernel(a_ref, w_ref, r_ref, o_ref, wb_ref):
    _cast_weights_once((w_ref,), (wb_ref,))
    o_ref[...] = r_ref[...] + jnp.dot(a_ref[...], wb_ref[...], preferred_element_type=F32)


def _matmul(a, w, out_dtype, tm, tn, residual=None, layer=None, name="matmul"):
    m, k = a.shape
    n = w.shape[-1]
    tm, tn = min(tm, m), min(tn, n)
    in_specs = [pl.BlockSpec((tm, k), lambda j, i: (i, 0)), _weight_spec((k, tn), lambda j, i: (0, j), layer)]
    args = [a, w]
    body = _mm_kernel
    if residual is not None:
        body = _mm_res_kernel
        in_specs.append(pl.BlockSpec((tm, tn), lambda j, i: (i, j)))
        args.append(residual)
    return pl.pallas_call(
        body,
        out_shape=jax.ShapeDtypeStruct((m, n), out_dtype),
        grid=(n // tn, m // tm),
        in_specs=in_specs,
        out_specs=pl.BlockSpec((tm, tn), lambda j, i: (i, j)),
        scratch_shapes=[pltpu.VMEM((k, tn), BF16)],
        compiler_params=_cp("parallel", "arbitrary"),
        name=name,
    )(*args)


def _mm_res_norm_kernel(a_ref, w_ref, r_ref, g_ref, x_ref, h_ref, wb_ref):
    @pl.when(pl.program_id(0) == 0)
    def _():
        wb_ref[...] = w_ref[...].astype(BF16)

    x = r_ref[...] + jnp.dot(a_ref[...], wb_ref[...], preferred_element_type=F32)
    x_ref[...] = x
    ms = jnp.mean(x * x, axis=-1, keepdims=True)
    h_ref[...] = (x * lax.rsqrt(ms + NORM_EPS) * g_ref[...]).astype(h_ref.dtype)


def _matmul_res_norm(a, w, layer, residual, gain, tm=512):
    m, k = a.shape
    n = w.shape[-1]
    tm = min(tm, m)
    rows = lambda width: pl.BlockSpec((tm, width), lambda i: (i, 0))
    return pl.pallas_call(
        _mm_res_norm_kernel,
        out_shape=[jax.ShapeDtypeStruct((m, n), F32), jax.ShapeDtypeStruct((m, n), BF16)],
        grid=(m // tm,),
        in_specs=[rows(k), _weight_spec((k, n), lambda i: (0, 0), layer), rows(n), pl.BlockSpec((1, n), lambda i: (0, 0))],
        out_specs=[rows(n), rows(n)],
        scratch_shapes=[pltpu.VMEM((k, n), BF16)],
        compiler_params=_cp("arbitrary"),
        name="out_proj_norm",
    )(a, w, residual, gain.reshape(1, n))


def _rope_tables(seq):
    rows = seq // GRID_W
    row_idx = jnp.repeat(jnp.arange(rows, dtype=F32), GRID_W)
    col_idx = jnp.tile(jnp.arange(GRID_W, dtype=F32), rows)
    axis_dim = A_HEAD_DIM // 2
    inv_freq = ROPE_THETA ** (-jnp.arange(0, axis_dim, 2, dtype=F32) / axis_dim)
    ang_r = row_idx[:, None] * inv_freq[None, :]
    ang_c = col_idx[:, None] * inv_freq[None, :]
    ang = jnp.concatenate([ang_r, ang_r, ang_c, ang_c], axis=-1)
    return jnp.cos(ang), jnp.sin(ang)


def _qk_prep_kernel(x_ref, g_ref, cos_ref, sin_ref, o_ref):
    cos = cos_ref[...]
    sin = sin_ref[...]
    lane = lax.broadcasted_iota(jnp.int32, cos.shape, 1)
    quarter = A_HEAD_DIM // 4
    first = (lane % (2 * quarter)) < quarter
    for h in range(A_HEADS + A_KV_HEADS):
        cols = slice(h * A_HEAD_DIM, (h + 1) * A_HEAD_DIM)
        x = x_ref[:, cols]
        xn = x * lax.rsqrt(jnp.mean(x * x, axis=-1, keepdims=True) + NORM_EPS) * g_ref[h]
        rot = jnp.where(first, -pltpu.roll(xn, A_HEAD_DIM - quarter, 1), pltpu.roll(xn, quarter, 1))
        y = xn * cos + rot * sin
        if h < A_HEADS:
            y = y * A_HEAD_DIM**-0.5
        o_ref[:, cols] = y.astype(o_ref.dtype)


def _qk_prep(proj, gains, cos, sin, seq, tm=256):
    m = proj.shape[0]
    tm = min(tm, seq)
    width = (A_HEADS + A_KV_HEADS) * A_HEAD_DIM
    nt = seq // tm
    return pl.pallas_call(
        _qk_prep_kernel,
        out_shape=jax.ShapeDtypeStruct((m, width), BF16),
        grid=(m // tm,),
        in_specs=[
            pl.BlockSpec((tm, width), lambda i: (i, 0)),
            pl.BlockSpec(gains.shape, lambda i: (0, 0, 0)),
            pl.BlockSpec((tm, A_HEAD_DIM), lambda i: (i % nt, 0)),
            pl.BlockSpec((tm, A_HEAD_DIM), lambda i: (i % nt, 0)),
        ],
        out_specs=pl.BlockSpec((tm, width), lambda i: (i, 0)),
        compiler_params=_cp("parallel"),
        name="a_qk_prep",
    )(proj, gains, cos, sin)


class _OnlineSoftmax:
    def __init__(self, rows, width):
        self.m = jnp.full((rows, 1), -jnp.inf, F32)
        self.l = jnp.zeros((rows, 1), F32)
        self.acc = jnp.zeros((rows, width), F32)

    def add(self, s, v):
        m_new = jnp.maximum(self.m, jnp.max(s, axis=-1, keepdims=True))
        p = jnp.exp(s - m_new)
        alpha = jnp.exp(self.m - m_new)
        self.l = alpha * self.l + jnp.sum(p, axis=-1, keepdims=True)
        self.acc = alpha * self.acc + jnp.dot(p.astype(BF16), v, preferred_element_type=F32)
        self.m = m_new

    def result(self):
        return self.acc / self.l


def _attn_a_kernel(q_ref, k_ref, v_ref, o_ref, *, kc):
    q = q_ref[...]
    sm = _OnlineSoftmax(q.shape[0], A_HEAD_DIM)
    for c in range(k_ref.shape[0] // kc):
        keys = slice(c * kc, (c + 1) * kc)
        s = lax.dot_general(q, k_ref[keys, :], NT_DIMS, preferred_element_type=F32)
        sm.add(s, v_ref[keys, :].astype(BF16))
    o_ref[...] = sm.result().astype(o_ref.dtype)


def _attn_a(qk, proj, batch, seq, tq=1024, kc=512):
    tq = min(tq, seq)
    kc = min(kc, seq)
    nq = seq // tq
    group = A_HEADS // A_KV_HEADS
    k_blk = COL_AK // A_HEAD_DIM
    v_blk = COL_AV // A_HEAD_DIM
    return pl.pallas_call(
        functools.partial(_attn_a_kernel, kc=kc),
        out_shape=jax.ShapeDtypeStruct((batch * seq, A_WIDTH), BF16),
        grid=(batch, A_HEADS, nq),
        in_specs=[
            pl.BlockSpec((tq, A_HEAD_DIM), lambda b, h, i: (b * nq + i, h)),
            pl.BlockSpec((seq, A_HEAD_DIM), lambda b, h, i: (b, k_blk + h // group)),
            pl.BlockSpec((seq, A_HEAD_DIM), lambda b, h, i: (b, v_blk + h // group)),
        ],
        out_specs=pl.BlockSpec((tq, A_HEAD_DIM), lambda b, h, i: (b * nq + i, h)),
        compiler_params=_cp("parallel", "arbitrary", "arbitrary"),
        name="a_attention",
    )(qk, qk, proj)


def _attn_d_kernel(q_ref, k_ref, v_ref, slope_ref, lam_ref, g_ref, o_ref, *, tq, kc, lam_init):
    q = q_ref[...] * D_HEAD_DIM**-0.5
    lane = lax.broadcasted_iota(jnp.int32, q.shape, 1)
    q_maps = (jnp.where(lane < D_HEAD_DIM, q, 0.0).astype(BF16), jnp.where(lane >= D_HEAD_DIM, q, 0.0).astype(BF16))
    slope = slope_ref[0][:, 0:1]
    rel = pl.program_id(2) * tq + lax.broadcasted_iota(jnp.int32, (tq, kc), 0) - lax.broadcasted_iota(jnp.int32, (tq, kc), 1)
    rel = slope * rel.astype(F32)
    maps = (_OnlineSoftmax(tq, D_V_DIM), _OnlineSoftmax(tq, D_V_DIM))
    for c in range(k_ref.shape[0] // kc):
        keys = slice(c * kc, (c + 1) * kc)
        k = k_ref[keys, :].astype(BF16)
        v = v_ref[keys, :].astype(BF16)
        bias = jnp.abs(rel - slope * float(c * kc))
        for q_map, sm in zip(q_maps, maps):
            sm.add(lax.dot_general(q_map, k, NT_DIMS, preferred_element_type=F32) - bias, v)
    lam_v = lam_ref[...]
    lam = (
        jnp.exp(jnp.sum(lam_v[0:1] * lam_v[1:2], axis=-1, keepdims=True))
        - jnp.exp(jnp.sum(lam_v[2:3] * lam_v[3:4], axis=-1, keepdims=True))
        + lam_init
    )
    o = maps[0].result() - lam * maps[1].result()
    o = o * lax.rsqrt(jnp.mean(o * o, axis=-1, keepdims=True) + NORM_EPS) * g_ref[...]
    o_ref[...] = (o * (1.0 - lam_init)).astype(o_ref.dtype)


def _attn_d(proj, lam_vecs, subln, lam_init, batch, seq, tq=1024, kc=512):
    tq = min(tq, seq)
    kc = min(kc, seq)
    nq = seq // tq
    slopes = 2.0 ** (-8.0 * np.arange(1, D_HEADS + 1, dtype=np.float32) / D_HEADS)
    slopes = jnp.asarray(np.broadcast_to(slopes[:, None, None], (D_HEADS, 1, 128)).astype(np.float32))
    q_blk, k_blk, v_blk = COL_DQ // D_V_DIM, COL_DK // D_V_DIM, COL_DV // D_V_DIM
    return pl.pallas_call(
        functools.partial(_attn_d_kernel, tq=tq, kc=kc, lam_init=lam_init),
        out_shape=jax.ShapeDtypeStruct((batch * seq, D_WIDTH), BF16),
        grid=(batch, D_HEADS, nq),
        in_specs=[
            pl.BlockSpec((tq, D_V_DIM), lambda b, h, i: (b * nq + i, q_blk + h)),
            pl.BlockSpec((seq, D_V_DIM), lambda b, h, i: (b, k_blk + h)),
            pl.BlockSpec((seq, D_V_DIM), lambda b, h, i: (b, v_blk + h)),
            pl.BlockSpec((1, 1, 128), lambda b, h, i: (h, 0, 0)),
            pl.BlockSpec((4, D_HEAD_DIM), lambda b, h, i: (0, 0)),
            pl.BlockSpec((1, D_V_DIM), lambda b, h, i: (0, 0)),
        ],
        out_specs=pl.BlockSpec((tq, D_V_DIM), lambda b, h, i: (b * nq + i, h)),
        compiler_params=_cp("parallel", "arbitrary", "arbitrary"),
        name="d_attention",
    )(proj, proj, proj, slopes, lam_vecs, subln.reshape(1, D_V_DIM))


def _dft_tables(seq):
    n = 2 * seq
    f = jnp.arange(seq, dtype=jnp.int32)
    odd = 2 * f + 1
    m_half = (odd[:, None] * odd[None, :]) % (4 * n)
    ang_half = m_half.astype(F32) * (2.0 * math.pi / (4 * n))
    m_int = (odd[:, None] * f[None, :]) % (2 * n)
    ang_int = m_int.astype(F32) * (2.0 * math.pi / (2 * n))
    return jnp.cos(ang_half).astype(BF16), jnp.sin(ang_half).astype(BF16), jnp.cos(ang_int), jnp.sin(ang_int)


def _filter_features(seq):
    t = jnp.linspace(0.0, 1.0, seq, dtype=F32)[:, None]
    n_bands = (B_EMB_DIM - 1) // 2
    bands = jnp.linspace(1e-4, n_bands - 1, n_bands, dtype=F32)[None, :]
    ang = (2.0 * math.pi / seq) * jnp.arange(seq, dtype=F32)[:, None] * bands
    z = jnp.concatenate([t, jnp.cos(ang), -jnp.sin(ang)], axis=-1)
    z = jnp.pad(z, ((0, 0), (0, B_FILTER_HIDDEN - B_EMB_DIM)))
    max_decay = math.log(B_DECAY_TARGET) / B_FAST_DECAY_PCT
    min_decay = math.log(B_DECAY_TARGET) / B_SLOW_DECAY_PCT
    deltas = jnp.abs(jnp.linspace(min_decay, max_decay, B_WIDTH, dtype=F32))[None, :]
    return z, t, deltas


def _filter_kernel(z_ref, t_ref, dl_ref, w1_ref, b1_ref, w2_ref, b2_ref, w3_ref, b3_ref, w4_ref, fr_ref, hs_ref, hd_ref):
    fr = fr_ref[...]
    hid = jnp.sin(fr * (jnp.dot(z_ref[...], w1_ref[...], preferred_element_type=F32, precision=HIGHEST) + b1_ref[...]))
    hid = jnp.sin(fr * (jnp.dot(hid, w2_ref[...], preferred_element_type=F32, precision=HIGHEST) + b2_ref[...]))
    hid = jnp.sin(fr * (jnp.dot(hid, w3_ref[...], preferred_element_type=F32, precision=HIGHEST) + b3_ref[...]))
    h = jnp.dot(hid, w4_ref[...], preferred_element_type=F32, precision=HIGHEST)
    window = jnp.exp(-t_ref[...] * dl_ref[...])
    h_fwd = h[:, :B_WIDTH] * window
    h_bwd = h[:, B_WIDTH:] * window
    row = lax.broadcasted_iota(jnp.int32, h_bwd.shape, 0)
    h_bwd = jnp.where(row == 0, 0.0, h_bwd)
    norm = jnp.sum(jnp.abs(h_fwd), axis=0, keepdims=True) + jnp.sum(jnp.abs(h_bwd), axis=0, keepdims=True)
    seq = h.shape[0]
    inv_n = 1.0 / seq
    hs_ref[...] = (h_fwd + h_bwd) / norm * inv_n
    hd_ref[...] = (h_bwd - h_fwd) / norm * inv_n


def _hyena_filter_spectrum(seq, feats, cos_int, sin_int, w1, b1, w2, b2, w3, b3, w4, freq):
    z, t, deltas = feats
    w1p = jnp.pad(w1, ((0, B_FILTER_HIDDEN - B_EMB_DIM), (0, 0)))
    row = lambda v: v.reshape(1, -1)
    args = (z, t, deltas, w1p, row(b1), w2, row(b2), w3, row(b3), w4, row(freq))
    h_sum, h_diff = pl.pallas_call(
        _filter_kernel,
        out_shape=[jax.ShapeDtypeStruct((seq, B_WIDTH), F32)] * 2,
        in_specs=[pl.BlockSpec(a.shape, lambda: (0, 0)) for a in args],
        out_specs=[pl.BlockSpec((seq, B_WIDTH), lambda: (0, 0))] * 2,
        compiler_params=pltpu.CompilerParams(vmem_limit_bytes=VMEM_LIMIT_V7X),
        name="b_filter",
    )(*args)
    k_re = _matmul_f32(cos_int, h_sum, 256, B_WIDTH, name="b_filter_dft_re")
    k_im = _matmul_f32(sin_int, h_diff, 256, B_WIDTH, name="b_filter_dft_im")
    return k_re, k_im


def _hyena_kernel(v_ref, x1_ref, x0_ref, wv_ref, w1_ref, w0_ref, bv_ref, b1_ref, b0_ref, skip_ref, c_ref, s_ref, kre_ref, kim_ref, o_ref,
                  *, f_blk):
    seq = v_ref.shape[0]
    row = lax.broadcasted_iota(jnp.int32, v_ref.shape, 0)

    def conv3(u_ref, w_ref, b_ref):
        u = u_ref[...]
        w = w_ref[...]
        u_prev = jnp.where(row == 0, 0.0, pltpu.roll(u, 1, 0))
        u_next = jnp.where(row == seq - 1, 0.0, pltpu.roll(u, seq - 1, 0))
        return w[0:1] * u_prev + w[1:2] * u + w[2:3] * u_next + b_ref[...]

    z = conv3(v_ref, wv_ref, bv_ref) * conv3(x1_ref, w1_ref, b1_ref)
    zb = z.astype(BF16)
    y = z * skip_ref[...]
    for f0 in range(0, seq, f_blk):
        fs = slice(f0, f0 + f_blk)
        cz = jnp.dot(c_ref[fs, :], zb, preferred_element_type=F32)
        sz = jnp.dot(s_ref[fs, :], zb, preferred_element_type=F32)
        k_re = kre_ref[fs, :]
        k_im = kim_ref[fs, :]
        y_re = (cz * k_re + sz * k_im).astype(BF16)
        y_im = (cz * k_im - sz * k_re).astype(BF16)
        y = y + (jnp.dot(c_ref[:, fs], y_re, preferred_element_type=F32) - jnp.dot(s_ref[:, fs], y_im, preferred_element_type=F32))
    o_ref[...] = (y * conv3(x0_ref, w0_ref, b0_ref)).astype(o_ref.dtype)


def _hyena(proj, conv_w, conv_b, skip, cos_half, sin_half, k_re, k_im, batch, seq, cb=256, f_blk=1024):
    ncb = B_WIDTH // cb
    u_blk = COL_BU // cb
    f_blk = min(f_blk, seq)

    def u_spec(part):
        return pl.BlockSpec((seq, cb), lambda j, b: (b, u_blk + part * ncb + j))

    def w_spec(rows, part):
        return pl.BlockSpec((rows, cb), lambda j, b: (0, part * ncb + j))

    k_spec = pl.BlockSpec((seq, cb), lambda j, b: (0, j), pipeline_mode=pl.Buffered(1))
    return pl.pallas_call(
        functools.partial(_hyena_kernel, f_blk=f_blk),
        out_shape=jax.ShapeDtypeStruct((batch * seq, B_WIDTH), BF16),
        grid=(ncb, batch),
        in_specs=[
            u_spec(0), u_spec(1), u_spec(2),
            w_spec(3, 0), w_spec(3, 1), w_spec(3, 2),
            w_spec(1, 0), w_spec(1, 1), w_spec(1, 2),
            pl.BlockSpec((1, cb), lambda j, b: (0, j)),
            _const_spec((seq, seq)), _const_spec((seq, seq)),
            k_spec, k_spec,
        ],
        out_specs=pl.BlockSpec((seq, cb), lambda j, b: (b, j)),
        compiler_params=_cp("parallel", "arbitrary"),
        name="b_hyena",
    )(proj, proj, proj, conv_w, conv_w, conv_w, conv_b.reshape(1, -1), conv_b.reshape(1, -1), conv_b.reshape(1, -1),
      skip.reshape(1, -1), cos_half, sin_half, k_re, k_im)


def _rwkv_prep_kernel(cur_ref, lo_ref, pcur_ref, plo_ref, ncur_ref, nlo_ref, mu_ref, mulo_ref, w0_ref, wup_ref, a0_ref, aup_ref,
                      kk_ref, ka_ref, o_ref):
    first = pl.program_id(1) == 0
    last = pl.program_id(1) == pl.num_programs(1) - 1
    halo = pcur_ref.shape[0]
    tm = cur_ref.shape[0]

    def neighbours(x_ref, p_ref, n_ref):
        x = x_ref[...]
        row = lax.broadcasted_iota(jnp.int32, x.shape, 0)
        p_row = jnp.where(first, 0.0, p_ref[halo - 1:halo, :])
        n_row = jnp.where(last, 0.0, n_ref[0:1, :])
        prev = jnp.where(row == 0, p_row, pltpu.roll(x, 1, 0))
        nxt = jnp.where(row == tm - 1, n_row, pltpu.roll(x, tm - 1, 0))
        return x, (prev, nxt)

    cur, cur_sh = neighbours(cur_ref, pcur_ref, ncur_ref)
    lo, lo_sh = neighbours(lo_ref, plo_ref, nlo_ref)
    k_k = kk_ref[...]
    k_a = ka_ref[...]
    for d in range(2):
        f = cur + (cur_sh[d] - cur) * mu_ref[d]
        f_lo = lo + (lo_sh[d] - lo) * mulo_ref[d]
        r = f[:, :C_WIDTH]
        k = f[:, C_WIDTH:2 * C_WIDTH]
        v = f[:, 2 * C_WIDTH:]
        w_lo = f_lo[:, :C_LORA_PAD]
        a_lo = f_lo[:, C_LORA_PAD:]
        x = w0_ref[d] + jnp.dot(jnp.tanh(w_lo), wup_ref[d], preferred_element_type=F32, precision=HIGHEST)
        w = -(jnp.maximum(-x, 0.0) + jnp.log(1.0 + jnp.exp(-jnp.abs(x)))) - 0.5
        decay = jnp.exp(-jnp.exp(w))
        a = jax.nn.sigmoid(a0_ref[d] + jnp.dot(a_lo, aup_ref[d], preferred_element_type=F32, precision=HIGHEST))
        transposed = lambda val: val.T.reshape(C_HEAD_DIM, C_HEADS, tm)
        kk = transposed(k * k_k)
        kk = kk / jnp.maximum(jnp.sqrt(jnp.sum(kk * kk, axis=0, keepdims=True)), 1e-12)
        for q, val in enumerate((r, decay, k * (1.0 + (a - 1.0) * k_a), v)):
            o_ref[d, q] = transposed(val)
        o_ref[d, 4] = -kk
        o_ref[d, 5] = kk * transposed(a)


def _rwkv_prep(proj, mu, mu_lo, w0, w_up, a0, a_up, k_k, k_a, batch, seq, tm=256):
    tm = min(tm, seq)
    nt = seq // tm
    halo = 8
    hb = tm // halo
    n_hblk = batch * seq // halo
    wide, narrow = 3 * C_WIDTH, 2 * C_LORA_PAD
    cur_blk, lo_blk = COL_CF // wide, COL_CLO // narrow

    def cur_map(blk):
        return lambda b, j: (b * nt + j, blk)

    def prev_map(blk):
        return lambda b, j: (jnp.maximum((b * nt + j) * hb - 1, 0), blk)

    def next_map(blk):
        return lambda b, j: (jnp.minimum((b * nt + j + 1) * hb, n_hblk - 1), blk)

    small = lambda a: pl.BlockSpec(a.shape, lambda b, j: (0,) * a.ndim)
    consts = (mu, mu_lo, w0, w_up, a0, a_up, k_k, k_a)
    return pl.pallas_call(
        _rwkv_prep_kernel,
        out_shape=jax.ShapeDtypeStruct((2, 6, C_HEAD_DIM, batch, C_HEADS, seq), F32),
        grid=(batch, nt),
        in_specs=[
            pl.BlockSpec((tm, wide), cur_map(cur_blk)),
            pl.BlockSpec((tm, narrow), cur_map(lo_blk)),
            pl.BlockSpec((halo, wide), prev_map(cur_blk)),
            pl.BlockSpec((halo, narrow), prev_map(lo_blk)),
            pl.BlockSpec((halo, wide), next_map(cur_blk)),
            pl.BlockSpec((halo, narrow), next_map(lo_blk)),
        ] + [small(a) for a in consts],
        out_specs=pl.BlockSpec((2, 6, C_HEAD_DIM, None, C_HEADS, tm), lambda b, j: (0, 0, 0, b, 0, j)),
        compiler_params=_cp("parallel", "arbitrary"),
        name="c_prep",
    )(proj, proj, proj, proj, proj, proj, *consts)


RELAYOUT_T = 128


def _rwkv_relayout_kernel(pf_ref, pb_ref, o_ref, *, tc):
    n = pf_ref.shape[0]
    t_blk = pf_ref.shape[-1]
    mirror = t_blk - 1 - lax.broadcasted_iota(jnp.int32, (pb_ref.shape[1] * pb_ref.shape[2], t_blk), 1)
    for k in range(n):
        bwd = jnp.take_along_axis(pb_ref[k].reshape(-1, t_blk), mirror, axis=1)
        cols = jnp.concatenate([pf_ref[k].reshape(-1, t_blk), bwd], axis=0).T
        for c in range(t_blk // tc):
            o_ref[c, k] = cols[c * tc:(c + 1) * tc]


def _rwkv_relayout(p, tc):
    _, nq, n, batch, heads, seq = p.shape
    t_blk = min(RELAYOUT_T, seq)
    nb = seq // t_blk
    lanes = 2 * batch * heads
    per_blk = t_blk // tc
    return pl.pallas_call(
        functools.partial(_rwkv_relayout_kernel, tc=tc),
        out_shape=jax.ShapeDtypeStruct((seq // tc, nq, n, tc, lanes), F32),
        grid=(nb, nq),
        in_specs=[
            pl.BlockSpec((None, None, n, batch, heads, t_blk), lambda c, q: (0, q, 0, 0, 0, c)),
            pl.BlockSpec((None, None, n, batch, heads, t_blk), lambda c, q: (1, q, 0, 0, 0, nb - 1 - c)),
        ],
        out_specs=pl.BlockSpec((per_blk, None, n, tc, lanes), lambda c, q: (c, q, 0, 0, 0)),
        compiler_params=_cp("parallel", "arbitrary"),
        name="c_relayout",
    )(p, p)


def _rwkv_scan_kernel(x_ref, rk_ref, lnw_ref, lnb_ref, out_ref, s_ref, o_ref, *, tc, k_chunk):
    n = s_ref.shape[0]
    lanes = s_ref.shape[2]

    @pl.when(pl.program_id(0) == 0)
    def _():
        s_ref[...] = jnp.zeros_like(s_ref)

    def row(q, k, t):
        return x_ref[pl.ds((q * n + k) * tc + t, 1), :]

    def step(t, carry):
        vv = x_ref[pl.ds(3 * n * tc + t, n, stride=tc), :]

        def sa_chunk(c, sa):
            base = pl.multiple_of(c * k_chunk, k_chunk)
            for j in range(k_chunk):
                sa = sa + s_ref[base + j] * row(4, base + j, t)
            return sa

        sa = lax.fori_loop(0, n // k_chunk, sa_chunk, jnp.zeros((n, lanes), F32))

        def update_chunk(c, o):
            base = pl.multiple_of(c * k_chunk, k_chunk)
            for j in range(k_chunk):
                k = base + j
                sk = s_ref[k] * row(1, k, t) + sa * row(5, k, t) + vv * row(2, k, t)
                s_ref[k] = sk
                o = o + sk * row(0, k, t)
            return o

        o_ref[pl.ds(t, n, stride=tc), :] = lax.fori_loop(0, n // k_chunk, update_chunk, jnp.zeros((n, lanes), F32))
        return carry

    lax.fori_loop(0, tc, step, 0)

    quantity = lambda q: x_ref[q * n * tc:(q + 1) * n * tc, :].reshape(n, tc, lanes)
    o = o_ref[...].reshape(n, tc, lanes)
    mean = jnp.mean(o, axis=0, keepdims=True)
    var = jnp.mean(jnp.square(o - mean), axis=0, keepdims=True)
    o = (o - mean) * lax.rsqrt(var + C_GN_EPS) * lnw_ref[...] + lnb_ref[...]
    bonus = jnp.sum(quantity(0) * quantity(2) * rk_ref[...], axis=0, keepdims=True)
    out_ref[...] = o + bonus * quantity(3)


def _rwkv_scan(xs, rk, lnw, lnb, k_chunk=32):
    nc, nq, n, tc, lanes = xs.shape
    tile = lambda: pl.BlockSpec((n, 1, lanes), lambda i: (0, 0, 0))
    return pl.pallas_call(
        functools.partial(_rwkv_scan_kernel, tc=tc, k_chunk=min(k_chunk, n)),
        out_shape=jax.ShapeDtypeStruct((n, nc * tc, lanes), F32),
        grid=(nc,),
        in_specs=[pl.BlockSpec((nq * n * tc, lanes), lambda i: (i, 0)), tile(), tile(), tile()],
        out_specs=pl.BlockSpec((n, tc, lanes), lambda i: (0, i, 0)),
        scratch_shapes=[pltpu.VMEM((n, n, lanes), F32), pltpu.VMEM((n * tc, lanes), F32)],
        compiler_params=_cp("arbitrary"),
        name="c_scan",
    )(xs.reshape(nc * nq * n * tc, lanes), rk, lnw, lnb)


def _rwkv_unlayout_kernel(o_ref, q_ref):
    for v in range(o_ref.shape[0]):
        rows = o_ref[v].T
        q_ref[:, v] = rows.reshape(q_ref.shape[0], q_ref.shape[2], q_ref.shape[3])


def _rwkv_unlayout(o, batch):
    n, seq, lanes = o.shape
    t_blk = min(RELAYOUT_T, seq)
    heads = lanes // (2 * batch)
    return pl.pallas_call(
        _rwkv_unlayout_kernel,
        out_shape=jax.ShapeDtypeStruct((2 * batch, n, heads, seq), F32),
        grid=(seq // t_blk,),
        in_specs=[pl.BlockSpec((n, t_blk, lanes), lambda c: (0, c, 0))],
        out_specs=pl.BlockSpec((2 * batch, n, heads, t_blk), lambda c: (0, 0, 0, c)),
        compiler_params=_cp("parallel"),
        name="c_unlayout",
    )(o)


def _rwkv_out_kernel(qf_ref, qb_ref, glo_ref, gup_ref, y_ref):
    t_blk = qf_ref.shape[-1]
    fwd = qf_ref[...].reshape(-1, t_blk)
    bwd = qb_ref[...].reshape(-1, t_blk)
    mirror = t_blk - 1 - lax.broadcasted_iota(jnp.int32, bwd.shape, 1)
    o = (fwd + jnp.take_along_axis(bwd, mirror, axis=1)).T
    g = jnp.dot(jax.nn.sigmoid(glo_ref[...]).astype(BF16), gup_ref[...], preferred_element_type=F32)
    y_ref[...] = (o * g).astype(y_ref.dtype)


def _rwkv_out(q, proj, g_up, batch, seq):
    _, n, heads, _ = q.shape
    t_blk = min(RELAYOUT_T, seq)
    nt = seq // t_blk
    g_blk = COL_CG // C_GATE_LORA
    return pl.pallas_call(
        _rwkv_out_kernel,
        out_shape=jax.ShapeDtypeStruct((batch * seq, C_WIDTH), BF16),
        grid=(batch, nt),
        in_specs=[
            pl.BlockSpec((None, n, heads, t_blk), lambda b, j: (b, 0, 0, j)),
            pl.BlockSpec((None, n, heads, t_blk), lambda b, j: (batch + b, 0, 0, nt - 1 - j)),
            pl.BlockSpec((t_blk, C_GATE_LORA), lambda b, j: (b * nt + j, g_blk)),
            pl.BlockSpec((C_GATE_LORA, C_WIDTH), lambda b, j: (0, 0)),
        ],
        out_specs=pl.BlockSpec((t_blk, C_WIDTH), lambda b, j: (b * nt + j, 0)),
        compiler_params=_cp("parallel", "arbitrary"),
        name="c_out",
    )(q, q, proj, g_up)


SCAN_TC = 16


def _heads_minor(p):
    return p.reshape(p.shape[:-1] + (C_HEADS, C_HEAD_DIM)).swapaxes(-1, -2).reshape(p.shape)


def _rwkv(proj, mu, w0, w_up, a0, a_up, g_up, k_k, k_a, r_k, ln_w, ln_b, batch, seq):
    pad_lo = C_LORA_PAD - C_LORA
    split = 3 * C_WIDTH
    mu_wide = jnp.concatenate([_heads_minor(mu[:, i * C_WIDTH:(i + 1) * C_WIDTH]) for i in range(3)], axis=-1)[:, None, :]
    mu_lo = jnp.concatenate([
        jnp.pad(mu[:, split:split + C_LORA], ((0, 0), (0, pad_lo))),
        jnp.pad(mu[:, split + C_LORA:], ((0, 0), (0, pad_lo))),
    ], axis=-1)[:, None, :]
    w_up_p = jnp.pad(_heads_minor(w_up), ((0, 0), (0, pad_lo), (0, 0)))
    a_up_p = jnp.pad(_heads_minor(a_up), ((0, 0), (0, pad_lo), (0, 0)))
    feats = _rwkv_prep(proj, mu_wide, mu_lo, _heads_minor(w0)[:, None, :], w_up_p, _heads_minor(a0)[:, None, :], a_up_p,
                       _heads_minor(k_k).reshape(1, C_WIDTH), _heads_minor(k_a).reshape(1, C_WIDTH), batch, seq)
    xs = _rwkv_relayout(feats, min(SCAN_TC, seq))
    per_lane = lambda p: jnp.tile(p.reshape(C_HEADS, C_HEAD_DIM).T, (1, 2 * batch))[:, None, :]
    o = _rwkv_scan(xs, per_lane(r_k), per_lane(ln_w), per_lane(ln_b))
    return _rwkv_out(_rwkv_unlayout(o, batch), proj, _heads_minor(g_up).astype(BF16), batch, seq)


def _merge_kernel(h_ref, wg_ref, wbr_ref, oa_ref, ob_ref, oc_ref, od_ref, o_ref, wgb_ref, wbrb_ref):
    _cast_weights_once((wg_ref, wbr_ref), (wgb_ref, wbrb_ref))
    h = h_ref[...]
    acc = None
    row = 0
    for i, b_ref in enumerate((oa_ref, ob_ref, oc_ref, od_ref)):
        width = b_ref.shape[1]
        gate = jax.nn.sigmoid(jnp.dot(h, wgb_ref[i], preferred_element_type=F32))
        term = gate * jnp.dot(b_ref[...], wbrb_ref[row:row + width, :], preferred_element_type=F32)
        acc = term if acc is None else acc + term
        row += width
    o_ref[...] = acc.astype(o_ref.dtype)


def _merge(h, w_gate, layer, w_branch, branches, tm=512, tn=512):
    m = h.shape[0]
    tm = min(tm, m)
    d_mix = w_branch.shape[0]
    row = lambda a: pl.BlockSpec((tm, a.shape[1]), lambda j, i: (i, 0))
    return pl.pallas_call(
        _merge_kernel,
        out_shape=jax.ShapeDtypeStruct((m, D_MODEL), BF16),
        grid=(D_MODEL // tn, m // tm),
        in_specs=[
            row(h),
            _weight_spec((4, D_MODEL, tn), lambda j, i: (0, 0, j), layer),
            _weight_spec((d_mix, tn), lambda j, i: (0, j)),
        ] + [row(b) for b in branches],
        out_specs=pl.BlockSpec((tm, tn), lambda j, i: (i, j)),
        scratch_shapes=[pltpu.VMEM((4, D_MODEL, tn), BF16), pltpu.VMEM((d_mix, tn), BF16)],
        compiler_params=_cp("parallel", "arbitrary"),
        name="gated_merge",
    )(h, w_gate, w_branch, *branches)


def _ffn_up_kernel(h_ref, wg_ref, wu_ref, o_ref, wgb_ref, wub_ref):
    _cast_weights_once((wg_ref, wu_ref), (wgb_ref, wub_ref))
    h = h_ref[...]
    g = jnp.dot(h, wgb_ref[...], preferred_element_type=F32)
    u = jnp.dot(h, wub_ref[...], preferred_element_type=F32)
    o_ref[...] = (g * jax.nn.sigmoid(g) * u).astype(o_ref.dtype)


def _ffn_up(h, w_gate, w_up, layer, tm=1024, tn=512):
    m, k = h.shape
    n = w_gate.shape[-1]
    tm = min(tm, m)
    return pl.pallas_call(
        _ffn_up_kernel,
        out_shape=jax.ShapeDtypeStruct((m, n), BF16),
        grid=(n // tn, m // tm),
        in_specs=[
            pl.BlockSpec((tm, k), lambda j, i: (i, 0)),
            _weight_spec((k, tn), lambda j, i: (0, j), layer),
            _weight_spec((k, tn), lambda j, i: (0, j), layer),
        ],
        out_specs=pl.BlockSpec((tm, tn), lambda j, i: (i, j)),
        scratch_shapes=[pltpu.VMEM((k, tn), BF16), pltpu.VMEM((k, tn), BF16)],
        compiler_params=_cp("parallel", "arbitrary"),
        name="ffn_up",
    )(h, w_gate, w_up)


def _pad_w_in(w):
    pad = ((0, 0), (0, C_LORA_PAD - C_LORA))
    lo = COL_CLO
    cf = [_heads_minor(w[:, COL_CF + i * C_WIDTH:COL_CF + (i + 1) * C_WIDTH]) for i in range(3)]
    return jnp.concatenate([
        w[:, :COL_CF],
        *cf,
        jnp.pad(w[:, lo:lo + C_LORA], pad),
        jnp.pad(w[:, lo + C_LORA:lo + 2 * C_LORA], pad),
        w[:, lo + 2 * C_LORA:],
    ], axis=1)


def kernel(x, norm_mix, w_in, a_q_norm, a_k_norm, b_conv_w, b_conv_b, b_filt_w1, b_filt_b1, b_filt_w2, b_filt_b2, b_filt_w3, b_filt_b3, b_filt_w4, b_filt_freq, b_skip, c_mu, c_w0, c_w_up, c_a0, c_a_up, c_g_up, c_k_k, c_k_a, c_r_k, c_ln_w, c_ln_b, d_lq1, d_lk1, d_lq2, d_lk2, d_subln, w_gate, w_branch, w_out, norm_ffn, w_ff_gate, w_ff_up, w_ff_down, norm_final):
    batch, seq, _ = x.shape
    m = batch * seq
    cos, sin = _rope_tables(seq)
    cos_half, sin_half, cos_int, sin_int = _dft_tables(seq)
    filt_feats = _filter_features(seq)
    x = x.reshape(m, D_MODEL)
    for l in range(DEPTH):
        h = _rmsnorm(x, norm_mix[l], BF16)
        proj = _matmul(h, _pad_w_in(w_in[l]), F32, 512, D_IN_PAD // 4, name="in_proj")

        gains = jnp.concatenate([
            jnp.broadcast_to(a_q_norm[l], (A_HEADS, A_HEAD_DIM)),
            jnp.broadcast_to(a_k_norm[l], (A_KV_HEADS, A_HEAD_DIM)),
        ])[:, None, :]
        o_a = _attn_a(_qk_prep(proj, gains, cos, sin, seq), proj, batch, seq)

        k_re, k_im = _hyena_filter_spectrum(seq, filt_feats, cos_int, sin_int, b_filt_w1[l], b_filt_b1[l], b_filt_w2[l],
                                            b_filt_b2[l], b_filt_w3[l], b_filt_b3[l], b_filt_w4[l], b_filt_freq[l])
        o_b = _hyena(proj, b_conv_w[l], b_conv_b[l], b_skip[l], cos_half, sin_half, k_re, k_im, batch, seq)

        o_c = _rwkv(proj, c_mu[l], c_w0[l], c_w_up[l], c_a0[l], c_a_up[l], c_g_up[l], c_k_k[l], c_k_a[l], c_r_k[l],
                    c_ln_w[l], c_ln_b[l], batch, seq)

        lam_init = 0.8 - 0.6 * math.exp(-0.3 * l)
        lam_vecs = jnp.stack([d_lq1[l], d_lk1[l], d_lq2[l], d_lk2[l]])
        o_d = _attn_d(proj, lam_vecs, d_subln[l], lam_init, batch, seq)

        c_lo, c_hi = A_WIDTH + B_WIDTH, A_WIDTH + B_WIDTH + C_WIDTH
        w_br = jnp.concatenate([w_branch[l, :c_lo], _heads_minor(w_branch[l, c_lo:c_hi].T).T, w_branch[l, c_hi:]])
        merged = _merge(h, w_gate, l, w_br, (o_a, o_b, o_c, o_d))
        x, h2 = _matmul_res_norm(merged, w_out, l, x, norm_ffn[l])
        mid = _ffn_up(h2, w_ff_gate, w_ff_up, l)
        x = _matmul(mid, w_ff_down, F32, 256, 1024, residual=x, layer=l, name="ffn_down")
    return _rmsnorm(x, norm_final, F32).reshape(batch, seq, D_MODEL)
```

```python
import functools
import math

import jax
import jax.numpy as jnp
import numpy as np
from jax import lax
from jax.experimental import pallas as pl
from jax.experimental.pallas import tpu as pltpu

D_MODEL = 2048
DEPTH = 2
GRID_W = 64
NORM_EPS = 1e-6

A_HEADS = 8
A_KV_HEADS = 2
A_HEAD_DIM = 128
A_WIDTH = A_HEADS * A_HEAD_DIM
ROPE_THETA = 10000.0

B_WIDTH = 512
B_EMB_DIM = 33
B_FILTER_HIDDEN = 64
B_DECAY_TARGET = 1e-2
B_FAST_DECAY_PCT = 0.3
B_SLOW_DECAY_PCT = 1.5

C_HEADS = 8
C_HEAD_DIM = 64
C_WIDTH = C_HEADS * C_HEAD_DIM
C_LORA = 96
C_LORA_PAD = 128
C_GATE_LORA = 256
C_GN_EPS = 64e-5

D_HEADS = 4
D_HEAD_DIM = 64
D_V_DIM = 2 * D_HEAD_DIM
D_WIDTH = D_HEADS * D_V_DIM

FFN_HIDDEN = -(-8 * D_MODEL // (3 * 256)) * 256

COL_AQ = 0
COL_AK = COL_AQ + A_WIDTH
COL_AV = COL_AK + A_KV_HEADS * A_HEAD_DIM
COL_BU = COL_AV + A_KV_HEADS * A_HEAD_DIM
COL_CF = COL_BU + 3 * B_WIDTH
COL_CLO = COL_CF + 3 * C_WIDTH
COL_CG = COL_CLO + 2 * C_LORA_PAD
COL_DQ = COL_CG + C_GATE_LORA
COL_DK = COL_DQ + 2 * D_HEADS * D_HEAD_DIM
COL_DV = COL_DK + 2 * D_HEADS * D_HEAD_DIM
D_IN_PAD = COL_DV + D_WIDTH

VMEM_LIMIT_V7X = 56 * 1024 * 1024
F32 = jnp.float32
BF16 = jnp.bfloat16
HIGHEST = lax.Precision.HIGHEST
NT_DIMS = (((1,), (1,)), ((), ()))
LOG2_E = math.log2(math.e)


def _cp(*sem):
    return pltpu.CompilerParams(dimension_semantics=sem, vmem_limit_bytes=VMEM_LIMIT_V7X)


def _const_spec(shape):
    return pl.BlockSpec(shape, lambda *_: (0,) * len(shape), pipeline_mode=pl.Buffered(1))


SINGLE_BUFFER_BYTES = 8 * 1024 * 1024


def _weight_spec(shape, index_map, layer=None):
    mode = {"pipeline_mode": pl.Buffered(1)} if 4 * math.prod(shape) > SINGLE_BUFFER_BYTES else {}
    if layer is None:
        return pl.BlockSpec(shape, index_map, **mode)
    return pl.BlockSpec((None,) + tuple(shape), lambda *g: (layer,) + tuple(index_map(*g)), **mode)


def _rmsnorm_kernel(x_ref, g_ref, o_ref):
    x = x_ref[...]
    ms = jnp.mean(x * x, axis=-1, keepdims=True)
    o_ref[...] = (x * lax.rsqrt(ms + NORM_EPS) * g_ref[...]).astype(o_ref.dtype)


def _rmsnorm(x, g, out_dtype, tm=512):
    m, d = x.shape
    tm = min(tm, m)
    return pl.pallas_call(
        _rmsnorm_kernel,
        out_shape=jax.ShapeDtypeStruct((m, d), out_dtype),
        grid=(m // tm,),
        in_specs=[pl.BlockSpec((tm, d), lambda i: (i, 0)), pl.BlockSpec((1, d), lambda i: (0, 0))],
        out_specs=pl.BlockSpec((tm, d), lambda i: (i, 0)),
        compiler_params=_cp("parallel"),
        name="rmsnorm",
    )(x, g.reshape(1, d))


def _mm_f32_kernel(a_ref, b_ref, o_ref):
    def split(x):
        hi = x.astype(BF16)
        return hi, (x - hi.astype(F32)).astype(BF16)

    (a_hi, a_lo), (b_hi, b_lo) = split(a_ref[...]), split(b_ref[...])
    dot = functools.partial(jnp.dot, preferred_element_type=F32)
    o_ref[...] = dot(a_hi, b_hi) + (dot(a_hi, b_lo) + dot(a_lo, b_hi))


def _matmul_f32(a, b, tm, tn, name):
    m, k = a.shape
    n = b.shape[1]
    tm, tn = min(tm, m), min(tn, n)
    return pl.pallas_call(
        _mm_f32_kernel,
        out_shape=jax.ShapeDtypeStruct((m, n), F32),
        grid=(m // tm, n // tn),
        in_specs=[pl.BlockSpec((tm, k), lambda i, j: (i, 0)), pl.BlockSpec((k, tn), lambda i, j: (0, j))],
        out_specs=pl.BlockSpec((tm, tn), lambda i, j: (i, j)),
        compiler_params=_cp("parallel", "arbitrary"),
        name=name,
    )(a, b)


def _cast_weights_once(w_refs, wb_refs):
    @pl.when(pl.program_id(1) == 0)
    def _():
        for w_ref, wb_ref in zip(w_refs, wb_refs):
            wb_ref[...] = w_ref[...].astype(BF16)


def _mm_kernel(a_ref, w_ref, o_ref, wb_ref):
    _cast_weights_once((w_ref,), (wb_ref,))
    o_ref[...] = jnp.dot(a_ref[...], wb_ref[...], preferred_element_type=F32).astype(o_ref.dtype)


def _mm_res_kernel(a_ref, w_ref, r_ref, o_ref, wb_ref):
    _cast_weights_once((w_ref,), (wb_ref,))
    o_ref[...] = r_ref[...] + jnp.dot(a_ref[...], wb_ref[...], preferred_element_type=F32)


def _matmul(a, w, out_dtype, tm, tn, residual=None, layer=None, name="matmul"):
    m, k = a.shape
    n = w.shape[-1]
    tm, tn = min(tm, m), min(tn, n)
    in_specs = [pl.BlockSpec((tm, k), lambda j, i: (i, 0)), _weight_spec((k, tn), lambda j, i: (0, j), layer)]
    args = [a, w]
    body = _mm_kernel
    if residual is not None:
        body = _mm_res_kernel
        in_specs.append(pl.BlockSpec((tm, tn), lambda j, i: (i, j)))
        args.append(residual)
    return pl.pallas_call(
        body,
        out_shape=jax.ShapeDtypeStruct((m, n), out_dtype),
        grid=(n // tn, m // tm),
        in_specs=in_specs,
        out_specs=pl.BlockSpec((tm, tn), lambda j, i: (i, j)),
        scratch_shapes=[pltpu.VMEM((k, tn), BF16)],
        compiler_params=_cp("parallel", "arbitrary"),
        name=name,
    )(*args)


def _mm_res_norm_kernel(a_ref, w_ref, r_ref, g_ref, x_ref, h_ref, wb_ref):
    @pl.when(pl.program_id(0) == 0)
    def _():
        wb_ref[...] = w_ref[...].astype(BF16)

    x = r_ref[...] + jnp.dot(a_ref[...], wb_ref[...], preferred_element_type=F32)
    x_ref[...] = x
    ms = jnp.mean(x * x, axis=-1, keepdims=True)
    h_ref[...] = (x * lax.rsqrt(ms + NORM_EPS) * g_ref[...]).astype(h_ref.dtype)


def _matmul_res_norm(a, w, layer, residual, gain, tm=512):
    m, k = a.shape
    n = w.shape[-1]
    tm = min(tm, m)
    rows = lambda width: pl.BlockSpec((tm, width), lambda i: (i, 0))
    return pl.pallas_call(
        _mm_res_norm_kernel,
        out_shape=[jax.ShapeDtypeStruct((m, n), F32), jax.ShapeDtypeStruct((m, n), BF16)],
        grid=(m // tm,),
        in_specs=[rows(k), _weight_spec((k, n), lambda i: (0, 0), layer), rows(n), pl.BlockSpec((1, n), lambda i: (0, 0))],
        out_specs=[rows(n), rows(n)],
        scratch_shapes=[pltpu.VMEM((k, n), BF16)],
        compiler_params=_cp("arbitrary"),
        name="out_proj_norm",
    )(a, w, residual, gain.reshape(1, n))


def _rope_tables(seq):
    rows = seq // GRID_W
    row_idx = jnp.repeat(jnp.arange(rows, dtype=F32), GRID_W)
    col_idx = jnp.tile(jnp.arange(GRID_W, dtype=F32), rows)
    axis_dim = A_HEAD_DIM // 2
    inv_freq = ROPE_THETA ** (-jnp.arange(0, axis_dim, 2, dtype=F32) / axis_dim)
    ang_r = row_idx[:, None] * inv_freq[None, :]
    ang_c = col_idx[:, None] * inv_freq[None, :]
    ang = jnp.concatenate([ang_r, ang_r, ang_c, ang_c], axis=-1)
    return jnp.cos(ang), jnp.sin(ang)


def _qk_prep_kernel(x_ref, g_ref, cos_ref, sin_ref, o_ref):
    cos = cos_ref[...]
    sin = sin_ref[...]
    lane = lax.broadcasted_iota(jnp.int32, cos.shape, 1)
    quarter = A_HEAD_DIM // 4
    first = (lane % (2 * quarter)) < quarter
    for h in range(A_HEADS + A_KV_HEADS):
        cols = slice(h * A_HEAD_DIM, (h + 1) * A_HEAD_DIM)
        x = x_ref[:, cols]
        xn = x * lax.rsqrt(jnp.mean(x * x, axis=-1, keepdims=True) + NORM_EPS) * g_ref[h]
        rot = jnp.where(first, -pltpu.roll(xn, A_HEAD_DIM - quarter, 1), pltpu.roll(xn, quarter, 1))
        y = xn * cos + rot * sin
        if h < A_HEADS:
            y = y * (A_HEAD_DIM**-0.5 * LOG2_E)
        o_ref[:, cols] = y.astype(o_ref.dtype)


def _qk_prep(proj, gains, cos, sin, seq, tm=256):
    m = proj.shape[0]
    tm = min(tm, seq)
    width = (A_HEADS + A_KV_HEADS) * A_HEAD_DIM
    nt = seq // tm
    return pl.pallas_call(
        _qk_prep_kernel,
        out_shape=jax.ShapeDtypeStruct((m, width), BF16),
        grid=(m // tm,),
        in_specs=[
            pl.BlockSpec((tm, width), lambda i: (i, 0)),
            pl.BlockSpec(gains.shape, lambda i: (0, 0, 0)),
            pl.BlockSpec((tm, A_HEAD_DIM), lambda i: (i % nt, 0)),
            pl.BlockSpec((tm, A_HEAD_DIM), lambda i: (i % nt, 0)),
        ],
        out_specs=pl.BlockSpec((tm, width), lambda i: (i, 0)),
        compiler_params=_cp("parallel"),
        name="a_qk_prep",
    )(proj, gains, cos, sin)


class _OnlineSoftmax:
    def __init__(self, rows, width):
        self.m = jnp.full((rows, 1), -jnp.inf, F32)
        self.l = jnp.zeros((rows, 1), F32)
        self.acc = jnp.zeros((rows, width), F32)

    def add(self, s, v):
        m_new = jnp.maximum(self.m, jnp.max(s, axis=-1, keepdims=True))
        p = jnp.exp2(s - m_new)
        alpha = jnp.exp2(self.m - m_new)
        self.l = alpha * self.l + jnp.sum(p, axis=-1, keepdims=True)
        self.acc = alpha * self.acc + jnp.dot(p.astype(BF16), v, preferred_element_type=F32)
        self.m = m_new

    def result(self):
        return self.acc / self.l


def _attn_a_kernel(q_ref, k_ref, v_ref, o_ref, *, kc):
    q = q_ref[...]
    sm = _OnlineSoftmax(q.shape[0], A_HEAD_DIM)
    for c in range(k_ref.shape[0] // kc):
        keys = slice(c * kc, (c + 1) * kc)
        s = lax.dot_general(q, k_ref[keys, :], NT_DIMS, preferred_element_type=F32)
        sm.add(s, v_ref[keys, :].astype(BF16))
    o_ref[...] = sm.result().astype(o_ref.dtype)


def _attn_a(qk, proj, batch, seq, tq=1024, kc=512):
    tq = min(tq, seq)
    kc = min(kc, seq)
    nq = seq // tq
    group = A_HEADS // A_KV_HEADS
    k_blk = COL_AK // A_HEAD_DIM
    v_blk = COL_AV // A_HEAD_DIM
    return pl.pallas_call(
        functools.partial(_attn_a_kernel, kc=kc),
        out_shape=jax.ShapeDtypeStruct((batch * seq, A_WIDTH), BF16),
        grid=(batch, A_HEADS, nq),
        in_specs=[
            pl.BlockSpec((tq, A_HEAD_DIM), lambda b, h, i: (b * nq + i, h)),
            pl.BlockSpec((seq, A_HEAD_DIM), lambda b, h, i: (b, k_blk + h // group)),
            pl.BlockSpec((seq, A_HEAD_DIM), lambda b, h, i: (b, v_blk + h // group)),
        ],
        out_specs=pl.BlockSpec((tq, A_HEAD_DIM), lambda b, h, i: (b * nq + i, h)),
        compiler_params=_cp("parallel", "arbitrary", "arbitrary"),
        name="a_attention",
    )(qk, qk, proj)


def _attn_d_kernel(q_ref, k_ref, v_ref, slope_ref, lam_ref, g_ref, o_ref, *, tq, kc, lam_init):
    q = q_ref[...] * (D_HEAD_DIM**-0.5 * LOG2_E)
    lane = lax.broadcasted_iota(jnp.int32, q.shape, 1)
    q_maps = (jnp.where(lane < D_HEAD_DIM, q, 0.0).astype(BF16), jnp.where(lane >= D_HEAD_DIM, q, 0.0).astype(BF16))
    slope = slope_ref[0][:, 0:1] * LOG2_E
    rel = pl.program_id(2) * tq + lax.broadcasted_iota(jnp.int32, (tq, kc), 0) - lax.broadcasted_iota(jnp.int32, (tq, kc), 1)
    rel = slope * rel.astype(F32)
    maps = (_OnlineSoftmax(tq, D_V_DIM), _OnlineSoftmax(tq, D_V_DIM))
    for c in range(k_ref.shape[0] // kc):
        keys = slice(c * kc, (c + 1) * kc)
        k = k_ref[keys, :].astype(BF16)
        v = v_ref[keys, :].astype(BF16)
        bias = jnp.abs(rel - slope * float(c * kc))
        for q_map, sm in zip(q_maps, maps):
            sm.add(lax.dot_general(q_map, k, NT_DIMS, preferred_element_type=F32) - bias, v)
    lam_v = lam_ref[...]
    lam = (
        jnp.exp(jnp.sum(lam_v[0:1] * lam_v[1:2], axis=-1, keepdims=True))
        - jnp.exp(jnp.sum(lam_v[2:3] * lam_v[3:4], axis=-1, keepdims=True))
        + lam_init
    )
    o = maps[0].result() - lam * maps[1].result()
    o = o * lax.rsqrt(jnp.mean(o * o, axis=-1, keepdims=True) + NORM_EPS) * g_ref[...]
    o_ref[...] = (o * (1.0 - lam_init)).astype(o_ref.dtype)


def _attn_d(proj, lam_vecs, subln, lam_init, batch, seq, tq=1024, kc=512):
    tq = min(tq, seq)
    kc = min(kc, seq)
    nq = seq // tq
    slopes = 2.0 ** (-8.0 * np.arange(1, D_HEADS + 1, dtype=np.float32) / D_HEADS)
    slopes = jnp.asarray(np.broadcast_to(slopes[:, None, None], (D_HEADS, 1, 128)).astype(np.float32))
    q_blk, k_blk, v_blk = COL_DQ // D_V_DIM, COL_DK // D_V_DIM, COL_DV // D_V_DIM
    return pl.pallas_call(
        functools.partial(_attn_d_kernel, tq=tq, kc=kc, lam_init=lam_init),
        out_shape=jax.ShapeDtypeStruct((batch * seq, D_WIDTH), BF16),
        grid=(batch, D_HEADS, nq),
        in_specs=[
            pl.BlockSpec((tq, D_V_DIM), lambda b, h, i: (b * nq + i, q_blk + h)),
            pl.BlockSpec((seq, D_V_DIM), lambda b, h, i: (b, k_blk + h)),
            pl.BlockSpec((seq, D_V_DIM), lambda b, h, i: (b, v_blk + h)),
            pl.BlockSpec((1, 1, 128), lambda b, h, i: (h, 0, 0)),
            pl.BlockSpec((4, D_HEAD_DIM), lambda b, h, i: (0, 0)),
            pl.BlockSpec((1, D_V_DIM), lambda b, h, i: (0, 0)),
        ],
        out_specs=pl.BlockSpec((tq, D_V_DIM), lambda b, h, i: (b * nq + i, h)),
        compiler_params=_cp("parallel", "arbitrary", "arbitrary"),
        name="d_attention",
    )(proj, proj, proj, slopes, lam_vecs, subln.reshape(1, D_V_DIM))


DFT_COL_SPLIT = 64


def _dft_tables(seq):
    n = 2 * seq
    blk = min(DFT_COL_SPLIT, seq)
    odd = 2 * jnp.arange(seq, dtype=jnp.int32)[:, None] + 1
    hi = jnp.arange(seq // blk, dtype=jnp.int32)[None, :]
    lo = jnp.arange(blk, dtype=jnp.int32)[None, :]

    def cos_sin(index, period):
        ang = (index % period).astype(F32) * (2.0 * math.pi / period)
        return jnp.cos(ang), jnp.sin(ang)

    def tables(hi_index, lo_index, period):
        (ca, sa), (cb, sb) = cos_sin(odd * hi_index, period), cos_sin(odd * lo_index, period)
        ca, sa, cb, sb = ca[:, :, None], sa[:, :, None], cb[:, None, :], sb[:, None, :]
        return (ca * cb - sa * sb).reshape(seq, seq), (sa * cb + ca * sb).reshape(seq, seq)

    cos_half, sin_half = tables(2 * blk * hi, 2 * lo + 1, 4 * n)
    cos_int, sin_int = tables(blk * hi, lo, 2 * n)
    return cos_half.astype(BF16), sin_half.astype(BF16), cos_int, sin_int


def _filter_features(seq):
    t = jnp.linspace(0.0, 1.0, seq, dtype=F32)[:, None]
    n_bands = (B_EMB_DIM - 1) // 2
    bands = jnp.linspace(1e-4, n_bands - 1, n_bands, dtype=F32)[None, :]
    ang = (2.0 * math.pi / seq) * jnp.arange(seq, dtype=F32)[:, None] * bands
    z = jnp.concatenate([t, jnp.cos(ang), -jnp.sin(ang)], axis=-1)
    z = jnp.pad(z, ((0, 0), (0, B_FILTER_HIDDEN - B_EMB_DIM)))
    max_decay = math.log(B_DECAY_TARGET) / B_FAST_DECAY_PCT
    min_decay = math.log(B_DECAY_TARGET) / B_SLOW_DECAY_PCT
    deltas = jnp.abs(jnp.linspace(min_decay, max_decay, B_WIDTH, dtype=F32))[None, :]
    return z, t, deltas


def _filter_kernel(z_ref, t_ref, dl_ref, w1_ref, b1_ref, w2_ref, b2_ref, w3_ref, b3_ref, w4_ref, fr_ref, hs_ref, hd_ref):
    fr = fr_ref[...]
    hid = jnp.sin(fr * (jnp.dot(z_ref[...], w1_ref[...], preferred_element_type=F32, precision=HIGHEST) + b1_ref[...]))
    hid = jnp.sin(fr * (jnp.dot(hid, w2_ref[...], preferred_element_type=F32, precision=HIGHEST) + b2_ref[...]))
    hid = jnp.sin(fr * (jnp.dot(hid, w3_ref[...], preferred_element_type=F32, precision=HIGHEST) + b3_ref[...]))
    h = jnp.dot(hid, w4_ref[...], preferred_element_type=F32, precision=HIGHEST)
    window = jnp.exp(-t_ref[...] * dl_ref[...])
    h_fwd = h[:, :B_WIDTH] * window
    h_bwd = h[:, B_WIDTH:] * window
    row = lax.broadcasted_iota(jnp.int32, h_bwd.shape, 0)
    h_bwd = jnp.where(row == 0, 0.0, h_bwd)
    norm = jnp.sum(jnp.abs(h_fwd), axis=0, keepdims=True) + jnp.sum(jnp.abs(h_bwd), axis=0, keepdims=True)
    seq = h.shape[0]
    inv_n = 1.0 / seq
    hs_ref[...] = (h_fwd + h_bwd) / norm * inv_n
    hd_ref[...] = (h_bwd - h_fwd) / norm * inv_n


def _hyena_filter_spectrum(seq, feats, cos_int, sin_int, w1, b1, w2, b2, w3, b3, w4, freq):
    z, t, deltas = feats
    w1p = jnp.pad(w1, ((0, B_FILTER_HIDDEN - B_EMB_DIM), (0, 0)))
    row = lambda v: v.reshape(1, -1)
    args = (z, t, deltas, w1p, row(b1), w2, row(b2), w3, row(b3), w4, row(freq))
    h_sum, h_diff = pl.pallas_call(
        _filter_kernel,
        out_shape=[jax.ShapeDtypeStruct((seq, B_WIDTH), F32)] * 2,
        in_specs=[pl.BlockSpec(a.shape, lambda: (0, 0)) for a in args],
        out_specs=[pl.BlockSpec((seq, B_WIDTH), lambda: (0, 0))] * 2,
        compiler_params=pltpu.CompilerParams(vmem_limit_bytes=VMEM_LIMIT_V7X),
        name="b_filter",
    )(*args)
    k_re = _matmul_f32(cos_int, h_sum, 256, B_WIDTH, name="b_filter_dft_re")
    k_im = _matmul_f32(sin_int, h_diff, 256, B_WIDTH, name="b_filter_dft_im")
    return k_re, k_im


def _hyena_kernel(v_ref, x1_ref, x0_ref, wv_ref, w1_ref, w0_ref, bv_ref, b1_ref, b0_ref, skip_ref, c_ref, s_ref, kre_ref, kim_ref, o_ref,
                  *, f_blk):
    seq = v_ref.shape[0]
    row = lax.broadcasted_iota(jnp.int32, v_ref.shape, 0)

    def conv3(u_ref, w_ref, b_ref):
        u = u_ref[...]
        w = w_ref[...]
        u_prev = jnp.where(row == 0, 0.0, pltpu.roll(u, 1, 0))
        u_next = jnp.where(row == seq - 1, 0.0, pltpu.roll(u, seq - 1, 0))
        return w[0:1] * u_prev + w[1:2] * u + w[2:3] * u_next + b_ref[...]

    z = conv3(v_ref, wv_ref, bv_ref) * conv3(x1_ref, w1_ref, b1_ref)
    zb = z.astype(BF16)
    y = z * skip_ref[...]
    for f0 in range(0, seq, f_blk):
        fs = slice(f0, f0 + f_blk)
        cz = jnp.dot(c_ref[fs, :], zb, preferred_element_type=F32)
        sz = jnp.dot(s_ref[fs, :], zb, preferred_element_type=F32)
        k_re = kre_ref[fs, :]
        k_im = kim_ref[fs, :]
        y_re = (cz * k_re + sz * k_im).astype(BF16)
        y_im = (cz * k_im - sz * k_re).astype(BF16)
        y = y + (jnp.dot(c_ref[:, fs], y_re, preferred_element_type=F32) - jnp.dot(s_ref[:, fs], y_im, preferred_element_type=F32))
    o_ref[...] = (y * conv3(x0_ref, w0_ref, b0_ref)).astype(o_ref.dtype)


def _hyena(proj, conv_w, conv_b, skip, cos_half, sin_half, k_re, k_im, batch, seq, cb=256, f_blk=1024):
    ncb = B_WIDTH // cb
    u_blk = COL_BU // cb
    f_blk = min(f_blk, seq)

    def u_spec(part):
        return pl.BlockSpec((seq, cb), lambda j, b: (b, u_blk + part * ncb + j))

    def w_spec(rows, part):
        return pl.BlockSpec((rows, cb), lambda j, b: (0, part * ncb + j))

    k_spec = pl.BlockSpec((seq, cb), lambda j, b: (0, j), pipeline_mode=pl.Buffered(1))
    return pl.pallas_call(
        functools.partial(_hyena_kernel, f_blk=f_blk),
        out_shape=jax.ShapeDtypeStruct((batch * seq, B_WIDTH), BF16),
        grid=(ncb, batch),
        in_specs=[
            u_spec(0), u_spec(1), u_spec(2),
            w_spec(3, 0), w_spec(3, 1), w_spec(3, 2),
            w_spec(1, 0), w_spec(1, 1), w_spec(1, 2),
            pl.BlockSpec((1, cb), lambda j, b: (0, j)),
            _const_spec((seq, seq)), _const_spec((seq, seq)),
            k_spec, k_spec,
        ],
        out_specs=pl.BlockSpec((seq, cb), lambda j, b: (b, j)),
        compiler_params=_cp("parallel", "arbitrary"),
        name="b_hyena",
    )(proj, proj, proj, conv_w, conv_w, conv_w, conv_b.reshape(1, -1), conv_b.reshape(1, -1), conv_b.reshape(1, -1),
      skip.reshape(1, -1), cos_half, sin_half, k_re, k_im)


def _rwkv_prep_kernel(cur_ref, lo_ref, pcur_ref, plo_ref, ncur_ref, nlo_ref, mu_ref, mulo_ref, w0_ref, wup_ref, a0_ref, aup_ref,
                      kk_ref, ka_ref, o_ref):
    first = pl.program_id(1) == 0
    last = pl.program_id(1) == pl.num_programs(1) - 1
    halo = pcur_ref.shape[0]
    tm = cur_ref.shape[0]

    def neighbours(x_ref, p_ref, n_ref):
        x = x_ref[...]
        row = lax.broadcasted_iota(jnp.int32, x.shape, 0)
        p_row = jnp.where(first, 0.0, p_ref[halo - 1:halo, :])
        n_row = jnp.where(last, 0.0, n_ref[0:1, :])
        prev = jnp.where(row == 0, p_row, pltpu.roll(x, 1, 0))
        nxt = jnp.where(row == tm - 1, n_row, pltpu.roll(x, tm - 1, 0))
        return x, (prev, nxt)

    cur, cur_sh = neighbours(cur_ref, pcur_ref, ncur_ref)
    lo, lo_sh = neighbours(lo_ref, plo_ref, nlo_ref)
    k_k = kk_ref[...]
    k_a = ka_ref[...]
    for d in range(2):
        f = cur + (cur_sh[d] - cur) * mu_ref[d]
        f_lo = lo + (lo_sh[d] - lo) * mulo_ref[d]
        r = f[:, :C_WIDTH]
        k = f[:, C_WIDTH:2 * C_WIDTH]
        v = f[:, 2 * C_WIDTH:]
        w_lo = f_lo[:, :C_LORA_PAD]
        a_lo = f_lo[:, C_LORA_PAD:]
        x = w0_ref[d] + jnp.dot(jnp.tanh(w_lo), wup_ref[d], preferred_element_type=F32, precision=HIGHEST)
        w = -(jnp.maximum(-x, 0.0) + jnp.log(1.0 + jnp.exp(-jnp.abs(x)))) - 0.5
        decay = jnp.exp(-jnp.exp(w))
        a = jax.nn.sigmoid(a0_ref[d] + jnp.dot(a_lo, aup_ref[d], preferred_element_type=F32, precision=HIGHEST))
        transposed = lambda val: val.T.reshape(C_HEAD_DIM, C_HEADS, tm)
        kk = transposed(k * k_k)
        kk = kk / jnp.maximum(jnp.sqrt(jnp.sum(kk * kk, axis=0, keepdims=True)), 1e-12)
        for q, val in enumerate((r, decay, k * (1.0 + (a - 1.0) * k_a), v)):
            o_ref[d, q] = transposed(val)
        o_ref[d, 4] = -kk
        o_ref[d, 5] = kk * transposed(a)


def _rwkv_prep(proj, mu, mu_lo, w0, w_up, a0, a_up, k_k, k_a, batch, seq, tm=256):
    tm = min(tm, seq)
    nt = seq // tm
    halo = 8
    hb = tm // halo
    n_hblk = batch * seq // halo
    wide, narrow = 3 * C_WIDTH, 2 * C_LORA_PAD
    cur_blk, lo_blk = COL_CF // wide, COL_CLO // narrow

    def cur_map(blk):
        return lambda b, j: (b * nt + j, blk)

    def prev_map(blk):
        return lambda b, j: (jnp.maximum((b * nt + j) * hb - 1, 0), blk)

    def next_map(blk):
        return lambda b, j: (jnp.minimum((b * nt + j + 1) * hb, n_hblk - 1), blk)

    small = lambda a: pl.BlockSpec(a.shape, lambda b, j: (0,) * a.ndim)
    consts = (mu, mu_lo, w0, w_up, a0, a_up, k_k, k_a)
    return pl.pallas_call(
        _rwkv_prep_kernel,
        out_shape=jax.ShapeDtypeStruct((2, 6, C_HEAD_DIM, batch, C_HEADS, seq), F32),
        grid=(batch, nt),
        in_specs=[
            pl.BlockSpec((tm, wide), cur_map(cur_blk)),
            pl.BlockSpec((tm, narrow), cur_map(lo_blk)),
            pl.BlockSpec((halo, wide), prev_map(cur_blk)),
            pl.BlockSpec((halo, narrow), prev_map(lo_blk)),
            pl.BlockSpec((halo, wide), next_map(cur_blk)),
            pl.BlockSpec((halo, narrow), next_map(lo_blk)),
        ] + [small(a) for a in consts],
        out_specs=pl.BlockSpec((2, 6, C_HEAD_DIM, None, C_HEADS, tm), lambda b, j: (0, 0, 0, b, 0, j)),
        compiler_params=_cp("parallel", "arbitrary"),
        name="c_prep",
    )(proj, proj, proj, proj, proj, proj, *consts)


RELAYOUT_T = 128


def _rwkv_relayout_kernel(pf_ref, pb_ref, o_ref, *, tc):
    n = pf_ref.shape[0]
    t_blk = pf_ref.shape[-1]
    mirror = t_blk - 1 - lax.broadcasted_iota(jnp.int32, (pb_ref.shape[1] * pb_ref.shape[2], t_blk), 1)
    for k in range(n):
        bwd = jnp.take_along_axis(pb_ref[k].reshape(-1, t_blk), mirror, axis=1)
        cols = jnp.concatenate([pf_ref[k].reshape(-1, t_blk), bwd], axis=0).T
        for c in range(t_blk // tc):
            o_ref[c, k] = cols[c * tc:(c + 1) * tc]


def _rwkv_relayout(p, tc):
    _, nq, n, batch, heads, seq = p.shape
    t_blk = min(RELAYOUT_T, seq)
    nb = seq // t_blk
    lanes = 2 * batch * heads
    per_blk = t_blk // tc
    return pl.pallas_call(
        functools.partial(_rwkv_relayout_kernel, tc=tc),
        out_shape=jax.ShapeDtypeStruct((seq // tc, nq, n, tc, lanes), F32),
        grid=(nb, nq),
        in_specs=[
            pl.BlockSpec((None, None, n, batch, heads, t_blk), lambda c, q: (0, q, 0, 0, 0, c)),
            pl.BlockSpec((None, None, n, batch, heads, t_blk), lambda c, q: (1, q, 0, 0, 0, nb - 1 - c)),
        ],
        out_specs=pl.BlockSpec((per_blk, None, n, tc, lanes), lambda c, q: (c, q, 0, 0, 0)),
        compiler_params=_cp("parallel", "arbitrary"),
        name="c_relayout",
    )(p, p)


def _rwkv_scan_kernel(x_ref, rk_ref, lnw_ref, lnb_ref, out_ref, s_ref, o_ref, *, tc, k_chunk):
    n = s_ref.shape[0]
    lanes = s_ref.shape[2]

    @pl.when(pl.program_id(0) == 0)
    def _():
        s_ref[...] = jnp.zeros_like(s_ref)

    def row(q, k, t):
        return x_ref[pl.ds((q * n + k) * tc + t, 1), :]

    def step(t, carry):
        vv = x_ref[pl.ds(3 * n * tc + t, n, stride=tc), :]

        def sa_chunk(c, sa):
            base = pl.multiple_of(c * k_chunk, k_chunk)
            for j in range(k_chunk):
                sa = sa + s_ref[base + j] * row(4, base + j, t)
            return sa

        sa = lax.fori_loop(0, n // k_chunk, sa_chunk, jnp.zeros((n, lanes), F32))

        def update_chunk(c, o):
            base = pl.multiple_of(c * k_chunk, k_chunk)
            for j in range(k_chunk):
                k = base + j
                sk = s_ref[k] * row(1, k, t) + sa * row(5, k, t) + vv * row(2, k, t)
                s_ref[k] = sk
                o = o + sk * row(0, k, t)
            return o

        o_ref[pl.ds(t, n, stride=tc), :] = lax.fori_loop(0, n // k_chunk, update_chunk, jnp.zeros((n, lanes), F32))
        return carry

    lax.fori_loop(0, tc, step, 0)

    quantity = lambda q: x_ref[q * n * tc:(q + 1) * n * tc, :].reshape(n, tc, lanes)
    o = o_ref[...].reshape(n, tc, lanes)
    mean = jnp.mean(o, axis=0, keepdims=True)
    var = jnp.mean(jnp.square(o - mean), axis=0, keepdims=True)
    o = (o - mean) * lax.rsqrt(var + C_GN_EPS) * lnw_ref[...] + lnb_ref[...]
    bonus = jnp.sum(quantity(0) * quantity(2) * rk_ref[...], axis=0, keepdims=True)
    out_ref[...] = o + bonus * quantity(3)


def _rwkv_scan(xs, rk, lnw, lnb, k_chunk=32):
    nc, nq, n, tc, lanes = xs.shape
    tile = lambda: pl.BlockSpec((n, 1, lanes), lambda i: (0, 0, 0))
    return pl.pallas_call(
        functools.partial(_rwkv_scan_kernel, tc=tc, k_chunk=min(k_chunk, n)),
        out_shape=jax.ShapeDtypeStruct((n, nc * tc, lanes), F32),
        grid=(nc,),
        in_specs=[pl.BlockSpec((nq * n * tc, lanes), lambda i: (i, 0)), tile(), tile(), tile()],
        out_specs=pl.BlockSpec((n, tc, lanes), lambda i: (0, i, 0)),
        scratch_shapes=[pltpu.VMEM((n, n, lanes), F32), pltpu.VMEM((n * tc, lanes), F32)],
        compiler_params=_cp("arbitrary"),
        name="c_scan",
    )(xs.reshape(nc * nq * n * tc, lanes), rk, lnw, lnb)


def _rwkv_unlayout_kernel(o_ref, q_ref):
    for v in range(o_ref.shape[0]):
        rows = o_ref[v].T
        q_ref[:, v] = rows.reshape(q_ref.shape[0], q_ref.shape[2], q_ref.shape[3])


def _rwkv_unlayout(o, batch):
    n, seq, lanes = o.shape
    t_blk = min(RELAYOUT_T, seq)
    heads = lanes // (2 * batch)
    return pl.pallas_call(
        _rwkv_unlayout_kernel,
        out_shape=jax.ShapeDtypeStruct((2 * batch, n, heads, seq), F32),
        grid=(seq // t_blk,),
        in_specs=[pl.BlockSpec((n, t_blk, lanes), lambda c: (0, c, 0))],
        out_specs=pl.BlockSpec((2 * batch, n, heads, t_blk), lambda c: (0, 0, 0, c)),
        compiler_params=_cp("parallel"),
        name="c_unlayout",
    )(o)


def _rwkv_out_kernel(qf_ref, qb_ref, glo_ref, gup_ref, y_ref):
    t_blk = qf_ref.shape[-1]
    fwd = qf_ref[...].reshape(-1, t_blk)
    bwd = qb_ref[...].reshape(-1, t_blk)
    mirror = t_blk - 1 - lax.broadcasted_iota(jnp.int32, bwd.shape, 1)
    o = (fwd + jnp.take_along_axis(bwd, mirror, axis=1)).T
    g = jnp.dot(jax.nn.sigmoid(glo_ref[...]).astype(BF16), gup_ref[...], preferred_element_type=F32)
    y_ref[...] = (o * g).astype(y_ref.dtype)


def _rwkv_out(q, proj, g_up, batch, seq):
    _, n, heads, _ = q.shape
    t_blk = min(RELAYOUT_T, seq)
    nt = seq // t_blk
    g_blk = COL_CG // C_GATE_LORA
    return pl.pallas_call(
        _rwkv_out_kernel,
        out_shape=jax.ShapeDtypeStruct((batch * seq, C_WIDTH), BF16),
        grid=(batch, nt),
        in_specs=[
            pl.BlockSpec((None, n, heads, t_blk), lambda b, j: (b, 0, 0, j)),
            pl.BlockSpec((None, n, heads, t_blk), lambda b, j: (batch + b, 0, 0, nt - 1 - j)),
            pl.BlockSpec((t_blk, C_GATE_LORA), lambda b, j: (b * nt + j, g_blk)),
            pl.BlockSpec((C_GATE_LORA, C_WIDTH), lambda b, j: (0, 0)),
        ],
        out_specs=pl.BlockSpec((t_blk, C_WIDTH), lambda b, j: (b * nt + j, 0)),
        compiler_params=_cp("parallel", "arbitrary"),
        name="c_out",
    )(q, q, proj, g_up)


SCAN_TC = 32


def _heads_minor(p):
    return p.reshape(p.shape[:-1] + (C_HEADS, C_HEAD_DIM)).swapaxes(-1, -2).reshape(p.shape)


def _rwkv(proj, mu, w0, w_up, a0, a_up, g_up, k_k, k_a, r_k, ln_w, ln_b, batch, seq):
    pad_lo = C_LORA_PAD - C_LORA
    split = 3 * C_WIDTH
    mu_wide = jnp.concatenate([_heads_minor(mu[:, i * C_WIDTH:(i + 1) * C_WIDTH]) for i in range(3)], axis=-1)[:, None, :]
    mu_lo = jnp.concatenate([
        jnp.pad(mu[:, split:split + C_LORA], ((0, 0), (0, pad_lo))),
        jnp.pad(mu[:, split + C_LORA:], ((0, 0), (0, pad_lo))),
    ], axis=-1)[:, None, :]
    w_up_p = jnp.pad(_heads_minor(w_up), ((0, 0), (0, pad_lo), (0, 0)))
    a_up_p = jnp.pad(_heads_minor(a_up), ((0, 0), (0, pad_lo), (0, 0)))
    feats = _rwkv_prep(proj, mu_wide, mu_lo, _heads_minor(w0)[:, None, :], w_up_p, _heads_minor(a0)[:, None, :], a_up_p,
                       _heads_minor(k_k).reshape(1, C_WIDTH), _heads_minor(k_a).reshape(1, C_WIDTH), batch, seq)
    xs = _rwkv_relayout(feats, min(SCAN_TC, seq))
    per_lane = lambda p: jnp.tile(p.reshape(C_HEADS, C_HEAD_DIM).T, (1, 2 * batch))[:, None, :]
    o = _rwkv_scan(xs, per_lane(r_k), per_lane(ln_w), per_lane(ln_b))
    return _rwkv_out(_rwkv_unlayout(o, batch), proj, _heads_minor(g_up).astype(BF16), batch, seq)


def _merge_kernel(h_ref, wg_ref, wbr_ref, oa_ref, ob_ref, oc_ref, od_ref, o_ref, wgb_ref, wbrb_ref):
    _cast_weights_once((wg_ref, wbr_ref), (wgb_ref, wbrb_ref))
    h = h_ref[...]
    acc = None
    row = 0
    for i, b_ref in enumerate((oa_ref, ob_ref, oc_ref, od_ref)):
        width = b_ref.shape[1]
        gate = jax.nn.sigmoid(jnp.dot(h, wgb_ref[i], preferred_element_type=F32))
        term = gate * jnp.dot(b_ref[...], wbrb_ref[row:row + width, :], preferred_element_type=F32)
        acc = term if acc is None else acc + term
        row += width
    o_ref[...] = acc.astype(o_ref.dtype)


def _merge(h, w_gate, layer, w_branch, branches, tm=512, tn=512):
    m = h.shape[0]
    tm = min(tm, m)
    d_mix = w_branch.shape[0]
    row = lambda a: pl.BlockSpec((tm, a.shape[1]), lambda j, i: (i, 0))
    return pl.pallas_call(
        _merge_kernel,
        out_shape=jax.ShapeDtypeStruct((m, D_MODEL), BF16),
        grid=(D_MODEL // tn, m // tm),
        in_specs=[
            row(h),
            _weight_spec((4, D_MODEL, tn), lambda j, i: (0, 0, j), layer),
            _weight_spec((d_mix, tn), lambda j, i: (0, j)),
        ] + [row(b) for b in branches],
        out_specs=pl.BlockSpec((tm, tn), lambda j, i: (i, j)),
        scratch_shapes=[pltpu.VMEM((4, D_MODEL, tn), BF16), pltpu.VMEM((d_mix, tn), BF16)],
        compiler_params=_cp("parallel", "arbitrary"),
        name="gated_merge",
    )(h, w_gate, w_branch, *branches)


def _ffn_up_kernel(h_ref, wg_ref, wu_ref, o_ref, wgb_ref, wub_ref):
    _cast_weights_once((wg_ref, wu_ref), (wgb_ref, wub_ref))
    h = h_ref[...]
    g = jnp.dot(h, wgb_ref[...], preferred_element_type=F32)
    u = jnp.dot(h, wub_ref[...], preferred_element_type=F32)
    o_ref[...] = (g * jax.nn.sigmoid(g) * u).astype(o_ref.dtype)


def _ffn_up(h, w_gate, w_up, layer, tm=1024, tn=512):
    m, k = h.shape
    n = w_gate.shape[-1]
    tm = min(tm, m)
    return pl.pallas_call(
        _ffn_up_kernel,
        out_shape=jax.ShapeDtypeStruct((m, n), BF16),
        grid=(n // tn, m // tm),
        in_specs=[
            pl.BlockSpec((tm, k), lambda j, i: (i, 0)),
            _weight_spec((k, tn), lambda j, i: (0, j), layer),
            _weight_spec((k, tn), lambda j, i: (0, j), layer),
        ],
        out_specs=pl.BlockSpec((tm, tn), lambda j, i: (i, j)),
        scratch_shapes=[pltpu.VMEM((k, tn), BF16), pltpu.VMEM((k, tn), BF16)],
        compiler_params=_cp("parallel", "arbitrary"),
        name="ffn_up",
    )(h, w_gate, w_up)


def _pad_w_in(w):
    pad = ((0, 0), (0, C_LORA_PAD - C_LORA))
    lo = COL_CLO
    cf = [_heads_minor(w[:, COL_CF + i * C_WIDTH:COL_CF + (i + 1) * C_WIDTH]) for i in range(3)]
    return jnp.concatenate([
        w[:, :COL_CF],
        *cf,
        jnp.pad(w[:, lo:lo + C_LORA], pad),
        jnp.pad(w[:, lo + C_LORA:lo + 2 * C_LORA], pad),
        w[:, lo + 2 * C_LORA:],
    ], axis=1)


def kernel(x, norm_mix, w_in, a_q_norm, a_k_norm, b_conv_w, b_conv_b, b_filt_w1, b_filt_b1, b_filt_w2, b_filt_b2, b_filt_w3, b_filt_b3, b_filt_w4, b_filt_freq, b_skip, c_mu, c_w0, c_w_up, c_a0, c_a_up, c_g_up, c_k_k, c_k_a, c_r_k, c_ln_w, c_ln_b, d_lq1, d_lk1, d_lq2, d_lk2, d_subln, w_gate, w_branch, w_out, norm_ffn, w_ff_gate, w_ff_up, w_ff_down, norm_final):
    batch, seq, _ = x.shape
    m = batch * seq
    cos, sin = _rope_tables(seq)
    cos_half, sin_half, cos_int, sin_int = _dft_tables(seq)
    filt_feats = _filter_features(seq)
    x = x.reshape(m, D_MODEL)
    for l in range(DEPTH):
        h = _rmsnorm(x, norm_mix[l], BF16)
        proj = _matmul(h, _pad_w_in(w_in[l]), F32, 512, D_IN_PAD // 4, name="in_proj")

        gains = jnp.concatenate([
            jnp.broadcast_to(a_q_norm[l], (A_HEADS, A_HEAD_DIM)),
            jnp.broadcast_to(a_k_norm[l], (A_KV_HEADS, A_HEAD_DIM)),
        ])[:, None, :]
        o_a = _attn_a(_qk_prep(proj, gains, cos, sin, seq), proj, batch, seq)

        k_re, k_im = _hyena_filter_spectrum(seq, filt_feats, cos_int, sin_int, b_filt_w1[l], b_filt_b1[l], b_filt_w2[l],
                                            b_filt_b2[l], b_filt_w3[l], b_filt_b3[l], b_filt_w4[l], b_filt_freq[l])
        o_b = _hyena(proj, b_conv_w[l], b_conv_b[l], b_skip[l], cos_half, sin_half, k_re, k_im, batch, seq)

        o_c = _rwkv(proj, c_mu[l], c_w0[l], c_w_up[l], c_a0[l], c_a_up[l], c_g_up[l], c_k_k[l], c_k_a[l], c_r_k[l],
                    c_ln_w[l], c_ln_b[l], batch, seq)

        lam_init = 0.8 - 0.6 * math.exp(-0.3 * l)
        lam_vecs = jnp.stack([d_lq1[l], d_lk1[l], d_lq2[l], d_lk2[l]])
        o_d = _attn_d(proj, lam_vecs, d_subln[l], lam_init, batch, seq)

        c_lo, c_hi = A_WIDTH + B_WIDTH, A_WIDTH + B_WIDTH + C_WIDTH
        w_br = jnp.concatenate([w_branch[l, :c_lo], _heads_minor(w_branch[l, c_lo:c_hi].T).T, w_branch[l, c_hi:]])
        merged = _merge(h, w_gate, l, w_br, (o_a, o_b, o_c, o_d))
        x, h2 = _matmul_res_norm(merged, w_out, l, x, norm_ffn[l])
        mid = _ffn_up(h2, w_ff_gate, w_ff_up, l)
        x = _matmul(mid, w_ff_down, F32, 256, 1024, residual=x, layer=l, name="ffn_down")
    return _rmsnorm(x, norm_final, F32).reshape(batch, seq, D_MODEL)
```

```python
import functools
import math

import jax
import jax.numpy as jnp
import numpy as np
from jax import lax
from jax.experimental import pallas as pl
from jax.experimental.pallas import tpu as pltpu

D_MODEL = 2048
DEPTH = 2
GRID_W = 64
NORM_EPS = 1e-6

A_HEADS = 8
A_KV_HEADS = 2
A_HEAD_DIM = 128
A_WIDTH = A_HEADS * A_HEAD_DIM
ROPE_THETA = 10000.0

B_WIDTH = 512
B_EMB_DIM = 33
B_FILTER_HIDDEN = 64
B_DECAY_TARGET = 1e-2
B_FAST_DECAY_PCT = 0.3
B_SLOW_DECAY_PCT = 1.5

C_HEADS = 8
C_HEAD_DIM = 64
C_WIDTH = C_HEADS * C_HEAD_DIM
C_LORA = 96
C_LORA_PAD = 128
C_GATE_LORA = 256
C_GN_EPS = 64e-5

D_HEADS = 4
D_HEAD_DIM = 64
D_V_DIM = 2 * D_HEAD_DIM
D_WIDTH = D_HEADS * D_V_DIM

FFN_HIDDEN = -(-8 * D_MODEL // (3 * 256)) * 256

COL_AQ = 0
COL_AK = COL_AQ + A_WIDTH
COL_AV = COL_AK + A_KV_HEADS * A_HEAD_DIM
COL_BU = COL_AV + A_KV_HEADS * A_HEAD_DIM
COL_CF = COL_BU + 3 * B_WIDTH
COL_CLO = COL_CF + 3 * C_WIDTH
COL_CG = COL_CLO + 2 * C_LORA_PAD
COL_DQ = COL_CG + C_GATE_LORA
COL_DK = COL_DQ + 2 * D_HEADS * D_HEAD_DIM
COL_DV = COL_DK + 2 * D_HEADS * D_HEAD_DIM
D_IN_PAD = COL_DV + D_WIDTH

VMEM_LIMIT_V7X = 56 * 1024 * 1024
F32 = jnp.float32
BF16 = jnp.bfloat16
HIGHEST = lax.Precision.HIGHEST
NT_DIMS = (((1,), (1,)), ((), ()))
LOG2_E = math.log2(math.e)


def _cp(*sem):
    return pltpu.CompilerParams(dimension_semantics=sem, vmem_limit_bytes=VMEM_LIMIT_V7X)


def _const_spec(shape):
    return pl.BlockSpec(shape, lambda *_: (0,) * len(shape), pipeline_mode=pl.Buffered(1))


SINGLE_BUFFER_BYTES = 8 * 1024 * 1024


def _weight_spec(shape, index_map, layer=None):
    mode = {"pipeline_mode": pl.Buffered(1)} if 4 * math.prod(shape) > SINGLE_BUFFER_BYTES else {}
    if layer is None:
        return pl.BlockSpec(shape, index_map, **mode)
    return pl.BlockSpec((None,) + tuple(shape), lambda *g: (layer,) + tuple(index_map(*g)), **mode)


def _rmsnorm_kernel(x_ref, g_ref, o_ref):
    x = x_ref[...]
    ms = jnp.mean(x * x, axis=-1, keepdims=True)
    o_ref[...] = (x * lax.rsqrt(ms + NORM_EPS) * g_ref[...]).astype(o_ref.dtype)


def _rmsnorm(x, g, out_dtype, tm=512):
    m, d = x.shape
    tm = min(tm, m)
    return pl.pallas_call(
        _rmsnorm_kernel,
        out_shape=jax.ShapeDtypeStruct((m, d), out_dtype),
        grid=(m // tm,),
        in_specs=[pl.BlockSpec((tm, d), lambda i: (i, 0)), pl.BlockSpec((1, d), lambda i: (0, 0))],
        out_specs=pl.BlockSpec((tm, d), lambda i: (i, 0)),
        compiler_params=_cp("parallel"),
        name="rmsnorm",
    )(x, g.reshape(1, d))


def _mm_f32_kernel(a_ref, b_ref, o_ref):
    def split(x):
        hi = x.astype(BF16)
        return hi, (x - hi.astype(F32)).astype(BF16)

    (a_hi, a_lo), (b_hi, b_lo) = split(a_ref[...]), split(b_ref[...])
    dot = functools.partial(jnp.dot, preferred_element_type=F32)
    o_ref[...] = dot(a_hi, b_hi) + (dot(a_hi, b_lo) + dot(a_lo, b_hi))


def _matmul_f32(a, b, tm, tn, name):
    m, k = a.shape
    n = b.shape[1]
    tm, tn = min(tm, m), min(tn, n)
    return pl.pallas_call(
        _mm_f32_kernel,
        out_shape=jax.ShapeDtypeStruct((m, n), F32),
        grid=(m // tm, n // tn),
        in_specs=[pl.BlockSpec((tm, k), lambda i, j: (i, 0)), pl.BlockSpec((k, tn), lambda i, j: (0, j))],
        out_specs=pl.BlockSpec((tm, tn), lambda i, j: (i, j)),
        compiler_params=_cp("parallel", "arbitrary"),
        name=name,
    )(a, b)


def _cast_weights_once(w_refs, wb_refs):
    @pl.when(pl.program_id(1) == 0)
    def _():
        for w_ref, wb_ref in zip(w_refs, wb_refs):
            wb_ref[...] = w_ref[...].astype(BF16)


def _mm_kernel(a_ref, w_ref, o_ref, wb_ref):
    _cast_weights_once((w_ref,), (wb_ref,))
    o_ref[...] = jnp.dot(a_ref[...], wb_ref[...], preferred_element_type=F32).astype(o_ref.dtype)


def _mm_res_kernel(a_ref, w_ref, r_ref, o_ref, wb_ref):
    _cast_weights_once((w_ref,), (wb_ref,))
    o_ref[...] = r_ref[...] + jnp.dot(a_ref[...], wb_ref[...], preferred_element_type=F32)


def _matmul(a, w, out_dtype, tm, tn, residual=None, layer=None, name="matmul"):
    m, k = a.shape
    n = w.shape[-1]
    tm, tn = min(tm, m), min(tn, n)
    in_specs = [pl.BlockSpec((tm, k), lambda j, i: (i, 0)), _weight_spec((k, tn), lambda j, i: (0, j), layer)]
    args = [a, w]
    body = _mm_kernel
    if residual is not None:
        body = _mm_res_kernel
        in_specs.append(pl.BlockSpec((tm, tn), lambda j, i: (i, j)))
        args.append(residual)
    return pl.pallas_call(
        body,
        out_shape=jax.ShapeDtypeStruct((m, n), out_dtype),
        grid=(n // tn, m // tm),
        in_specs=in_specs,
        out_specs=pl.BlockSpec((tm, tn), lambda j, i: (i, j)),
        scratch_shapes=[pltpu.VMEM((k, tn), BF16)],
        compiler_params=_cp("parallel", "arbitrary"),
        name=name,
    )(*args)


def _mm_res_norm_kernel(a_ref, w_ref, r_ref, g_ref, x_ref, h_ref, wb_ref):
    @pl.when(pl.program_id(0) == 0)
    def _():
        wb_ref[...] = w_ref[...].astype(BF16)

    x = r_ref[...] + jnp.dot(a_ref[...], wb_ref[...], preferred_element_type=F32)
    x_ref[...] = x
    ms = jnp.mean(x * x, axis=-1, keepdims=True)
    h_ref[...] = (x * lax.rsqrt(ms + NORM_EPS) * g_ref[...]).astype(h_ref.dtype)


def _matmul_res_norm(a, w, layer, residual, gain, tm=512):
    m, k = a.shape
    n = w.shape[-1]
    tm = min(tm, m)
    rows = lambda width: pl.BlockSpec((tm, width), lambda i: (i, 0))
    return pl.pallas_call(
        _mm_res_norm_kernel,
        out_shape=[jax.ShapeDtypeStruct((m, n), F32), jax.ShapeDtypeStruct((m, n), BF16)],
        grid=(m // tm,),
        in_specs=[rows(k), _weight_spec((k, n), lambda i: (0, 0), layer), rows(n), pl.BlockSpec((1, n), lambda i: (0, 0))],
        out_specs=[rows(n), rows(n)],
        scratch_shapes=[pltpu.VMEM((k, n), BF16)],
        compiler_params=_cp("arbitrary"),
        name="out_proj_norm",
    )(a, w, residual, gain.reshape(1, n))


def _rope_tables(seq):
    rows = seq // GRID_W
    row_idx = jnp.repeat(jnp.arange(rows, dtype=F32), GRID_W)
    col_idx = jnp.tile(jnp.arange(GRID_W, dtype=F32), rows)
    axis_dim = A_HEAD_DIM // 2
    inv_freq = ROPE_THETA ** (-jnp.arange(0, axis_dim, 2, dtype=F32) / axis_dim)
    ang_r = row_idx[:, None] * inv_freq[None, :]
    ang_c = col_idx[:, None] * inv_freq[None, :]
    ang = jnp.concatenate([ang_r, ang_r, ang_c, ang_c], axis=-1)
    return jnp.cos(ang), jnp.sin(ang)


def _qk_prep_kernel(x_ref, g_ref, cos_ref, sin_ref, o_ref):
    cos = cos_ref[...]
    sin = sin_ref[...]
    lane = lax.broadcasted_iota(jnp.int32, cos.shape, 1)
    quarter = A_HEAD_DIM // 4
    first = (lane % (2 * quarter)) < quarter
    for h in range(A_HEADS + A_KV_HEADS):
        cols = slice(h * A_HEAD_DIM, (h + 1) * A_HEAD_DIM)
        x = x_ref[:, cols]
        xn = x * lax.rsqrt(jnp.mean(x * x, axis=-1, keepdims=True) + NORM_EPS) * g_ref[h]
        rot = jnp.where(first, -pltpu.roll(xn, A_HEAD_DIM - quarter, 1), pltpu.roll(xn, quarter, 1))
        y = xn * cos + rot * sin
        if h < A_HEADS:
            y = y * (A_HEAD_DIM**-0.5 * LOG2_E)
        o_ref[:, cols] = y.astype(o_ref.dtype)


def _qk_prep(proj, gains, cos, sin, seq, tm=256):
    m = proj.shape[0]
    tm = min(tm, seq)
    width = (A_HEADS + A_KV_HEADS) * A_HEAD_DIM
    nt = seq // tm
    return pl.pallas_call(
        _qk_prep_kernel,
        out_shape=jax.ShapeDtypeStruct((m, width), BF16),
        grid=(m // tm,),
        in_specs=[
            pl.BlockSpec((tm, width), lambda i: (i, 0)),
            pl.BlockSpec(gains.shape, lambda i: (0, 0, 0)),
            pl.BlockSpec((tm, A_HEAD_DIM), lambda i: (i % nt, 0)),
            pl.BlockSpec((tm, A_HEAD_DIM), lambda i: (i % nt, 0)),
        ],
        out_specs=pl.BlockSpec((tm, width), lambda i: (i, 0)),
        compiler_params=_cp("parallel"),
        name="a_qk_prep",
    )(proj, gains, cos, sin)


class _OnlineSoftmax:
    def __init__(self, rows, width):
        self.m = jnp.full((rows, 1), -jnp.inf, F32)
        self.l = jnp.zeros((rows, 1), F32)
        self.acc = jnp.zeros((rows, width), F32)

    def add(self, s, v):
        m_new = jnp.maximum(self.m, jnp.max(s, axis=-1, keepdims=True))
        p = jnp.exp2(s - m_new)
        alpha = jnp.exp2(self.m - m_new)
        self.l = alpha * self.l + jnp.sum(p, axis=-1, keepdims=True)
        self.acc = alpha * self.acc + jnp.dot(p.astype(BF16), v, preferred_element_type=F32)
        self.m = m_new

    def result(self):
        return self.acc / self.l


def _attn_a_kernel(q_ref, k_ref, v_ref, o_ref, *, kc):
    q = q_ref[...]
    sm = _OnlineSoftmax(q.shape[0], A_HEAD_DIM)
    for c in range(k_ref.shape[0] // kc):
        keys = slice(c * kc, (c + 1) * kc)
        s = lax.dot_general(q, k_ref[keys, :], NT_DIMS, preferred_element_type=F32)
        sm.add(s, v_ref[keys, :].astype(BF16))
    o_ref[...] = sm.result().astype(o_ref.dtype)


def _attn_a(qk, proj, batch, seq, tq=1024, kc=512):
    tq = min(tq, seq)
    kc = min(kc, seq)
    nq = seq // tq
    group = A_HEADS // A_KV_HEADS
    k_blk = COL_AK // A_HEAD_DIM
    v_blk = COL_AV // A_HEAD_DIM
    return pl.pallas_call(
        functools.partial(_attn_a_kernel, kc=kc),
        out_shape=jax.ShapeDtypeStruct((batch * seq, A_WIDTH), BF16),
        grid=(batch, A_HEADS, nq),
        in_specs=[
            pl.BlockSpec((tq, A_HEAD_DIM), lambda b, h, i: (b * nq + i, h)),
            pl.BlockSpec((seq, A_HEAD_DIM), lambda b, h, i: (b, k_blk + h // group)),
            pl.BlockSpec((seq, A_HEAD_DIM), lambda b, h, i: (b, v_blk + h // group)),
        ],
        out_specs=pl.BlockSpec((tq, A_HEAD_DIM), lambda b, h, i: (b * nq + i, h)),
        compiler_params=_cp("parallel", "arbitrary", "arbitrary"),
        name="a_attention",
    )(qk, qk, proj)


def _attn_d_kernel(q_ref, k_ref, v_ref, slope_ref, lam_ref, g_ref, o_ref, *, tq, kc, lam_init):
    q = q_ref[...] * (D_HEAD_DIM**-0.5 * LOG2_E)
    lane = lax.broadcasted_iota(jnp.int32, q.shape, 1)
    q_maps = (jnp.where(lane < D_HEAD_DIM, q, 0.0).astype(BF16), jnp.where(lane >= D_HEAD_DIM, q, 0.0).astype(BF16))
    slope = slope_ref[0][:, 0:1] * LOG2_E
    rel = pl.program_id(2) * tq + lax.broadcasted_iota(jnp.int32, (tq, kc), 0) - lax.broadcasted_iota(jnp.int32, (tq, kc), 1)
    rel = slope * rel.astype(F32)
    maps = (_OnlineSoftmax(tq, D_V_DIM), _OnlineSoftmax(tq, D_V_DIM))
    for c in range(k_ref.shape[0] // kc):
        keys = slice(c * kc, (c + 1) * kc)
        k = k_ref[keys, :].astype(BF16)
        v = v_ref[keys, :].astype(BF16)
        bias = jnp.abs(rel - slope * float(c * kc))
        for q_map, sm in zip(q_maps, maps):
            sm.add(lax.dot_general(q_map, k, NT_DIMS, preferred_element_type=F32) - bias, v)
    lam_v = lam_ref[...]
    lam = (
        jnp.exp(jnp.sum(lam_v[0:1] * lam_v[1:2], axis=-1, keepdims=True))
        - jnp.exp(jnp.sum(lam_v[2:3] * lam_v[3:4], axis=-1, keepdims=True))
        + lam_init
    )
    o = maps[0].result() - lam * maps[1].result()
    o = o * lax.rsqrt(jnp.mean(o * o, axis=-1, keepdims=True) + NORM_EPS) * g_ref[...]
    o_ref[...] = (o * (1.0 - lam_init)).astype(o_ref.dtype)


def _attn_d(proj, lam_vecs, subln, lam_init, batch, seq, tq=1024, kc=512):
    tq = min(tq, seq)
    kc = min(kc, seq)
    nq = seq // tq
    slopes = 2.0 ** (-8.0 * np.arange(1, D_HEADS + 1, dtype=np.float32) / D_HEADS)
    slopes = jnp.asarray(np.broadcast_to(slopes[:, None, None], (D_HEADS, 1, 128)).astype(np.float32))
    q_blk, k_blk, v_blk = COL_DQ // D_V_DIM, COL_DK // D_V_DIM, COL_DV // D_V_DIM
    return pl.pallas_call(
        functools.partial(_attn_d_kernel, tq=tq, kc=kc, lam_init=lam_init),
        out_shape=jax.ShapeDtypeStruct((batch * seq, D_WIDTH), BF16),
        grid=(batch, D_HEADS, nq),
        in_specs=[
            pl.BlockSpec((tq, D_V_DIM), lambda b, h, i: (b * nq + i, q_blk + h)),
            pl.BlockSpec((seq, D_V_DIM), lambda b, h, i: (b, k_blk + h)),
            pl.BlockSpec((seq, D_V_DIM), lambda b, h, i: (b, v_blk + h)),
            pl.BlockSpec((1, 1, 128), lambda b, h, i: (h, 0, 0)),
            pl.BlockSpec((4, D_HEAD_DIM), lambda b, h, i: (0, 0)),
            pl.BlockSpec((1, D_V_DIM), lambda b, h, i: (0, 0)),
        ],
        out_specs=pl.BlockSpec((tq, D_V_DIM), lambda b, h, i: (b * nq + i, h)),
        compiler_params=_cp("parallel", "arbitrary", "arbitrary"),
        name="d_attention",
    )(proj, proj, proj, slopes, lam_vecs, subln.reshape(1, D_V_DIM))


DFT_ROW_SPLIT = 64


def _dft_tables(seq):
    n = 2 * seq
    blk = min(DFT_ROW_SPLIT, seq)
    col = jnp.arange(seq, dtype=jnp.int32)[None, :]
    hi = jnp.arange(seq // blk, dtype=jnp.int32)[:, None]
    lo = jnp.arange(blk, dtype=jnp.int32)[:, None]

    def cos_sin(index, period):
        ang = (index % period).astype(F32) * (2.0 * math.pi / period)
        return jnp.cos(ang), jnp.sin(ang)

    def tables(col_term, period):
        (ca, sa), (cb, sb) = cos_sin(2 * blk * hi * col_term, period), cos_sin((2 * lo + 1) * col_term, period)
        ca, sa, cb, sb = ca[:, None, :], sa[:, None, :], cb[None], sb[None]
        return (ca * cb - sa * sb).reshape(seq, seq), (sa * cb + ca * sb).reshape(seq, seq)

    cos_half, sin_half = tables(2 * col + 1, 4 * n)
    cos_int, sin_int = tables(col, 2 * n)
    return cos_half.astype(BF16), sin_half.astype(BF16), cos_int, sin_int


def _filter_features(seq):
    t = jnp.linspace(0.0, 1.0, seq, dtype=F32)[:, None]
    n_bands = (B_EMB_DIM - 1) // 2
    bands = jnp.linspace(1e-4, n_bands - 1, n_bands, dtype=F32)[None, :]
    ang = (2.0 * math.pi / seq) * jnp.arange(seq, dtype=F32)[:, None] * bands
    z = jnp.concatenate([t, jnp.cos(ang), -jnp.sin(ang)], axis=-1)
    z = jnp.pad(z, ((0, 0), (0, B_FILTER_HIDDEN - B_EMB_DIM)))
    max_decay = math.log(B_DECAY_TARGET) / B_FAST_DECAY_PCT
    min_decay = math.log(B_DECAY_TARGET) / B_SLOW_DECAY_PCT
    deltas = jnp.abs(jnp.linspace(min_decay, max_decay, B_WIDTH, dtype=F32))[None, :]
    return z, t, deltas


def _filter_kernel(z_ref, t_ref, dl_ref, w1_ref, b1_ref, w2_ref, b2_ref, w3_ref, b3_ref, w4_ref, fr_ref, hs_ref, hd_ref):
    fr = fr_ref[...]
    hid = jnp.sin(fr * (jnp.dot(z_ref[...], w1_ref[...], preferred_element_type=F32, precision=HIGHEST) + b1_ref[...]))
    hid = jnp.sin(fr * (jnp.dot(hid, w2_ref[...], preferred_element_type=F32, precision=HIGHEST) + b2_ref[...]))
    hid = jnp.sin(fr * (jnp.dot(hid, w3_ref[...], preferred_element_type=F32, precision=HIGHEST) + b3_ref[...]))
    h = jnp.dot(hid, w4_ref[...], preferred_element_type=F32, precision=HIGHEST)
    window = jnp.exp(-t_ref[...] * dl_ref[...])
    h_fwd = h[:, :B_WIDTH] * window
    h_bwd = h[:, B_WIDTH:] * window
    row = lax.broadcasted_iota(jnp.int32, h_bwd.shape, 0)
    h_bwd = jnp.where(row == 0, 0.0, h_bwd)
    norm = jnp.sum(jnp.abs(h_fwd), axis=0, keepdims=True) + jnp.sum(jnp.abs(h_bwd), axis=0, keepdims=True)
    seq = h.shape[0]
    inv_n = 1.0 / seq
    hs_ref[...] = (h_fwd + h_bwd) / norm * inv_n
    hd_ref[...] = (h_bwd - h_fwd) / norm * inv_n


def _hyena_filter_spectrum(seq, feats, cos_int, sin_int, w1, b1, w2, b2, w3, b3, w4, freq):
    z, t, deltas = feats
    w1p = jnp.pad(w1, ((0, B_FILTER_HIDDEN - B_EMB_DIM), (0, 0)))
    row = lambda v: v.reshape(1, -1)
    args = (z, t, deltas, w1p, row(b1), w2, row(b2), w3, row(b3), w4, row(freq))
    h_sum, h_diff = pl.pallas_call(
        _filter_kernel,
        out_shape=[jax.ShapeDtypeStruct((seq, B_WIDTH), F32)] * 2,
        in_specs=[pl.BlockSpec(a.shape, lambda: (0, 0)) for a in args],
        out_specs=[pl.BlockSpec((seq, B_WIDTH), lambda: (0, 0))] * 2,
        compiler_params=pltpu.CompilerParams(vmem_limit_bytes=VMEM_LIMIT_V7X),
        name="b_filter",
    )(*args)
    k_re = _matmul_f32(cos_int, h_sum, 256, B_WIDTH, name="b_filter_dft_re")
    k_im = _matmul_f32(sin_int, h_diff, 256, B_WIDTH, name="b_filter_dft_im")
    return k_re, k_im


def _hyena_kernel(v_ref, x1_ref, x0_ref, wv_ref, w1_ref, w0_ref, bv_ref, b1_ref, b0_ref, skip_ref, c_ref, s_ref, kre_ref, kim_ref, o_ref,
                  *, f_blk):
    seq = v_ref.shape[0]
    row = lax.broadcasted_iota(jnp.int32, v_ref.shape, 0)

    def conv3(u_ref, w_ref, b_ref):
        u = u_ref[...]
        w = w_ref[...]
        u_prev = jnp.where(row == 0, 0.0, pltpu.roll(u, 1, 0))
        u_next = jnp.where(row == seq - 1, 0.0, pltpu.roll(u, seq - 1, 0))
        return w[0:1] * u_prev + w[1:2] * u + w[2:3] * u_next + b_ref[...]

    z = conv3(v_ref, wv_ref, bv_ref) * conv3(x1_ref, w1_ref, b1_ref)
    zb = z.astype(BF16)
    y = z * skip_ref[...]
    for f0 in range(0, seq, f_blk):
        fs = slice(f0, f0 + f_blk)
        cz = jnp.dot(c_ref[fs, :], zb, preferred_element_type=F32)
        sz = jnp.dot(s_ref[fs, :], zb, preferred_element_type=F32)
        k_re = kre_ref[fs, :]
        k_im = kim_ref[fs, :]
        y_re = (cz * k_re + sz * k_im).astype(BF16)
        y_im = (cz * k_im - sz * k_re).astype(BF16)
        y = y + (jnp.dot(c_ref[:, fs], y_re, preferred_element_type=F32) - jnp.dot(s_ref[:, fs], y_im, preferred_element_type=F32))
    o_ref[...] = (y * conv3(x0_ref, w0_ref, b0_ref)).astype(o_ref.dtype)


def _hyena(proj, conv_w, conv_b, skip, cos_half, sin_half, k_re, k_im, batch, seq, cb=256, f_blk=1024):
    ncb = B_WIDTH // cb
    u_blk = COL_BU // cb
    f_blk = min(f_blk, seq)

    def u_spec(part):
        return pl.BlockSpec((seq, cb), lambda j, b: (b, u_blk + part * ncb + j))

    def w_spec(rows, part):
        return pl.BlockSpec((rows, cb), lambda j, b: (0, part * ncb + j))

    k_spec = pl.BlockSpec((seq, cb), lambda j, b: (0, j), pipeline_mode=pl.Buffered(1))
    return pl.pallas_call(
        functools.partial(_hyena_kernel, f_blk=f_blk),
        out_shape=jax.ShapeDtypeStruct((batch * seq, B_WIDTH), BF16),
        grid=(ncb, batch),
        in_specs=[
            u_spec(0), u_spec(1), u_spec(2),
            w_spec(3, 0), w_spec(3, 1), w_spec(3, 2),
            w_spec(1, 0), w_spec(1, 1), w_spec(1, 2),
            pl.BlockSpec((1, cb), lambda j, b: (0, j)),
            _const_spec((seq, seq)), _const_spec((seq, seq)),
            k_spec, k_spec,
        ],
        out_specs=pl.BlockSpec((seq, cb), lambda j, b: (b, j)),
        compiler_params=_cp("parallel", "arbitrary"),
        name="b_hyena",
    )(proj, proj, proj, conv_w, conv_w, conv_w, conv_b.reshape(1, -1), conv_b.reshape(1, -1), conv_b.reshape(1, -1),
      skip.reshape(1, -1), cos_half, sin_half, k_re, k_im)


def _rwkv_prep_kernel(cur_ref, lo_ref, pcur_ref, plo_ref, ncur_ref, nlo_ref, mu_ref, mulo_ref, w0_ref, wup_ref, a0_ref, aup_ref,
                      kk_ref, ka_ref, o_ref):
    first = pl.program_id(1) == 0
    last = pl.program_id(1) == pl.num_programs(1) - 1
    halo = pcur_ref.shape[0]
    tm = cur_ref.shape[0]

    def neighbours(x_ref, p_ref, n_ref):
        x = x_ref[...]
        row = lax.broadcasted_iota(jnp.int32, x.shape, 0)
        p_row = jnp.where(first, 0.0, p_ref[halo - 1:halo, :])
        n_row = jnp.where(last, 0.0, n_ref[0:1, :])
        prev = jnp.where(row == 0, p_row, pltpu.roll(x, 1, 0))
        nxt = jnp.where(row == tm - 1, n_row, pltpu.roll(x, tm - 1, 0))
        return x, (prev, nxt)

    cur, cur_sh = neighbours(cur_ref, pcur_ref, ncur_ref)
    lo, lo_sh = neighbours(lo_ref, plo_ref, nlo_ref)
    k_k = kk_ref[...]
    k_a = ka_ref[...]
    for d in range(2):
        f = cur + (cur_sh[d] - cur) * mu_ref[d]
        f_lo = lo + (lo_sh[d] - lo) * mulo_ref[d]
        r = f[:, :C_WIDTH]
        k = f[:, C_WIDTH:2 * C_WIDTH]
        v = f[:, 2 * C_WIDTH:]
        w_lo = f_lo[:, :C_LORA_PAD]
        a_lo = f_lo[:, C_LORA_PAD:]
        x = w0_ref[d] + jnp.dot(jnp.tanh(w_lo), wup_ref[d], preferred_element_type=F32, precision=HIGHEST)
        w = -(jnp.maximum(-x, 0.0) + jnp.log(1.0 + jnp.exp(-jnp.abs(x)))) - 0.5
        decay = jnp.exp(-jnp.exp(w))
        a = jax.nn.sigmoid(a0_ref[d] + jnp.dot(a_lo, aup_ref[d], preferred_element_type=F32, precision=HIGHEST))
        transposed = lambda val: val.T.reshape(C_HEAD_DIM, C_HEADS, tm)
        kk = transposed(k * k_k)
        kk = kk / jnp.maximum(jnp.sqrt(jnp.sum(kk * kk, axis=0, keepdims=True)), 1e-12)
        for q, val in enumerate((r, decay, k * (1.0 + (a - 1.0) * k_a), v)):
            o_ref[d, q] = transposed(val)
        o_ref[d, 4] = -kk
        o_ref[d, 5] = kk * transposed(a)


def _rwkv_prep(proj, mu, mu_lo, w0, w_up, a0, a_up, k_k, k_a, batch, seq, tm=256):
    tm = min(tm, seq)
    nt = seq // tm
    halo = 8
    hb = tm // halo
    n_hblk = batch * seq // halo
    wide, narrow = 3 * C_WIDTH, 2 * C_LORA_PAD
    cur_blk, lo_blk = COL_CF // wide, COL_CLO // narrow

    def cur_map(blk):
        return lambda b, j: (b * nt + j, blk)

    def prev_map(blk):
        return lambda b, j: (jnp.maximum((b * nt + j) * hb - 1, 0), blk)

    def next_map(blk):
        return lambda b, j: (jnp.minimum((b * nt + j + 1) * hb, n_hblk - 1), blk)

    small = lambda a: pl.BlockSpec(a.shape, lambda b, j: (0,) * a.ndim)
    consts = (mu, mu_lo, w0, w_up, a0, a_up, k_k, k_a)
    return pl.pallas_call(
        _rwkv_prep_kernel,
        out_shape=jax.ShapeDtypeStruct((2, 6, C_HEAD_DIM, batch, C_HEADS, seq), F32),
        grid=(batch, nt),
        in_specs=[
            pl.BlockSpec((tm, wide), cur_map(cur_blk)),
            pl.BlockSpec((tm, narrow), cur_map(lo_blk)),
            pl.BlockSpec((halo, wide), prev_map(cur_blk)),
            pl.BlockSpec((halo, narrow), prev_map(lo_blk)),
            pl.BlockSpec((halo, wide), next_map(cur_blk)),
            pl.BlockSpec((halo, narrow), next_map(lo_blk)),
        ] + [small(a) for a in consts],
        out_specs=pl.BlockSpec((2, 6, C_HEAD_DIM, None, C_HEADS, tm), lambda b, j: (0, 0, 0, b, 0, j)),
        compiler_params=_cp("parallel", "arbitrary"),
        name="c_prep",
    )(proj, proj, proj, proj, proj, proj, *consts)


RELAYOUT_T = 128


def _rwkv_relayout_kernel(pf_ref, pb_ref, o_ref, *, tc):
    n = pf_ref.shape[0]
    t_blk = pf_ref.shape[-1]
    mirror = t_blk - 1 - lax.broadcasted_iota(jnp.int32, (pb_ref.shape[1] * pb_ref.shape[2], t_blk), 1)
    for k in range(n):
        bwd = jnp.take_along_axis(pb_ref[k].reshape(-1, t_blk), mirror, axis=1)
        cols = jnp.concatenate([pf_ref[k].reshape(-1, t_blk), bwd], axis=0).T
        for c in range(t_blk // tc):
            o_ref[c, k] = cols[c * tc:(c + 1) * tc]


def _rwkv_relayout(p, tc):
    _, nq, n, batch, heads, seq = p.shape
    t_blk = min(RELAYOUT_T, seq)
    nb = seq // t_blk
    lanes = 2 * batch * heads
    per_blk = t_blk // tc
    return pl.pallas_call(
        functools.partial(_rwkv_relayout_kernel, tc=tc),
        out_shape=jax.ShapeDtypeStruct((seq // tc, nq, n, tc, lanes), F32),
        grid=(nb, nq),
        in_specs=[
            pl.BlockSpec((None, None, n, batch, heads, t_blk), lambda c, q: (0, q, 0, 0, 0, c)),
            pl.BlockSpec((None, None, n, batch, heads, t_blk), lambda c, q: (1, q, 0, 0, 0, nb - 1 - c)),
        ],
        out_specs=pl.BlockSpec((per_blk, None, n, tc, lanes), lambda c, q: (c, q, 0, 0, 0)),
        compiler_params=_cp("parallel", "arbitrary"),
        name="c_relayout",
    )(p, p)


def _rwkv_scan_kernel(x_ref, rk_ref, lnw_ref, lnb_ref, out_ref, s_ref, o_ref, *, tc, k_chunk):
    n = s_ref.shape[0]
    lanes = s_ref.shape[2]

    @pl.when(pl.program_id(0) == 0)
    def _():
        s_ref[...] = jnp.zeros_like(s_ref)

    def row(q, k, t):
        return x_ref[pl.ds((q * n + k) * tc + t, 1), :]

    def step(t, carry):
        vv = x_ref[pl.ds(3 * n * tc + t, n, stride=tc), :]

        def sa_chunk(c, sa):
            base = pl.multiple_of(c * k_chunk, k_chunk)
            for j in range(k_chunk):
                sa = sa + s_ref[base + j] * row(4, base + j, t)
            return sa

        sa = lax.fori_loop(0, n // k_chunk, sa_chunk, jnp.zeros((n, lanes), F32))

        def update_chunk(c, o):
            base = pl.multiple_of(c * k_chunk, k_chunk)
            for j in range(k_chunk):
                k = base + j
                sk = s_ref[k] * row(1, k, t) + sa * row(5, k, t) + vv * row(2, k, t)
                s_ref[k] = sk
                o = o + sk * row(0, k, t)
            return o

        o_ref[pl.ds(t, n, stride=tc), :] = lax.fori_loop(0, n // k_chunk, update_chunk, jnp.zeros((n, lanes), F32))
        return carry

    lax.fori_loop(0, tc, step, 0)

    quantity = lambda q: x_ref[q * n * tc:(q + 1) * n * tc, :].reshape(n, tc, lanes)
    o = o_ref[...].reshape(n, tc, lanes)
    mean = jnp.mean(o, axis=0, keepdims=True)
    var = jnp.mean(jnp.square(o - mean), axis=0, keepdims=True)
    o = (o - mean) * lax.rsqrt(var + C_GN_EPS) * lnw_ref[...] + lnb_ref[...]
    bonus = jnp.sum(quantity(0) * quantity(2) * rk_ref[...], axis=0, keepdims=True)
    out_ref[...] = o + bonus * quantity(3)


def _rwkv_scan(xs, rk, lnw, lnb, k_chunk=32):
    nc, nq, n, tc, lanes = xs.shape
    tile = lambda: pl.BlockSpec((n, 1, lanes), lambda i: (0, 0, 0))
    return pl.pallas_call(
        functools.partial(_rwkv_scan_kernel, tc=tc, k_chunk=min(k_chunk, n)),
        out_shape=jax.ShapeDtypeStruct((n, nc * tc, lanes), F32),
        grid=(nc,),
        in_specs=[pl.BlockSpec((nq * n * tc, lanes), lambda i: (i, 0)), tile(), tile(), tile()],
        out_specs=pl.BlockSpec((n, tc, lanes), lambda i: (0, i, 0)),
        scratch_shapes=[pltpu.VMEM((n, n, lanes), F32), pltpu.VMEM((n * tc, lanes), F32)],
        compiler_params=_cp("arbitrary"),
        name="c_scan",
    )(xs.reshape(nc * nq * n * tc, lanes), rk, lnw, lnb)


def _rwkv_unlayout_kernel(o_ref, q_ref):
    for v in range(o_ref.shape[0]):
        rows = o_ref[v].T
        q_ref[:, v] = rows.reshape(q_ref.shape[0], q_ref.shape[2], q_ref.shape[3])


def _rwkv_unlayout(o, batch):
    n, seq, lanes = o.shape
    t_blk = min(RELAYOUT_T, seq)
    heads = lanes // (2 * batch)
    return pl.pallas_call(
        _rwkv_unlayout_kernel,
        out_shape=jax.ShapeDtypeStruct((2 * batch, n, heads, seq), F32),
        grid=(seq // t_blk,),
        in_specs=[pl.BlockSpec((n, t_blk, lanes), lambda c: (0, c, 0))],
        out_specs=pl.BlockSpec((2 * batch, n, heads, t_blk), lambda c: (0, 0, 0, c)),
        compiler_params=_cp("parallel"),
        name="c_unlayout",
    )(o)


def _rwkv_out_kernel(qf_ref, qb_ref, glo_ref, gup_ref, y_ref):
    t_blk = qf_ref.shape[-1]
    fwd = qf_ref[...].reshape(-1, t_blk)
    bwd = qb_ref[...].reshape(-1, t_blk)
    mirror = t_blk - 1 - lax.broadcasted_iota(jnp.int32, bwd.shape, 1)
    o = (fwd + jnp.take_along_axis(bwd, mirror, axis=1)).T
    g = jnp.dot(jax.nn.sigmoid(glo_ref[...]).astype(BF16), gup_ref[...], preferred_element_type=F32)
    y_ref[...] = (o * g).astype(y_ref.dtype)


def _rwkv_out(q, proj, g_up, batch, seq):
    _, n, heads, _ = q.shape
    t_blk = min(RELAYOUT_T, seq)
    nt = seq // t_blk
    g_blk = COL_CG // C_GATE_LORA
    return pl.pallas_call(
        _rwkv_out_kernel,
        out_shape=jax.ShapeDtypeStruct((batch * seq, C_WIDTH), BF16),
        grid=(batch, nt),
        in_specs=[
            pl.BlockSpec((None, n, heads, t_blk), lambda b, j: (b, 0, 0, j)),
            pl.BlockSpec((None, n, heads, t_blk), lambda b, j: (batch + b, 0, 0, nt - 1 - j)),
            pl.BlockSpec((t_blk, C_GATE_LORA), lambda b, j: (b * nt + j, g_blk)),
            pl.BlockSpec((C_GATE_LORA, C_WIDTH), lambda b, j: (0, 0)),
        ],
        out_specs=pl.BlockSpec((t_blk, C_WIDTH), lambda b, j: (b * nt + j, 0)),
        compiler_params=_cp("parallel", "arbitrary"),
        name="c_out",
    )(q, q, proj, g_up)


SCAN_TC = 16


def _heads_minor(p):
    return p.reshape(p.shape[:-1] + (C_HEADS, C_HEAD_DIM)).swapaxes(-1, -2).reshape(p.shape)


def _rwkv(proj, mu, w0, w_up, a0, a_up, g_up, k_k, k_a, r_k, ln_w, ln_b, batch, seq):
    pad_lo = C_LORA_PAD - C_LORA
    split = 3 * C_WIDTH
    mu_wide = jnp.concatenate([_heads_minor(mu[:, i * C_WIDTH:(i + 1) * C_WIDTH]) for i in range(3)], axis=-1)[:, None, :]
    mu_lo = jnp.concatenate([
        jnp.pad(mu[:, split:split + C_LORA], ((0, 0), (0, pad_lo))),
        jnp.pad(mu[:, split + C_LORA:], ((0, 0), (0, pad_lo))),
    ], axis=-1)[:, None, :]
    w_up_p = jnp.pad(_heads_minor(w_up), ((0, 0), (0, pad_lo), (0, 0)))
    a_up_p = jnp.pad(_heads_minor(a_up), ((0, 0), (0, pad_lo), (0, 0)))
    feats = _rwkv_prep(proj, mu_wide, mu_lo, _heads_minor(w0)[:, None, :], w_up_p, _heads_minor(a0)[:, None, :], a_up_p,
                       _heads_minor(k_k).reshape(1, C_WIDTH), _heads_minor(k_a).reshape(1, C_WIDTH), batch, seq)
    xs = _rwkv_relayout(feats, min(SCAN_TC, seq))
    per_lane = lambda p: jnp.tile(p.reshape(C_HEADS, C_HEAD_DIM).T, (1, 2 * batch))[:, None, :]
    o = _rwkv_scan(xs, per_lane(r_k), per_lane(ln_w), per_lane(ln_b))
    return _rwkv_out(_rwkv_unlayout(o, batch), proj, _heads_minor(g_up).astype(BF16), batch, seq)


def _merge_kernel(h_ref, wg_ref, wbr_ref, oa_ref, ob_ref, oc_ref, od_ref, o_ref, wgb_ref, wbrb_ref):
    _cast_weights_once((wg_ref, wbr_ref), (wgb_ref, wbrb_ref))
    h = h_ref[...]
    acc = None
    row = 0
    for i, b_ref in enumerate((oa_ref, ob_ref, oc_ref, od_ref)):
        width = b_ref.shape[1]
        gate = jax.nn.sigmoid(jnp.dot(h, wgb_ref[i], preferred_element_type=F32))
        term = gate * jnp.dot(b_ref[...], wbrb_ref[row:row + width, :], preferred_element_type=F32)
        acc = term if acc is None else acc + term
        row += width
    o_ref[...] = acc.astype(o_ref.dtype)


def _merge(h, w_gate, layer, w_branch, branches, tm=512, tn=512):
    m = h.shape[0]
    tm = min(tm, m)
    d_mix = w_branch.shape[0]
    row = lambda a: pl.BlockSpec((tm, a.shape[1]), lambda j, i: (i, 0))
    return pl.pallas_call(
        _merge_kernel,
        out_shape=jax.ShapeDtypeStruct((m, D_MODEL), BF16),
        grid=(D_MODEL // tn, m // tm),
        in_specs=[
            row(h),
            _weight_spec((4, D_MODEL, tn), lambda j, i: (0, 0, j), layer),
            _weight_spec((d_mix, tn), lambda j, i: (0, j)),
        ] + [row(b) for b in branches],
        out_specs=pl.BlockSpec((tm, tn), lambda j, i: (i, j)),
        scratch_shapes=[pltpu.VMEM((4, D_MODEL, tn), BF16), pltpu.VMEM((d_mix, tn), BF16)],
        compiler_params=_cp("parallel", "arbitrary"),
        name="gated_merge",
    )(h, w_gate, w_branch, *branches)


def _ffn_up_kernel(h_ref, wg_ref, wu_ref, o_ref, wgb_ref, wub_ref):
    _cast_weights_once((wg_ref, wu_ref), (wgb_ref, wub_ref))
    h = h_ref[...]
    g = jnp.dot(h, wgb_ref[...], preferred_element_type=F32)
    u = jnp.dot(h, wub_ref[...], preferred_element_type=F32)
    o_ref[...] = (g * jax.nn.sigmoid(g) * u).astype(o_ref.dtype)


def _ffn_up(h, w_gate, w_up, layer, tm=1024, tn=512):
    m, k = h.shape
    n = w_gate.shape[-1]
    tm = min(tm, m)
    return pl.pallas_call(
        _ffn_up_kernel,
        out_shape=jax.ShapeDtypeStruct((m, n), BF16),
        grid=(n // tn, m // tm),
        in_specs=[
            pl.BlockSpec((tm, k), lambda j, i: (i, 0)),
            _weight_spec((k, tn), lambda j, i: (0, j), layer),
            _weight_spec((k, tn), lambda j, i: (0, j), layer),
        ],
        out_specs=pl.BlockSpec((tm, tn), lambda j, i: (i, j)),
        scratch_shapes=[pltpu.VMEM((k, tn), BF16), pltpu.VMEM((k, tn), BF16)],
        compiler_params=_cp("parallel", "arbitrary"),
        name="ffn_up",
    )(h, w_gate, w_up)


def _pad_w_in(w):
    pad = ((0, 0), (0, C_LORA_PAD - C_LORA))
    lo = COL_CLO
    cf = [_heads_minor(w[:, COL_CF + i * C_WIDTH:COL_CF + (i + 1) * C_WIDTH]) for i in range(3)]
    return jnp.concatenate([
        w[:, :COL_CF],
        *cf,
        jnp.pad(w[:, lo:lo + C_LORA], pad),
        jnp.pad(w[:, lo + C_LORA:lo + 2 * C_LORA], pad),
        w[:, lo + 2 * C_LORA:],
    ], axis=1)


def kernel(x, norm_mix, w_in, a_q_norm, a_k_norm, b_conv_w, b_conv_b, b_filt_w1, b_filt_b1, b_filt_w2, b_filt_b2, b_filt_w3, b_filt_b3, b_filt_w4, b_filt_freq, b_skip, c_mu, c_w0, c_w_up, c_a0, c_a_up, c_g_up, c_k_k, c_k_a, c_r_k, c_ln_w, c_ln_b, d_lq1, d_lk1, d_lq2, d_lk2, d_subln, w_gate, w_branch, w_out, norm_ffn, w_ff_gate, w_ff_up, w_ff_down, norm_final):
    batch, seq, _ = x.shape
    m = batch * seq
    cos, sin = _rope_tables(seq)
    cos_half, sin_half, cos_int, sin_int = _dft_tables(seq)
    filt_feats = _filter_features(seq)
    x = x.reshape(m, D_MODEL)
    for l in range(DEPTH):
        h = _rmsnorm(x, norm_mix[l], BF16)
        proj = _matmul(h, _pad_w_in(w_in[l]), F32, 512, D_IN_PAD // 4, name="in_proj")

        gains = jnp.concatenate([
            jnp.broadcast_to(a_q_norm[l], (A_HEADS, A_HEAD_DIM)),
            jnp.broadcast_to(a_k_norm[l], (A_KV_HEADS, A_HEAD_DIM)),
        ])[:, None, :]
        o_a = _attn_a(_qk_prep(proj, gains, cos, sin, seq), proj, batch, seq)

        k_re, k_im = _hyena_filter_spectrum(seq, filt_feats, cos_int, sin_int, b_filt_w1[l], b_filt_b1[l], b_filt_w2[l],
                                            b_filt_b2[l], b_filt_w3[l], b_filt_b3[l], b_filt_w4[l], b_filt_freq[l])
        o_b = _hyena(proj, b_conv_w[l], b_conv_b[l], b_skip[l], cos_half, sin_half, k_re, k_im, batch, seq)

        o_c = _rwkv(proj, c_mu[l], c_w0[l], c_w_up[l], c_a0[l], c_a_up[l], c_g_up[l], c_k_k[l], c_k_a[l], c_r_k[l],
                    c_ln_w[l], c_ln_b[l], batch, seq)

        lam_init = 0.8 - 0.6 * math.exp(-0.3 * l)
        lam_vecs = jnp.stack([d_lq1[l], d_lk1[l], d_lq2[l], d_lk2[l]])
        o_d = _attn_d(proj, lam_vecs, d_subln[l], lam_init, batch, seq)

        c_lo, c_hi = A_WIDTH + B_WIDTH, A_WIDTH + B_WIDTH + C_WIDTH
        w_br = jnp.concatenate([w_branch[l, :c_lo], _heads_minor(w_branch[l, c_lo:c_hi].T).T, w_branch[l, c_hi:]])
        merged = _merge(h, w_gate, l, w_br, (o_a, o_b, o_c, o_d))
        x, h2 = _matmul_res_norm(merged, w_out, l, x, norm_ffn[l])
        mid = _ffn_up(h2, w_ff_gate, w_ff_up, l)
        x = _matmul(mid, w_ff_down, F32, 256, 1024, residual=x, layer=l, name="ffn_down")
    return _rmsnorm(x, norm_final, F32).reshape(batch, seq, D_MODEL)
```

```python
import functools
import math

import jax
import jax.numpy as jnp
import numpy as np
from jax import lax
from jax.experimental import pallas as pl
from jax.experimental.pallas import tpu as pltpu

D_MODEL = 2048
DEPTH = 2
GRID_W = 64
NORM_EPS = 1e-6

A_HEADS = 8
A_KV_HEADS = 2
A_HEAD_DIM = 128
A_WIDTH = A_HEADS * A_HEAD_DIM
ROPE_THETA = 10000.0

B_WIDTH = 512
B_EMB_DIM = 33
B_FILTER_HIDDEN = 64
B_DECAY_TARGET = 1e-2
B_FAST_DECAY_PCT = 0.3
B_SLOW_DECAY_PCT = 1.5

C_HEADS = 8
C_HEAD_DIM = 64
C_WIDTH = C_HEADS * C_HEAD_DIM
C_LORA = 96
C_LORA_PAD = 128
C_GATE_LORA = 256
C_GN_EPS = 64e-5

D_HEADS = 4
D_HEAD_DIM = 64
D_V_DIM = 2 * D_HEAD_DIM
D_WIDTH = D_HEADS * D_V_DIM

FFN_HIDDEN = -(-8 * D_MODEL // (3 * 256)) * 256

COL_AQ = 0
COL_AK = COL_AQ + A_WIDTH
COL_AV = COL_AK + A_KV_HEADS * A_HEAD_DIM
COL_BU = COL_AV + A_KV_HEADS * A_HEAD_DIM
COL_CF = COL_BU + 3 * B_WIDTH
COL_CLO = COL_CF + 3 * C_WIDTH
COL_CG = COL_CLO + 2 * C_LORA_PAD
COL_DQ = COL_CG + C_GATE_LORA
COL_DK = COL_DQ + 2 * D_HEADS * D_HEAD_DIM
COL_DV = COL_DK + 2 * D_HEADS * D_HEAD_DIM
D_IN_PAD = COL_DV + D_WIDTH

VMEM_LIMIT_V7X = 56 * 1024 * 1024
F32 = jnp.float32
BF16 = jnp.bfloat16
HIGHEST = lax.Precision.HIGHEST
NT_DIMS = (((1,), (1,)), ((), ()))
LOG2_E = math.log2(math.e)


def _cp(*sem):
    return pltpu.CompilerParams(dimension_semantics=sem, vmem_limit_bytes=VMEM_LIMIT_V7X)


def _const_spec(shape):
    return pl.BlockSpec(shape, lambda *_: (0,) * len(shape), pipeline_mode=pl.Buffered(1))


SINGLE_BUFFER_BYTES = 8 * 1024 * 1024


def _weight_spec(shape, index_map, layer=None):
    mode = {"pipeline_mode": pl.Buffered(1)} if 4 * math.prod(shape) > SINGLE_BUFFER_BYTES else {}
    if layer is None:
        return pl.BlockSpec(shape, index_map, **mode)
    return pl.BlockSpec((None,) + tuple(shape), lambda *g: (layer,) + tuple(index_map(*g)), **mode)


def _rmsnorm_kernel(x_ref, g_ref, o_ref):
    x = x_ref[...]
    ms = jnp.mean(x * x, axis=-1, keepdims=True)
    o_ref[...] = (x * lax.rsqrt(ms + NORM_EPS) * g_ref[...]).astype(o_ref.dtype)


def _rmsnorm(x, g, out_dtype, tm=512):
    m, d = x.shape
    tm = min(tm, m)
    return pl.pallas_call(
        _rmsnorm_kernel,
        out_shape=jax.ShapeDtypeStruct((m, d), out_dtype),
        grid=(m // tm,),
        in_specs=[pl.BlockSpec((tm, d), lambda i: (i, 0)), pl.BlockSpec((1, d), lambda i: (0, 0))],
        out_specs=pl.BlockSpec((tm, d), lambda i: (i, 0)),
        compiler_params=_cp("parallel"),
        name="rmsnorm",
    )(x, g.reshape(1, d))


def _mm_f32_kernel(a_ref, b_ref, o_ref):
    def split(x):
        hi = x.astype(BF16)
        return hi, (x - hi.astype(F32)).astype(BF16)

    (a_hi, a_lo), (b_hi, b_lo) = split(a_ref[...]), split(b_ref[...])
    dot = functools.partial(jnp.dot, preferred_element_type=F32)
    o_ref[...] = dot(a_hi, b_hi) + (dot(a_hi, b_lo) + dot(a_lo, b_hi))


def _matmul_f32(a, b, tm, tn, name):
    m, k = a.shape
    n = b.shape[1]
    tm, tn = min(tm, m), min(tn, n)
    return pl.pallas_call(
        _mm_f32_kernel,
        out_shape=jax.ShapeDtypeStruct((m, n), F32),
        grid=(m // tm, n // tn),
        in_specs=[pl.BlockSpec((tm, k), lambda i, j: (i, 0)), pl.BlockSpec((k, tn), lambda i, j: (0, j))],
        out_specs=pl.BlockSpec((tm, tn), lambda i, j: (i, j)),
        compiler_params=_cp("parallel", "arbitrary"),
        name=name,
    )(a, b)


def _cast_weights_once(w_refs, wb_refs):
    @pl.when(pl.program_id(1) == 0)
    def _():
        for w_ref, wb_ref in zip(w_refs, wb_refs):
            wb_ref[...] = w_ref[...].astype(BF16)


def _mm_kernel(a_ref, w_ref, o_ref, wb_ref):
    _cast_weights_once((w_ref,), (wb_ref,))
    o_ref[...] = jnp.dot(a_ref[...], wb_ref[...], preferred_element_type=F32).astype(o_ref.dtype)


def _mm_res_kernel(a_ref, w_ref, r_ref, o_ref, wb_ref):
    _cast_weights_once((w_ref,), (wb_ref,))
    o_ref[...] = r_ref[...] + jnp.dot(a_ref[...], wb_ref[...], preferred_element_type=F32)


def _matmul(a, w, out_dtype, tm, tn, residual=None, layer=None, name="matmul"):
    m, k = a.shape
    n = w.shape[-1]
    tm, tn = min(tm, m), min(tn, n)
    in_specs = [pl.BlockSpec((tm, k), lambda j, i: (i, 0)), _weight_spec((k, tn), lambda j, i: (0, j), layer)]
    args = [a, w]
    body = _mm_kernel
    if residual is not None:
        body = _mm_res_kernel
        in_specs.append(pl.BlockSpec((tm, tn), lambda j, i: (i, j)))
        args.append(residual)
    return pl.pallas_call(
        body,
        out_shape=jax.ShapeDtypeStruct((m, n), out_dtype),
        grid=(n // tn, m // tm),
        in_specs=in_specs,
        out_specs=pl.BlockSpec((tm, tn), lambda j, i: (i, j)),
        scratch_shapes=[pltpu.VMEM((k, tn), BF16)],
        compiler_params=_cp("parallel", "arbitrary"),
        name=name,
    )(*args)


def _mm_res_norm_kernel(a_ref, w_ref, r_ref, g_ref, x_ref, h_ref, wb_ref):
    @pl.when(pl.program_id(0) == 0)
    def _():
        wb_ref[...] = w_ref[...].astype(BF16)

    x = r_ref[...] + jnp.dot(a_ref[...], wb_ref[...], preferred_element_type=F32)
    x_ref[...] = x
    ms = jnp.mean(x * x, axis=-1, keepdims=True)
    h_ref[...] = (x * lax.rsqrt(ms + NORM_EPS) * g_ref[...]).astype(h_ref.dtype)


def _matmul_res_norm(a, w, layer, residual, gain, tm=512):
    m, k = a.shape
    n = w.shape[-1]
    tm = min(tm, m)
    rows = lambda width: pl.BlockSpec((tm, width), lambda i: (i, 0))
    return pl.pallas_call(
        _mm_res_norm_kernel,
        out_shape=[jax.ShapeDtypeStruct((m, n), F32), jax.ShapeDtypeStruct((m, n), BF16)],
        grid=(m // tm,),
        in_specs=[rows(k), _weight_spec((k, n), lambda i: (0, 0), layer), rows(n), pl.BlockSpec((1, n), lambda i: (0, 0))],
        out_specs=[rows(n), rows(n)],
        scratch_shapes=[pltpu.VMEM((k, n), BF16)],
        compiler_params=_cp("arbitrary"),
        name="out_proj_norm",
    )(a, w, residual, gain.reshape(1, n))


def _rope_tables(seq):
    rows = seq // GRID_W
    row_idx = jnp.repeat(jnp.arange(rows, dtype=F32), GRID_W)
    col_idx = jnp.tile(jnp.arange(GRID_W, dtype=F32), rows)
    axis_dim = A_HEAD_DIM // 2
    inv_freq = ROPE_THETA ** (-jnp.arange(0, axis_dim, 2, dtype=F32) / axis_dim)
    ang_r = row_idx[:, None] * inv_freq[None, :]
    ang_c = col_idx[:, None] * inv_freq[None, :]
    ang = jnp.concatenate([ang_r, ang_r, ang_c, ang_c], axis=-1)
    return jnp.cos(ang), jnp.sin(ang)


def _qk_prep_kernel(x_ref, g_ref, cos_ref, sin_ref, o_ref):
    cos = cos_ref[...]
    sin = sin_ref[...]
    lane = lax.broadcasted_iota(jnp.int32, cos.shape, 1)
    quarter = A_HEAD_DIM // 4
    first = (lane % (2 * quarter)) < quarter
    for h in range(A_HEADS + A_KV_HEADS):
        cols = slice(h * A_HEAD_DIM, (h + 1) * A_HEAD_DIM)
        x = x_ref[:, cols]
        xn = x * lax.rsqrt(jnp.mean(x * x, axis=-1, keepdims=True) + NORM_EPS) * g_ref[h]
        rot = jnp.where(first, -pltpu.roll(xn, A_HEAD_DIM - quarter, 1), pltpu.roll(xn, quarter, 1))
        y = xn * cos + rot * sin
        if h < A_HEADS:
            y = y * (A_HEAD_DIM**-0.5 * LOG2_E)
        o_ref[:, cols] = y.astype(o_ref.dtype)


def _qk_prep(proj, gains, cos, sin, seq, tm=256):
    m = proj.shape[0]
    tm = min(tm, seq)
    width = (A_HEADS + A_KV_HEADS) * A_HEAD_DIM
    nt = seq // tm
    return pl.pallas_call(
        _qk_prep_kernel,
        out_shape=jax.ShapeDtypeStruct((m, width), BF16),
        grid=(m // tm,),
        in_specs=[
            pl.BlockSpec((tm, width), lambda i: (i, 0)),
            pl.BlockSpec(gains.shape, lambda i: (0, 0, 0)),
            pl.BlockSpec((tm, A_HEAD_DIM), lambda i: (i % nt, 0)),
            pl.BlockSpec((tm, A_HEAD_DIM), lambda i: (i % nt, 0)),
        ],
        out_specs=pl.BlockSpec((tm, width), lambda i: (i, 0)),
        compiler_params=_cp("parallel"),
        name="a_qk_prep",
    )(proj, gains, cos, sin)


class _OnlineSoftmax:
    def __init__(self, rows, width):
        self.m = jnp.full((rows, 1), -jnp.inf, F32)
        self.l = jnp.zeros((rows, 1), F32)
        self.acc = jnp.zeros((rows, width), F32)

    def add(self, s, v):
        m_new = jnp.maximum(self.m, jnp.max(s, axis=-1, keepdims=True))
        p = jnp.exp2(s - m_new)
        alpha = jnp.exp2(self.m - m_new)
        self.l = alpha * self.l + jnp.sum(p, axis=-1, keepdims=True)
        self.acc = alpha * self.acc + jnp.dot(p.astype(BF16), v, preferred_element_type=F32)
        self.m = m_new

    def result(self):
        return self.acc / self.l


def _attn_a_kernel(q_ref, k_ref, v_ref, o_ref, *, kc):
    q = q_ref[...]
    sm = _OnlineSoftmax(q.shape[0], A_HEAD_DIM)
    for c in range(k_ref.shape[0] // kc):
        keys = slice(c * kc, (c + 1) * kc)
        s = lax.dot_general(q, k_ref[keys, :], NT_DIMS, preferred_element_type=F32)
        sm.add(s, v_ref[keys, :].astype(BF16))
    o_ref[...] = sm.result().astype(o_ref.dtype)


def _attn_a(qk, proj, batch, seq, tq=1024, kc=1024):
    tq = min(tq, seq)
    kc = min(kc, seq)
    nq = seq // tq
    group = A_HEADS // A_KV_HEADS
    k_blk = COL_AK // A_HEAD_DIM
    v_blk = COL_AV // A_HEAD_DIM
    return pl.pallas_call(
        functools.partial(_attn_a_kernel, kc=kc),
        out_shape=jax.ShapeDtypeStruct((batch * seq, A_WIDTH), BF16),
        grid=(batch, A_HEADS, nq),
        in_specs=[
            pl.BlockSpec((tq, A_HEAD_DIM), lambda b, h, i: (b * nq + i, h)),
            pl.BlockSpec((seq, A_HEAD_DIM), lambda b, h, i: (b, k_blk + h // group)),
            pl.BlockSpec((seq, A_HEAD_DIM), lambda b, h, i: (b, v_blk + h // group)),
        ],
        out_specs=pl.BlockSpec((tq, A_HEAD_DIM), lambda b, h, i: (b * nq + i, h)),
        compiler_params=_cp("parallel", "arbitrary", "arbitrary"),
        name="a_attention",
    )(qk, qk, proj)


def _attn_d_kernel(q_ref, k_ref, v_ref, slope_ref, lam_ref, g_ref, o_ref, *, tq, kc, lam_init):
    q = q_ref[...] * (D_HEAD_DIM**-0.5 * LOG2_E)
    lane = lax.broadcasted_iota(jnp.int32, q.shape, 1)
    q_maps = (jnp.where(lane < D_HEAD_DIM, q, 0.0).astype(BF16), jnp.where(lane >= D_HEAD_DIM, q, 0.0).astype(BF16))
    slope = slope_ref[0][:, 0:1] * LOG2_E
    rel = pl.program_id(2) * tq + lax.broadcasted_iota(jnp.int32, (tq, kc), 0) - lax.broadcasted_iota(jnp.int32, (tq, kc), 1)
    rel = slope * rel.astype(F32)
    maps = (_OnlineSoftmax(tq, D_V_DIM), _OnlineSoftmax(tq, D_V_DIM))
    for c in range(k_ref.shape[0] // kc):
        keys = slice(c * kc, (c + 1) * kc)
        k = k_ref[keys, :].astype(BF16)
        v = v_ref[keys, :].astype(BF16)
        bias = jnp.abs(rel - slope * float(c * kc))
        for q_map, sm in zip(q_maps, maps):
            sm.add(lax.dot_general(q_map, k, NT_DIMS, preferred_element_type=F32) - bias, v)
    lam_v = lam_ref[...]
    lam = (
        jnp.exp(jnp.sum(lam_v[0:1] * lam_v[1:2], axis=-1, keepdims=True))
        - jnp.exp(jnp.sum(lam_v[2:3] * lam_v[3:4], axis=-1, keepdims=True))
        + lam_init
    )
    o = maps[0].result() - lam * maps[1].result()
    o = o * lax.rsqrt(jnp.mean(o * o, axis=-1, keepdims=True) + NORM_EPS) * g_ref[...]
    o_ref[...] = (o * (1.0 - lam_init)).astype(o_ref.dtype)


def _attn_d(proj, lam_vecs, subln, lam_init, batch, seq, tq=1024, kc=512):
    tq = min(tq, seq)
    kc = min(kc, seq)
    nq = seq // tq
    slopes = 2.0 ** (-8.0 * np.arange(1, D_HEADS + 1, dtype=np.float32) / D_HEADS)
    slopes = jnp.asarray(np.broadcast_to(slopes[:, None, None], (D_HEADS, 1, 128)).astype(np.float32))
    q_blk, k_blk, v_blk = COL_DQ // D_V_DIM, COL_DK // D_V_DIM, COL_DV // D_V_DIM
    return pl.pallas_call(
        functools.partial(_attn_d_kernel, tq=tq, kc=kc, lam_init=lam_init),
        out_shape=jax.ShapeDtypeStruct((batch * seq, D_WIDTH), BF16),
        grid=(batch, D_HEADS, nq),
        in_specs=[
            pl.BlockSpec((tq, D_V_DIM), lambda b, h, i: (b * nq + i, q_blk + h)),
            pl.BlockSpec((seq, D_V_DIM), lambda b, h, i: (b, k_blk + h)),
            pl.BlockSpec((seq, D_V_DIM), lambda b, h, i: (b, v_blk + h)),
            pl.BlockSpec((1, 1, 128), lambda b, h, i: (h, 0, 0)),
            pl.BlockSpec((4, D_HEAD_DIM), lambda b, h, i: (0, 0)),
            pl.BlockSpec((1, D_V_DIM), lambda b, h, i: (0, 0)),
        ],
        out_specs=pl.BlockSpec((tq, D_V_DIM), lambda b, h, i: (b * nq + i, h)),
        compiler_params=_cp("parallel", "arbitrary", "arbitrary"),
        name="d_attention",
    )(proj, proj, proj, slopes, lam_vecs, subln.reshape(1, D_V_DIM))


DFT_ROW_SPLIT = 64


def _dft_tables(seq):
    n = 2 * seq
    blk = min(DFT_ROW_SPLIT, seq)
    col = jnp.arange(seq, dtype=jnp.int32)[None, :]
    hi = jnp.arange(seq // blk, dtype=jnp.int32)[:, None]
    lo = jnp.arange(blk, dtype=jnp.int32)[:, None]

    def cos_sin(index, period):
        ang = (index % period).astype(F32) * (2.0 * math.pi / period)
        return jnp.cos(ang), jnp.sin(ang)

    def tables(col_term, period):
        (ca, sa), (cb, sb) = cos_sin(2 * blk * hi * col_term, period), cos_sin((2 * lo + 1) * col_term, period)
        ca, sa, cb, sb = ca[:, None, :], sa[:, None, :], cb[None], sb[None]
        return (ca * cb - sa * sb).reshape(seq, seq), (sa * cb + ca * sb).reshape(seq, seq)

    cos_half, sin_half = tables(2 * col + 1, 4 * n)
    cos_int, sin_int = tables(col, 2 * n)
    return cos_half.astype(BF16), sin_half.astype(BF16), cos_int, sin_int


def _filter_features(seq):
    t = jnp.linspace(0.0, 1.0, seq, dtype=F32)[:, None]
    n_bands = (B_EMB_DIM - 1) // 2
    bands = jnp.linspace(1e-4, n_bands - 1, n_bands, dtype=F32)[None, :]
    ang = (2.0 * math.pi / seq) * jnp.arange(seq, dtype=F32)[:, None] * bands
    z = jnp.concatenate([t, jnp.cos(ang), -jnp.sin(ang)], axis=-1)
    z = jnp.pad(z, ((0, 0), (0, B_FILTER_HIDDEN - B_EMB_DIM)))
    max_decay = math.log(B_DECAY_TARGET) / B_FAST_DECAY_PCT
    min_decay = math.log(B_DECAY_TARGET) / B_SLOW_DECAY_PCT
    deltas = jnp.abs(jnp.linspace(min_decay, max_decay, B_WIDTH, dtype=F32))[None, :]
    return z, t, deltas


def _filter_kernel(z_ref, t_ref, dl_ref, w1_ref, b1_ref, w2_ref, b2_ref, w3_ref, b3_ref, w4_ref, fr_ref, hs_ref, hd_ref):
    fr = fr_ref[...]
    hid = jnp.sin(fr * (jnp.dot(z_ref[...], w1_ref[...], preferred_element_type=F32, precision=HIGHEST) + b1_ref[...]))
    hid = jnp.sin(fr * (jnp.dot(hid, w2_ref[...], preferred_element_type=F32, precision=HIGHEST) + b2_ref[...]))
    hid = jnp.sin(fr * (jnp.dot(hid, w3_ref[...], preferred_element_type=F32, precision=HIGHEST) + b3_ref[...]))
    h = jnp.dot(hid, w4_ref[...], preferred_element_type=F32, precision=HIGHEST)
    window = jnp.exp(-t_ref[...] * dl_ref[...])
    h_fwd = h[:, :B_WIDTH] * window
    h_bwd = h[:, B_WIDTH:] * window
    row = lax.broadcasted_iota(jnp.int32, h_bwd.shape, 0)
    h_bwd = jnp.where(row == 0, 0.0, h_bwd)
    norm = jnp.sum(jnp.abs(h_fwd), axis=0, keepdims=True) + jnp.sum(jnp.abs(h_bwd), axis=0, keepdims=True)
    seq = h.shape[0]
    inv_n = 1.0 / seq
    hs_ref[...] = (h_fwd + h_bwd) / norm * inv_n
    hd_ref[...] = (h_bwd - h_fwd) / norm * inv_n


def _hyena_filter_spectrum(seq, feats, cos_int, sin_int, w1, b1, w2, b2, w3, b3, w4, freq):
    z, t, deltas = feats
    w1p = jnp.pad(w1, ((0, B_FILTER_HIDDEN - B_EMB_DIM), (0, 0)))
    row = lambda v: v.reshape(1, -1)
    args = (z, t, deltas, w1p, row(b1), w2, row(b2), w3, row(b3), w4, row(freq))
    h_sum, h_diff = pl.pallas_call(
        _filter_kernel,
        out_shape=[jax.ShapeDtypeStruct((seq, B_WIDTH), F32)] * 2,
        in_specs=[pl.BlockSpec(a.shape, lambda: (0, 0)) for a in args],
        out_specs=[pl.BlockSpec((seq, B_WIDTH), lambda: (0, 0))] * 2,
        compiler_params=pltpu.CompilerParams(vmem_limit_bytes=VMEM_LIMIT_V7X),
        name="b_filter",
    )(*args)
    k_re = _matmul_f32(cos_int, h_sum, 256, B_WIDTH, name="b_filter_dft_re")
    k_im = _matmul_f32(sin_int, h_diff, 256, B_WIDTH, name="b_filter_dft_im")
    return k_re, k_im


def _hyena_kernel(v_ref, x1_ref, x0_ref, wv_ref, w1_ref, w0_ref, bv_ref, b1_ref, b0_ref, skip_ref, c_ref, s_ref, kre_ref, kim_ref, o_ref,
                  *, f_blk):
    seq = v_ref.shape[0]
    row = lax.broadcasted_iota(jnp.int32, v_ref.shape, 0)

    def conv3(u_ref, w_ref, b_ref):
        u = u_ref[...]
        w = w_ref[...]
        u_prev = jnp.where(row == 0, 0.0, pltpu.roll(u, 1, 0))
        u_next = jnp.where(row == seq - 1, 0.0, pltpu.roll(u, seq - 1, 0))
        return w[0:1] * u_prev + w[1:2] * u + w[2:3] * u_next + b_ref[...]

    z = conv3(v_ref, wv_ref, bv_ref) * conv3(x1_ref, w1_ref, b1_ref)
    zb = z.astype(BF16)
    y = z * skip_ref[...]
    for f0 in range(0, seq, f_blk):
        fs = slice(f0, f0 + f_blk)
        cz = jnp.dot(c_ref[fs, :], zb, preferred_element_type=F32)
        sz = jnp.dot(s_ref[fs, :], zb, preferred_element_type=F32)
        k_re = kre_ref[fs, :]
        k_im = kim_ref[fs, :]
        y_re = (cz * k_re + sz * k_im).astype(BF16)
        y_im = (cz * k_im - sz * k_re).astype(BF16)
        y = y + (jnp.dot(c_ref[:, fs], y_re, preferred_element_type=F32) - jnp.dot(s_ref[:, fs], y_im, preferred_element_type=F32))
    o_ref[...] = (y * conv3(x0_ref, w0_ref, b0_ref)).astype(o_ref.dtype)


def _hyena(proj, conv_w, conv_b, skip, cos_half, sin_half, k_re, k_im, batch, seq, cb=256, f_blk=1024):
    ncb = B_WIDTH // cb
    u_blk = COL_BU // cb
    f_blk = min(f_blk, seq)

    def u_spec(part):
        return pl.BlockSpec((seq, cb), lambda j, b: (b, u_blk + part * ncb + j))

    def w_spec(rows, part):
        return pl.BlockSpec((rows, cb), lambda j, b: (0, part * ncb + j))

    k_spec = pl.BlockSpec((seq, cb), lambda j, b: (0, j), pipeline_mode=pl.Buffered(1))
    return pl.pallas_call(
        functools.partial(_hyena_kernel, f_blk=f_blk),
        out_shape=jax.ShapeDtypeStruct((batch * seq, B_WIDTH), BF16),
        grid=(ncb, batch),
        in_specs=[
            u_spec(0), u_spec(1), u_spec(2),
            w_spec(3, 0), w_spec(3, 1), w_spec(3, 2),
            w_spec(1, 0), w_spec(1, 1), w_spec(1, 2),
            pl.BlockSpec((1, cb), lambda j, b: (0, j)),
            _const_spec((seq, seq)), _const_spec((seq, seq)),
            k_spec, k_spec,
        ],
        out_specs=pl.BlockSpec((seq, cb), lambda j, b: (b, j)),
        compiler_params=_cp("parallel", "arbitrary"),
        name="b_hyena",
    )(proj, proj, proj, conv_w, conv_w, conv_w, conv_b.reshape(1, -1), conv_b.reshape(1, -1), conv_b.reshape(1, -1),
      skip.reshape(1, -1), cos_half, sin_half, k_re, k_im)


def _rwkv_prep_kernel(cur_ref, lo_ref, pcur_ref, plo_ref, ncur_ref, nlo_ref, mu_ref, mulo_ref, w0_ref, wup_ref, a0_ref, aup_ref,
                      kk_ref, ka_ref, o_ref):
    first = pl.program_id(1) == 0
    last = pl.program_id(1) == pl.num_programs(1) - 1
    halo = pcur_ref.shape[0]
    tm = cur_ref.shape[0]

    def neighbours(x_ref, p_ref, n_ref):
        x = x_ref[...]
        row = lax.broadcasted_iota(jnp.int32, x.shape, 0)
        p_row = jnp.where(first, 0.0, p_ref[halo - 1:halo, :])
        n_row = jnp.where(last, 0.0, n_ref[0:1, :])
        prev = jnp.where(row == 0, p_row, pltpu.roll(x, 1, 0))
        nxt = jnp.where(row == tm - 1, n_row, pltpu.roll(x, tm - 1, 0))
        return x, (prev, nxt)

    cur, cur_sh = neighbours(cur_ref, pcur_ref, ncur_ref)
    lo, lo_sh = neighbours(lo_ref, plo_ref, nlo_ref)
    k_k = kk_ref[...]
    k_a = ka_ref[...]
    for d in range(2):
        f = cur + (cur_sh[d] - cur) * mu_ref[d]
        f_lo = lo + (lo_sh[d] - lo) * mulo_ref[d]
        r = f[:, :C_WIDTH]
        k = f[:, C_WIDTH:2 * C_WIDTH]
        v = f[:, 2 * C_WIDTH:]
        w_lo = f_lo[:, :C_LORA_PAD]
        a_lo = f_lo[:, C_LORA_PAD:]
        x = w0_ref[d] + jnp.dot(jnp.tanh(w_lo), wup_ref[d], preferred_element_type=F32, precision=HIGHEST)
        w = -(jnp.maximum(-x, 0.0) + jnp.log(1.0 + jnp.exp(-jnp.abs(x)))) - 0.5
        decay = jnp.exp(-jnp.exp(w))
        a = jax.nn.sigmoid(a0_ref[d] + jnp.dot(a_lo, aup_ref[d], preferred_element_type=F32, precision=HIGHEST))
        transposed = lambda val: val.T.reshape(C_HEAD_DIM, C_HEADS, tm)
        kk = transposed(k * k_k)
        kk = kk / jnp.maximum(jnp.sqrt(jnp.sum(kk * kk, axis=0, keepdims=True)), 1e-12)
        for q, val in enumerate((r, decay, k * (1.0 + (a - 1.0) * k_a), v)):
            o_ref[d, q] = transposed(val)
        o_ref[d, 4] = -kk
        o_ref[d, 5] = kk * transposed(a)


def _rwkv_prep(proj, mu, mu_lo, w0, w_up, a0, a_up, k_k, k_a, batch, seq, tm=256):
    tm = min(tm, seq)
    nt = seq // tm
    halo = 8
    hb = tm // halo
    n_hblk = batch * seq // halo
    wide, narrow = 3 * C_WIDTH, 2 * C_LORA_PAD
    cur_blk, lo_blk = COL_CF // wide, COL_CLO // narrow

    def cur_map(blk):
        return lambda b, j: (b * nt + j, blk)

    def prev_map(blk):
        return lambda b, j: (jnp.maximum((b * nt + j) * hb - 1, 0), blk)

    def next_map(blk):
        return lambda b, j: (jnp.minimum((b * nt + j + 1) * hb, n_hblk - 1), blk)

    small = lambda a: pl.BlockSpec(a.shape, lambda b, j: (0,) * a.ndim)
    consts = (mu, mu_lo, w0, w_up, a0, a_up, k_k, k_a)
    return pl.pallas_call(
        _rwkv_prep_kernel,
        out_shape=jax.ShapeDtypeStruct((2, 6, C_HEAD_DIM, batch, C_HEADS, seq), F32),
        grid=(batch, nt),
        in_specs=[
            pl.BlockSpec((tm, wide), cur_map(cur_blk)),
            pl.BlockSpec((tm, narrow), cur_map(lo_blk)),
            pl.BlockSpec((halo, wide), prev_map(cur_blk)),
            pl.BlockSpec((halo, narrow), prev_map(lo_blk)),
            pl.BlockSpec((halo, wide), next_map(cur_blk)),
            pl.BlockSpec((halo, narrow), next_map(lo_blk)),
        ] + [small(a) for a in consts],
        out_specs=pl.BlockSpec((2, 6, C_HEAD_DIM, None, C_HEADS, tm), lambda b, j: (0, 0, 0, b, 0, j)),
        compiler_params=_cp("parallel", "arbitrary"),
        name="c_prep",
    )(proj, proj, proj, proj, proj, proj, *consts)


RELAYOUT_T = 128


def _rwkv_relayout_kernel(pf_ref, pb_ref, o_ref, *, tc):
    n = pf_ref.shape[0]
    t_blk = pf_ref.shape[-1]
    mirror = t_blk - 1 - lax.broadcasted_iota(jnp.int32, (pb_ref.shape[1] * pb_ref.shape[2], t_blk), 1)
    for k in range(n):
        bwd = jnp.take_along_axis(pb_ref[k].reshape(-1, t_blk), mirror, axis=1)
        cols = jnp.concatenate([pf_ref[k].reshape(-1, t_blk), bwd], axis=0).T
        for c in range(t_blk // tc):
            o_ref[c, k] = cols[c * tc:(c + 1) * tc]


def _rwkv_relayout(p, tc):
    _, nq, n, batch, heads, seq = p.shape
    t_blk = min(RELAYOUT_T, seq)
    nb = seq // t_blk
    lanes = 2 * batch * heads
    per_blk = t_blk // tc
    return pl.pallas_call(
        functools.partial(_rwkv_relayout_kernel, tc=tc),
        out_shape=jax.ShapeDtypeStruct((seq // tc, nq, n, tc, lanes), F32),
        grid=(nb, nq),
        in_specs=[
            pl.BlockSpec((None, None, n, batch, heads, t_blk), lambda c, q: (0, q, 0, 0, 0, c)),
            pl.BlockSpec((None, None, n, batch, heads, t_blk), lambda c, q: (1, q, 0, 0, 0, nb - 1 - c)),
        ],
        out_specs=pl.BlockSpec((per_blk, None, n, tc, lanes), lambda c, q: (c, q, 0, 0, 0)),
        compiler_params=_cp("parallel", "arbitrary"),
        name="c_relayout",
    )(p, p)


def _rwkv_scan_kernel(x_ref, rk_ref, lnw_ref, lnb_ref, out_ref, s_ref, o_ref, *, tc, k_chunk):
    n = s_ref.shape[0]
    lanes = s_ref.shape[2]

    @pl.when(pl.program_id(0) == 0)
    def _():
        s_ref[...] = jnp.zeros_like(s_ref)

    def row(q, k, t):
        return x_ref[pl.ds((q * n + k) * tc + t, 1), :]

    def step(t, carry):
        vv = x_ref[pl.ds(3 * n * tc + t, n, stride=tc), :]

        def sa_chunk(c, sa):
            base = pl.multiple_of(c * k_chunk, k_chunk)
            for j in range(k_chunk):
                sa = sa + s_ref[base + j] * row(4, base + j, t)
            return sa

        sa = lax.fori_loop(0, n // k_chunk, sa_chunk, jnp.zeros((n, lanes), F32))

        def update_chunk(c, o):
            base = pl.multiple_of(c * k_chunk, k_chunk)
            for j in range(k_chunk):
                k = base + j
                sk = s_ref[k] * row(1, k, t) + sa * row(5, k, t) + vv * row(2, k, t)
                s_ref[k] = sk
                o = o + sk * row(0, k, t)
            return o

        o_ref[pl.ds(t, n, stride=tc), :] = lax.fori_loop(0, n // k_chunk, update_chunk, jnp.zeros((n, lanes), F32))
        return carry

    lax.fori_loop(0, tc, step, 0)

    quantity = lambda q: x_ref[q * n * tc:(q + 1) * n * tc, :].reshape(n, tc, lanes)
    o = o_ref[...].reshape(n, tc, lanes)
    mean = jnp.mean(o, axis=0, keepdims=True)
    var = jnp.mean(jnp.square(o - mean), axis=0, keepdims=True)
    o = (o - mean) * lax.rsqrt(var + C_GN_EPS) * lnw_ref[...] + lnb_ref[...]
    bonus = jnp.sum(quantity(0) * quantity(2) * rk_ref[...], axis=0, keepdims=True)
    out_ref[...] = o + bonus * quantity(3)


def _rwkv_scan(xs, rk, lnw, lnb, k_chunk=32):
    nc, nq, n, tc, lanes = xs.shape
    tile = lambda: pl.BlockSpec((n, 1, lanes), lambda i: (0, 0, 0))
    return pl.pallas_call(
        functools.partial(_rwkv_scan_kernel, tc=tc, k_chunk=min(k_chunk, n)),
        out_shape=jax.ShapeDtypeStruct((n, nc * tc, lanes), F32),
        grid=(nc,),
        in_specs=[pl.BlockSpec((nq * n * tc, lanes), lambda i: (i, 0)), tile(), tile(), tile()],
        out_specs=pl.BlockSpec((n, tc, lanes), lambda i: (0, i, 0)),
        scratch_shapes=[pltpu.VMEM((n, n, lanes), F32), pltpu.VMEM((n * tc, lanes), F32)],
        compiler_params=_cp("arbitrary"),
        name="c_scan",
    )(xs.reshape(nc * nq * n * tc, lanes), rk, lnw, lnb)


def _rwkv_unlayout_kernel(o_ref, q_ref):
    for v in range(o_ref.shape[0]):
        rows = o_ref[v].T
        q_ref[:, v] = rows.reshape(q_ref.shape[0], q_ref.shape[2], q_ref.shape[3])


def _rwkv_unlayout(o, batch):
    n, seq, lanes = o.shape
    t_blk = min(RELAYOUT_T, seq)
    heads = lanes // (2 * batch)
    return pl.pallas_call(
        _rwkv_unlayout_kernel,
        out_shape=jax.ShapeDtypeStruct((2 * batch, n, heads, seq), F32),
        grid=(seq // t_blk,),
        in_specs=[pl.BlockSpec((n, t_blk, lanes), lambda c: (0, c, 0))],
        out_specs=pl.BlockSpec((2 * batch, n, heads, t_blk), lambda c: (0, 0, 0, c)),
        compiler_params=_cp("parallel"),
        name="c_unlayout",
    )(o)


def _rwkv_out_kernel(qf_ref, qb_ref, glo_ref, gup_ref, y_ref):
    t_blk = qf_ref.shape[-1]
    fwd = qf_ref[...].reshape(-1, t_blk)
    bwd = qb_ref[...].reshape(-1, t_blk)
    mirror = t_blk - 1 - lax.broadcasted_iota(jnp.int32, bwd.shape, 1)
    o = (fwd + jnp.take_along_axis(bwd, mirror, axis=1)).T
    g = jnp.dot(jax.nn.sigmoid(glo_ref[...]).astype(BF16), gup_ref[...], preferred_element_type=F32)
    y_ref[...] = (o * g).astype(y_ref.dtype)


def _rwkv_out(q, proj, g_up, batch, seq):
    _, n, heads, _ = q.shape
    t_blk = min(RELAYOUT_T, seq)
    nt = seq // t_blk
    g_blk = COL_CG // C_GATE_LORA
    return pl.pallas_call(
        _rwkv_out_kernel,
        out_shape=jax.ShapeDtypeStruct((batch * seq, C_WIDTH), BF16),
        grid=(batch, nt),
        in_specs=[
            pl.BlockSpec((None, n, heads, t_blk), lambda b, j: (b, 0, 0, j)),
            pl.BlockSpec((None, n, heads, t_blk), lambda b, j: (batch + b, 0, 0, nt - 1 - j)),
            pl.BlockSpec((t_blk, C_GATE_LORA), lambda b, j: (b * nt + j, g_blk)),
            pl.BlockSpec((C_GATE_LORA, C_WIDTH), lambda b, j: (0, 0)),
        ],
        out_specs=pl.BlockSpec((t_blk, C_WIDTH), lambda b, j: (b * nt + j, 0)),
        compiler_params=_cp("parallel", "arbitrary"),
        name="c_out",
    )(q, q, proj, g_up)


SCAN_TC = 16


def _heads_minor(p):
    return p.reshape(p.shape[:-1] + (C_HEADS, C_HEAD_DIM)).swapaxes(-1, -2).reshape(p.shape)


def _rwkv(proj, mu, w0, w_up, a0, a_up, g_up, k_k, k_a, r_k, ln_w, ln_b, batch, seq):
    pad_lo = C_LORA_PAD - C_LORA
    split = 3 * C_WIDTH
    mu_wide = jnp.concatenate([_heads_minor(mu[:, i * C_WIDTH:(i + 1) * C_WIDTH]) for i in range(3)], axis=-1)[:, None, :]
    mu_lo = jnp.concatenate([
        jnp.pad(mu[:, split:split + C_LORA], ((0, 0), (0, pad_lo))),
        jnp.pad(mu[:, split + C_LORA:], ((0, 0), (0, pad_lo))),
    ], axis=-1)[:, None, :]
    w_up_p = jnp.pad(_heads_minor(w_up), ((0, 0), (0, pad_lo), (0, 0)))
    a_up_p = jnp.pad(_heads_minor(a_up), ((0, 0), (0, pad_lo), (0, 0)))
    feats = _rwkv_prep(proj, mu_wide, mu_lo, _heads_minor(w0)[:, None, :], w_up_p, _heads_minor(a0)[:, None, :], a_up_p,
                       _heads_minor(k_k).reshape(1, C_WIDTH), _heads_minor(k_a).reshape(1, C_WIDTH), batch, seq)
    xs = _rwkv_relayout(feats, min(SCAN_TC, seq))
    per_lane = lambda p: jnp.tile(p.reshape(C_HEADS, C_HEAD_DIM).T, (1, 2 * batch))[:, None, :]
    o = _rwkv_scan(xs, per_lane(r_k), per_lane(ln_w), per_lane(ln_b))
    return _rwkv_out(_rwkv_unlayout(o, batch), proj, _heads_minor(g_up).astype(BF16), batch, seq)


def _merge_kernel(h_ref, wg_ref, wbr_ref, wc_ref, oa_ref, ob_ref, oc_ref, od_ref, o_ref, wgb_ref, wbrb_ref, wcb_ref):
    _cast_weights_once((wg_ref, wbr_ref, wc_ref), (wgb_ref, wbrb_ref, wcb_ref))
    h = h_ref[...]
    acc = None
    row = 0
    for i, b_ref in enumerate((oa_ref, ob_ref, oc_ref, od_ref)):
        width = b_ref.shape[1]
        w = wcb_ref[...] if b_ref is oc_ref else wbrb_ref[row:row + width, :]
        gate = jax.nn.sigmoid(jnp.dot(h, wgb_ref[i], preferred_element_type=F32))
        term = gate * jnp.dot(b_ref[...], w, preferred_element_type=F32)
        acc = term if acc is None else acc + term
        row += width
    o_ref[...] = acc.astype(o_ref.dtype)


def _merge(h, w_gate, w_branch, layer, w_branch_c, branches, tm=512, tn=512):
    m = h.shape[0]
    tm = min(tm, m)
    d_mix = w_branch.shape[-2]
    row = lambda a: pl.BlockSpec((tm, a.shape[1]), lambda j, i: (i, 0))
    return pl.pallas_call(
        _merge_kernel,
        out_shape=jax.ShapeDtypeStruct((m, D_MODEL), BF16),
        grid=(D_MODEL // tn, m // tm),
        in_specs=[
            row(h),
            _weight_spec((4, D_MODEL, tn), lambda j, i: (0, 0, j), layer),
            _weight_spec((d_mix, tn), lambda j, i: (0, j), layer),
            _weight_spec((w_branch_c.shape[0], tn), lambda j, i: (0, j)),
        ] + [row(b) for b in branches],
        out_specs=pl.BlockSpec((tm, tn), lambda j, i: (i, j)),
        scratch_shapes=[pltpu.VMEM((4, D_MODEL, tn), BF16), pltpu.VMEM((d_mix, tn), BF16), pltpu.VMEM((w_branch_c.shape[0], tn), BF16)],
        compiler_params=_cp("parallel", "arbitrary"),
        name="gated_merge",
    )(h, w_gate, w_branch, w_branch_c, *branches)


def _ffn_up_kernel(h_ref, wg_ref, wu_ref, o_ref, wgb_ref, wub_ref):
    _cast_weights_once((wg_ref, wu_ref), (wgb_ref, wub_ref))
    h = h_ref[...]
    g = jnp.dot(h, wgb_ref[...], preferred_element_type=F32)
    u = jnp.dot(h, wub_ref[...], preferred_element_type=F32)
    o_ref[...] = (g * jax.nn.sigmoid(g) * u).astype(o_ref.dtype)


def _ffn_up(h, w_gate, w_up, layer, tm=1024, tn=512):
    m, k = h.shape
    n = w_gate.shape[-1]
    tm = min(tm, m)
    return pl.pallas_call(
        _ffn_up_kernel,
        out_shape=jax.ShapeDtypeStruct((m, n), BF16),
        grid=(n // tn, m // tm),
        in_specs=[
            pl.BlockSpec((tm, k), lambda j, i: (i, 0)),
            _weight_spec((k, tn), lambda j, i: (0, j), layer),
            _weight_spec((k, tn), lambda j, i: (0, j), layer),
        ],
        out_specs=pl.BlockSpec((tm, tn), lambda j, i: (i, j)),
        scratch_shapes=[pltpu.VMEM((k, tn), BF16), pltpu.VMEM((k, tn), BF16)],
        compiler_params=_cp("parallel", "arbitrary"),
        name="ffn_up",
    )(h, w_gate, w_up)


def _pad_w_in(w):
    pad = ((0, 0), (0, C_LORA_PAD - C_LORA))
    lo = COL_CLO
    cf = [_heads_minor(w[:, COL_CF + i * C_WIDTH:COL_CF + (i + 1) * C_WIDTH]) for i in range(3)]
    return jnp.concatenate([
        w[:, :COL_CF],
        *cf,
        jnp.pad(w[:, lo:lo + C_LORA], pad),
        jnp.pad(w[:, lo + C_LORA:lo + 2 * C_LORA], pad),
        w[:, lo + 2 * C_LORA:],
    ], axis=1)


def kernel(x, norm_mix, w_in, a_q_norm, a_k_norm, b_conv_w, b_conv_b, b_filt_w1, b_filt_b1, b_filt_w2, b_filt_b2, b_filt_w3, b_filt_b3, b_filt_w4, b_filt_freq, b_skip, c_mu, c_w0, c_w_up, c_a0, c_a_up, c_g_up, c_k_k, c_k_a, c_r_k, c_ln_w, c_ln_b, d_lq1, d_lk1, d_lq2, d_lk2, d_subln, w_gate, w_branch, w_out, norm_ffn, w_ff_gate, w_ff_up, w_ff_down, norm_final):
    batch, seq, _ = x.shape
    m = batch * seq
    cos, sin = _rope_tables(seq)
    cos_half, sin_half, cos_int, sin_int = _dft_tables(seq)
    filt_feats = _filter_features(seq)
    x = x.reshape(m, D_MODEL)
    for l in range(DEPTH):
        h = _rmsnorm(x, norm_mix[l], BF16)
        proj = _matmul(h, _pad_w_in(w_in[l]), F32, 512, D_IN_PAD // 4, name="in_proj")

        gains = jnp.concatenate([
            jnp.broadcast_to(a_q_norm[l], (A_HEADS, A_HEAD_DIM)),
            jnp.broadcast_to(a_k_norm[l], (A_KV_HEADS, A_HEAD_DIM)),
        ])[:, None, :]
        o_a = _attn_a(_qk_prep(proj, gains, cos, sin, seq), proj, batch, seq)

        k_re, k_im = _hyena_filter_spectrum(seq, filt_feats, cos_int, sin_int, b_filt_w1[l], b_filt_b1[l], b_filt_w2[l],
                                            b_filt_b2[l], b_filt_w3[l], b_filt_b3[l], b_filt_w4[l], b_filt_freq[l])
        o_b = _hyena(proj, b_conv_w[l], b_conv_b[l], b_skip[l], cos_half, sin_half, k_re, k_im, batch, seq)

        o_c = _rwkv(proj, c_mu[l], c_w0[l], c_w_up[l], c_a0[l], c_a_up[l], c_g_up[l], c_k_k[l], c_k_a[l], c_r_k[l],
                    c_ln_w[l], c_ln_b[l], batch, seq)

        lam_init = 0.8 - 0.6 * math.exp(-0.3 * l)
        lam_vecs = jnp.stack([d_lq1[l], d_lk1[l], d_lq2[l], d_lk2[l]])
        o_d = _attn_d(proj, lam_vecs, d_subln[l], lam_init, batch, seq)

        c_lo, c_hi = A_WIDTH + B_WIDTH, A_WIDTH + B_WIDTH + C_WIDTH
        w_br_c = _heads_minor(w_branch[l, c_lo:c_hi].T).T
        merged = _merge(h, w_gate, w_branch, l, w_br_c, (o_a, o_b, o_c, o_d))
        x, h2 = _matmul_res_norm(merged, w_out, l, x, norm_ffn[l])
        mid = _ffn_up(h2, w_ff_gate, w_ff_up, l)
        x = _matmul(mid, w_ff_down, F32, 256, 1024, residual=x, layer=l, name="ffn_down")
    return _rmsnorm(x, norm_final, F32).reshape(batch, seq, D_MODEL)
```

```python
import functools
import math

import jax
import jax.numpy as jnp
import numpy as np
from jax import lax
from jax.experimental import pallas as pl
from jax.experimental.pallas import tpu as pltpu

D_MODEL = 2048
DEPTH = 2
GRID_W = 64
NORM_EPS = 1e-6

A_HEADS = 8
A_KV_HEADS = 2
A_HEAD_DIM = 128
A_WIDTH = A_HEADS * A_HEAD_DIM
ROPE_THETA = 10000.0

B_WIDTH = 512
B_EMB_DIM = 33
B_FILTER_HIDDEN = 64
B_DECAY_TARGET = 1e-2
B_FAST_DECAY_PCT = 0.3
B_SLOW_DECAY_PCT = 1.5

C_HEADS = 8
C_HEAD_DIM = 64
C_WIDTH = C_HEADS * C_HEAD_DIM
C_LORA = 96
C_LORA_PAD = 128
C_GATE_LORA = 256
C_GN_EPS = 64e-5

D_HEADS = 4
D_HEAD_DIM = 64
D_V_DIM = 2 * D_HEAD_DIM
D_WIDTH = D_HEADS * D_V_DIM

FFN_HIDDEN = -(-8 * D_MODEL // (3 * 256)) * 256

COL_AQ = 0
COL_AK = COL_AQ + A_WIDTH
COL_AV = COL_AK + A_KV_HEADS * A_HEAD_DIM
COL_BU = COL_AV + A_KV_HEADS * A_HEAD_DIM
COL_CF = COL_BU + 3 * B_WIDTH
COL_CLO = COL_CF + 3 * C_WIDTH
COL_CG = COL_CLO + 2 * C_LORA_PAD
COL_DQ = COL_CG + C_GATE_LORA
COL_DK = COL_DQ + 2 * D_HEADS * D_HEAD_DIM
COL_DV = COL_DK + 2 * D_HEADS * D_HEAD_DIM
D_IN_PAD = COL_DV + D_WIDTH

VMEM_LIMIT_V7X = 56 * 1024 * 1024
F32 = jnp.float32
BF16 = jnp.bfloat16
HIGHEST = lax.Precision.HIGHEST
NT_DIMS = (((1,), (1,)), ((), ()))
LOG2_E = math.log2(math.e)


def _cp(*sem):
    return pltpu.CompilerParams(dimension_semantics=sem, vmem_limit_bytes=VMEM_LIMIT_V7X)


def _const_spec(shape):
    return pl.BlockSpec(shape, lambda *_: (0,) * len(shape), pipeline_mode=pl.Buffered(1))


SINGLE_BUFFER_BYTES = 8 * 1024 * 1024


def _weight_spec(shape, index_map, layer=None):
    mode = {"pipeline_mode": pl.Buffered(1)} if 4 * math.prod(shape) > SINGLE_BUFFER_BYTES else {}
    if layer is None:
        return pl.BlockSpec(shape, index_map, **mode)
    return pl.BlockSpec((None,) + tuple(shape), lambda *g: (layer,) + tuple(index_map(*g)), **mode)


def _rmsnorm_kernel(x_ref, g_ref, o_ref):
    x = x_ref[...]
    ms = jnp.mean(x * x, axis=-1, keepdims=True)
    o_ref[...] = (x * lax.rsqrt(ms + NORM_EPS) * g_ref[...]).astype(o_ref.dtype)


def _rmsnorm(x, g, out_dtype, tm=512):
    m, d = x.shape
    tm = min(tm, m)
    return pl.pallas_call(
        _rmsnorm_kernel,
        out_shape=jax.ShapeDtypeStruct((m, d), out_dtype),
        grid=(m // tm,),
        in_specs=[pl.BlockSpec((tm, d), lambda i: (i, 0)), pl.BlockSpec((1, d), lambda i: (0, 0))],
        out_specs=pl.BlockSpec((tm, d), lambda i: (i, 0)),
        compiler_params=_cp("parallel"),
        name="rmsnorm",
    )(x, g.reshape(1, d))


def _mm_f32_kernel(a_ref, b_ref, o_ref):
    def split(x):
        hi = x.astype(BF16)
        return hi, (x - hi.astype(F32)).astype(BF16)

    (a_hi, a_lo), (b_hi, b_lo) = split(a_ref[...]), split(b_ref[...])
    dot = functools.partial(jnp.dot, preferred_element_type=F32)
    o_ref[...] = dot(a_hi, b_hi) + (dot(a_hi, b_lo) + dot(a_lo, b_hi))


def _matmul_f32(a, b, tm, tn, name):
    m, k = a.shape
    n = b.shape[1]
    tm, tn = min(tm, m), min(tn, n)
    return pl.pallas_call(
        _mm_f32_kernel,
        out_shape=jax.ShapeDtypeStruct((m, n), F32),
        grid=(m // tm, n // tn),
        in_specs=[pl.BlockSpec((tm, k), lambda i, j: (i, 0)), pl.BlockSpec((k, tn), lambda i, j: (0, j))],
        out_specs=pl.BlockSpec((tm, tn), lambda i, j: (i, j)),
        compiler_params=_cp("parallel", "arbitrary"),
        name=name,
    )(a, b)


def _cast_weights_once(w_refs, wb_refs):
    @pl.when(pl.program_id(1) == 0)
    def _():
        for w_ref, wb_ref in zip(w_refs, wb_refs):
            wb_ref[...] = w_ref[...].astype(BF16)


def _mm_kernel(a_ref, w_ref, o_ref, wb_ref):
    _cast_weights_once((w_ref,), (wb_ref,))
    o_ref[...] = jnp.dot(a_ref[...], wb_ref[...], preferred_element_type=F32).astype(o_ref.dtype)


def _mm_res_kernel(a_ref, w_ref, r_ref, o_ref, wb_ref):
    _cast_weights_once((w_ref,), (wb_ref,))
    o_ref[...] = r_ref[...] + jnp.dot(a_ref[...], wb_ref[...], preferred_element_type=F32)


def _matmul(a, w, out_dtype, tm, tn, residual=None, layer=None, name="matmul"):
    m, k = a.shape
    n = w.shape[-1]
    tm, tn = min(tm, m), min(tn, n)
    in_specs = [pl.BlockSpec((tm, k), lambda j, i: (i, 0)), _weight_spec((k, tn), lambda j, i: (0, j), layer)]
    args = [a, w]
    body = _mm_kernel
    if residual is not None:
        body = _mm_res_kernel
        in_specs.append(pl.BlockSpec((tm, tn), lambda j, i: (i, j)))
        args.append(residual)
    return pl.pallas_call(
        body,
        out_shape=jax.ShapeDtypeStruct((m, n), out_dtype),
        grid=(n // tn, m // tm),
        in_specs=in_specs,
        out_specs=pl.BlockSpec((tm, tn), lambda j, i: (i, j)),
        scratch_shapes=[pltpu.VMEM((k, tn), BF16)],
        compiler_params=_cp("parallel", "arbitrary"),
        name=name,
    )(*args)


def _mm_res_norm_kernel(a_ref, w_ref, r_ref, g_ref, x_ref, h_ref, wb_ref):
    @pl.when(pl.program_id(0) == 0)
    def _():
        wb_ref[...] = w_ref[...].astype(BF16)

    x = r_ref[...] + jnp.dot(a_ref[...], wb_ref[...], preferred_element_type=F32)
    x_ref[...] = x
    ms = jnp.mean(x * x, axis=-1, keepdims=True)
    h_ref[...] = (x * lax.rsqrt(ms + NORM_EPS) * g_ref[...]).astype(h_ref.dtype)


def _matmul_res_norm(a, w, layer, residual, gain, tm=512):
    m, k = a.shape
    n = w.shape[-1]
    tm = min(tm, m)
    rows = lambda width: pl.BlockSpec((tm, width), lambda i: (i, 0))
    return pl.pallas_call(
        _mm_res_norm_kernel,
        out_shape=[jax.ShapeDtypeStruct((m, n), F32), jax.ShapeDtypeStruct((m, n), BF16)],
        grid=(m // tm,),
        in_specs=[rows(k), _weight_spec((k, n), lambda i: (0, 0), layer), rows(n), pl.BlockSpec((1, n), lambda i: (0, 0))],
        out_specs=[rows(n), rows(n)],
        scratch_shapes=[pltpu.VMEM((k, n), BF16)],
        compiler_params=_cp("arbitrary"),
        name="out_proj_norm",
    )(a, w, residual, gain.reshape(1, n))


def _rope_tables(seq):
    rows = seq // GRID_W
    row_idx = jnp.repeat(jnp.arange(rows, dtype=F32), GRID_W)
    col_idx = jnp.tile(jnp.arange(GRID_W, dtype=F32), rows)
    axis_dim = A_HEAD_DIM // 2
    inv_freq = ROPE_THETA ** (-jnp.arange(0, axis_dim, 2, dtype=F32) / axis_dim)
    ang_r = row_idx[:, None] * inv_freq[None, :]
    ang_c = col_idx[:, None] * inv_freq[None, :]
    ang = jnp.concatenate([ang_r, ang_r, ang_c, ang_c], axis=-1)
    return jnp.cos(ang), jnp.sin(ang)


def _qk_prep_kernel(x_ref, g_ref, cos_ref, sin_ref, o_ref):
    cos = cos_ref[...]
    sin = sin_ref[...]
    lane = lax.broadcasted_iota(jnp.int32, cos.shape, 1)
    quarter = A_HEAD_DIM // 4
    first = (lane % (2 * quarter)) < quarter
    for h in range(A_HEADS + A_KV_HEADS):
        cols = slice(h * A_HEAD_DIM, (h + 1) * A_HEAD_DIM)
        x = x_ref[:, cols]
        xn = x * lax.rsqrt(jnp.mean(x * x, axis=-1, keepdims=True) + NORM_EPS) * g_ref[h]
        rot = jnp.where(first, -pltpu.roll(xn, A_HEAD_DIM - quarter, 1), pltpu.roll(xn, quarter, 1))
        y = xn * cos + rot * sin
        if h < A_HEADS:
            y = y * (A_HEAD_DIM**-0.5 * LOG2_E)
        o_ref[:, cols] = y.astype(o_ref.dtype)


def _qk_prep(proj, gains, cos, sin, seq, tm=256):
    m = proj.shape[0]
    tm = min(tm, seq)
    width = (A_HEADS + A_KV_HEADS) * A_HEAD_DIM
    nt = seq // tm
    return pl.pallas_call(
        _qk_prep_kernel,
        out_shape=jax.ShapeDtypeStruct((m, width), BF16),
        grid=(m // tm,),
        in_specs=[
            pl.BlockSpec((tm, width), lambda i: (i, 0)),
            pl.BlockSpec(gains.shape, lambda i: (0, 0, 0)),
            pl.BlockSpec((tm, A_HEAD_DIM), lambda i: (i % nt, 0)),
            pl.BlockSpec((tm, A_HEAD_DIM), lambda i: (i % nt, 0)),
        ],
        out_specs=pl.BlockSpec((tm, width), lambda i: (i, 0)),
        compiler_params=_cp("parallel"),
        name="a_qk_prep",
    )(proj, gains, cos, sin)


class _OnlineSoftmax:
    def __init__(self, rows, width):
        self.m = jnp.full((rows, 1), -jnp.inf, F32)
        self.l = jnp.zeros((rows, 1), F32)
        self.acc = jnp.zeros((rows, width), F32)

    def add(self, s, v):
        m_new = jnp.maximum(self.m, jnp.max(s, axis=-1, keepdims=True))
        p = jnp.exp2(s - m_new)
        alpha = jnp.exp2(self.m - m_new)
        self.l = alpha * self.l + jnp.sum(p, axis=-1, keepdims=True)
        self.acc = alpha * self.acc + jnp.dot(p.astype(BF16), v, preferred_element_type=F32)
        self.m = m_new

    def result(self):
        return self.acc / self.l


def _attn_a_kernel(q_ref, k_ref, v_ref, o_ref, *, kc):
    q = q_ref[...]
    sm = _OnlineSoftmax(q.shape[0], A_HEAD_DIM)
    for c in range(k_ref.shape[0] // kc):
        keys = slice(c * kc, (c + 1) * kc)
        s = lax.dot_general(q, k_ref[keys, :], NT_DIMS, preferred_element_type=F32)
        sm.add(s, v_ref[keys, :].astype(BF16))
    o_ref[...] = sm.result().astype(o_ref.dtype)


def _attn_a(qk, proj, batch, seq, tq=1024, kc=1024):
    tq = min(tq, seq)
    kc = min(kc, seq)
    nq = seq // tq
    group = A_HEADS // A_KV_HEADS
    k_blk = COL_AK // A_HEAD_DIM
    v_blk = COL_AV // A_HEAD_DIM
    return pl.pallas_call(
        functools.partial(_attn_a_kernel, kc=kc),
        out_shape=jax.ShapeDtypeStruct((batch * seq, A_WIDTH), BF16),
        grid=(batch, A_HEADS, nq),
        in_specs=[
            pl.BlockSpec((tq, A_HEAD_DIM), lambda b, h, i: (b * nq + i, h)),
            pl.BlockSpec((seq, A_HEAD_DIM), lambda b, h, i: (b, k_blk + h // group)),
            pl.BlockSpec((seq, A_HEAD_DIM), lambda b, h, i: (b, v_blk + h // group)),
        ],
        out_specs=pl.BlockSpec((tq, A_HEAD_DIM), lambda b, h, i: (b * nq + i, h)),
        compiler_params=_cp("parallel", "arbitrary", "arbitrary"),
        name="a_attention",
    )(qk, qk, proj)


def _attn_d_kernel(q_ref, k_ref, v_ref, slope_ref, lam_ref, g_ref, o_ref, *, tq, kc, lam_init):
    q = q_ref[...] * (D_HEAD_DIM**-0.5 * LOG2_E)
    lane = lax.broadcasted_iota(jnp.int32, q.shape, 1)
    q_maps = (jnp.where(lane < D_HEAD_DIM, q, 0.0).astype(BF16), jnp.where(lane >= D_HEAD_DIM, q, 0.0).astype(BF16))
    slope = slope_ref[0][:, 0:1] * LOG2_E
    rel = pl.program_id(2) * tq + lax.broadcasted_iota(jnp.int32, (tq, kc), 0) - lax.broadcasted_iota(jnp.int32, (tq, kc), 1)
    rel = slope * rel.astype(F32)
    maps = (_OnlineSoftmax(tq, D_V_DIM), _OnlineSoftmax(tq, D_V_DIM))
    for c in range(k_ref.shape[0] // kc):
        keys = slice(c * kc, (c + 1) * kc)
        k = k_ref[keys, :].astype(BF16)
        v = v_ref[keys, :].astype(BF16)
        bias = jnp.abs(rel - slope * float(c * kc))
        for q_map, sm in zip(q_maps, maps):
            sm.add(lax.dot_general(q_map, k, NT_DIMS, preferred_element_type=F32) - bias, v)
    lam_v = lam_ref[...]
    lam = (
        jnp.exp(jnp.sum(lam_v[0:1] * lam_v[1:2], axis=-1, keepdims=True))
        - jnp.exp(jnp.sum(lam_v[2:3] * lam_v[3:4], axis=-1, keepdims=True))
        + lam_init
    )
    o = maps[0].result() - lam * maps[1].result()
    o = o * lax.rsqrt(jnp.mean(o * o, axis=-1, keepdims=True) + NORM_EPS) * g_ref[...]
    o_ref[...] = (o * (1.0 - lam_init)).astype(o_ref.dtype)


def _attn_d(proj, lam_vecs, subln, lam_init, batch, seq, tq=1024, kc=512):
    tq = min(tq, seq)
    kc = min(kc, seq)
    nq = seq // tq
    slopes = 2.0 ** (-8.0 * np.arange(1, D_HEADS + 1, dtype=np.float32) / D_HEADS)
    slopes = jnp.asarray(np.broadcast_to(slopes[:, None, None], (D_HEADS, 1, 128)).astype(np.float32))
    q_blk, k_blk, v_blk = COL_DQ // D_V_DIM, COL_DK // D_V_DIM, COL_DV // D_V_DIM
    return pl.pallas_call(
        functools.partial(_attn_d_kernel, tq=tq, kc=kc, lam_init=lam_init),
        out_shape=jax.ShapeDtypeStruct((batch * seq, D_WIDTH), BF16),
        grid=(batch, D_HEADS, nq),
        in_specs=[
            pl.BlockSpec((tq, D_V_DIM), lambda b, h, i: (b * nq + i, q_blk + h)),
            pl.BlockSpec((seq, D_V_DIM), lambda b, h, i: (b, k_blk + h)),
            pl.BlockSpec((seq, D_V_DIM), lambda b, h, i: (b, v_blk + h)),
            pl.BlockSpec((1, 1, 128), lambda b, h, i: (h, 0, 0)),
            pl.BlockSpec((4, D_HEAD_DIM), lambda b, h, i: (0, 0)),
            pl.BlockSpec((1, D_V_DIM), lambda b, h, i: (0, 0)),
        ],
        out_specs=pl.BlockSpec((tq, D_V_DIM), lambda b, h, i: (b * nq + i, h)),
        compiler_params=_cp("parallel", "arbitrary", "arbitrary"),
        name="d_attention",
    )(proj, proj, proj, slopes, lam_vecs, subln.reshape(1, D_V_DIM))


DFT_ROW_SPLIT = 64


def _dft_tables(seq):
    n = 2 * seq
    blk = min(DFT_ROW_SPLIT, seq)
    col = jnp.arange(seq, dtype=jnp.int32)[None, :]
    hi = jnp.arange(seq // blk, dtype=jnp.int32)[:, None]
    lo = jnp.arange(blk, dtype=jnp.int32)[:, None]

    def cos_sin(index, period):
        ang = (index % period).astype(F32) * (2.0 * math.pi / period)
        return jnp.cos(ang), jnp.sin(ang)

    def tables(col_term, period):
        (ca, sa), (cb, sb) = cos_sin(2 * blk * hi * col_term, period), cos_sin((2 * lo + 1) * col_term, period)
        ca, sa, cb, sb = ca[:, None, :], sa[:, None, :], cb[None], sb[None]
        return (ca * cb - sa * sb).reshape(seq, seq), (sa * cb + ca * sb).reshape(seq, seq)

    cos_half, sin_half = tables(2 * col + 1, 4 * n)
    cos_int, sin_int = tables(col, 2 * n)
    return cos_half.astype(BF16), sin_half.astype(BF16), cos_int, sin_int


def _filter_features(seq):
    t = jnp.linspace(0.0, 1.0, seq, dtype=F32)[:, None]
    n_bands = (B_EMB_DIM - 1) // 2
    bands = jnp.linspace(1e-4, n_bands - 1, n_bands, dtype=F32)[None, :]
    ang = (2.0 * math.pi / seq) * jnp.arange(seq, dtype=F32)[:, None] * bands
    z = jnp.concatenate([t, jnp.cos(ang), -jnp.sin(ang)], axis=-1)
    z = jnp.pad(z, ((0, 0), (0, B_FILTER_HIDDEN - B_EMB_DIM)))
    max_decay = math.log(B_DECAY_TARGET) / B_FAST_DECAY_PCT
    min_decay = math.log(B_DECAY_TARGET) / B_SLOW_DECAY_PCT
    deltas = jnp.abs(jnp.linspace(min_decay, max_decay, B_WIDTH, dtype=F32))[None, :]
    return z, t, deltas


def _filter_kernel(z_ref, t_ref, dl_ref, w1_ref, b1_ref, w2_ref, b2_ref, w3_ref, b3_ref, w4_ref, fr_ref, hs_ref, hd_ref):
    fr = fr_ref[...]
    hid = jnp.sin(fr * (jnp.dot(z_ref[...], w1_ref[...], preferred_element_type=F32, precision=HIGHEST) + b1_ref[...]))
    hid = jnp.sin(fr * (jnp.dot(hid, w2_ref[...], preferred_element_type=F32, precision=HIGHEST) + b2_ref[...]))
    hid = jnp.sin(fr * (jnp.dot(hid, w3_ref[...], preferred_element_type=F32, precision=HIGHEST) + b3_ref[...]))
    h = jnp.dot(hid, w4_ref[...], preferred_element_type=F32, precision=HIGHEST)
    window = jnp.exp(-t_ref[...] * dl_ref[...])
    h_fwd = h[:, :B_WIDTH] * window
    h_bwd = h[:, B_WIDTH:] * window
    row = lax.broadcasted_iota(jnp.int32, h_bwd.shape, 0)
    h_bwd = jnp.where(row == 0, 0.0, h_bwd)
    norm = jnp.sum(jnp.abs(h_fwd), axis=0, keepdims=True) + jnp.sum(jnp.abs(h_bwd), axis=0, keepdims=True)
    seq = h.shape[0]
    inv_n = 1.0 / seq
    hs_ref[...] = (h_fwd + h_bwd) / norm * inv_n
    hd_ref[...] = (h_bwd - h_fwd) / norm * inv_n


def _hyena_filter_spectrum(seq, feats, cos_int, sin_int, w1, b1, w2, b2, w3, b3, w4, freq):
    z, t, deltas = feats
    w1p = jnp.pad(w1, ((0, B_FILTER_HIDDEN - B_EMB_DIM), (0, 0)))
    row = lambda v: v.reshape(1, -1)
    args = (z, t, deltas, w1p, row(b1), w2, row(b2), w3, row(b3), w4, row(freq))
    h_sum, h_diff = pl.pallas_call(
        _filter_kernel,
        out_shape=[jax.ShapeDtypeStruct((seq, B_WIDTH), F32)] * 2,
        in_specs=[pl.BlockSpec(a.shape, lambda: (0, 0)) for a in args],
        out_specs=[pl.BlockSpec((seq, B_WIDTH), lambda: (0, 0))] * 2,
        compiler_params=pltpu.CompilerParams(vmem_limit_bytes=VMEM_LIMIT_V7X),
        name="b_filter",
    )(*args)
    k_re = _matmul_f32(cos_int, h_sum, 256, B_WIDTH, name="b_filter_dft_re")
    k_im = _matmul_f32(sin_int, h_diff, 256, B_WIDTH, name="b_filter_dft_im")
    return k_re, k_im


def _hyena_kernel(v_ref, x1_ref, x0_ref, wv_ref, w1_ref, w0_ref, bv_ref, b1_ref, b0_ref, skip_ref, c_ref, s_ref, kre_ref, kim_ref, o_ref,
                  *, f_blk):
    seq = v_ref.shape[0]
    row = lax.broadcasted_iota(jnp.int32, v_ref.shape, 0)

    def conv3(u_ref, w_ref, b_ref):
        u = u_ref[...]
        w = w_ref[...]
        u_prev = jnp.where(row == 0, 0.0, pltpu.roll(u, 1, 0))
        u_next = jnp.where(row == seq - 1, 0.0, pltpu.roll(u, seq - 1, 0))
        return w[0:1] * u_prev + w[1:2] * u + w[2:3] * u_next + b_ref[...]

    z = conv3(v_ref, wv_ref, bv_ref) * conv3(x1_ref, w1_ref, b1_ref)
    zb = z.astype(BF16)
    y = z * skip_ref[...]
    for f0 in range(0, seq, f_blk):
        fs = slice(f0, f0 + f_blk)
        cz = jnp.dot(c_ref[fs, :], zb, preferred_element_type=F32)
        sz = jnp.dot(s_ref[fs, :], zb, preferred_element_type=F32)
        k_re = kre_ref[fs, :]
        k_im = kim_ref[fs, :]
        y_re = (cz * k_re + sz * k_im).astype(BF16)
        y_im = (cz * k_im - sz * k_re).astype(BF16)
        y = y + (jnp.dot(c_ref[:, fs], y_re, preferred_element_type=F32) - jnp.dot(s_ref[:, fs], y_im, preferred_element_type=F32))
    o_ref[...] = (y * conv3(x0_ref, w0_ref, b0_ref)).astype(o_ref.dtype)


def _hyena(proj, conv_w, conv_b, skip, cos_half, sin_half, k_re, k_im, batch, seq, cb=256, f_blk=1024):
    ncb = B_WIDTH // cb
    u_blk = COL_BU // cb
    f_blk = min(f_blk, seq)

    def u_spec(part):
        return pl.BlockSpec((seq, cb), lambda j, b: (b, u_blk + part * ncb + j))

    def w_spec(rows, part):
        return pl.BlockSpec((rows, cb), lambda j, b: (0, part * ncb + j))

    k_spec = pl.BlockSpec((seq, cb), lambda j, b: (0, j), pipeline_mode=pl.Buffered(1))
    return pl.pallas_call(
        functools.partial(_hyena_kernel, f_blk=f_blk),
        out_shape=jax.ShapeDtypeStruct((batch * seq, B_WIDTH), BF16),
        grid=(ncb, batch),
        in_specs=[
            u_spec(0), u_spec(1), u_spec(2),
            w_spec(3, 0), w_spec(3, 1), w_spec(3, 2),
            w_spec(1, 0), w_spec(1, 1), w_spec(1, 2),
            pl.BlockSpec((1, cb), lambda j, b: (0, j)),
            _const_spec((seq, seq)), _const_spec((seq, seq)),
            k_spec, k_spec,
        ],
        out_specs=pl.BlockSpec((seq, cb), lambda j, b: (b, j)),
        compiler_params=_cp("parallel", "arbitrary"),
        name="b_hyena",
    )(proj, proj, proj, conv_w, conv_w, conv_w, conv_b.reshape(1, -1), conv_b.reshape(1, -1), conv_b.reshape(1, -1),
      skip.reshape(1, -1), cos_half, sin_half, k_re, k_im)


def _rwkv_prep_kernel(cur_ref, lo_ref, pcur_ref, plo_ref, ncur_ref, nlo_ref, mu_ref, mulo_ref, w0_ref, wup_ref, a0_ref, aup_ref,
                      kk_ref, ka_ref, o_ref):
    first = pl.program_id(1) == 0
    last = pl.program_id(1) == pl.num_programs(1) - 1
    halo = pcur_ref.shape[0]
    tm = cur_ref.shape[0]

    def neighbours(x_ref, p_ref, n_ref):
        x = x_ref[...]
        row = lax.broadcasted_iota(jnp.int32, x.shape, 0)
        p_row = jnp.where(first, 0.0, p_ref[halo - 1:halo, :])
        n_row = jnp.where(last, 0.0, n_ref[0:1, :])
        prev = jnp.where(row == 0, p_row, pltpu.roll(x, 1, 0))
        nxt = jnp.where(row == tm - 1, n_row, pltpu.roll(x, tm - 1, 0))
        return x, (prev, nxt)

    cur, cur_sh = neighbours(cur_ref, pcur_ref, ncur_ref)
    lo, lo_sh = neighbours(lo_ref, plo_ref, nlo_ref)
    k_k = kk_ref[...]
    k_a = ka_ref[...]
    for d in range(2):
        f = cur + (cur_sh[d] - cur) * mu_ref[d]
        f_lo = lo + (lo_sh[d] - lo) * mulo_ref[d]
        r = f[:, :C_WIDTH]
        k = f[:, C_WIDTH:2 * C_WIDTH]
        v = f[:, 2 * C_WIDTH:]
        w_lo = f_lo[:, :C_LORA_PAD]
        a_lo = f_lo[:, C_LORA_PAD:]
        x = w0_ref[d] + jnp.dot(jnp.tanh(w_lo), wup_ref[d], preferred_element_type=F32, precision=HIGHEST)
        w = -(jnp.maximum(-x, 0.0) + jnp.log(1.0 + jnp.exp(-jnp.abs(x)))) - 0.5
        decay = jnp.exp(-jnp.exp(w))
        a = jax.nn.sigmoid(a0_ref[d] + jnp.dot(a_lo, aup_ref[d], preferred_element_type=F32, precision=HIGHEST))
        transposed = lambda val: val.T.reshape(C_HEAD_DIM, C_HEADS, tm)
        kk = transposed(k * k_k)
        kk = kk / jnp.maximum(jnp.sqrt(jnp.sum(kk * kk, axis=0, keepdims=True)), 1e-12)
        for q, val in enumerate((r, decay, k * (1.0 + (a - 1.0) * k_a), v)):
            o_ref[d, q] = transposed(val)
        o_ref[d, 4] = -kk
        o_ref[d, 5] = kk * transposed(a)


def _rwkv_prep(proj, mu, mu_lo, w0, w_up, a0, a_up, k_k, k_a, batch, seq, tm=256):
    tm = min(tm, seq)
    nt = seq // tm
    halo = 8
    hb = tm // halo
    n_hblk = batch * seq // halo
    wide, narrow = 3 * C_WIDTH, 2 * C_LORA_PAD
    cur_blk, lo_blk = COL_CF // wide, COL_CLO // narrow

    def cur_map(blk):
        return lambda b, j: (b * nt + j, blk)

    def prev_map(blk):
        return lambda b, j: (jnp.maximum((b * nt + j) * hb - 1, 0), blk)

    def next_map(blk):
        return lambda b, j: (jnp.minimum((b * nt + j + 1) * hb, n_hblk - 1), blk)

    small = lambda a: pl.BlockSpec(a.shape, lambda b, j: (0,) * a.ndim)
    consts = (mu, mu_lo, w0, w_up, a0, a_up, k_k, k_a)
    return pl.pallas_call(
        _rwkv_prep_kernel,
        out_shape=jax.ShapeDtypeStruct((2, 6, C_HEAD_DIM, batch, C_HEADS, seq), F32),
        grid=(batch, nt),
        in_specs=[
            pl.BlockSpec((tm, wide), cur_map(cur_blk)),
            pl.BlockSpec((tm, narrow), cur_map(lo_blk)),
            pl.BlockSpec((halo, wide), prev_map(cur_blk)),
            pl.BlockSpec((halo, narrow), prev_map(lo_blk)),
            pl.BlockSpec((halo, wide), next_map(cur_blk)),
            pl.BlockSpec((halo, narrow), next_map(lo_blk)),
        ] + [small(a) for a in consts],
        out_specs=pl.BlockSpec((2, 6, C_HEAD_DIM, None, C_HEADS, tm), lambda b, j: (0, 0, 0, b, 0, j)),
        compiler_params=_cp("parallel", "arbitrary"),
        name="c_prep",
    )(proj, proj, proj, proj, proj, proj, *consts)


RELAYOUT_T = 128


def _rwkv_relayout_kernel(pf_ref, pb_ref, o_ref, *, tc):
    n = pf_ref.shape[0]
    t_blk = pf_ref.shape[-1]
    mirror = t_blk - 1 - lax.broadcasted_iota(jnp.int32, (pb_ref.shape[1] * pb_ref.shape[2], t_blk), 1)
    for k in range(n):
        bwd = jnp.take_along_axis(pb_ref[k].reshape(-1, t_blk), mirror, axis=1)
        cols = jnp.concatenate([pf_ref[k].reshape(-1, t_blk), bwd], axis=0).T
        for c in range(t_blk // tc):
            o_ref[c, k] = cols[c * tc:(c + 1) * tc]


def _rwkv_relayout(p, tc):
    _, nq, n, batch, heads, seq = p.shape
    t_blk = min(RELAYOUT_T, seq)
    nb = seq // t_blk
    lanes = 2 * batch * heads
    per_blk = t_blk // tc
    return pl.pallas_call(
        functools.partial(_rwkv_relayout_kernel, tc=tc),
        out_shape=jax.ShapeDtypeStruct((seq // tc, nq, n, tc, lanes), F32),
        grid=(nb, nq),
        in_specs=[
            pl.BlockSpec((None, None, n, batch, heads, t_blk), lambda c, q: (0, q, 0, 0, 0, c)),
            pl.BlockSpec((None, None, n, batch, heads, t_blk), lambda c, q: (1, q, 0, 0, 0, nb - 1 - c)),
        ],
        out_specs=pl.BlockSpec((per_blk, None, n, tc, lanes), lambda c, q: (c, q, 0, 0, 0)),
        compiler_params=_cp("parallel", "arbitrary"),
        name="c_relayout",
    )(p, p)


def _rwkv_scan_kernel(x_ref, rk_ref, lnw_ref, lnb_ref, out_ref, s_ref, o_ref, *, tc, k_chunk):
    n = s_ref.shape[0]
    lanes = s_ref.shape[2]

    @pl.when(pl.program_id(0) == 0)
    def _():
        s_ref[...] = jnp.zeros_like(s_ref)

    def row(q, k, t):
        return x_ref[pl.ds((q * n + k) * tc + t, 1), :]

    def step(t, carry):
        vv = x_ref[pl.ds(3 * n * tc + t, n, stride=tc), :]

        def sa_chunk(c, sa):
            base = pl.multiple_of(c * k_chunk, k_chunk)
            for j in range(k_chunk):
                sa = sa + s_ref[base + j] * row(4, base + j, t)
            return sa

        sa = lax.fori_loop(0, n // k_chunk, sa_chunk, jnp.zeros((n, lanes), F32))

        def update_chunk(c, o):
            base = pl.multiple_of(c * k_chunk, k_chunk)
            for j in range(k_chunk):
                k = base + j
                sk = s_ref[k] * row(1, k, t) + sa * row(5, k, t) + vv * row(2, k, t)
                s_ref[k] = sk
                o = o + sk * row(0, k, t)
            return o

        o_ref[pl.ds(t, n, stride=tc), :] = lax.fori_loop(0, n // k_chunk, update_chunk, jnp.zeros((n, lanes), F32))
        return carry

    lax.fori_loop(0, tc, step, 0)

    quantity = lambda q: x_ref[q * n * tc:(q + 1) * n * tc, :].reshape(n, tc, lanes)
    o = o_ref[...].reshape(n, tc, lanes)
    mean = jnp.mean(o, axis=0, keepdims=True)
    var = jnp.mean(jnp.square(o - mean), axis=0, keepdims=True)
    o = (o - mean) * lax.rsqrt(var + C_GN_EPS) * lnw_ref[...] + lnb_ref[...]
    bonus = jnp.sum(quantity(0) * quantity(2) * rk_ref[...], axis=0, keepdims=True)
    out_ref[...] = o + bonus * quantity(3)


def _rwkv_scan(xs, rk, lnw, lnb, k_chunk=32):
    nc, nq, n, tc, lanes = xs.shape
    tile = lambda: pl.BlockSpec((n, 1, lanes), lambda i: (0, 0, 0))
    return pl.pallas_call(
        functools.partial(_rwkv_scan_kernel, tc=tc, k_chunk=min(k_chunk, n)),
        out_shape=jax.ShapeDtypeStruct((n, nc * tc, lanes), F32),
        grid=(nc,),
        in_specs=[pl.BlockSpec((nq * n * tc, lanes), lambda i: (i, 0)), tile(), tile(), tile()],
        out_specs=pl.BlockSpec((n, tc, lanes), lambda i: (0, i, 0)),
        scratch_shapes=[pltpu.VMEM((n, n, lanes), F32), pltpu.VMEM((n * tc, lanes), F32)],
        compiler_params=_cp("arbitrary"),
        name="c_scan",
    )(xs.reshape(nc * nq * n * tc, lanes), rk, lnw, lnb)


def _rwkv_unlayout_kernel(o_ref, q_ref):
    for v in range(o_ref.shape[0]):
        rows = o_ref[v].T
        q_ref[:, v] = rows.reshape(q_ref.shape[0], q_ref.shape[2], q_ref.shape[3])


def _rwkv_unlayout(o, batch):
    n, seq, lanes = o.shape
    t_blk = min(RELAYOUT_T, seq)
    heads = lanes // (2 * batch)
    return pl.pallas_call(
        _rwkv_unlayout_kernel,
        out_shape=jax.ShapeDtypeStruct((2 * batch, n, heads, seq), F32),
        grid=(seq // t_blk,),
        in_specs=[pl.BlockSpec((n, t_blk, lanes), lambda c: (0, c, 0))],
        out_specs=pl.BlockSpec((2 * batch, n, heads, t_blk), lambda c: (0, 0, 0, c)),
        compiler_params=_cp("parallel"),
        name="c_unlayout",
    )(o)


def _rwkv_out_kernel(qf_ref, qb_ref, glo_ref, gup_ref, y_ref):
    t_blk = qf_ref.shape[-1]
    fwd = qf_ref[...].reshape(-1, t_blk)
    bwd = qb_ref[...].reshape(-1, t_blk)
    mirror = t_blk - 1 - lax.broadcasted_iota(jnp.int32, bwd.shape, 1)
    o = (fwd + jnp.take_along_axis(bwd, mirror, axis=1)).T
    g = jnp.dot(jax.nn.sigmoid(glo_ref[...]).astype(BF16), gup_ref[...], preferred_element_type=F32)
    y_ref[...] = (o * g).astype(y_ref.dtype)


def _rwkv_out(q, proj, g_up, batch, seq):
    _, n, heads, _ = q.shape
    t_blk = min(RELAYOUT_T, seq)
    nt = seq // t_blk
    g_blk = COL_CG // C_GATE_LORA
    return pl.pallas_call(
        _rwkv_out_kernel,
        out_shape=jax.ShapeDtypeStruct((batch * seq, C_WIDTH), BF16),
        grid=(batch, nt),
        in_specs=[
            pl.BlockSpec((None, n, heads, t_blk), lambda b, j: (b, 0, 0, j)),
            pl.BlockSpec((None, n, heads, t_blk), lambda b, j: (batch + b, 0, 0, nt - 1 - j)),
            pl.BlockSpec((t_blk, C_GATE_LORA), lambda b, j: (b * nt + j, g_blk)),
            pl.BlockSpec((C_GATE_LORA, C_WIDTH), lambda b, j: (0, 0)),
        ],
        out_specs=pl.BlockSpec((t_blk, C_WIDTH), lambda b, j: (b * nt + j, 0)),
        compiler_params=_cp("parallel", "arbitrary"),
        name="c_out",
    )(q, q, proj, g_up)


SCAN_TC = 16


def _heads_minor(p):
    return p.reshape(p.shape[:-1] + (C_HEADS, C_HEAD_DIM)).swapaxes(-1, -2).reshape(p.shape)


def _rwkv(proj, mu, w0, w_up, a0, a_up, g_up, k_k, k_a, r_k, ln_w, ln_b, batch, seq):
    pad_lo = C_LORA_PAD - C_LORA
    split = 3 * C_WIDTH
    mu_wide = jnp.concatenate([_heads_minor(mu[:, i * C_WIDTH:(i + 1) * C_WIDTH]) for i in range(3)], axis=-1)[:, None, :]
    mu_lo = jnp.concatenate([
        jnp.pad(mu[:, split:split + C_LORA], ((0, 0), (0, pad_lo))),
        jnp.pad(mu[:, split + C_LORA:], ((0, 0), (0, pad_lo))),
    ], axis=-1)[:, None, :]
    w_up_p = jnp.pad(_heads_minor(w_up), ((0, 0), (0, pad_lo), (0, 0)))
    a_up_p = jnp.pad(_heads_minor(a_up), ((0, 0), (0, pad_lo), (0, 0)))
    feats = _rwkv_prep(proj, mu_wide, mu_lo, _heads_minor(w0)[:, None, :], w_up_p, _heads_minor(a0)[:, None, :], a_up_p,
                       _heads_minor(k_k).reshape(1, C_WIDTH), _heads_minor(k_a).reshape(1, C_WIDTH), batch, seq)
    xs = _rwkv_relayout(feats, min(SCAN_TC, seq))
    per_lane = lambda p: jnp.tile(p.reshape(C_HEADS, C_HEAD_DIM).T, (1, 2 * batch))[:, None, :]
    o = _rwkv_scan(xs, per_lane(r_k), per_lane(ln_w), per_lane(ln_b))
    return _rwkv_out(_rwkv_unlayout(o, batch), proj, _heads_minor(g_up).astype(BF16), batch, seq)


def _merge_kernel(h_ref, wg_ref, wbr_ref, wc_ref, oa_ref, ob_ref, oc_ref, od_ref, o_ref, wgb_ref, wbrb_ref, wcb_ref):
    _cast_weights_once((wg_ref, wbr_ref, wc_ref), (wgb_ref, wbrb_ref, wcb_ref))
    h = h_ref[...]
    acc = None
    row = 0
    for i, b_ref in enumerate((oa_ref, ob_ref, oc_ref, od_ref)):
        width = b_ref.shape[1]
        w = wcb_ref[...] if b_ref is oc_ref else wbrb_ref[row:row + width, :]
        gate = jax.nn.sigmoid(jnp.dot(h, wgb_ref[i], preferred_element_type=F32))
        term = gate * jnp.dot(b_ref[...], w, preferred_element_type=F32)
        acc = term if acc is None else acc + term
        row += width
    o_ref[...] = acc.astype(o_ref.dtype)


def _merge(h, w_gate, w_branch, layer, w_branch_c, branches, tm=512, tn=512):
    m = h.shape[0]
    tm = min(tm, m)
    d_mix = w_branch.shape[-2]
    row = lambda a: pl.BlockSpec((tm, a.shape[1]), lambda j, i: (i, 0))
    return pl.pallas_call(
        _merge_kernel,
        out_shape=jax.ShapeDtypeStruct((m, D_MODEL), BF16),
        grid=(D_MODEL // tn, m // tm),
        in_specs=[
            row(h),
            _weight_spec((4, D_MODEL, tn), lambda j, i: (0, 0, j), layer),
            _weight_spec((d_mix, tn), lambda j, i: (0, j), layer),
            _weight_spec((w_branch_c.shape[0], tn), lambda j, i: (0, j)),
        ] + [row(b) for b in branches],
        out_specs=pl.BlockSpec((tm, tn), lambda j, i: (i, j)),
        scratch_shapes=[pltpu.VMEM((4, D_MODEL, tn), BF16), pltpu.VMEM((d_mix, tn), BF16), pltpu.VMEM((w_branch_c.shape[0], tn), BF16)],
        compiler_params=_cp("parallel", "arbitrary"),
        name="gated_merge",
    )(h, w_gate, w_branch, w_branch_c, *branches)


def _ffn_up_kernel(h_ref, wg_ref, wu_ref, o_ref, wgb_ref, wub_ref):
    _cast_weights_once((wg_ref, wu_ref), (wgb_ref, wub_ref))
    h = h_ref[...]
    g = jnp.dot(h, wgb_ref[...], preferred_element_type=F32)
    u = jnp.dot(h, wub_ref[...], preferred_element_type=F32)
    o_ref[...] = (g * jax.nn.sigmoid(g) * u).astype(o_ref.dtype)


def _ffn_up(h, w_gate, w_up, layer, tm=1024, tn=512):
    m, k = h.shape
    n = w_gate.shape[-1]
    tm = min(tm, m)
    return pl.pallas_call(
        _ffn_up_kernel,
        out_shape=jax.ShapeDtypeStruct((m, n), BF16),
        grid=(n // tn, m // tm),
        in_specs=[
            pl.BlockSpec((tm, k), lambda j, i: (i, 0)),
            _weight_spec((k, tn), lambda j, i: (0, j), layer),
            _weight_spec((k, tn), lambda j, i: (0, j), layer),
        ],
        out_specs=pl.BlockSpec((tm, tn), lambda j, i: (i, j)),
        scratch_shapes=[pltpu.VMEM((k, tn), BF16), pltpu.VMEM((k, tn), BF16)],
        compiler_params=_cp("parallel", "arbitrary"),
        name="ffn_up",
    )(h, w_gate, w_up)


def _pad_w_in(w):
    pad = ((0, 0), (0, C_LORA_PAD - C_LORA))
    lo = COL_CLO
    cf = [_heads_minor(w[:, COL_CF + i * C_WIDTH:COL_CF + (i + 1) * C_WIDTH]) for i in range(3)]
    return jnp.concatenate([
        w[:, :COL_CF],
        *cf,
        jnp.pad(w[:, lo:lo + C_LORA], pad),
        jnp.pad(w[:, lo + C_LORA:lo + 2 * C_LORA], pad),
        w[:, lo + 2 * C_LORA:],
    ], axis=1)


def kernel(x, norm_mix, w_in, a_q_norm, a_k_norm, b_conv_w, b_conv_b, b_filt_w1, b_filt_b1, b_filt_w2, b_filt_b2, b_filt_w3, b_filt_b3, b_filt_w4, b_filt_freq, b_skip, c_mu, c_w0, c_w_up, c_a0, c_a_up, c_g_up, c_k_k, c_k_a, c_r_k, c_ln_w, c_ln_b, d_lq1, d_lk1, d_lq2, d_lk2, d_subln, w_gate, w_branch, w_out, norm_ffn, w_ff_gate, w_ff_up, w_ff_down, norm_final):
    batch, seq, _ = x.shape
    m = batch * seq
    cos, sin = _rope_tables(seq)
    cos_half, sin_half, cos_int, sin_int = _dft_tables(seq)
    filt_feats = _filter_features(seq)
    x = x.reshape(m, D_MODEL)
    for l in range(DEPTH):
        h = _rmsnorm(x, norm_mix[l], BF16)
        proj = _matmul(h, _pad_w_in(w_in[l]), F32, 1024, D_IN_PAD // 4, name="in_proj")

        gains = jnp.concatenate([
            jnp.broadcast_to(a_q_norm[l], (A_HEADS, A_HEAD_DIM)),
            jnp.broadcast_to(a_k_norm[l], (A_KV_HEADS, A_HEAD_DIM)),
        ])[:, None, :]
        o_a = _attn_a(_qk_prep(proj, gains, cos, sin, seq), proj, batch, seq)

        k_re, k_im = _hyena_filter_spectrum(seq, filt_feats, cos_int, sin_int, b_filt_w1[l], b_filt_b1[l], b_filt_w2[l],
                                            b_filt_b2[l], b_filt_w3[l], b_filt_b3[l], b_filt_w4[l], b_filt_freq[l])
        o_b = _hyena(proj, b_conv_w[l], b_conv_b[l], b_skip[l], cos_half, sin_half, k_re, k_im, batch, seq)

        o_c = _rwkv(proj, c_mu[l], c_w0[l], c_w_up[l], c_a0[l], c_a_up[l], c_g_up[l], c_k_k[l], c_k_a[l], c_r_k[l],
                    c_ln_w[l], c_ln_b[l], batch, seq)

        lam_init = 0.8 - 0.6 * math.exp(-0.3 * l)
        lam_vecs = jnp.stack([d_lq1[l], d_lk1[l], d_lq2[l], d_lk2[l]])
        o_d = _attn_d(proj, lam_vecs, d_subln[l], lam_init, batch, seq)

        c_lo, c_hi = A_WIDTH + B_WIDTH, A_WIDTH + B_WIDTH + C_WIDTH
        w_br_c = _heads_minor(w_branch[l, c_lo:c_hi].T).T
        merged = _merge(h, w_gate, w_branch, l, w_br_c, (o_a, o_b, o_c, o_d))
        x, h2 = _matmul_res_norm(merged, w_out, l, x, norm_ffn[l])
        mid = _ffn_up(h2, w_ff_gate, w_ff_up, l)
        x = _matmul(mid, w_ff_down, F32, 1024, 512, residual=x, layer=l, name="ffn_down")
    return _rmsnorm(x, norm_final, F32).reshape(batch, seq, D_MODEL)
```

```python
import functools
import math

import jax
import jax.numpy as jnp
import numpy as np
from jax import lax
from jax.experimental import pallas as pl
from jax.experimental.pallas import tpu as pltpu

D_MODEL = 2048
DEPTH = 2
GRID_W = 64
NORM_EPS = 1e-6

A_HEADS = 8
A_KV_HEADS = 2
A_HEAD_DIM = 128
A_WIDTH = A_HEADS * A_HEAD_DIM
ROPE_THETA = 10000.0

B_WIDTH = 512
B_EMB_DIM = 33
B_FILTER_HIDDEN = 64
B_DECAY_TARGET = 1e-2
B_FAST_DECAY_PCT = 0.3
B_SLOW_DECAY_PCT = 1.5

C_HEADS = 8
C_HEAD_DIM = 64
C_WIDTH = C_HEADS * C_HEAD_DIM
C_LORA = 96
C_LORA_PAD = 128
C_GATE_LORA = 256
C_GN_EPS = 64e-5

D_HEADS = 4
D_HEAD_DIM = 64
D_V_DIM = 2 * D_HEAD_DIM
D_WIDTH = D_HEADS * D_V_DIM

FFN_HIDDEN = -(-8 * D_MODEL // (3 * 256)) * 256

COL_AQ = 0
COL_AK = COL_AQ + A_WIDTH
COL_AV = COL_AK + A_KV_HEADS * A_HEAD_DIM
COL_BU = COL_AV + A_KV_HEADS * A_HEAD_DIM
COL_CF = COL_BU + 3 * B_WIDTH
COL_CLO = COL_CF + 3 * C_WIDTH
COL_CG = COL_CLO + 2 * C_LORA_PAD
COL_DQ = COL_CG + C_GATE_LORA
COL_DK = COL_DQ + 2 * D_HEADS * D_HEAD_DIM
COL_DV = COL_DK + 2 * D_HEADS * D_HEAD_DIM
D_IN_PAD = COL_DV + D_WIDTH

VMEM_LIMIT_V7X = 56 * 1024 * 1024
F32 = jnp.float32
BF16 = jnp.bfloat16
HIGHEST = lax.Precision.HIGHEST
NT_DIMS = (((1,), (1,)), ((), ()))
LOG2_E = math.log2(math.e)


def _cp(*sem):
    return pltpu.CompilerParams(dimension_semantics=sem, vmem_limit_bytes=VMEM_LIMIT_V7X)


def _const_spec(shape):
    return pl.BlockSpec(shape, lambda *_: (0,) * len(shape), pipeline_mode=pl.Buffered(1))


SINGLE_BUFFER_BYTES = 8 * 1024 * 1024


def _weight_spec(shape, index_map, layer=None):
    mode = {"pipeline_mode": pl.Buffered(1)} if 4 * math.prod(shape) > SINGLE_BUFFER_BYTES else {}
    if layer is None:
        return pl.BlockSpec(shape, index_map, **mode)
    return pl.BlockSpec((None,) + tuple(shape), lambda *g: (layer,) + tuple(index_map(*g)), **mode)


def _rmsnorm_kernel(x_ref, g_ref, o_ref):
    x = x_ref[...]
    ms = jnp.mean(x * x, axis=-1, keepdims=True)
    o_ref[...] = (x * lax.rsqrt(ms + NORM_EPS) * g_ref[...]).astype(o_ref.dtype)


def _rmsnorm(x, g, out_dtype, tm=512):
    m, d = x.shape
    tm = min(tm, m)
    return pl.pallas_call(
        _rmsnorm_kernel,
        out_shape=jax.ShapeDtypeStruct((m, d), out_dtype),
        grid=(m // tm,),
        in_specs=[pl.BlockSpec((tm, d), lambda i: (i, 0)), pl.BlockSpec((1, d), lambda i: (0, 0))],
        out_specs=pl.BlockSpec((tm, d), lambda i: (i, 0)),
        compiler_params=_cp("parallel"),
        name="rmsnorm",
    )(x, g.reshape(1, d))


def _mm_f32_kernel(a_ref, b_ref, o_ref):
    def split(x):
        hi = x.astype(BF16)
        return hi, (x - hi.astype(F32)).astype(BF16)

    (a_hi, a_lo), (b_hi, b_lo) = split(a_ref[...]), split(b_ref[...])
    dot = functools.partial(jnp.dot, preferred_element_type=F32)
    o_ref[...] = dot(a_hi, b_hi) + (dot(a_hi, b_lo) + dot(a_lo, b_hi))


def _matmul_f32(a, b, tm, tn, name):
    m, k = a.shape
    n = b.shape[1]
    tm, tn = min(tm, m), min(tn, n)
    return pl.pallas_call(
        _mm_f32_kernel,
        out_shape=jax.ShapeDtypeStruct((m, n), F32),
        grid=(m // tm, n // tn),
        in_specs=[pl.BlockSpec((tm, k), lambda i, j: (i, 0)), pl.BlockSpec((k, tn), lambda i, j: (0, j))],
        out_specs=pl.BlockSpec((tm, tn), lambda i, j: (i, j)),
        compiler_params=_cp("parallel", "arbitrary"),
        name=name,
    )(a, b)


def _cast_weights_once(w_refs, wb_refs):
    @pl.when(pl.program_id(1) == 0)
    def _():
        for w_ref, wb_ref in zip(w_refs, wb_refs):
            wb_ref[...] = w_ref[...].astype(BF16)


def _mm_kernel(a_ref, w_ref, o_ref, wb_ref):
    _cast_weights_once((w_ref,), (wb_ref,))
    o_ref[...] = jnp.dot(a_ref[...], wb_ref[...], preferred_element_type=F32).astype(o_ref.dtype)


def _mm_res_kernel(a_ref, w_ref, r_ref, o_ref, wb_ref):
    _cast_weights_once((w_ref,), (wb_ref,))
    o_ref[...] = r_ref[...] + jnp.dot(a_ref[...], wb_ref[...], preferred_element_type=F32)


def _matmul(a, w, out_dtype, tm, tn, residual=None, layer=None, name="matmul"):
    m, k = a.shape
    n = w.shape[-1]
    tm, tn = min(tm, m), min(tn, n)
    in_specs = [pl.BlockSpec((tm, k), lambda j, i: (i, 0)), _weight_spec((k, tn), lambda j, i: (0, j), layer)]
    args = [a, w]
    body = _mm_kernel
    if residual is not None:
        body = _mm_res_kernel
        in_specs.append(pl.BlockSpec((tm, tn), lambda j, i: (i, j)))
        args.append(residual)
    return pl.pallas_call(
        body,
        out_shape=jax.ShapeDtypeStruct((m, n), out_dtype),
        grid=(n // tn, m // tm),
        in_specs=in_specs,
        out_specs=pl.BlockSpec((tm, tn), lambda j, i: (i, j)),
        scratch_shapes=[pltpu.VMEM((k, tn), BF16)],
        compiler_params=_cp("parallel", "arbitrary"),
        name=name,
    )(*args)


def _mm_res_norm_kernel(a_ref, w_ref, r_ref, g_ref, x_ref, h_ref, wb_ref):
    @pl.when(pl.program_id(0) == 0)
    def _():
        wb_ref[...] = w_ref[...].astype(BF16)

    x = r_ref[...] + jnp.dot(a_ref[...], wb_ref[...], preferred_element_type=F32)
    x_ref[...] = x
    ms = jnp.mean(x * x, axis=-1, keepdims=True)
    h_ref[...] = (x * lax.rsqrt(ms + NORM_EPS) * g_ref[...]).astype(h_ref.dtype)


def _matmul_res_norm(a, w, layer, residual, gain, tm=512):
    m, k = a.shape
    n = w.shape[-1]
    tm = min(tm, m)
    rows = lambda width: pl.BlockSpec((tm, width), lambda i: (i, 0))
    return pl.pallas_call(
        _mm_res_norm_kernel,
        out_shape=[jax.ShapeDtypeStruct((m, n), F32), jax.ShapeDtypeStruct((m, n), BF16)],
        grid=(m // tm,),
        in_specs=[rows(k), _weight_spec((k, n), lambda i: (0, 0), layer), rows(n), pl.BlockSpec((1, n), lambda i: (0, 0))],
        out_specs=[rows(n), rows(n)],
        scratch_shapes=[pltpu.VMEM((k, n), BF16)],
        compiler_params=_cp("arbitrary"),
        name="out_proj_norm",
    )(a, w, residual, gain.reshape(1, n))


def _rope_tables(seq):
    rows = seq // GRID_W
    row_idx = jnp.repeat(jnp.arange(rows, dtype=F32), GRID_W)
    col_idx = jnp.tile(jnp.arange(GRID_W, dtype=F32), rows)
    axis_dim = A_HEAD_DIM // 2
    inv_freq = ROPE_THETA ** (-jnp.arange(0, axis_dim, 2, dtype=F32) / axis_dim)
    ang_r = row_idx[:, None] * inv_freq[None, :]
    ang_c = col_idx[:, None] * inv_freq[None, :]
    ang = jnp.concatenate([ang_r, ang_r, ang_c, ang_c], axis=-1)
    return jnp.cos(ang), jnp.sin(ang)


def _qk_prep_kernel(x_ref, g_ref, cos_ref, sin_ref, o_ref):
    cos = cos_ref[...]
    sin = sin_ref[...]
    lane = lax.broadcasted_iota(jnp.int32, cos.shape, 1)
    quarter = A_HEAD_DIM // 4
    first = (lane % (2 * quarter)) < quarter
    for h in range(A_HEADS + A_KV_HEADS):
        cols = slice(h * A_HEAD_DIM, (h + 1) * A_HEAD_DIM)
        x = x_ref[:, cols]
        xn = x * lax.rsqrt(jnp.mean(x * x, axis=-1, keepdims=True) + NORM_EPS) * g_ref[h]
        rot = jnp.where(first, -pltpu.roll(xn, A_HEAD_DIM - quarter, 1), pltpu.roll(xn, quarter, 1))
        y = xn * cos + rot * sin
        if h < A_HEADS:
            y = y * (A_HEAD_DIM**-0.5 * LOG2_E)
        o_ref[:, cols] = y.astype(o_ref.dtype)


def _qk_prep(proj, gains, cos, sin, seq, tm=256):
    m = proj.shape[0]
    tm = min(tm, seq)
    width = (A_HEADS + A_KV_HEADS) * A_HEAD_DIM
    nt = seq // tm
    return pl.pallas_call(
        _qk_prep_kernel,
        out_shape=jax.ShapeDtypeStruct((m, width), BF16),
        grid=(m // tm,),
        in_specs=[
            pl.BlockSpec((tm, width), lambda i: (i, 0)),
            pl.BlockSpec(gains.shape, lambda i: (0, 0, 0)),
            pl.BlockSpec((tm, A_HEAD_DIM), lambda i: (i % nt, 0)),
            pl.BlockSpec((tm, A_HEAD_DIM), lambda i: (i % nt, 0)),
        ],
        out_specs=pl.BlockSpec((tm, width), lambda i: (i, 0)),
        compiler_params=_cp("parallel"),
        name="a_qk_prep",
    )(proj, gains, cos, sin)


class _OnlineSoftmax:
    def __init__(self, rows, width):
        self.m = jnp.full((rows, 1), -jnp.inf, F32)
        self.l = jnp.zeros((rows, 1), F32)
        self.acc = jnp.zeros((rows, width), F32)

    def add(self, s, v):
        m_new = jnp.maximum(self.m, jnp.max(s, axis=-1, keepdims=True))
        p = jnp.exp2(s - m_new)
        alpha = jnp.exp2(self.m - m_new)
        self.l = alpha * self.l + jnp.sum(p, axis=-1, keepdims=True)
        self.acc = alpha * self.acc + jnp.dot(p.astype(BF16), v, preferred_element_type=F32)
        self.m = m_new

    def result(self):
        return self.acc / self.l


def _attn_a_kernel(q_ref, k_ref, v_ref, o_ref, *, kc):
    q = q_ref[...]
    sm = _OnlineSoftmax(q.shape[0], A_HEAD_DIM)
    for c in range(k_ref.shape[0] // kc):
        keys = slice(c * kc, (c + 1) * kc)
        s = lax.dot_general(q, k_ref[keys, :], NT_DIMS, preferred_element_type=F32)
        sm.add(s, v_ref[keys, :].astype(BF16))
    o_ref[...] = sm.result().astype(o_ref.dtype)


def _attn_a(qk, proj, batch, seq, tq=1024, kc=1024):
    tq = min(tq, seq)
    kc = min(kc, seq)
    nq = seq // tq
    group = A_HEADS // A_KV_HEADS
    k_blk = COL_AK // A_HEAD_DIM
    v_blk = COL_AV // A_HEAD_DIM
    return pl.pallas_call(
        functools.partial(_attn_a_kernel, kc=kc),
        out_shape=jax.ShapeDtypeStruct((batch * seq, A_WIDTH), BF16),
        grid=(batch, A_HEADS, nq),
        in_specs=[
            pl.BlockSpec((tq, A_HEAD_DIM), lambda b, h, i: (b * nq + i, h)),
            pl.BlockSpec((seq, A_HEAD_DIM), lambda b, h, i: (b, k_blk + h // group)),
            pl.BlockSpec((seq, A_HEAD_DIM), lambda b, h, i: (b, v_blk + h // group)),
        ],
        out_specs=pl.BlockSpec((tq, A_HEAD_DIM), lambda b, h, i: (b * nq + i, h)),
        compiler_params=_cp("parallel", "arbitrary", "arbitrary"),
        name="a_attention",
    )(qk, qk, proj)


def _attn_d_kernel(q_ref, k_ref, v_ref, slope_ref, lam_ref, g_ref, o_ref, *, tq, kc, lam_init):
    q = q_ref[...] * (D_HEAD_DIM**-0.5 * LOG2_E)
    lane = lax.broadcasted_iota(jnp.int32, q.shape, 1)
    q_maps = (jnp.where(lane < D_HEAD_DIM, q, 0.0).astype(BF16), jnp.where(lane >= D_HEAD_DIM, q, 0.0).astype(BF16))
    slope = slope_ref[0][:, 0:1] * LOG2_E
    rel = pl.program_id(2) * tq + lax.broadcasted_iota(jnp.int32, (tq, kc), 0) - lax.broadcasted_iota(jnp.int32, (tq, kc), 1)
    rel = slope * rel.astype(F32)
    maps = (_OnlineSoftmax(tq, D_V_DIM), _OnlineSoftmax(tq, D_V_DIM))
    for c in range(k_ref.shape[0] // kc):
        keys = slice(c * kc, (c + 1) * kc)
        k = k_ref[keys, :].astype(BF16)
        v = v_ref[keys, :].astype(BF16)
        bias = jnp.abs(rel - slope * float(c * kc))
        for q_map, sm in zip(q_maps, maps):
            sm.add(lax.dot_general(q_map, k, NT_DIMS, preferred_element_type=F32) - bias, v)
    lam_v = lam_ref[...]
    lam = (
        jnp.exp(jnp.sum(lam_v[0:1] * lam_v[1:2], axis=-1, keepdims=True))
        - jnp.exp(jnp.sum(lam_v[2:3] * lam_v[3:4], axis=-1, keepdims=True))
        + lam_init
    )
    o = maps[0].result() - lam * maps[1].result()
    o = o * lax.rsqrt(jnp.mean(o * o, axis=-1, keepdims=True) + NORM_EPS) * g_ref[...]
    o_ref[...] = (o * (1.0 - lam_init)).astype(o_ref.dtype)


def _attn_d(proj, lam_vecs, subln, lam_init, batch, seq, tq=1024, kc=512):
    tq = min(tq, seq)
    kc = min(kc, seq)
    nq = seq // tq
    slopes = 2.0 ** (-8.0 * np.arange(1, D_HEADS + 1, dtype=np.float32) / D_HEADS)
    slopes = jnp.asarray(np.broadcast_to(slopes[:, None, None], (D_HEADS, 1, 128)).astype(np.float32))
    q_blk, k_blk, v_blk = COL_DQ // D_V_DIM, COL_DK // D_V_DIM, COL_DV // D_V_DIM
    return pl.pallas_call(
        functools.partial(_attn_d_kernel, tq=tq, kc=kc, lam_init=lam_init),
        out_shape=jax.ShapeDtypeStruct((batch * seq, D_WIDTH), BF16),
        grid=(batch, D_HEADS, nq),
        in_specs=[
            pl.BlockSpec((tq, D_V_DIM), lambda b, h, i: (b * nq + i, q_blk + h)),
            pl.BlockSpec((seq, D_V_DIM), lambda b, h, i: (b, k_blk + h)),
            pl.BlockSpec((seq, D_V_DIM), lambda b, h, i: (b, v_blk + h)),
            pl.BlockSpec((1, 1, 128), lambda b, h, i: (h, 0, 0)),
            pl.BlockSpec((4, D_HEAD_DIM), lambda b, h, i: (0, 0)),
            pl.BlockSpec((1, D_V_DIM), lambda b, h, i: (0, 0)),
        ],
        out_specs=pl.BlockSpec((tq, D_V_DIM), lambda b, h, i: (b * nq + i, h)),
        compiler_params=_cp("parallel", "arbitrary", "arbitrary"),
        name="d_attention",
    )(proj, proj, proj, slopes, lam_vecs, subln.reshape(1, D_V_DIM))


DFT_ROW_SPLIT = 64


def _dft_tables(seq):
    n = 2 * seq
    blk = min(DFT_ROW_SPLIT, seq)
    col = jnp.arange(seq, dtype=jnp.int32)[None, :]
    hi = jnp.arange(seq // blk, dtype=jnp.int32)[:, None]
    lo = jnp.arange(blk, dtype=jnp.int32)[:, None]

    def cos_sin(index, period):
        ang = (index % period).astype(F32) * (2.0 * math.pi / period)
        return jnp.cos(ang), jnp.sin(ang)

    def tables(col_term, period):
        (ca, sa), (cb, sb) = cos_sin(2 * blk * hi * col_term, period), cos_sin((2 * lo + 1) * col_term, period)
        ca, sa, cb, sb = ca[:, None, :], sa[:, None, :], cb[None], sb[None]
        return (ca * cb - sa * sb).reshape(seq, seq), (sa * cb + ca * sb).reshape(seq, seq)

    cos_half, sin_half = tables(2 * col + 1, 4 * n)
    cos_int, sin_int = tables(col, 2 * n)
    return cos_half.astype(BF16), sin_half.astype(BF16), cos_int, sin_int


def _filter_features(seq):
    t = jnp.linspace(0.0, 1.0, seq, dtype=F32)[:, None]
    n_bands = (B_EMB_DIM - 1) // 2
    bands = jnp.linspace(1e-4, n_bands - 1, n_bands, dtype=F32)[None, :]
    ang = (2.0 * math.pi / seq) * jnp.arange(seq, dtype=F32)[:, None] * bands
    z = jnp.concatenate([t, jnp.cos(ang), -jnp.sin(ang)], axis=-1)
    z = jnp.pad(z, ((0, 0), (0, B_FILTER_HIDDEN - B_EMB_DIM)))
    max_decay = math.log(B_DECAY_TARGET) / B_FAST_DECAY_PCT
    min_decay = math.log(B_DECAY_TARGET) / B_SLOW_DECAY_PCT
    deltas = jnp.abs(jnp.linspace(min_decay, max_decay, B_WIDTH, dtype=F32))[None, :]
    return z, t, deltas


def _filter_kernel(z_ref, t_ref, dl_ref, w1_ref, b1_ref, w2_ref, b2_ref, w3_ref, b3_ref, w4_ref, fr_ref, hs_ref, hd_ref):
    fr = fr_ref[...]
    hid = jnp.sin(fr * (jnp.dot(z_ref[...], w1_ref[...], preferred_element_type=F32, precision=HIGHEST) + b1_ref[...]))
    hid = jnp.sin(fr * (jnp.dot(hid, w2_ref[...], preferred_element_type=F32, precision=HIGHEST) + b2_ref[...]))
    hid = jnp.sin(fr * (jnp.dot(hid, w3_ref[...], preferred_element_type=F32, precision=HIGHEST) + b3_ref[...]))
    h = jnp.dot(hid, w4_ref[...], preferred_element_type=F32, precision=HIGHEST)
    window = jnp.exp(-t_ref[...] * dl_ref[...])
    h_fwd = h[:, :B_WIDTH] * window
    h_bwd = h[:, B_WIDTH:] * window
    row = lax.broadcasted_iota(jnp.int32, h_bwd.shape, 0)
    h_bwd = jnp.where(row == 0, 0.0, h_bwd)
    norm = jnp.sum(jnp.abs(h_fwd), axis=0, keepdims=True) + jnp.sum(jnp.abs(h_bwd), axis=0, keepdims=True)
    seq = h.shape[0]
    inv_n = 1.0 / seq
    hs_ref[...] = (h_fwd + h_bwd) / norm * inv_n
    hd_ref[...] = (h_bwd - h_fwd) / norm * inv_n


def _hyena_filter_spectrum(seq, feats, cos_int, sin_int, w1, b1, w2, b2, w3, b3, w4, freq):
    z, t, deltas = feats
    w1p = jnp.pad(w1, ((0, B_FILTER_HIDDEN - B_EMB_DIM), (0, 0)))
    row = lambda v: v.reshape(1, -1)
    args = (z, t, deltas, w1p, row(b1), w2, row(b2), w3, row(b3), w4, row(freq))
    h_sum, h_diff = pl.pallas_call(
        _filter_kernel,
        out_shape=[jax.ShapeDtypeStruct((seq, B_WIDTH), F32)] * 2,
        in_specs=[pl.BlockSpec(a.shape, lambda: (0, 0)) for a in args],
        out_specs=[pl.BlockSpec((seq, B_WIDTH), lambda: (0, 0))] * 2,
        compiler_params=pltpu.CompilerParams(vmem_limit_bytes=VMEM_LIMIT_V7X),
        name="b_filter",
    )(*args)
    k_re = _matmul_f32(cos_int, h_sum, 256, B_WIDTH, name="b_filter_dft_re")
    k_im = _matmul_f32(sin_int, h_diff, 256, B_WIDTH, name="b_filter_dft_im")
    return k_re, k_im


def _hyena_kernel(v_ref, x1_ref, x0_ref, wv_ref, w1_ref, w0_ref, bv_ref, b1_ref, b0_ref, skip_ref, c_ref, s_ref, kre_ref, kim_ref, o_ref,
                  *, f_blk):
    seq = v_ref.shape[0]
    row = lax.broadcasted_iota(jnp.int32, v_ref.shape, 0)

    def conv3(u_ref, w_ref, b_ref):
        u = u_ref[...]
        w = w_ref[...]
        u_prev = jnp.where(row == 0, 0.0, pltpu.roll(u, 1, 0))
        u_next = jnp.where(row == seq - 1, 0.0, pltpu.roll(u, seq - 1, 0))
        return w[0:1] * u_prev + w[1:2] * u + w[2:3] * u_next + b_ref[...]

    z = conv3(v_ref, wv_ref, bv_ref) * conv3(x1_ref, w1_ref, b1_ref)
    zb = z.astype(BF16)
    y = z * skip_ref[...]
    for f0 in range(0, seq, f_blk):
        fs = slice(f0, f0 + f_blk)
        cz = jnp.dot(c_ref[fs, :], zb, preferred_element_type=F32)
        sz = jnp.dot(s_ref[fs, :], zb, preferred_element_type=F32)
        k_re = kre_ref[fs, :]
        k_im = kim_ref[fs, :]
        y_re = (cz * k_re + sz * k_im).astype(BF16)
        y_im = (cz * k_im - sz * k_re).astype(BF16)
        y = y + (jnp.dot(c_ref[:, fs], y_re, preferred_element_type=F32) - jnp.dot(s_ref[:, fs], y_im, preferred_element_type=F32))
    o_ref[...] = (y * conv3(x0_ref, w0_ref, b0_ref)).astype(o_ref.dtype)


def _hyena(proj, conv_w, conv_b, skip, cos_half, sin_half, k_re, k_im, batch, seq, cb=256, f_blk=1024):
    ncb = B_WIDTH // cb
    u_blk = COL_BU // cb
    f_blk = min(f_blk, seq)

    def u_spec(part):
        return pl.BlockSpec((seq, cb), lambda j, b: (b, u_blk + part * ncb + j))

    def w_spec(rows, part):
        return pl.BlockSpec((rows, cb), lambda j, b: (0, part * ncb + j))

    k_spec = pl.BlockSpec((seq, cb), lambda j, b: (0, j), pipeline_mode=pl.Buffered(1))
    return pl.pallas_call(
        functools.partial(_hyena_kernel, f_blk=f_blk),
        out_shape=jax.ShapeDtypeStruct((batch * seq, B_WIDTH), BF16),
        grid=(ncb, batch),
        in_specs=[
            u_spec(0), u_spec(1), u_spec(2),
            w_spec(3, 0), w_spec(3, 1), w_spec(3, 2),
            w_spec(1, 0), w_spec(1, 1), w_spec(1, 2),
            pl.BlockSpec((1, cb), lambda j, b: (0, j)),
            _const_spec((seq, seq)), _const_spec((seq, seq)),
            k_spec, k_spec,
        ],
        out_specs=pl.BlockSpec((seq, cb), lambda j, b: (b, j)),
        compiler_params=_cp("parallel", "arbitrary"),
        name="b_hyena",
    )(proj, proj, proj, conv_w, conv_w, conv_w, conv_b.reshape(1, -1), conv_b.reshape(1, -1), conv_b.reshape(1, -1),
      skip.reshape(1, -1), cos_half, sin_half, k_re, k_im)


def _rwkv_prep_kernel(cur_ref, lo_ref, pcur_ref, plo_ref, ncur_ref, nlo_ref, mu_ref, mulo_ref, w0_ref, wup_ref, a0_ref, aup_ref,
                      kk_ref, ka_ref, o_ref, *, chunk):
    first = pl.program_id(1) == 0
    last = pl.program_id(1) == pl.num_programs(1) - 1
    halo = pcur_ref.shape[0]
    tm = cur_ref.shape[0]
    row_in_chunk = lax.broadcasted_iota(jnp.int32, (tm, C_WIDTH), 0) % chunk

    def neighbours(x_ref, p_ref, n_ref):
        x = x_ref[...]
        row = lax.broadcasted_iota(jnp.int32, x.shape, 0)
        p_row = jnp.where(first, 0.0, p_ref[halo - 1:halo, :])
        n_row = jnp.where(last, 0.0, n_ref[0:1, :])
        prev = jnp.where(row == 0, p_row, pltpu.roll(x, 1, 0))
        nxt = jnp.where(row == tm - 1, n_row, pltpu.roll(x, tm - 1, 0))
        return x, (prev, nxt)

    cur, cur_sh = neighbours(cur_ref, pcur_ref, ncur_ref)
    lo, lo_sh = neighbours(lo_ref, plo_ref, nlo_ref)
    k_k = kk_ref[...]
    k_a = ka_ref[...]
    for d in range(2):
        f = cur + (cur_sh[d] - cur) * mu_ref[d]
        f_lo = lo + (lo_sh[d] - lo) * mulo_ref[d]
        r = f[:, :C_WIDTH]
        k = f[:, C_WIDTH:2 * C_WIDTH]
        v = f[:, 2 * C_WIDTH:]
        w_lo = f_lo[:, :C_LORA_PAD]
        a_lo = f_lo[:, C_LORA_PAD:]
        x = w0_ref[d] + jnp.dot(jnp.tanh(w_lo), wup_ref[d], preferred_element_type=F32, precision=HIGHEST)
        w = -(jnp.maximum(-x, 0.0) + jnp.log(1.0 + jnp.exp(-jnp.abs(x)))) - 0.5
        neg_log_decay = jnp.exp(w)
        log_g = neg_log_decay
        shift = 1
        while shift < chunk:
            if d == 0:
                moved = jnp.where(row_in_chunk >= shift, pltpu.roll(log_g, shift, 0), 0.0)
            else:
                moved = jnp.where(row_in_chunk < chunk - shift, pltpu.roll(log_g, tm - shift, 0), 0.0)
            log_g = log_g + moved
            shift *= 2
        g = jnp.exp(-log_g)
        g_inv = jnp.exp(log_g)
        g_before = jnp.exp(neg_log_decay - log_g)
        a = jax.nn.sigmoid(a0_ref[d] + jnp.dot(a_lo, aup_ref[d], preferred_element_type=F32, precision=HIGHEST))
        transposed = lambda val: val.T.reshape(C_HEAD_DIM, C_HEADS, tm)
        kk = transposed(k * k_k)
        kk = kk / jnp.maximum(jnp.sqrt(jnp.sum(kk * kk, axis=0, keepdims=True)), 1e-12)
        for q, val in enumerate((r * g, g, k * (1.0 + (a - 1.0) * k_a) * g_inv, v)):
            o_ref[d, q] = transposed(val)
        o_ref[d, 4] = -kk * transposed(g_before)
        o_ref[d, 5] = kk * transposed(a * g_inv)


def _rwkv_prep(proj, mu, mu_lo, w0, w_up, a0, a_up, k_k, k_a, batch, seq, chunk, tm=256):
    tm = min(tm, seq)
    nt = seq // tm
    halo = 8
    hb = tm // halo
    n_hblk = batch * seq // halo
    wide, narrow = 3 * C_WIDTH, 2 * C_LORA_PAD
    cur_blk, lo_blk = COL_CF // wide, COL_CLO // narrow

    def cur_map(blk):
        return lambda b, j: (b * nt + j, blk)

    def prev_map(blk):
        return lambda b, j: (jnp.maximum((b * nt + j) * hb - 1, 0), blk)

    def next_map(blk):
        return lambda b, j: (jnp.minimum((b * nt + j + 1) * hb, n_hblk - 1), blk)

    small = lambda a: pl.BlockSpec(a.shape, lambda b, j: (0,) * a.ndim)
    consts = (mu, mu_lo, w0, w_up, a0, a_up, k_k, k_a)
    return pl.pallas_call(
        functools.partial(_rwkv_prep_kernel, chunk=chunk),
        out_shape=jax.ShapeDtypeStruct((2, 6, C_HEAD_DIM, batch, C_HEADS, seq), F32),
        grid=(batch, nt),
        in_specs=[
            pl.BlockSpec((tm, wide), cur_map(cur_blk)),
            pl.BlockSpec((tm, narrow), cur_map(lo_blk)),
            pl.BlockSpec((halo, wide), prev_map(cur_blk)),
            pl.BlockSpec((halo, narrow), prev_map(lo_blk)),
            pl.BlockSpec((halo, wide), next_map(cur_blk)),
            pl.BlockSpec((halo, narrow), next_map(lo_blk)),
        ] + [small(a) for a in consts],
        out_specs=pl.BlockSpec((2, 6, C_HEAD_DIM, None, C_HEADS, tm), lambda b, j: (0, 0, 0, b, 0, j)),
        compiler_params=_cp("parallel", "arbitrary"),
        name="c_prep",
    )(proj, proj, proj, proj, proj, proj, *consts)


RELAYOUT_T = 128


def _rwkv_relayout_kernel(pf_ref, pb_ref, o_ref, *, tc):
    n = pf_ref.shape[0]
    t_blk = pf_ref.shape[-1]
    mirror = t_blk - 1 - lax.broadcasted_iota(jnp.int32, (pb_ref.shape[1] * pb_ref.shape[2], t_blk), 1)
    for k in range(n):
        bwd = jnp.take_along_axis(pb_ref[k].reshape(-1, t_blk), mirror, axis=1)
        cols = jnp.concatenate([pf_ref[k].reshape(-1, t_blk), bwd], axis=0).T
        for c in range(t_blk // tc):
            o_ref[c, k] = cols[c * tc:(c + 1) * tc]


def _rwkv_relayout(p, tc):
    _, nq, n, batch, heads, seq = p.shape
    t_blk = min(RELAYOUT_T, seq)
    nb = seq // t_blk
    lanes = 2 * batch * heads
    per_blk = t_blk // tc
    return pl.pallas_call(
        functools.partial(_rwkv_relayout_kernel, tc=tc),
        out_shape=jax.ShapeDtypeStruct((seq // tc, nq, n, tc, lanes), F32),
        grid=(nb, nq),
        in_specs=[
            pl.BlockSpec((None, None, n, batch, heads, t_blk), lambda c, q: (0, q, 0, 0, 0, c)),
            pl.BlockSpec((None, None, n, batch, heads, t_blk), lambda c, q: (1, q, 0, 0, 0, nb - 1 - c)),
        ],
        out_specs=pl.BlockSpec((per_blk, None, n, tc, lanes), lambda c, q: (c, q, 0, 0, 0)),
        compiler_params=_cp("parallel", "arbitrary"),
        name="c_relayout",
    )(p, p)


def _rwkv_scan_kernel(x_ref, rk_ref, lnw_ref, lnb_ref, out_ref, s_ref, o_ref, *, tc, k_chunk):
    n = s_ref.shape[0]
    lanes = s_ref.shape[2]

    @pl.when(pl.program_id(0) == 0)
    def _():
        s_ref[...] = jnp.zeros_like(s_ref)

    def row(q, k, t):
        return x_ref[pl.ds((q * n + k) * tc + t, 1), :]

    def step(t, carry):
        vv = x_ref[pl.ds(3 * n * tc + t, n, stride=tc), :]

        def sa_chunk(c, sa):
            base = pl.multiple_of(c * k_chunk, k_chunk)
            for j in range(k_chunk):
                sa = sa + s_ref[base + j] * row(4, base + j, t)
            return sa

        sa = lax.fori_loop(0, n // k_chunk, sa_chunk, jnp.zeros((n, lanes), F32))

        def update_chunk(c, o):
            base = pl.multiple_of(c * k_chunk, k_chunk)
            for j in range(k_chunk):
                k = base + j
                sk = s_ref[k] + (sa * row(5, k, t) + vv * row(2, k, t))
                s_ref[k] = sk
                o = o + sk * row(0, k, t)
            return o

        o_ref[pl.ds(t, n, stride=tc), :] = lax.fori_loop(0, n // k_chunk, update_chunk, jnp.zeros((n, lanes), F32))
        return carry

    lax.fori_loop(0, tc, step, 0)
    for k in range(n):
        s_ref[k] = s_ref[k] * x_ref[(n + k) * tc + tc - 1:(n + k) * tc + tc, :]

    quantity = lambda q: x_ref[q * n * tc:(q + 1) * n * tc, :].reshape(n, tc, lanes)
    o = o_ref[...].reshape(n, tc, lanes)
    mean = jnp.mean(o, axis=0, keepdims=True)
    var = jnp.mean(jnp.square(o - mean), axis=0, keepdims=True)
    o = (o - mean) * lax.rsqrt(var + C_GN_EPS) * lnw_ref[...] + lnb_ref[...]
    bonus = jnp.sum(quantity(0) * quantity(2) * rk_ref[...], axis=0, keepdims=True)
    out_ref[...] = o + bonus * quantity(3)


def _rwkv_scan(xs, rk, lnw, lnb, k_chunk=32):
    nc, nq, n, tc, lanes = xs.shape
    tile = lambda: pl.BlockSpec((n, 1, lanes), lambda i: (0, 0, 0))
    return pl.pallas_call(
        functools.partial(_rwkv_scan_kernel, tc=tc, k_chunk=min(k_chunk, n)),
        out_shape=jax.ShapeDtypeStruct((n, nc * tc, lanes), F32),
        grid=(nc,),
        in_specs=[pl.BlockSpec((nq * n * tc, lanes), lambda i: (i, 0)), tile(), tile(), tile()],
        out_specs=pl.BlockSpec((n, tc, lanes), lambda i: (0, i, 0)),
        scratch_shapes=[pltpu.VMEM((n, n, lanes), F32), pltpu.VMEM((n * tc, lanes), F32)],
        compiler_params=_cp("arbitrary"),
        name="c_scan",
    )(xs.reshape(nc * nq * n * tc, lanes), rk, lnw, lnb)


def _rwkv_unlayout_kernel(o_ref, q_ref):
    for v in range(o_ref.shape[0]):
        rows = o_ref[v].T
        q_ref[:, v] = rows.reshape(q_ref.shape[0], q_ref.shape[2], q_ref.shape[3])


def _rwkv_unlayout(o, batch):
    n, seq, lanes = o.shape
    t_blk = min(RELAYOUT_T, seq)
    heads = lanes // (2 * batch)
    return pl.pallas_call(
        _rwkv_unlayout_kernel,
        out_shape=jax.ShapeDtypeStruct((2 * batch, n, heads, seq), F32),
        grid=(seq // t_blk,),
        in_specs=[pl.BlockSpec((n, t_blk, lanes), lambda c: (0, c, 0))],
        out_specs=pl.BlockSpec((2 * batch, n, heads, t_blk), lambda c: (0, 0, 0, c)),
        compiler_params=_cp("parallel"),
        name="c_unlayout",
    )(o)


def _rwkv_out_kernel(qf_ref, qb_ref, glo_ref, gup_ref, y_ref):
    t_blk = qf_ref.shape[-1]
    fwd = qf_ref[...].reshape(-1, t_blk)
    bwd = qb_ref[...].reshape(-1, t_blk)
    mirror = t_blk - 1 - lax.broadcasted_iota(jnp.int32, bwd.shape, 1)
    o = (fwd + jnp.take_along_axis(bwd, mirror, axis=1)).T
    g = jnp.dot(jax.nn.sigmoid(glo_ref[...]).astype(BF16), gup_ref[...], preferred_element_type=F32)
    y_ref[...] = (o * g).astype(y_ref.dtype)


def _rwkv_out(q, proj, g_up, batch, seq):
    _, n, heads, _ = q.shape
    t_blk = min(RELAYOUT_T, seq)
    nt = seq // t_blk
    g_blk = COL_CG // C_GATE_LORA
    return pl.pallas_call(
        _rwkv_out_kernel,
        out_shape=jax.ShapeDtypeStruct((batch * seq, C_WIDTH), BF16),
        grid=(batch, nt),
        in_specs=[
            pl.BlockSpec((None, n, heads, t_blk), lambda b, j: (b, 0, 0, j)),
            pl.BlockSpec((None, n, heads, t_blk), lambda b, j: (batch + b, 0, 0, nt - 1 - j)),
            pl.BlockSpec((t_blk, C_GATE_LORA), lambda b, j: (b * nt + j, g_blk)),
            pl.BlockSpec((C_GATE_LORA, C_WIDTH), lambda b, j: (0, 0)),
        ],
        out_specs=pl.BlockSpec((t_blk, C_WIDTH), lambda b, j: (b * nt + j, 0)),
        compiler_params=_cp("parallel", "arbitrary"),
        name="c_out",
    )(q, q, proj, g_up)


SCAN_TC = 16


def _heads_minor(p):
    return p.reshape(p.shape[:-1] + (C_HEADS, C_HEAD_DIM)).swapaxes(-1, -2).reshape(p.shape)


def _rwkv(proj, mu, w0, w_up, a0, a_up, g_up, k_k, k_a, r_k, ln_w, ln_b, batch, seq):
    pad_lo = C_LORA_PAD - C_LORA
    split = 3 * C_WIDTH
    mu_wide = jnp.concatenate([_heads_minor(mu[:, i * C_WIDTH:(i + 1) * C_WIDTH]) for i in range(3)], axis=-1)[:, None, :]
    mu_lo = jnp.concatenate([
        jnp.pad(mu[:, split:split + C_LORA], ((0, 0), (0, pad_lo))),
        jnp.pad(mu[:, split + C_LORA:], ((0, 0), (0, pad_lo))),
    ], axis=-1)[:, None, :]
    w_up_p = jnp.pad(_heads_minor(w_up), ((0, 0), (0, pad_lo), (0, 0)))
    a_up_p = jnp.pad(_heads_minor(a_up), ((0, 0), (0, pad_lo), (0, 0)))
    feats = _rwkv_prep(proj, mu_wide, mu_lo, _heads_minor(w0)[:, None, :], w_up_p, _heads_minor(a0)[:, None, :], a_up_p,
                       _heads_minor(k_k).reshape(1, C_WIDTH), _heads_minor(k_a).reshape(1, C_WIDTH), batch, seq, min(SCAN_TC, seq))
    xs = _rwkv_relayout(feats, min(SCAN_TC, seq))
    per_lane = lambda p: jnp.tile(p.reshape(C_HEADS, C_HEAD_DIM).T, (1, 2 * batch))[:, None, :]
    o = _rwkv_scan(xs, per_lane(r_k), per_lane(ln_w), per_lane(ln_b))
    return _rwkv_out(_rwkv_unlayout(o, batch), proj, _heads_minor(g_up).astype(BF16), batch, seq)


def _merge_kernel(h_ref, wg_ref, wbr_ref, wc_ref, oa_ref, ob_ref, oc_ref, od_ref, o_ref, wgb_ref, wbrb_ref, wcb_ref):
    _cast_weights_once((wg_ref, wbr_ref, wc_ref), (wgb_ref, wbrb_ref, wcb_ref))
    h = h_ref[...]
    acc = None
    row = 0
    for i, b_ref in enumerate((oa_ref, ob_ref, oc_ref, od_ref)):
        width = b_ref.shape[1]
        w = wcb_ref[...] if b_ref is oc_ref else wbrb_ref[row:row + width, :]
        gate = jax.nn.sigmoid(jnp.dot(h, wgb_ref[i], preferred_element_type=F32))
        term = gate * jnp.dot(b_ref[...], w, preferred_element_type=F32)
        acc = term if acc is None else acc + term
        row += width
    o_ref[...] = acc.astype(o_ref.dtype)


def _merge(h, w_gate, w_branch, layer, w_branch_c, branches, tm=512, tn=512):
    m = h.shape[0]
    tm = min(tm, m)
    d_mix = w_branch.shape[-2]
    row = lambda a: pl.BlockSpec((tm, a.shape[1]), lambda j, i: (i, 0))
    return pl.pallas_call(
        _merge_kernel,
        out_shape=jax.ShapeDtypeStruct((m, D_MODEL), BF16),
        grid=(D_MODEL // tn, m // tm),
        in_specs=[
            row(h),
            _weight_spec((4, D_MODEL, tn), lambda j, i: (0, 0, j), layer),
            _weight_spec((d_mix, tn), lambda j, i: (0, j), layer),
            _weight_spec((w_branch_c.shape[0], tn), lambda j, i: (0, j)),
        ] + [row(b) for b in branches],
        out_specs=pl.BlockSpec((tm, tn), lambda j, i: (i, j)),
        scratch_shapes=[pltpu.VMEM((4, D_MODEL, tn), BF16), pltpu.VMEM((d_mix, tn), BF16), pltpu.VMEM((w_branch_c.shape[0], tn), BF16)],
        compiler_params=_cp("parallel", "arbitrary"),
        name="gated_merge",
    )(h, w_gate, w_branch, w_branch_c, *branches)


def _ffn_up_kernel(h_ref, wg_ref, wu_ref, o_ref, wgb_ref, wub_ref):
    _cast_weights_once((wg_ref, wu_ref), (wgb_ref, wub_ref))
    h = h_ref[...]
    g = jnp.dot(h, wgb_ref[...], preferred_element_type=F32)
    u = jnp.dot(h, wub_ref[...], preferred_element_type=F32)
    o_ref[...] = (g * jax.nn.sigmoid(g) * u).astype(o_ref.dtype)


def _ffn_up(h, w_gate, w_up, layer, tm=1024, tn=512):
    m, k = h.shape
    n = w_gate.shape[-1]
    tm = min(tm, m)
    return pl.pallas_call(
        _ffn_up_kernel,
        out_shape=jax.ShapeDtypeStruct((m, n), BF16),
        grid=(n // tn, m // tm),
        in_specs=[
            pl.BlockSpec((tm, k), lambda j, i: (i, 0)),
            _weight_spec((k, tn), lambda j, i: (0, j), layer),
            _weight_spec((k, tn), lambda j, i: (0, j), layer),
        ],
        out_specs=pl.BlockSpec((tm, tn), lambda j, i: (i, j)),
        scratch_shapes=[pltpu.VMEM((k, tn), BF16), pltpu.VMEM((k, tn), BF16)],
        compiler_params=_cp("parallel", "arbitrary"),
        name="ffn_up",
    )(h, w_gate, w_up)


def _pad_w_in(w):
    pad = ((0, 0), (0, C_LORA_PAD - C_LORA))
    lo = COL_CLO
    cf = [_heads_minor(w[:, COL_CF + i * C_WIDTH:COL_CF + (i + 1) * C_WIDTH]) for i in range(3)]
    return jnp.concatenate([
        w[:, :COL_CF],
        *cf,
        jnp.pad(w[:, lo:lo + C_LORA], pad),
        jnp.pad(w[:, lo + C_LORA:lo + 2 * C_LORA], pad),
        w[:, lo + 2 * C_LORA:],
    ], axis=1)


def kernel(x, norm_mix, w_in, a_q_norm, a_k_norm, b_conv_w, b_conv_b, b_filt_w1, b_filt_b1, b_filt_w2, b_filt_b2, b_filt_w3, b_filt_b3, b_filt_w4, b_filt_freq, b_skip, c_mu, c_w0, c_w_up, c_a0, c_a_up, c_g_up, c_k_k, c_k_a, c_r_k, c_ln_w, c_ln_b, d_lq1, d_lk1, d_lq2, d_lk2, d_subln, w_gate, w_branch, w_out, norm_ffn, w_ff_gate, w_ff_up, w_ff_down, norm_final):
    batch, seq, _ = x.shape
    m = batch * seq
    cos, sin = _rope_tables(seq)
    cos_half, sin_half, cos_int, sin_int = _dft_tables(seq)
    filt_feats = _filter_features(seq)
    x = x.reshape(m, D_MODEL)
    for l in range(DEPTH):
        h = _rmsnorm(x, norm_mix[l], BF16)
        proj = _matmul(h, _pad_w_in(w_in[l]), F32, 1024, D_IN_PAD // 4, name="in_proj")

        gains = jnp.concatenate([
            jnp.broadcast_to(a_q_norm[l], (A_HEADS, A_HEAD_DIM)),
            jnp.broadcast_to(a_k_norm[l], (A_KV_HEADS, A_HEAD_DIM)),
        ])[:, None, :]
        o_a = _attn_a(_qk_prep(proj, gains, cos, sin, seq), proj, batch, seq)

        k_re, k_im = _hyena_filter_spectrum(seq, filt_feats, cos_int, sin_int, b_filt_w1[l], b_filt_b1[l], b_filt_w2[l],
                                            b_filt_b2[l], b_filt_w3[l], b_filt_b3[l], b_filt_w4[l], b_filt_freq[l])
        o_b = _hyena(proj, b_conv_w[l], b_conv_b[l], b_skip[l], cos_half, sin_half, k_re, k_im, batch, seq)

        o_c = _rwkv(proj, c_mu[l], c_w0[l], c_w_up[l], c_a0[l], c_a_up[l], c_g_up[l], c_k_k[l], c_k_a[l], c_r_k[l],
                    c_ln_w[l], c_ln_b[l], batch, seq)

        lam_init = 0.8 - 0.6 * math.exp(-0.3 * l)
        lam_vecs = jnp.stack([d_lq1[l], d_lk1[l], d_lq2[l], d_lk2[l]])
        o_d = _attn_d(proj, lam_vecs, d_subln[l], lam_init, batch, seq)

        c_lo, c_hi = A_WIDTH + B_WIDTH, A_WIDTH + B_WIDTH + C_WIDTH
        w_br_c = _heads_minor(w_branch[l, c_lo:c_hi].T).T
        merged = _merge(h, w_gate, w_branch, l, w_br_c, (o_a, o_b, o_c, o_d))
        x, h2 = _matmul_res_norm(merged, w_out, l, x, norm_ffn[l])
        mid = _ffn_up(h2, w_ff_gate, w_ff_up, l)
        x = _matmul(mid, w_ff_down, F32, 1024, 512, residual=x, layer=l, name="ffn_down")
    return _rmsnorm(x, norm_final, F32).reshape(batch, seq, D_MODEL)
```

```python
import functools
import math

import jax
import jax.numpy as jnp
import numpy as np
from jax import lax
from jax.experimental import pallas as pl
from jax.experimental.pallas import tpu as pltpu

D_MODEL = 2048
DEPTH = 2
GRID_W = 64
NORM_EPS = 1e-6

A_HEADS = 8
A_KV_HEADS = 2
A_HEAD_DIM = 128
A_WIDTH = A_HEADS * A_HEAD_DIM
ROPE_THETA = 10000.0

B_WIDTH = 512
B_EMB_DIM = 33
B_FILTER_HIDDEN = 64
B_DECAY_TARGET = 1e-2
B_FAST_DECAY_PCT = 0.3
B_SLOW_DECAY_PCT = 1.5

C_HEADS = 8
C_HEAD_DIM = 64
C_WIDTH = C_HEADS * C_HEAD_DIM
C_LORA = 96
C_LORA_PAD = 128
C_GATE_LORA = 256
C_GN_EPS = 64e-5

D_HEADS = 4
D_HEAD_DIM = 64
D_V_DIM = 2 * D_HEAD_DIM
D_WIDTH = D_HEADS * D_V_DIM

FFN_HIDDEN = -(-8 * D_MODEL // (3 * 256)) * 256

COL_AQ = 0
COL_AK = COL_AQ + A_WIDTH
COL_AV = COL_AK + A_KV_HEADS * A_HEAD_DIM
COL_BU = COL_AV + A_KV_HEADS * A_HEAD_DIM
COL_CF = COL_BU + 3 * B_WIDTH
COL_CLO = COL_CF + 3 * C_WIDTH
COL_CG = COL_CLO + 2 * C_LORA_PAD
COL_DQ = COL_CG + C_GATE_LORA
COL_DK = COL_DQ + 2 * D_HEADS * D_HEAD_DIM
COL_DV = COL_DK + 2 * D_HEADS * D_HEAD_DIM
D_IN_PAD = COL_DV + D_WIDTH

VMEM_LIMIT_V7X = 56 * 1024 * 1024
F32 = jnp.float32
BF16 = jnp.bfloat16
HIGHEST = lax.Precision.HIGHEST
NT_DIMS = (((1,), (1,)), ((), ()))
LOG2_E = math.log2(math.e)


def _cp(*sem):
    return pltpu.CompilerParams(dimension_semantics=sem, vmem_limit_bytes=VMEM_LIMIT_V7X)


def _const_spec(shape):
    return pl.BlockSpec(shape, lambda *_: (0,) * len(shape), pipeline_mode=pl.Buffered(1))


SINGLE_BUFFER_BYTES = 8 * 1024 * 1024


def _weight_spec(shape, index_map, layer=None):
    mode = {"pipeline_mode": pl.Buffered(1)} if 4 * math.prod(shape) > SINGLE_BUFFER_BYTES else {}
    if layer is None:
        return pl.BlockSpec(shape, index_map, **mode)
    return pl.BlockSpec((None,) + tuple(shape), lambda *g: (layer,) + tuple(index_map(*g)), **mode)


def _rmsnorm_kernel(x_ref, g_ref, o_ref):
    x = x_ref[...]
    ms = jnp.mean(x * x, axis=-1, keepdims=True)
    o_ref[...] = (x * lax.rsqrt(ms + NORM_EPS) * g_ref[...]).astype(o_ref.dtype)


def _rmsnorm(x, g, out_dtype, tm=1024):
    m, d = x.shape
    tm = min(tm, m)
    return pl.pallas_call(
        _rmsnorm_kernel,
        out_shape=jax.ShapeDtypeStruct((m, d), out_dtype),
        grid=(m // tm,),
        in_specs=[pl.BlockSpec((tm, d), lambda i: (i, 0)), pl.BlockSpec((1, d), lambda i: (0, 0))],
        out_specs=pl.BlockSpec((tm, d), lambda i: (i, 0)),
        compiler_params=_cp("parallel"),
        name="rmsnorm",
    )(x, g.reshape(1, d))


def _mm_f32_kernel(a_ref, b_ref, o_ref):
    def split(x):
        hi = x.astype(BF16)
        return hi, (x - hi.astype(F32)).astype(BF16)

    (a_hi, a_lo), (b_hi, b_lo) = split(a_ref[...]), split(b_ref[...])
    dot = functools.partial(jnp.dot, preferred_element_type=F32)
    o_ref[...] = dot(a_hi, b_hi) + (dot(a_hi, b_lo) + dot(a_lo, b_hi))


def _matmul_f32(a, b, tm, tn, name):
    m, k = a.shape
    n = b.shape[1]
    tm, tn = min(tm, m), min(tn, n)
    return pl.pallas_call(
        _mm_f32_kernel,
        out_shape=jax.ShapeDtypeStruct((m, n), F32),
        grid=(m // tm, n // tn),
        in_specs=[pl.BlockSpec((tm, k), lambda i, j: (i, 0)), pl.BlockSpec((k, tn), lambda i, j: (0, j))],
        out_specs=pl.BlockSpec((tm, tn), lambda i, j: (i, j)),
        compiler_params=_cp("parallel", "arbitrary"),
        name=name,
    )(a, b)


def _cast_weights_once(w_refs, wb_refs):
    @pl.when(pl.program_id(1) == 0)
    def _():
        for w_ref, wb_ref in zip(w_refs, wb_refs):
            wb_ref[...] = w_ref[...].astype(BF16)


def _mm_kernel(a_ref, w_ref, o_ref, wb_ref):
    _cast_weights_once((w_ref,), (wb_ref,))
    o_ref[...] = jnp.dot(a_ref[...], wb_ref[...], preferred_element_type=F32).astype(o_ref.dtype)


def _mm_res_kernel(a_ref, w_ref, r_ref, o_ref, wb_ref):
    _cast_weights_once((w_ref,), (wb_ref,))
    o_ref[...] = r_ref[...] + jnp.dot(a_ref[...], wb_ref[...], preferred_element_type=F32)


def _matmul(a, w, out_dtype, tm, tn, residual=None, layer=None, name="matmul"):
    m, k = a.shape
    n = w.shape[-1]
    tm, tn = min(tm, m), min(tn, n)
    in_specs = [pl.BlockSpec((tm, k), lambda j, i: (i, 0)), _weight_spec((k, tn), lambda j, i: (0, j), layer)]
    args = [a, w]
    body = _mm_kernel
    if residual is not None:
        body = _mm_res_kernel
        in_specs.append(pl.BlockSpec((tm, tn), lambda j, i: (i, j)))
        args.append(residual)
    return pl.pallas_call(
        body,
        out_shape=jax.ShapeDtypeStruct((m, n), out_dtype),
        grid=(n // tn, m // tm),
        in_specs=in_specs,
        out_specs=pl.BlockSpec((tm, tn), lambda j, i: (i, j)),
        scratch_shapes=[pltpu.VMEM((k, tn), BF16)],
        compiler_params=_cp("parallel", "arbitrary"),
        name=name,
    )(*args)


def _mm_res_norm_kernel(a_ref, w_ref, r_ref, g_ref, x_ref, h_ref, wb_ref):
    @pl.when(pl.program_id(0) == 0)
    def _():
        wb_ref[...] = w_ref[...].astype(BF16)

    x = r_ref[...] + jnp.dot(a_ref[...], wb_ref[...], preferred_element_type=F32)
    x_ref[...] = x
    ms = jnp.mean(x * x, axis=-1, keepdims=True)
    h_ref[...] = (x * lax.rsqrt(ms + NORM_EPS) * g_ref[...]).astype(h_ref.dtype)


def _matmul_res_norm(a, w, layer, residual, gain, tm=512):
    m, k = a.shape
    n = w.shape[-1]
    tm = min(tm, m)
    rows = lambda width: pl.BlockSpec((tm, width), lambda i: (i, 0))
    return pl.pallas_call(
        _mm_res_norm_kernel,
        out_shape=[jax.ShapeDtypeStruct((m, n), F32), jax.ShapeDtypeStruct((m, n), BF16)],
        grid=(m // tm,),
        in_specs=[rows(k), _weight_spec((k, n), lambda i: (0, 0), layer), rows(n), pl.BlockSpec((1, n), lambda i: (0, 0))],
        out_specs=[rows(n), rows(n)],
        scratch_shapes=[pltpu.VMEM((k, n), BF16)],
        compiler_params=_cp("arbitrary"),
        name="out_proj_norm",
    )(a, w, residual, gain.reshape(1, n))


def _rope_tables(seq):
    rows = seq // GRID_W
    row_idx = jnp.repeat(jnp.arange(rows, dtype=F32), GRID_W)
    col_idx = jnp.tile(jnp.arange(GRID_W, dtype=F32), rows)
    axis_dim = A_HEAD_DIM // 2
    inv_freq = ROPE_THETA ** (-jnp.arange(0, axis_dim, 2, dtype=F32) / axis_dim)
    ang_r = row_idx[:, None] * inv_freq[None, :]
    ang_c = col_idx[:, None] * inv_freq[None, :]
    ang = jnp.concatenate([ang_r, ang_r, ang_c, ang_c], axis=-1)
    return jnp.cos(ang), jnp.sin(ang)


def _qk_prep_kernel(x_ref, g_ref, cos_ref, sin_ref, o_ref):
    cos = cos_ref[...]
    sin = sin_ref[...]
    lane = lax.broadcasted_iota(jnp.int32, cos.shape, 1)
    quarter = A_HEAD_DIM // 4
    first = (lane % (2 * quarter)) < quarter
    for h in range(A_HEADS + A_KV_HEADS):
        cols = slice(h * A_HEAD_DIM, (h + 1) * A_HEAD_DIM)
        x = x_ref[:, cols]
        xn = x * lax.rsqrt(jnp.mean(x * x, axis=-1, keepdims=True) + NORM_EPS) * g_ref[h]
        rot = jnp.where(first, -pltpu.roll(xn, A_HEAD_DIM - quarter, 1), pltpu.roll(xn, quarter, 1))
        y = xn * cos + rot * sin
        if h < A_HEADS:
            y = y * (A_HEAD_DIM**-0.5 * LOG2_E)
        o_ref[:, cols] = y.astype(o_ref.dtype)


def _qk_prep(proj, gains, cos, sin, seq, tm=512):
    m = proj.shape[0]
    tm = min(tm, seq)
    width = (A_HEADS + A_KV_HEADS) * A_HEAD_DIM
    nt = seq // tm
    return pl.pallas_call(
        _qk_prep_kernel,
        out_shape=jax.ShapeDtypeStruct((m, width), BF16),
        grid=(m // tm,),
        in_specs=[
            pl.BlockSpec((tm, width), lambda i: (i, 0)),
            pl.BlockSpec(gains.shape, lambda i: (0, 0, 0)),
            pl.BlockSpec((tm, A_HEAD_DIM), lambda i: (i % nt, 0)),
            pl.BlockSpec((tm, A_HEAD_DIM), lambda i: (i % nt, 0)),
        ],
        out_specs=pl.BlockSpec((tm, width), lambda i: (i, 0)),
        compiler_params=_cp("parallel"),
        name="a_qk_prep",
    )(proj, gains, cos, sin)


class _OnlineSoftmax:
    def __init__(self, rows, width):
        self.m = jnp.full((rows, 1), -jnp.inf, F32)
        self.l = jnp.zeros((rows, 1), F32)
        self.acc = jnp.zeros((rows, width), F32)

    def add(self, s, v):
        m_new = jnp.maximum(self.m, jnp.max(s, axis=-1, keepdims=True))
        p = jnp.exp2(s - m_new)
        alpha = jnp.exp2(self.m - m_new)
        self.l = alpha * self.l + jnp.sum(p, axis=-1, keepdims=True)
        self.acc = alpha * self.acc + jnp.dot(p.astype(BF16), v, preferred_element_type=F32)
        self.m = m_new

    def result(self):
        return self.acc / self.l


def _attn_a_kernel(q_ref, k_ref, v_ref, o_ref, *, kc):
    q = q_ref[...]
    sm = _OnlineSoftmax(q.shape[0], A_HEAD_DIM)
    for c in range(k_ref.shape[0] // kc):
        keys = slice(c * kc, (c + 1) * kc)
        s = lax.dot_general(q, k_ref[keys, :], NT_DIMS, preferred_element_type=F32)
        sm.add(s, v_ref[keys, :].astype(BF16))
    o_ref[...] = sm.result().astype(o_ref.dtype)


def _attn_a(qk, proj, batch, seq, tq=1024, kc=1024):
    tq = min(tq, seq)
    kc = min(kc, seq)
    nq = seq // tq
    group = A_HEADS // A_KV_HEADS
    k_blk = COL_AK // A_HEAD_DIM
    v_blk = COL_AV // A_HEAD_DIM
    return pl.pallas_call(
        functools.partial(_attn_a_kernel, kc=kc),
        out_shape=jax.ShapeDtypeStruct((batch * seq, A_WIDTH), BF16),
        grid=(batch, A_HEADS, nq),
        in_specs=[
            pl.BlockSpec((tq, A_HEAD_DIM), lambda b, h, i: (b * nq + i, h)),
            pl.BlockSpec((seq, A_HEAD_DIM), lambda b, h, i: (b, k_blk + h // group)),
            pl.BlockSpec((seq, A_HEAD_DIM), lambda b, h, i: (b, v_blk + h // group)),
        ],
        out_specs=pl.BlockSpec((tq, A_HEAD_DIM), lambda b, h, i: (b * nq + i, h)),
        compiler_params=_cp("parallel", "arbitrary", "arbitrary"),
        name="a_attention",
    )(qk, qk, proj)


def _attn_d_kernel(q_ref, k_ref, v_ref, slope_ref, lam_ref, g_ref, o_ref, *, tq, kc, lam_init):
    q = q_ref[...] * (D_HEAD_DIM**-0.5 * LOG2_E)
    lane = lax.broadcasted_iota(jnp.int32, q.shape, 1)
    q_maps = (jnp.where(lane < D_HEAD_DIM, q, 0.0).astype(BF16), jnp.where(lane >= D_HEAD_DIM, q, 0.0).astype(BF16))
    slope = slope_ref[0][:, 0:1] * LOG2_E
    rel = pl.program_id(2) * tq + lax.broadcasted_iota(jnp.int32, (tq, kc), 0) - lax.broadcasted_iota(jnp.int32, (tq, kc), 1)
    rel = slope * rel.astype(F32)
    maps = (_OnlineSoftmax(tq, D_V_DIM), _OnlineSoftmax(tq, D_V_DIM))
    for c in range(k_ref.shape[0] // kc):
        keys = slice(c * kc, (c + 1) * kc)
        k = k_ref[keys, :].astype(BF16)
        v = v_ref[keys, :].astype(BF16)
        bias = jnp.abs(rel - slope * float(c * kc))
        for q_map, sm in zip(q_maps, maps):
            sm.add(lax.dot_general(q_map, k, NT_DIMS, preferred_element_type=F32) - bias, v)
    lam_v = lam_ref[...]
    lam = (
        jnp.exp(jnp.sum(lam_v[0:1] * lam_v[1:2], axis=-1, keepdims=True))
        - jnp.exp(jnp.sum(lam_v[2:3] * lam_v[3:4], axis=-1, keepdims=True))
        + lam_init
    )
    o = maps[0].result() - lam * maps[1].result()
    o = o * lax.rsqrt(jnp.mean(o * o, axis=-1, keepdims=True) + NORM_EPS) * g_ref[...]
    o_ref[...] = (o * (1.0 - lam_init)).astype(o_ref.dtype)


def _attn_d(proj, lam_vecs, subln, lam_init, batch, seq, tq=1024, kc=512):
    tq = min(tq, seq)
    kc = min(kc, seq)
    nq = seq // tq
    slopes = 2.0 ** (-8.0 * np.arange(1, D_HEADS + 1, dtype=np.float32) / D_HEADS)
    slopes = jnp.asarray(np.broadcast_to(slopes[:, None, None], (D_HEADS, 1, 128)).astype(np.float32))
    q_blk, k_blk, v_blk = COL_DQ // D_V_DIM, COL_DK // D_V_DIM, COL_DV // D_V_DIM
    return pl.pallas_call(
        functools.partial(_attn_d_kernel, tq=tq, kc=kc, lam_init=lam_init),
        out_shape=jax.ShapeDtypeStruct((batch * seq, D_WIDTH), BF16),
        grid=(batch, D_HEADS, nq),
        in_specs=[
            pl.BlockSpec((tq, D_V_DIM), lambda b, h, i: (b * nq + i, q_blk + h)),
            pl.BlockSpec((seq, D_V_DIM), lambda b, h, i: (b, k_blk + h)),
            pl.BlockSpec((seq, D_V_DIM), lambda b, h, i: (b, v_blk + h)),
            pl.BlockSpec((1, 1, 128), lambda b, h, i: (h, 0, 0)),
            pl.BlockSpec((4, D_HEAD_DIM), lambda b, h, i: (0, 0)),
            pl.BlockSpec((1, D_V_DIM), lambda b, h, i: (0, 0)),
        ],
        out_specs=pl.BlockSpec((tq, D_V_DIM), lambda b, h, i: (b * nq + i, h)),
        compiler_params=_cp("parallel", "arbitrary", "arbitrary"),
        name="d_attention",
    )(proj, proj, proj, slopes, lam_vecs, subln.reshape(1, D_V_DIM))


DFT_ROW_SPLIT = 64


def _dft_tables(seq):
    n = 2 * seq
    blk = min(DFT_ROW_SPLIT, seq)
    col = jnp.arange(seq, dtype=jnp.int32)[None, :]
    hi = jnp.arange(seq // blk, dtype=jnp.int32)[:, None]
    lo = jnp.arange(blk, dtype=jnp.int32)[:, None]

    def cos_sin(index, period):
        ang = (index % period).astype(F32) * (2.0 * math.pi / period)
        return jnp.cos(ang), jnp.sin(ang)

    def tables(col_term, period):
        (ca, sa), (cb, sb) = cos_sin(2 * blk * hi * col_term, period), cos_sin((2 * lo + 1) * col_term, period)
        ca, sa, cb, sb = ca[:, None, :], sa[:, None, :], cb[None], sb[None]
        return (ca * cb - sa * sb).reshape(seq, seq), (sa * cb + ca * sb).reshape(seq, seq)

    cos_half, sin_half = tables(2 * col + 1, 4 * n)
    cos_int, sin_int = tables(col, 2 * n)
    return cos_half.astype(BF16), sin_half.astype(BF16), cos_int, sin_int


def _filter_features(seq):
    t = jnp.linspace(0.0, 1.0, seq, dtype=F32)[:, None]
    n_bands = (B_EMB_DIM - 1) // 2
    bands = jnp.linspace(1e-4, n_bands - 1, n_bands, dtype=F32)[None, :]
    ang = (2.0 * math.pi / seq) * jnp.arange(seq, dtype=F32)[:, None] * bands
    z = jnp.concatenate([t, jnp.cos(ang), -jnp.sin(ang)], axis=-1)
    z = jnp.pad(z, ((0, 0), (0, B_FILTER_HIDDEN - B_EMB_DIM)))
    max_decay = math.log(B_DECAY_TARGET) / B_FAST_DECAY_PCT
    min_decay = math.log(B_DECAY_TARGET) / B_SLOW_DECAY_PCT
    deltas = jnp.abs(jnp.linspace(min_decay, max_decay, B_WIDTH, dtype=F32))[None, :]
    return z, t, deltas


def _filter_kernel(z_ref, t_ref, dl_ref, w1_ref, b1_ref, w2_ref, b2_ref, w3_ref, b3_ref, w4_ref, fr_ref, hs_ref, hd_ref):
    fr = fr_ref[...]
    hid = jnp.sin(fr * (jnp.dot(z_ref[...], w1_ref[...], preferred_element_type=F32, precision=HIGHEST) + b1_ref[...]))
    hid = jnp.sin(fr * (jnp.dot(hid, w2_ref[...], preferred_element_type=F32, precision=HIGHEST) + b2_ref[...]))
    hid = jnp.sin(fr * (jnp.dot(hid, w3_ref[...], preferred_element_type=F32, precision=HIGHEST) + b3_ref[...]))
    h = jnp.dot(hid, w4_ref[...], preferred_element_type=F32, precision=HIGHEST)
    window = jnp.exp(-t_ref[...] * dl_ref[...])
    h_fwd = h[:, :B_WIDTH] * window
    h_bwd = h[:, B_WIDTH:] * window
    row = lax.broadcasted_iota(jnp.int32, h_bwd.shape, 0)
    h_bwd = jnp.where(row == 0, 0.0, h_bwd)
    norm = jnp.sum(jnp.abs(h_fwd), axis=0, keepdims=True) + jnp.sum(jnp.abs(h_bwd), axis=0, keepdims=True)
    seq = h.shape[0]
    inv_n = 1.0 / seq
    hs_ref[...] = (h_fwd + h_bwd) / norm * inv_n
    hd_ref[...] = (h_bwd - h_fwd) / norm * inv_n


def _hyena_filter_spectrum(seq, feats, cos_int, sin_int, w1, b1, w2, b2, w3, b3, w4, freq):
    z, t, deltas = feats
    w1p = jnp.pad(w1, ((0, B_FILTER_HIDDEN - B_EMB_DIM), (0, 0)))
    row = lambda v: v.reshape(1, -1)
    args = (z, t, deltas, w1p, row(b1), w2, row(b2), w3, row(b3), w4, row(freq))
    h_sum, h_diff = pl.pallas_call(
        _filter_kernel,
        out_shape=[jax.ShapeDtypeStruct((seq, B_WIDTH), F32)] * 2,
        in_specs=[pl.BlockSpec(a.shape, lambda: (0, 0)) for a in args],
        out_specs=[pl.BlockSpec((seq, B_WIDTH), lambda: (0, 0))] * 2,
        compiler_params=pltpu.CompilerParams(vmem_limit_bytes=VMEM_LIMIT_V7X),
        name="b_filter",
    )(*args)
    k_re = _matmul_f32(cos_int, h_sum, 256, B_WIDTH, name="b_filter_dft_re")
    k_im = _matmul_f32(sin_int, h_diff, 256, B_WIDTH, name="b_filter_dft_im")
    return k_re, k_im


def _hyena_kernel(v_ref, x1_ref, x0_ref, wv_ref, w1_ref, w0_ref, bv_ref, b1_ref, b0_ref, skip_ref, c_ref, s_ref, kre_ref, kim_ref, o_ref,
                  *, f_blk):
    seq = v_ref.shape[0]
    row = lax.broadcasted_iota(jnp.int32, v_ref.shape, 0)

    def conv3(u_ref, w_ref, b_ref):
        u = u_ref[...]
        w = w_ref[...]
        u_prev = jnp.where(row == 0, 0.0, pltpu.roll(u, 1, 0))
        u_next = jnp.where(row == seq - 1, 0.0, pltpu.roll(u, seq - 1, 0))
        return w[0:1] * u_prev + w[1:2] * u + w[2:3] * u_next + b_ref[...]

    z = conv3(v_ref, wv_ref, bv_ref) * conv3(x1_ref, w1_ref, b1_ref)
    zb = z.astype(BF16)
    y = z * skip_ref[...]
    for f0 in range(0, seq, f_blk):
        fs = slice(f0, f0 + f_blk)
        cz = jnp.dot(c_ref[fs, :], zb, preferred_element_type=F32)
        sz = jnp.dot(s_ref[fs, :], zb, preferred_element_type=F32)
        k_re = kre_ref[fs, :]
        k_im = kim_ref[fs, :]
        y_re = (cz * k_re + sz * k_im).astype(BF16)
        y_im = (cz * k_im - sz * k_re).astype(BF16)
        y = y + (jnp.dot(c_ref[:, fs], y_re, preferred_element_type=F32) - jnp.dot(s_ref[:, fs], y_im, preferred_element_type=F32))
    o_ref[...] = (y * conv3(x0_ref, w0_ref, b0_ref)).astype(o_ref.dtype)


def _hyena(proj, conv_w, conv_b, skip, cos_half, sin_half, k_re, k_im, batch, seq, cb=256, f_blk=1024):
    ncb = B_WIDTH // cb
    u_blk = COL_BU // cb
    f_blk = min(f_blk, seq)

    def u_spec(part):
        return pl.BlockSpec((seq, cb), lambda j, b: (b, u_blk + part * ncb + j))

    def w_spec(rows, part):
        return pl.BlockSpec((rows, cb), lambda j, b: (0, part * ncb + j))

    k_spec = pl.BlockSpec((seq, cb), lambda j, b: (0, j), pipeline_mode=pl.Buffered(1))
    return pl.pallas_call(
        functools.partial(_hyena_kernel, f_blk=f_blk),
        out_shape=jax.ShapeDtypeStruct((batch * seq, B_WIDTH), BF16),
        grid=(ncb, batch),
        in_specs=[
            u_spec(0), u_spec(1), u_spec(2),
            w_spec(3, 0), w_spec(3, 1), w_spec(3, 2),
            w_spec(1, 0), w_spec(1, 1), w_spec(1, 2),
            pl.BlockSpec((1, cb), lambda j, b: (0, j)),
            _const_spec((seq, seq)), _const_spec((seq, seq)),
            k_spec, k_spec,
        ],
        out_specs=pl.BlockSpec((seq, cb), lambda j, b: (b, j)),
        compiler_params=_cp("parallel", "arbitrary"),
        name="b_hyena",
    )(proj, proj, proj, conv_w, conv_w, conv_w, conv_b.reshape(1, -1), conv_b.reshape(1, -1), conv_b.reshape(1, -1),
      skip.reshape(1, -1), cos_half, sin_half, k_re, k_im)


def _rwkv_prep_kernel(cur_ref, lo_ref, pcur_ref, plo_ref, ncur_ref, nlo_ref, mu_ref, mulo_ref, w0_ref, wup_ref, a0_ref, aup_ref,
                      kk_ref, ka_ref, o_ref, *, chunk):
    first = pl.program_id(1) == 0
    last = pl.program_id(1) == pl.num_programs(1) - 1
    halo = pcur_ref.shape[0]
    tm = cur_ref.shape[0]
    row_in_chunk = lax.broadcasted_iota(jnp.int32, (tm, C_WIDTH), 0) % chunk

    def neighbours(x_ref, p_ref, n_ref):
        x = x_ref[...]
        row = lax.broadcasted_iota(jnp.int32, x.shape, 0)
        p_row = jnp.where(first, 0.0, p_ref[halo - 1:halo, :])
        n_row = jnp.where(last, 0.0, n_ref[0:1, :])
        prev = jnp.where(row == 0, p_row, pltpu.roll(x, 1, 0))
        nxt = jnp.where(row == tm - 1, n_row, pltpu.roll(x, tm - 1, 0))
        return x, (prev, nxt)

    cur, cur_sh = neighbours(cur_ref, pcur_ref, ncur_ref)
    lo, lo_sh = neighbours(lo_ref, plo_ref, nlo_ref)
    k_k = kk_ref[...]
    k_a = ka_ref[...]
    for d in range(2):
        f = cur + (cur_sh[d] - cur) * mu_ref[d]
        f_lo = lo + (lo_sh[d] - lo) * mulo_ref[d]
        r = f[:, :C_WIDTH]
        k = f[:, C_WIDTH:2 * C_WIDTH]
        v = f[:, 2 * C_WIDTH:]
        w_lo = f_lo[:, :C_LORA_PAD]
        a_lo = f_lo[:, C_LORA_PAD:]
        x = w0_ref[d] + jnp.dot(jnp.tanh(w_lo), wup_ref[d], preferred_element_type=F32, precision=HIGHEST)
        w = -(jnp.maximum(-x, 0.0) + jnp.log(1.0 + jnp.exp(-jnp.abs(x)))) - 0.5
        neg_log_decay = jnp.exp(w)
        log_g = neg_log_decay
        shift = 1
        while shift < chunk:
            if d == 0:
                moved = jnp.where(row_in_chunk >= shift, pltpu.roll(log_g, shift, 0), 0.0)
            else:
                moved = jnp.where(row_in_chunk < chunk - shift, pltpu.roll(log_g, tm - shift, 0), 0.0)
            log_g = log_g + moved
            shift *= 2
        g = jnp.exp(-log_g)
        g_inv = jnp.exp(log_g)
        g_before = jnp.exp(neg_log_decay - log_g)
        a = jax.nn.sigmoid(a0_ref[d] + jnp.dot(a_lo, aup_ref[d], preferred_element_type=F32, precision=HIGHEST))
        transposed = lambda val: val.T.reshape(C_HEAD_DIM, C_HEADS, tm)
        kk = transposed(k * k_k)
        kk = kk / jnp.maximum(jnp.sqrt(jnp.sum(kk * kk, axis=0, keepdims=True)), 1e-12)
        for q, val in enumerate((r * g, g, k * (1.0 + (a - 1.0) * k_a) * g_inv, v)):
            o_ref[d, q] = transposed(val)
        o_ref[d, 4] = -kk * transposed(g_before)
        o_ref[d, 5] = kk * transposed(a * g_inv)


def _rwkv_prep(proj, mu, mu_lo, w0, w_up, a0, a_up, k_k, k_a, batch, seq, chunk, tm=256):
    tm = min(tm, seq)
    nt = seq // tm
    halo = 8
    hb = tm // halo
    n_hblk = batch * seq // halo
    wide, narrow = 3 * C_WIDTH, 2 * C_LORA_PAD
    cur_blk, lo_blk = COL_CF // wide, COL_CLO // narrow

    def cur_map(blk):
        return lambda b, j: (b * nt + j, blk)

    def prev_map(blk):
        return lambda b, j: (jnp.maximum((b * nt + j) * hb - 1, 0), blk)

    def next_map(blk):
        return lambda b, j: (jnp.minimum((b * nt + j + 1) * hb, n_hblk - 1), blk)

    small = lambda a: pl.BlockSpec(a.shape, lambda b, j: (0,) * a.ndim)
    consts = (mu, mu_lo, w0, w_up, a0, a_up, k_k, k_a)
    return pl.pallas_call(
        functools.partial(_rwkv_prep_kernel, chunk=chunk),
        out_shape=jax.ShapeDtypeStruct((2, 6, C_HEAD_DIM, batch, C_HEADS, seq), F32),
        grid=(batch, nt),
        in_specs=[
            pl.BlockSpec((tm, wide), cur_map(cur_blk)),
            pl.BlockSpec((tm, narrow), cur_map(lo_blk)),
            pl.BlockSpec((halo, wide), prev_map(cur_blk)),
            pl.BlockSpec((halo, narrow), prev_map(lo_blk)),
            pl.BlockSpec((halo, wide), next_map(cur_blk)),
            pl.BlockSpec((halo, narrow), next_map(lo_blk)),
        ] + [small(a) for a in consts],
        out_specs=pl.BlockSpec((2, 6, C_HEAD_DIM, None, C_HEADS, tm), lambda b, j: (0, 0, 0, b, 0, j)),
        compiler_params=_cp("parallel", "arbitrary"),
        name="c_prep",
    )(proj, proj, proj, proj, proj, proj, *consts)


RELAYOUT_T = 128


def _rwkv_relayout_kernel(pf_ref, pb_ref, o_ref, *, tc):
    n = pf_ref.shape[0]
    t_blk = pf_ref.shape[-1]
    mirror = t_blk - 1 - lax.broadcasted_iota(jnp.int32, (pb_ref.shape[1] * pb_ref.shape[2], t_blk), 1)
    for k in range(n):
        bwd = jnp.take_along_axis(pb_ref[k].reshape(-1, t_blk), mirror, axis=1)
        cols = jnp.concatenate([pf_ref[k].reshape(-1, t_blk), bwd], axis=0).T
        for c in range(t_blk // tc):
            o_ref[c, k] = cols[c * tc:(c + 1) * tc]


def _rwkv_relayout(p, tc):
    _, nq, n, batch, heads, seq = p.shape
    t_blk = min(RELAYOUT_T, seq)
    nb = seq // t_blk
    lanes = 2 * batch * heads
    per_blk = t_blk // tc
    return pl.pallas_call(
        functools.partial(_rwkv_relayout_kernel, tc=tc),
        out_shape=jax.ShapeDtypeStruct((seq // tc, nq, n, tc, lanes), F32),
        grid=(nb, nq),
        in_specs=[
            pl.BlockSpec((None, None, n, batch, heads, t_blk), lambda c, q: (0, q, 0, 0, 0, c)),
            pl.BlockSpec((None, None, n, batch, heads, t_blk), lambda c, q: (1, q, 0, 0, 0, nb - 1 - c)),
        ],
        out_specs=pl.BlockSpec((per_blk, None, n, tc, lanes), lambda c, q: (c, q, 0, 0, 0)),
        compiler_params=_cp("parallel", "arbitrary"),
        name="c_relayout",
    )(p, p)


def _rwkv_scan_kernel(x_ref, rk_ref, lnw_ref, lnb_ref, out_ref, s_ref, o_ref, *, tc, k_chunk):
    n = s_ref.shape[0]
    lanes = s_ref.shape[2]

    @pl.when(pl.program_id(0) == 0)
    def _():
        s_ref[...] = jnp.zeros_like(s_ref)

    def row(q, k, t):
        return x_ref[pl.ds((q * n + k) * tc + t, 1), :]

    def step(t, carry):
        vv = x_ref[pl.ds(3 * n * tc + t, n, stride=tc), :]

        def sa_chunk(c, sa):
            base = pl.multiple_of(c * k_chunk, k_chunk)
            for j in range(k_chunk):
                sa = sa + s_ref[base + j] * row(4, base + j, t)
            return sa

        sa = lax.fori_loop(0, n // k_chunk, sa_chunk, jnp.zeros((n, lanes), F32))

        def update_chunk(c, o):
            base = pl.multiple_of(c * k_chunk, k_chunk)
            for j in range(k_chunk):
                k = base + j
                sk = s_ref[k] + (sa * row(5, k, t) + vv * row(2, k, t))
                s_ref[k] = sk
                o = o + sk * row(0, k, t)
            return o

        o_ref[pl.ds(t, n, stride=tc), :] = lax.fori_loop(0, n // k_chunk, update_chunk, jnp.zeros((n, lanes), F32))
        return carry

    lax.fori_loop(0, tc, step, 0)
    for k in range(n):
        s_ref[k] = s_ref[k] * x_ref[(n + k) * tc + tc - 1:(n + k) * tc + tc, :]

    quantity = lambda q: x_ref[q * n * tc:(q + 1) * n * tc, :].reshape(n, tc, lanes)
    o = o_ref[...].reshape(n, tc, lanes)
    mean = jnp.mean(o, axis=0, keepdims=True)
    var = jnp.mean(jnp.square(o - mean), axis=0, keepdims=True)
    o = (o - mean) * lax.rsqrt(var + C_GN_EPS) * lnw_ref[...] + lnb_ref[...]
    bonus = jnp.sum(quantity(0) * quantity(2) * rk_ref[...], axis=0, keepdims=True)
    out_ref[...] = o + bonus * quantity(3)


def _rwkv_scan(xs, rk, lnw, lnb, k_chunk=32):
    nc, nq, n, tc, lanes = xs.shape
    tile = lambda: pl.BlockSpec((n, 1, lanes), lambda i: (0, 0, 0))
    return pl.pallas_call(
        functools.partial(_rwkv_scan_kernel, tc=tc, k_chunk=min(k_chunk, n)),
        out_shape=jax.ShapeDtypeStruct((n, nc * tc, lanes), F32),
        grid=(nc,),
        in_specs=[pl.BlockSpec((nq * n * tc, lanes), lambda i: (i, 0)), tile(), tile(), tile()],
        out_specs=pl.BlockSpec((n, tc, lanes), lambda i: (0, i, 0)),
        scratch_shapes=[pltpu.VMEM((n, n, lanes), F32), pltpu.VMEM((n * tc, lanes), F32)],
        compiler_params=_cp("arbitrary"),
        name="c_scan",
    )(xs.reshape(nc * nq * n * tc, lanes), rk, lnw, lnb)


def _rwkv_unlayout_kernel(o_ref, q_ref):
    for v in range(o_ref.shape[0]):
        rows = o_ref[v].T
        q_ref[:, v] = rows.reshape(q_ref.shape[0], q_ref.shape[2], q_ref.shape[3])


def _rwkv_unlayout(o, batch):
    n, seq, lanes = o.shape
    t_blk = min(RELAYOUT_T, seq)
    heads = lanes // (2 * batch)
    return pl.pallas_call(
        _rwkv_unlayout_kernel,
        out_shape=jax.ShapeDtypeStruct((2 * batch, n, heads, seq), F32),
        grid=(seq // t_blk,),
        in_specs=[pl.BlockSpec((n, t_blk, lanes), lambda c: (0, c, 0))],
        out_specs=pl.BlockSpec((2 * batch, n, heads, t_blk), lambda c: (0, 0, 0, c)),
        compiler_params=_cp("parallel"),
        name="c_unlayout",
    )(o)


def _rwkv_out_kernel(qf_ref, qb_ref, glo_ref, gup_ref, y_ref):
    t_blk = qf_ref.shape[-1]
    fwd = qf_ref[...].reshape(-1, t_blk)
    bwd = qb_ref[...].reshape(-1, t_blk)
    mirror = t_blk - 1 - lax.broadcasted_iota(jnp.int32, bwd.shape, 1)
    o = (fwd + jnp.take_along_axis(bwd, mirror, axis=1)).T
    g = jnp.dot(jax.nn.sigmoid(glo_ref[...]).astype(BF16), gup_ref[...], preferred_element_type=F32)
    y_ref[...] = (o * g).astype(y_ref.dtype)


def _rwkv_out(q, proj, g_up, batch, seq):
    _, n, heads, _ = q.shape
    t_blk = min(RELAYOUT_T, seq)
    nt = seq // t_blk
    g_blk = COL_CG // C_GATE_LORA
    return pl.pallas_call(
        _rwkv_out_kernel,
        out_shape=jax.ShapeDtypeStruct((batch * seq, C_WIDTH), BF16),
        grid=(batch, nt),
        in_specs=[
            pl.BlockSpec((None, n, heads, t_blk), lambda b, j: (b, 0, 0, j)),
            pl.BlockSpec((None, n, heads, t_blk), lambda b, j: (batch + b, 0, 0, nt - 1 - j)),
            pl.BlockSpec((t_blk, C_GATE_LORA), lambda b, j: (b * nt + j, g_blk)),
            pl.BlockSpec((C_GATE_LORA, C_WIDTH), lambda b, j: (0, 0)),
        ],
        out_specs=pl.BlockSpec((t_blk, C_WIDTH), lambda b, j: (b * nt + j, 0)),
        compiler_params=_cp("parallel", "arbitrary"),
        name="c_out",
    )(q, q, proj, g_up)


SCAN_TC = 16


def _heads_minor(p):
    return p.reshape(p.shape[:-1] + (C_HEADS, C_HEAD_DIM)).swapaxes(-1, -2).reshape(p.shape)


def _rwkv(proj, mu, w0, w_up, a0, a_up, g_up, k_k, k_a, r_k, ln_w, ln_b, batch, seq):
    pad_lo = C_LORA_PAD - C_LORA
    split = 3 * C_WIDTH
    mu_wide = jnp.concatenate([_heads_minor(mu[:, i * C_WIDTH:(i + 1) * C_WIDTH]) for i in range(3)], axis=-1)[:, None, :]
    mu_lo = jnp.concatenate([
        jnp.pad(mu[:, split:split + C_LORA], ((0, 0), (0, pad_lo))),
        jnp.pad(mu[:, split + C_LORA:], ((0, 0), (0, pad_lo))),
    ], axis=-1)[:, None, :]
    w_up_p = jnp.pad(_heads_minor(w_up), ((0, 0), (0, pad_lo), (0, 0)))
    a_up_p = jnp.pad(_heads_minor(a_up), ((0, 0), (0, pad_lo), (0, 0)))
    feats = _rwkv_prep(proj, mu_wide, mu_lo, _heads_minor(w0)[:, None, :], w_up_p, _heads_minor(a0)[:, None, :], a_up_p,
                       _heads_minor(k_k).reshape(1, C_WIDTH), _heads_minor(k_a).reshape(1, C_WIDTH), batch, seq, min(SCAN_TC, seq))
    xs = _rwkv_relayout(feats, min(SCAN_TC, seq))
    per_lane = lambda p: jnp.tile(p.reshape(C_HEADS, C_HEAD_DIM).T, (1, 2 * batch))[:, None, :]
    o = _rwkv_scan(xs, per_lane(r_k), per_lane(ln_w), per_lane(ln_b))
    return _rwkv_out(_rwkv_unlayout(o, batch), proj, _heads_minor(g_up).astype(BF16), batch, seq)


def _merge_kernel(h_ref, wg_ref, wbr_ref, wc_ref, oa_ref, ob_ref, oc_ref, od_ref, o_ref, wgb_ref, wbrb_ref, wcb_ref):
    _cast_weights_once((wg_ref, wbr_ref, wc_ref), (wgb_ref, wbrb_ref, wcb_ref))
    h = h_ref[...]
    acc = None
    row = 0
    for i, b_ref in enumerate((oa_ref, ob_ref, oc_ref, od_ref)):
        width = b_ref.shape[1]
        w = wcb_ref[...] if b_ref is oc_ref else wbrb_ref[row:row + width, :]
        gate = jax.nn.sigmoid(jnp.dot(h, wgb_ref[i], preferred_element_type=F32))
        term = gate * jnp.dot(b_ref[...], w, preferred_element_type=F32)
        acc = term if acc is None else acc + term
        row += width
    o_ref[...] = acc.astype(o_ref.dtype)


def _merge(h, w_gate, w_branch, layer, w_branch_c, branches, tm=512, tn=512):
    m = h.shape[0]
    tm = min(tm, m)
    d_mix = w_branch.shape[-2]
    row = lambda a: pl.BlockSpec((tm, a.shape[1]), lambda j, i: (i, 0))
    return pl.pallas_call(
        _merge_kernel,
        out_shape=jax.ShapeDtypeStruct((m, D_MODEL), BF16),
        grid=(D_MODEL // tn, m // tm),
        in_specs=[
            row(h),
            _weight_spec((4, D_MODEL, tn), lambda j, i: (0, 0, j), layer),
            _weight_spec((d_mix, tn), lambda j, i: (0, j), layer),
            _weight_spec((w_branch_c.shape[0], tn), lambda j, i: (0, j)),
        ] + [row(b) for b in branches],
        out_specs=pl.BlockSpec((tm, tn), lambda j, i: (i, j)),
        scratch_shapes=[pltpu.VMEM((4, D_MODEL, tn), BF16), pltpu.VMEM((d_mix, tn), BF16), pltpu.VMEM((w_branch_c.shape[0], tn), BF16)],
        compiler_params=_cp("parallel", "arbitrary"),
        name="gated_merge",
    )(h, w_gate, w_branch, w_branch_c, *branches)


def _ffn_up_kernel(h_ref, wg_ref, wu_ref, o_ref, wgb_ref, wub_ref):
    _cast_weights_once((wg_ref, wu_ref), (wgb_ref, wub_ref))
    h = h_ref[...]
    g = jnp.dot(h, wgb_ref[...], preferred_element_type=F32)
    u = jnp.dot(h, wub_ref[...], preferred_element_type=F32)
    o_ref[...] = (g * jax.nn.sigmoid(g) * u).astype(o_ref.dtype)


def _ffn_up(h, w_gate, w_up, layer, tm=2048, tn=512):
    m, k = h.shape
    n = w_gate.shape[-1]
    tm = min(tm, m)
    return pl.pallas_call(
        _ffn_up_kernel,
        out_shape=jax.ShapeDtypeStruct((m, n), BF16),
        grid=(n // tn, m // tm),
        in_specs=[
            pl.BlockSpec((tm, k), lambda j, i: (i, 0)),
            _weight_spec((k, tn), lambda j, i: (0, j), layer),
            _weight_spec((k, tn), lambda j, i: (0, j), layer),
        ],
        out_specs=pl.BlockSpec((tm, tn), lambda j, i: (i, j)),
        scratch_shapes=[pltpu.VMEM((k, tn), BF16), pltpu.VMEM((k, tn), BF16)],
        compiler_params=_cp("parallel", "arbitrary"),
        name="ffn_up",
    )(h, w_gate, w_up)


def _pad_w_in(w):
    pad = ((0, 0), (0, C_LORA_PAD - C_LORA))
    lo = COL_CLO
    cf = [_heads_minor(w[:, COL_CF + i * C_WIDTH:COL_CF + (i + 1) * C_WIDTH]) for i in range(3)]
    return jnp.concatenate([
        w[:, :COL_CF],
        *cf,
        jnp.pad(w[:, lo:lo + C_LORA], pad),
        jnp.pad(w[:, lo + C_LORA:lo + 2 * C_LORA], pad),
        w[:, lo + 2 * C_LORA:],
    ], axis=1)


def kernel(x, norm_mix, w_in, a_q_norm, a_k_norm, b_conv_w, b_conv_b, b_filt_w1, b_filt_b1, b_filt_w2, b_filt_b2, b_filt_w3, b_filt_b3, b_filt_w4, b_filt_freq, b_skip, c_mu, c_w0, c_w_up, c_a0, c_a_up, c_g_up, c_k_k, c_k_a, c_r_k, c_ln_w, c_ln_b, d_lq1, d_lk1, d_lq2, d_lk2, d_subln, w_gate, w_branch, w_out, norm_ffn, w_ff_gate, w_ff_up, w_ff_down, norm_final):
    batch, seq, _ = x.shape
    m = batch * seq
    cos, sin = _rope_tables(seq)
    cos_half, sin_half, cos_int, sin_int = _dft_tables(seq)
    filt_feats = _filter_features(seq)
    x = x.reshape(m, D_MODEL)
    for l in range(DEPTH):
        h = _rmsnorm(x, norm_mix[l], BF16)
        proj = _matmul(h, _pad_w_in(w_in[l]), F32, 1024, D_IN_PAD // 4, name="in_proj")

        gains = jnp.concatenate([
            jnp.broadcast_to(a_q_norm[l], (A_HEADS, A_HEAD_DIM)),
            jnp.broadcast_to(a_k_norm[l], (A_KV_HEADS, A_HEAD_DIM)),
        ])[:, None, :]
        o_a = _attn_a(_qk_prep(proj, gains, cos, sin, seq), proj, batch, seq)

        k_re, k_im = _hyena_filter_spectrum(seq, filt_feats, cos_int, sin_int, b_filt_w1[l], b_filt_b1[l], b_filt_w2[l],
                                            b_filt_b2[l], b_filt_w3[l], b_filt_b3[l], b_filt_w4[l], b_filt_freq[l])
        o_b = _hyena(proj, b_conv_w[l], b_conv_b[l], b_skip[l], cos_half, sin_half, k_re, k_im, batch, seq)

        o_c = _rwkv(proj, c_mu[l], c_w0[l], c_w_up[l], c_a0[l], c_a_up[l], c_g_up[l], c_k_k[l], c_k_a[l], c_r_k[l],
                    c_ln_w[l], c_ln_b[l], batch, seq)

        lam_init = 0.8 - 0.6 * math.exp(-0.3 * l)
        lam_vecs = jnp.stack([d_lq1[l], d_lk1[l], d_lq2[l], d_lk2[l]])
        o_d = _attn_d(proj, lam_vecs, d_subln[l], lam_init, batch, seq)

        c_lo, c_hi = A_WIDTH + B_WIDTH, A_WIDTH + B_WIDTH + C_WIDTH
        w_br_c = _heads_minor(w_branch[l, c_lo:c_hi].T).T
        merged = _merge(h, w_gate, w_branch, l, w_br_c, (o_a, o_b, o_c, o_d))
        x, h2 = _matmul_res_norm(merged, w_out, l, x, norm_ffn[l])
        mid = _ffn_up(h2, w_ff_gate, w_ff_up, l)
        x = _matmul(mid, w_ff_down, F32, 1024, 512, residual=x, layer=l, name="ffn_down")
    return _rmsnorm(x, norm_final, F32).reshape(batch, seq, D_MODEL)
```

```python
import functools
import math

import jax
import jax.numpy as jnp
import numpy as np
from jax import lax
from jax.experimental import pallas as pl
from jax.experimental.pallas import tpu as pltpu

D_MODEL = 2048
DEPTH = 2
GRID_W = 64
NORM_EPS = 1e-6

A_HEADS = 8
A_KV_HEADS = 2
A_HEAD_DIM = 128
A_WIDTH = A_HEADS * A_HEAD_DIM
ROPE_THETA = 10000.0

B_WIDTH = 512
B_EMB_DIM = 33
B_FILTER_HIDDEN = 64
B_DECAY_TARGET = 1e-2
B_FAST_DECAY_PCT = 0.3
B_SLOW_DECAY_PCT = 1.5

C_HEADS = 8
C_HEAD_DIM = 64
C_WIDTH = C_HEADS * C_HEAD_DIM
C_LORA = 96
C_LORA_PAD = 128
C_GATE_LORA = 256
C_GN_EPS = 64e-5

D_HEADS = 4
D_HEAD_DIM = 64
D_V_DIM = 2 * D_HEAD_DIM
D_WIDTH = D_HEADS * D_V_DIM

FFN_HIDDEN = -(-8 * D_MODEL // (3 * 256)) * 256

COL_AQ = 0
COL_AK = COL_AQ + A_WIDTH
COL_AV = COL_AK + A_KV_HEADS * A_HEAD_DIM
COL_BU = COL_AV + A_KV_HEADS * A_HEAD_DIM
COL_CF = COL_BU + 3 * B_WIDTH
COL_CLO = COL_CF + 3 * C_WIDTH
COL_CG = COL_CLO + 2 * C_LORA_PAD
COL_DQ = COL_CG + C_GATE_LORA
COL_DK = COL_DQ + 2 * D_HEADS * D_HEAD_DIM
COL_DV = COL_DK + 2 * D_HEADS * D_HEAD_DIM
D_IN_PAD = COL_DV + D_WIDTH

VMEM_LIMIT_V7X = 56 * 1024 * 1024
F32 = jnp.float32
BF16 = jnp.bfloat16
HIGHEST = lax.Precision.HIGHEST
NT_DIMS = (((1,), (1,)), ((), ()))
LOG2_E = math.log2(math.e)


def _cp(*sem):
    return pltpu.CompilerParams(dimension_semantics=sem, vmem_limit_bytes=VMEM_LIMIT_V7X)


def _const_spec(shape):
    return pl.BlockSpec(shape, lambda *_: (0,) * len(shape), pipeline_mode=pl.Buffered(1))


SINGLE_BUFFER_BYTES = 8 * 1024 * 1024


def _weight_spec(shape, index_map, layer=None):
    mode = {"pipeline_mode": pl.Buffered(1)} if 4 * math.prod(shape) > SINGLE_BUFFER_BYTES else {}
    if layer is None:
        return pl.BlockSpec(shape, index_map, **mode)
    return pl.BlockSpec((None,) + tuple(shape), lambda *g: (layer,) + tuple(index_map(*g)), **mode)


def _rmsnorm_kernel(x_ref, g_ref, o_ref):
    x = x_ref[...]
    ms = jnp.mean(x * x, axis=-1, keepdims=True)
    o_ref[...] = (x * lax.rsqrt(ms + NORM_EPS) * g_ref[...]).astype(o_ref.dtype)


def _rmsnorm(x, g, out_dtype, tm=1024):
    m, d = x.shape
    tm = min(tm, m)
    return pl.pallas_call(
        _rmsnorm_kernel,
        out_shape=jax.ShapeDtypeStruct((m, d), out_dtype),
        grid=(m // tm,),
        in_specs=[pl.BlockSpec((tm, d), lambda i: (i, 0)), pl.BlockSpec((1, d), lambda i: (0, 0))],
        out_specs=pl.BlockSpec((tm, d), lambda i: (i, 0)),
        compiler_params=_cp("parallel"),
        name="rmsnorm",
    )(x, g.reshape(1, d))


def _mm_f32_kernel(a_ref, b_ref, o_ref):
    def split(x):
        hi = x.astype(BF16)
        return hi, (x - hi.astype(F32)).astype(BF16)

    (a_hi, a_lo), (b_hi, b_lo) = split(a_ref[...]), split(b_ref[...])
    dot = functools.partial(jnp.dot, preferred_element_type=F32)
    o_ref[...] = dot(a_hi, b_hi) + (dot(a_hi, b_lo) + dot(a_lo, b_hi))


def _matmul_f32(a, b, tm, tn, name):
    m, k = a.shape
    n = b.shape[1]
    tm, tn = min(tm, m), min(tn, n)
    return pl.pallas_call(
        _mm_f32_kernel,
        out_shape=jax.ShapeDtypeStruct((m, n), F32),
        grid=(m // tm, n // tn),
        in_specs=[pl.BlockSpec((tm, k), lambda i, j: (i, 0)), pl.BlockSpec((k, tn), lambda i, j: (0, j))],
        out_specs=pl.BlockSpec((tm, tn), lambda i, j: (i, j)),
        compiler_params=_cp("parallel", "arbitrary"),
        name=name,
    )(a, b)


def _cast_weights_once(w_refs, wb_refs):
    @pl.when(pl.program_id(1) == 0)
    def _():
        for w_ref, wb_ref in zip(w_refs, wb_refs):
            wb_ref[...] = w_ref[...].astype(BF16)


def _mm_kernel(a_ref, w_ref, o_ref, wb_ref):
    _cast_weights_once((w_ref,), (wb_ref,))
    o_ref[...] = jnp.dot(a_ref[...], wb_ref[...], preferred_element_type=F32).astype(o_ref.dtype)


def _mm_res_kernel(a_ref, w_ref, r_ref, o_ref, wb_ref):
    _cast_weights_once((w_ref,), (wb_ref,))
    o_ref[...] = r_ref[...] + jnp.dot(a_ref[...], wb_ref[...], preferred_element_type=F32)


def _matmul(a, w, out_dtype, tm, tn, residual=None, layer=None, name="matmul"):
    m, k = a.shape
    n = w.shape[-1]
    tm, tn = min(tm, m), min(tn, n)
    in_specs = [pl.BlockSpec((tm, k), lambda j, i: (i, 0)), _weight_spec((k, tn), lambda j, i: (0, j), layer)]
    args = [a, w]
    body = _mm_kernel
    if residual is not None:
        body = _mm_res_kernel
        in_specs.append(pl.BlockSpec((tm, tn), lambda j, i: (i, j)))
        args.append(residual)
    return pl.pallas_call(
        body,
        out_shape=jax.ShapeDtypeStruct((m, n), out_dtype),
        grid=(n // tn, m // tm),
        in_specs=in_specs,
        out_specs=pl.BlockSpec((tm, tn), lambda j, i: (i, j)),
        scratch_shapes=[pltpu.VMEM((k, tn), BF16)],
        compiler_params=_cp("parallel", "arbitrary"),
        name=name,
    )(*args)


def _mm_res_norm_kernel(a_ref, w_ref, r_ref, g_ref, x_ref, h_ref, wb_ref):
    @pl.when(pl.program_id(0) == 0)
    def _():
        wb_ref[...] = w_ref[...].astype(BF16)

    x = r_ref[...] + jnp.dot(a_ref[...], wb_ref[...], preferred_element_type=F32)
    x_ref[...] = x
    ms = jnp.mean(x * x, axis=-1, keepdims=True)
    h_ref[...] = (x * lax.rsqrt(ms + NORM_EPS) * g_ref[...]).astype(h_ref.dtype)


def _matmul_res_norm(a, w, layer, residual, gain, tm=512):
    m, k = a.shape
    n = w.shape[-1]
    tm = min(tm, m)
    rows = lambda width: pl.BlockSpec((tm, width), lambda i: (i, 0))
    return pl.pallas_call(
        _mm_res_norm_kernel,
        out_shape=[jax.ShapeDtypeStruct((m, n), F32), jax.ShapeDtypeStruct((m, n), BF16)],
        grid=(m // tm,),
        in_specs=[rows(k), _weight_spec((k, n), lambda i: (0, 0), layer), rows(n), pl.BlockSpec((1, n), lambda i: (0, 0))],
        out_specs=[rows(n), rows(n)],
        scratch_shapes=[pltpu.VMEM((k, n), BF16)],
        compiler_params=_cp("arbitrary"),
        name="out_proj_norm",
    )(a, w, residual, gain.reshape(1, n))


def _rope_tables(seq):
    rows = seq // GRID_W
    row_idx = jnp.repeat(jnp.arange(rows, dtype=F32), GRID_W)
    col_idx = jnp.tile(jnp.arange(GRID_W, dtype=F32), rows)
    axis_dim = A_HEAD_DIM // 2
    inv_freq = ROPE_THETA ** (-jnp.arange(0, axis_dim, 2, dtype=F32) / axis_dim)
    ang_r = row_idx[:, None] * inv_freq[None, :]
    ang_c = col_idx[:, None] * inv_freq[None, :]
    ang = jnp.concatenate([ang_r, ang_r, ang_c, ang_c], axis=-1)
    return jnp.cos(ang), jnp.sin(ang)


def _qk_prep_kernel(x_ref, g_ref, cos_ref, sin_ref, o_ref):
    cos = cos_ref[...]
    sin = sin_ref[...]
    lane = lax.broadcasted_iota(jnp.int32, cos.shape, 1)
    quarter = A_HEAD_DIM // 4
    first = (lane % (2 * quarter)) < quarter
    for h in range(A_HEADS + A_KV_HEADS):
        cols = slice(h * A_HEAD_DIM, (h + 1) * A_HEAD_DIM)
        x = x_ref[:, cols]
        xn = x * lax.rsqrt(jnp.mean(x * x, axis=-1, keepdims=True) + NORM_EPS) * g_ref[h]
        rot = jnp.where(first, -pltpu.roll(xn, A_HEAD_DIM - quarter, 1), pltpu.roll(xn, quarter, 1))
        y = xn * cos + rot * sin
        if h < A_HEADS:
            y = y * (A_HEAD_DIM**-0.5 * LOG2_E)
        o_ref[:, cols] = y.astype(o_ref.dtype)


def _qk_prep(proj, gains, cos, sin, seq, tm=512):
    m = proj.shape[0]
    tm = min(tm, seq)
    width = (A_HEADS + A_KV_HEADS) * A_HEAD_DIM
    nt = seq // tm
    return pl.pallas_call(
        _qk_prep_kernel,
        out_shape=jax.ShapeDtypeStruct((m, width), BF16),
        grid=(m // tm,),
        in_specs=[
            pl.BlockSpec((tm, width), lambda i: (i, 0)),
            pl.BlockSpec(gains.shape, lambda i: (0, 0, 0)),
            pl.BlockSpec((tm, A_HEAD_DIM), lambda i: (i % nt, 0)),
            pl.BlockSpec((tm, A_HEAD_DIM), lambda i: (i % nt, 0)),
        ],
        out_specs=pl.BlockSpec((tm, width), lambda i: (i, 0)),
        compiler_params=_cp("parallel"),
        name="a_qk_prep",
    )(proj, gains, cos, sin)


class _OnlineSoftmax:
    def __init__(self, rows, width):
        self.m = jnp.full((rows, 1), -jnp.inf, F32)
        self.l = jnp.zeros((rows, 1), F32)
        self.acc = jnp.zeros((rows, width), F32)

    def add(self, s, v):
        m_new = jnp.maximum(self.m, jnp.max(s, axis=-1, keepdims=True))
        p = jnp.exp2(s - m_new)
        alpha = jnp.exp2(self.m - m_new)
        self.l = alpha * self.l + jnp.sum(p, axis=-1, keepdims=True)
        self.acc = alpha * self.acc + jnp.dot(p.astype(BF16), v, preferred_element_type=F32)
        self.m = m_new

    def result(self):
        return self.acc / self.l


def _attn_a_kernel(q_ref, k_ref, v_ref, o_ref, *, kc):
    q = q_ref[...]
    sm = _OnlineSoftmax(q.shape[0], A_HEAD_DIM)
    for c in range(k_ref.shape[0] // kc):
        keys = slice(c * kc, (c + 1) * kc)
        s = lax.dot_general(q, k_ref[keys, :], NT_DIMS, preferred_element_type=F32)
        sm.add(s, v_ref[keys, :].astype(BF16))
    o_ref[...] = sm.result().astype(o_ref.dtype)


def _attn_a(qk, proj, batch, seq, tq=1024, kc=1024):
    tq = min(tq, seq)
    kc = min(kc, seq)
    nq = seq // tq
    group = A_HEADS // A_KV_HEADS
    k_blk = COL_AK // A_HEAD_DIM
    v_blk = COL_AV // A_HEAD_DIM
    return pl.pallas_call(
        functools.partial(_attn_a_kernel, kc=kc),
        out_shape=jax.ShapeDtypeStruct((batch * seq, A_WIDTH), BF16),
        grid=(batch, A_HEADS, nq),
        in_specs=[
            pl.BlockSpec((tq, A_HEAD_DIM), lambda b, h, i: (b * nq + i, h)),
            pl.BlockSpec((seq, A_HEAD_DIM), lambda b, h, i: (b, k_blk + h // group)),
            pl.BlockSpec((seq, A_HEAD_DIM), lambda b, h, i: (b, v_blk + h // group)),
        ],
        out_specs=pl.BlockSpec((tq, A_HEAD_DIM), lambda b, h, i: (b * nq + i, h)),
        compiler_params=_cp("parallel", "arbitrary", "arbitrary"),
        name="a_attention",
    )(qk, qk, proj)


def _attn_d_kernel(q_ref, k_ref, v_ref, slope_ref, lam_ref, g_ref, o_ref, *, tq, kc, lam_init):
    q = q_ref[...] * (D_HEAD_DIM**-0.5 * LOG2_E)
    lane = lax.broadcasted_iota(jnp.int32, q.shape, 1)
    q_maps = (jnp.where(lane < D_HEAD_DIM, q, 0.0).astype(BF16), jnp.where(lane >= D_HEAD_DIM, q, 0.0).astype(BF16))
    slope = slope_ref[0][:, 0:1] * LOG2_E
    rel = pl.program_id(2) * tq + lax.broadcasted_iota(jnp.int32, (tq, kc), 0) - lax.broadcasted_iota(jnp.int32, (tq, kc), 1)
    rel = slope * rel.astype(F32)
    maps = (_OnlineSoftmax(tq, D_V_DIM), _OnlineSoftmax(tq, D_V_DIM))
    for c in range(k_ref.shape[0] // kc):
        keys = slice(c * kc, (c + 1) * kc)
        k = k_ref[keys, :].astype(BF16)
        v = v_ref[keys, :].astype(BF16)
        bias = jnp.abs(rel - slope * float(c * kc))
        for q_map, sm in zip(q_maps, maps):
            sm.add(lax.dot_general(q_map, k, NT_DIMS, preferred_element_type=F32) - bias, v)
    lam_v = lam_ref[...]
    lam = (
        jnp.exp(jnp.sum(lam_v[0:1] * lam_v[1:2], axis=-1, keepdims=True))
        - jnp.exp(jnp.sum(lam_v[2:3] * lam_v[3:4], axis=-1, keepdims=True))
        + lam_init
    )
    o = maps[0].result() - lam * maps[1].result()
    o = o * lax.rsqrt(jnp.mean(o * o, axis=-1, keepdims=True) + NORM_EPS) * g_ref[...]
    o_ref[...] = (o * (1.0 - lam_init)).astype(o_ref.dtype)


def _attn_d(proj, lam_vecs, subln, lam_init, batch, seq, tq=1024, kc=512):
    tq = min(tq, seq)
    kc = min(kc, seq)
    nq = seq // tq
    slopes = 2.0 ** (-8.0 * np.arange(1, D_HEADS + 1, dtype=np.float32) / D_HEADS)
    slopes = jnp.asarray(np.broadcast_to(slopes[:, None, None], (D_HEADS, 1, 128)).astype(np.float32))
    q_blk, k_blk, v_blk = COL_DQ // D_V_DIM, COL_DK // D_V_DIM, COL_DV // D_V_DIM
    return pl.pallas_call(
        functools.partial(_attn_d_kernel, tq=tq, kc=kc, lam_init=lam_init),
        out_shape=jax.ShapeDtypeStruct((batch * seq, D_WIDTH), BF16),
        grid=(batch, D_HEADS, nq),
        in_specs=[
            pl.BlockSpec((tq, D_V_DIM), lambda b, h, i: (b * nq + i, q_blk + h)),
            pl.BlockSpec((seq, D_V_DIM), lambda b, h, i: (b, k_blk + h)),
            pl.BlockSpec((seq, D_V_DIM), lambda b, h, i: (b, v_blk + h)),
            pl.BlockSpec((1, 1, 128), lambda b, h, i: (h, 0, 0)),
            pl.BlockSpec((4, D_HEAD_DIM), lambda b, h, i: (0, 0)),
            pl.BlockSpec((1, D_V_DIM), lambda b, h, i: (0, 0)),
        ],
        out_specs=pl.BlockSpec((tq, D_V_DIM), lambda b, h, i: (b * nq + i, h)),
        compiler_params=_cp("parallel", "arbitrary", "arbitrary"),
        name="d_attention",
    )(proj, proj, proj, slopes, lam_vecs, subln.reshape(1, D_V_DIM))


DFT_ROW_SPLIT = 64


def _dft_tables(seq):
    n = 2 * seq
    blk = min(DFT_ROW_SPLIT, seq)
    col = jnp.arange(seq, dtype=jnp.int32)[None, :]
    hi = jnp.arange(seq // blk, dtype=jnp.int32)[:, None]
    lo = jnp.arange(blk, dtype=jnp.int32)[:, None]

    def cos_sin(index, period):
        ang = (index % period).astype(F32) * (2.0 * math.pi / period)
        return jnp.cos(ang), jnp.sin(ang)

    def tables(col_term, period):
        (ca, sa), (cb, sb) = cos_sin(2 * blk * hi * col_term, period), cos_sin((2 * lo + 1) * col_term, period)
        ca, sa, cb, sb = ca[:, None, :], sa[:, None, :], cb[None], sb[None]
        return (ca * cb - sa * sb).reshape(seq, seq), (sa * cb + ca * sb).reshape(seq, seq)

    cos_half, sin_half = tables(2 * col + 1, 4 * n)
    cos_int, sin_int = tables(col, 2 * n)
    return cos_half.astype(BF16), sin_half.astype(BF16), cos_int, sin_int


def _filter_features(seq):
    t = jnp.linspace(0.0, 1.0, seq, dtype=F32)[:, None]
    n_bands = (B_EMB_DIM - 1) // 2
    bands = jnp.linspace(1e-4, n_bands - 1, n_bands, dtype=F32)[None, :]
    ang = (2.0 * math.pi / seq) * jnp.arange(seq, dtype=F32)[:, None] * bands
    z = jnp.concatenate([t, jnp.cos(ang), -jnp.sin(ang)], axis=-1)
    z = jnp.pad(z, ((0, 0), (0, B_FILTER_HIDDEN - B_EMB_DIM)))
    max_decay = math.log(B_DECAY_TARGET) / B_FAST_DECAY_PCT
    min_decay = math.log(B_DECAY_TARGET) / B_SLOW_DECAY_PCT
    deltas = jnp.abs(jnp.linspace(min_decay, max_decay, B_WIDTH, dtype=F32))[None, :]
    return z, t, deltas


def _filter_kernel(z_ref, t_ref, dl_ref, w1_ref, b1_ref, w2_ref, b2_ref, w3_ref, b3_ref, w4_ref, fr_ref, hs_ref, hd_ref):
    fr = fr_ref[...]
    hid = jnp.sin(fr * (jnp.dot(z_ref[...], w1_ref[...], preferred_element_type=F32, precision=HIGHEST) + b1_ref[...]))
    hid = jnp.sin(fr * (jnp.dot(hid, w2_ref[...], preferred_element_type=F32, precision=HIGHEST) + b2_ref[...]))
    hid = jnp.sin(fr * (jnp.dot(hid, w3_ref[...], preferred_element_type=F32, precision=HIGHEST) + b3_ref[...]))
    h = jnp.dot(hid, w4_ref[...], preferred_element_type=F32, precision=HIGHEST)
    window = jnp.exp(-t_ref[...] * dl_ref[...])
    h_fwd = h[:, :B_WIDTH] * window
    h_bwd = h[:, B_WIDTH:] * window
    row = lax.broadcasted_iota(jnp.int32, h_bwd.shape, 0)
    h_bwd = jnp.where(row == 0, 0.0, h_bwd)
    norm = jnp.sum(jnp.abs(h_fwd), axis=0, keepdims=True) + jnp.sum(jnp.abs(h_bwd), axis=0, keepdims=True)
    seq = h.shape[0]
    inv_n = 1.0 / seq
    hs_ref[...] = (h_fwd + h_bwd) / norm * inv_n
    hd_ref[...] = (h_bwd - h_fwd) / norm * inv_n


def _hyena_filter_spectrum(seq, feats, cos_int, sin_int, w1, b1, w2, b2, w3, b3, w4, freq):
    z, t, deltas = feats
    w1p = jnp.pad(w1, ((0, B_FILTER_HIDDEN - B_EMB_DIM), (0, 0)))
    row = lambda v: v.reshape(1, -1)
    args = (z, t, deltas, w1p, row(b1), w2, row(b2), w3, row(b3), w4, row(freq))
    h_sum, h_diff = pl.pallas_call(
        _filter_kernel,
        out_shape=[jax.ShapeDtypeStruct((seq, B_WIDTH), F32)] * 2,
        in_specs=[pl.BlockSpec(a.shape, lambda: (0, 0)) for a in args],
        out_specs=[pl.BlockSpec((seq, B_WIDTH), lambda: (0, 0))] * 2,
        compiler_params=pltpu.CompilerParams(vmem_limit_bytes=VMEM_LIMIT_V7X),
        name="b_filter",
    )(*args)
    k_re = _matmul_f32(cos_int, h_sum, 256, B_WIDTH, name="b_filter_dft_re")
    k_im = _matmul_f32(sin_int, h_diff, 256, B_WIDTH, name="b_filter_dft_im")
    return k_re, k_im


def _hyena_kernel(v_ref, x1_ref, x0_ref, wv_ref, w1_ref, w0_ref, bv_ref, b1_ref, b0_ref, skip_ref, c_ref, s_ref, kre_ref, kim_ref, o_ref,
                  *, f_blk):
    seq = v_ref.shape[0]
    row = lax.broadcasted_iota(jnp.int32, v_ref.shape, 0)

    def conv3(u_ref, w_ref, b_ref):
        u = u_ref[...]
        w = w_ref[...]
        u_prev = jnp.where(row == 0, 0.0, pltpu.roll(u, 1, 0))
        u_next = jnp.where(row == seq - 1, 0.0, pltpu.roll(u, seq - 1, 0))
        return w[0:1] * u_prev + w[1:2] * u + w[2:3] * u_next + b_ref[...]

    z = conv3(v_ref, wv_ref, bv_ref) * conv3(x1_ref, w1_ref, b1_ref)
    zb = z.astype(BF16)
    y = z * skip_ref[...]
    for f0 in range(0, seq, f_blk):
        fs = slice(f0, f0 + f_blk)
        cz = jnp.dot(c_ref[fs, :], zb, preferred_element_type=F32)
        sz = jnp.dot(s_ref[fs, :], zb, preferred_element_type=F32)
        k_re = kre_ref[fs, :]
        k_im = kim_ref[fs, :]
        y_re = (cz * k_re + sz * k_im).astype(BF16)
        y_im = (cz * k_im - sz * k_re).astype(BF16)
        y = y + (jnp.dot(c_ref[:, fs], y_re, preferred_element_type=F32) - jnp.dot(s_ref[:, fs], y_im, preferred_element_type=F32))
    o_ref[...] = (y * conv3(x0_ref, w0_ref, b0_ref)).astype(o_ref.dtype)


def _hyena(proj, conv_w, conv_b, skip, cos_half, sin_half, k_re, k_im, batch, seq, cb=256, f_blk=1024):
    ncb = B_WIDTH // cb
    u_blk = COL_BU // cb
    f_blk = min(f_blk, seq)

    def u_spec(part):
        return pl.BlockSpec((seq, cb), lambda j, b: (b, u_blk + part * ncb + j))

    def w_spec(rows, part):
        return pl.BlockSpec((rows, cb), lambda j, b: (0, part * ncb + j))

    k_spec = pl.BlockSpec((seq, cb), lambda j, b: (0, j), pipeline_mode=pl.Buffered(1))
    return pl.pallas_call(
        functools.partial(_hyena_kernel, f_blk=f_blk),
        out_shape=jax.ShapeDtypeStruct((batch * seq, B_WIDTH), BF16),
        grid=(ncb, batch),
        in_specs=[
            u_spec(0), u_spec(1), u_spec(2),
            w_spec(3, 0), w_spec(3, 1), w_spec(3, 2),
            w_spec(1, 0), w_spec(1, 1), w_spec(1, 2),
            pl.BlockSpec((1, cb), lambda j, b: (0, j)),
            _const_spec((seq, seq)), _const_spec((seq, seq)),
            k_spec, k_spec,
        ],
        out_specs=pl.BlockSpec((seq, cb), lambda j, b: (b, j)),
        compiler_params=_cp("parallel", "arbitrary"),
        name="b_hyena",
    )(proj, proj, proj, conv_w, conv_w, conv_w, conv_b.reshape(1, -1), conv_b.reshape(1, -1), conv_b.reshape(1, -1),
      skip.reshape(1, -1), cos_half, sin_half, k_re, k_im)


def _rwkv_prep_kernel(cur_ref, lo_ref, pcur_ref, plo_ref, ncur_ref, nlo_ref, mu_ref, mulo_ref, w0_ref, wup_ref, a0_ref, aup_ref,
                      kk_ref, ka_ref, o_ref, *, chunk):
    first = pl.program_id(1) == 0
    last = pl.program_id(1) == pl.num_programs(1) - 1
    halo = pcur_ref.shape[0]
    tm = cur_ref.shape[0]
    row_in_chunk = lax.broadcasted_iota(jnp.int32, (tm, C_WIDTH), 0) % chunk

    def neighbours(x_ref, p_ref, n_ref):
        x = x_ref[...]
        row = lax.broadcasted_iota(jnp.int32, x.shape, 0)
        p_row = jnp.where(first, 0.0, p_ref[halo - 1:halo, :])
        n_row = jnp.where(last, 0.0, n_ref[0:1, :])
        prev = jnp.where(row == 0, p_row, pltpu.roll(x, 1, 0))
        nxt = jnp.where(row == tm - 1, n_row, pltpu.roll(x, tm - 1, 0))
        return x, (prev, nxt)

    cur, cur_sh = neighbours(cur_ref, pcur_ref, ncur_ref)
    lo, lo_sh = neighbours(lo_ref, plo_ref, nlo_ref)
    k_k = kk_ref[...]
    k_a = ka_ref[...]
    for d in range(2):
        f = cur + (cur_sh[d] - cur) * mu_ref[d]
        f_lo = lo + (lo_sh[d] - lo) * mulo_ref[d]
        r = f[:, :C_WIDTH]
        k = f[:, C_WIDTH:2 * C_WIDTH]
        v = f[:, 2 * C_WIDTH:]
        w_lo = f_lo[:, :C_LORA_PAD]
        a_lo = f_lo[:, C_LORA_PAD:]
        x = w0_ref[d] + jnp.dot(jnp.tanh(w_lo), wup_ref[d], preferred_element_type=F32, precision=HIGHEST)
        w = -(jnp.maximum(-x, 0.0) + jnp.log(1.0 + jnp.exp(-jnp.abs(x)))) - 0.5
        neg_log_decay = jnp.exp(w)
        log_g = neg_log_decay
        shift = 1
        while shift < chunk:
            if d == 0:
                moved = jnp.where(row_in_chunk >= shift, pltpu.roll(log_g, shift, 0), 0.0)
            else:
                moved = jnp.where(row_in_chunk < chunk - shift, pltpu.roll(log_g, tm - shift, 0), 0.0)
            log_g = log_g + moved
            shift *= 2
        g = jnp.exp(-log_g)
        g_inv = jnp.exp(log_g)
        g_before = jnp.exp(neg_log_decay - log_g)
        a = jax.nn.sigmoid(a0_ref[d] + jnp.dot(a_lo, aup_ref[d], preferred_element_type=F32, precision=HIGHEST))
        transposed = lambda val: val.T.reshape(C_HEAD_DIM, C_HEADS, tm)
        kk = transposed(k * k_k)
        kk = kk / jnp.maximum(jnp.sqrt(jnp.sum(kk * kk, axis=0, keepdims=True)), 1e-12)
        for q, val in enumerate((r * g, g, k * (1.0 + (a - 1.0) * k_a) * g_inv, v)):
            o_ref[d, q] = transposed(val)
        o_ref[d, 4] = -kk * transposed(g_before)
        o_ref[d, 5] = kk * transposed(a * g_inv)


def _rwkv_prep(proj, mu, mu_lo, w0, w_up, a0, a_up, k_k, k_a, batch, seq, chunk, tm=256):
    tm = min(tm, seq)
    nt = seq // tm
    halo = 8
    hb = tm // halo
    n_hblk = batch * seq // halo
    wide, narrow = 3 * C_WIDTH, 2 * C_LORA_PAD
    cur_blk, lo_blk = COL_CF // wide, COL_CLO // narrow

    def cur_map(blk):
        return lambda b, j: (b * nt + j, blk)

    def prev_map(blk):
        return lambda b, j: (jnp.maximum((b * nt + j) * hb - 1, 0), blk)

    def next_map(blk):
        return lambda b, j: (jnp.minimum((b * nt + j + 1) * hb, n_hblk - 1), blk)

    small = lambda a: pl.BlockSpec(a.shape, lambda b, j: (0,) * a.ndim)
    consts = (mu, mu_lo, w0, w_up, a0, a_up, k_k, k_a)
    return pl.pallas_call(
        functools.partial(_rwkv_prep_kernel, chunk=chunk),
        out_shape=jax.ShapeDtypeStruct((2, 6, C_HEAD_DIM, batch, C_HEADS, seq), F32),
        grid=(batch, nt),
        in_specs=[
            pl.BlockSpec((tm, wide), cur_map(cur_blk)),
            pl.BlockSpec((tm, narrow), cur_map(lo_blk)),
            pl.BlockSpec((halo, wide), prev_map(cur_blk)),
            pl.BlockSpec((halo, narrow), prev_map(lo_blk)),
            pl.BlockSpec((halo, wide), next_map(cur_blk)),
            pl.BlockSpec((halo, narrow), next_map(lo_blk)),
        ] + [small(a) for a in consts],
        out_specs=pl.BlockSpec((2, 6, C_HEAD_DIM, None, C_HEADS, tm), lambda b, j: (0, 0, 0, b, 0, j)),
        compiler_params=_cp("parallel", "arbitrary"),
        name="c_prep",
    )(proj, proj, proj, proj, proj, proj, *consts)


RELAYOUT_T = 128


def _rwkv_relayout_kernel(pf_ref, pb_ref, o_ref, *, tc):
    n = pf_ref.shape[0]
    t_blk = pf_ref.shape[-1]
    mirror = t_blk - 1 - lax.broadcasted_iota(jnp.int32, (pb_ref.shape[1] * pb_ref.shape[2], t_blk), 1)
    for k in range(n):
        bwd = jnp.take_along_axis(pb_ref[k].reshape(-1, t_blk), mirror, axis=1)
        cols = jnp.concatenate([pf_ref[k].reshape(-1, t_blk), bwd], axis=0).T
        for c in range(t_blk // tc):
            o_ref[c, k] = cols[c * tc:(c + 1) * tc]


def _rwkv_relayout(p, tc):
    _, nq, n, batch, heads, seq = p.shape
    t_blk = min(RELAYOUT_T, seq)
    nb = seq // t_blk
    lanes = 2 * batch * heads
    per_blk = t_blk // tc
    return pl.pallas_call(
        functools.partial(_rwkv_relayout_kernel, tc=tc),
        out_shape=jax.ShapeDtypeStruct((seq // tc, nq, n, tc, lanes), F32),
        grid=(nb, nq),
        in_specs=[
            pl.BlockSpec((None, None, n, batch, heads, t_blk), lambda c, q: (0, q, 0, 0, 0, c)),
            pl.BlockSpec((None, None, n, batch, heads, t_blk), lambda c, q: (1, q, 0, 0, 0, nb - 1 - c)),
        ],
        out_specs=pl.BlockSpec((per_blk, None, n, tc, lanes), lambda c, q: (c, q, 0, 0, 0)),
        compiler_params=_cp("parallel", "arbitrary"),
        name="c_relayout",
    )(p, p)


def _rwkv_scan_kernel(x_ref, rk_ref, lnw_ref, lnb_ref, out_ref, s_ref, o_ref, *, tc, k_chunk):
    n = s_ref.shape[0]
    lanes = s_ref.shape[2]

    @pl.when(pl.program_id(0) == 0)
    def _():
        s_ref[...] = jnp.zeros_like(s_ref)

    def row(q, k, t):
        return x_ref[pl.ds((q * n + k) * tc + t, 1), :]

    def step(t, carry):
        vv = x_ref[pl.ds(3 * n * tc + t, n, stride=tc), :]

        def sa_chunk(c, sa):
            base = pl.multiple_of(c * k_chunk, k_chunk)
            for j in range(k_chunk):
                sa = sa + s_ref[base + j] * row(4, base + j, t)
            return sa

        sa = lax.fori_loop(0, n // k_chunk, sa_chunk, jnp.zeros((n, lanes), F32))

        def update_chunk(c, o):
            base = pl.multiple_of(c * k_chunk, k_chunk)
            for j in range(k_chunk):
                k = base + j
                sk = s_ref[k] + (sa * row(5, k, t) + vv * row(2, k, t))
                s_ref[k] = sk
                o = o + sk * row(0, k, t)
            return o

        o_ref[pl.ds(t, n, stride=tc), :] = lax.fori_loop(0, n // k_chunk, update_chunk, jnp.zeros((n, lanes), F32))
        return carry

    lax.fori_loop(0, tc, step, 0)
    for k in range(n):
        s_ref[k] = s_ref[k] * x_ref[(n + k) * tc + tc - 1:(n + k) * tc + tc, :]

    quantity = lambda q: x_ref[q * n * tc:(q + 1) * n * tc, :].reshape(n, tc, lanes)
    o = o_ref[...].reshape(n, tc, lanes)
    mean = jnp.mean(o, axis=0, keepdims=True)
    var = jnp.mean(jnp.square(o - mean), axis=0, keepdims=True)
    o = (o - mean) * lax.rsqrt(var + C_GN_EPS) * lnw_ref[...] + lnb_ref[...]
    bonus = jnp.sum(quantity(0) * quantity(2) * rk_ref[...], axis=0, keepdims=True)
    out_ref[...] = o + bonus * quantity(3)


def _rwkv_scan(xs, rk, lnw, lnb, k_chunk=32):
    nc, nq, n, tc, lanes = xs.shape
    tile = lambda: pl.BlockSpec((n, 1, lanes), lambda i: (0, 0, 0))
    return pl.pallas_call(
        functools.partial(_rwkv_scan_kernel, tc=tc, k_chunk=min(k_chunk, n)),
        out_shape=jax.ShapeDtypeStruct((n, nc * tc, lanes), F32),
        grid=(nc,),
        in_specs=[pl.BlockSpec((nq * n * tc, lanes), lambda i: (i, 0)), tile(), tile(), tile()],
        out_specs=pl.BlockSpec((n, tc, lanes), lambda i: (0, i, 0)),
        scratch_shapes=[pltpu.VMEM((n, n, lanes), F32), pltpu.VMEM((n * tc, lanes), F32)],
        compiler_params=_cp("arbitrary"),
        name="c_scan",
    )(xs.reshape(nc * nq * n * tc, lanes), rk, lnw, lnb)


def _rwkv_unlayout_kernel(o_ref, q_ref):
    for v in range(o_ref.shape[0]):
        rows = o_ref[v].T
        q_ref[:, v] = rows.reshape(q_ref.shape[0], q_ref.shape[2], q_ref.shape[3])


def _rwkv_unlayout(o, batch):
    n, seq, lanes = o.shape
    t_blk = min(RELAYOUT_T, seq)
    heads = lanes // (2 * batch)
    return pl.pallas_call(
        _rwkv_unlayout_kernel,
        out_shape=jax.ShapeDtypeStruct((2 * batch, n, heads, seq), F32),
        grid=(seq // t_blk,),
        in_specs=[pl.BlockSpec((n, t_blk, lanes), lambda c: (0, c, 0))],
        out_specs=pl.BlockSpec((2 * batch, n, heads, t_blk), lambda c: (0, 0, 0, c)),
        compiler_params=_cp("parallel"),
        name="c_unlayout",
    )(o)


def _rwkv_out_kernel(qf_ref, qb_ref, glo_ref, gup_ref, y_ref):
    t_blk = qf_ref.shape[-1]
    fwd = qf_ref[...].reshape(-1, t_blk)
    bwd = qb_ref[...].reshape(-1, t_blk)
    mirror = t_blk - 1 - lax.broadcasted_iota(jnp.int32, bwd.shape, 1)
    o = (fwd + jnp.take_along_axis(bwd, mirror, axis=1)).T
    g = jnp.dot(jax.nn.sigmoid(glo_ref[...]).astype(BF16), gup_ref[...], preferred_element_type=F32)
    y_ref[...] = (o * g).astype(y_ref.dtype)


def _rwkv_out(q, proj, g_up, batch, seq):
    _, n, heads, _ = q.shape
    t_blk = min(RELAYOUT_T, seq)
    nt = seq // t_blk
    g_blk = COL_CG // C_GATE_LORA
    return pl.pallas_call(
        _rwkv_out_kernel,
        out_shape=jax.ShapeDtypeStruct((batch * seq, C_WIDTH), BF16),
        grid=(batch, nt),
        in_specs=[
            pl.BlockSpec((None, n, heads, t_blk), lambda b, j: (b, 0, 0, j)),
            pl.BlockSpec((None, n, heads, t_blk), lambda b, j: (batch + b, 0, 0, nt - 1 - j)),
            pl.BlockSpec((t_blk, C_GATE_LORA), lambda b, j: (b * nt + j, g_blk)),
            pl.BlockSpec((C_GATE_LORA, C_WIDTH), lambda b, j: (0, 0)),
        ],
        out_specs=pl.BlockSpec((t_blk, C_WIDTH), lambda b, j: (b * nt + j, 0)),
        compiler_params=_cp("parallel", "arbitrary"),
        name="c_out",
    )(q, q, proj, g_up)


SCAN_TC = 16


def _heads_minor(p):
    return p.reshape(p.shape[:-1] + (C_HEADS, C_HEAD_DIM)).swapaxes(-1, -2).reshape(p.shape)


def _rwkv(proj, mu, w0, w_up, a0, a_up, g_up, k_k, k_a, r_k, ln_w, ln_b, batch, seq):
    pad_lo = C_LORA_PAD - C_LORA
    split = 3 * C_WIDTH
    mu_wide = jnp.concatenate([_heads_minor(mu[:, i * C_WIDTH:(i + 1) * C_WIDTH]) for i in range(3)], axis=-1)[:, None, :]
    mu_lo = jnp.concatenate([
        jnp.pad(mu[:, split:split + C_LORA], ((0, 0), (0, pad_lo))),
        jnp.pad(mu[:, split + C_LORA:], ((0, 0), (0, pad_lo))),
    ], axis=-1)[:, None, :]
    w_up_p = jnp.pad(_heads_minor(w_up), ((0, 0), (0, pad_lo), (0, 0)))
    a_up_p = jnp.pad(_heads_minor(a_up), ((0, 0), (0, pad_lo), (0, 0)))
    feats = _rwkv_prep(proj, mu_wide, mu_lo, _heads_minor(w0)[:, None, :], w_up_p, _heads_minor(a0)[:, None, :], a_up_p,
                       _heads_minor(k_k).reshape(1, C_WIDTH), _heads_minor(k_a).reshape(1, C_WIDTH), batch, seq, min(SCAN_TC, seq))
    xs = _rwkv_relayout(feats, min(SCAN_TC, seq))
    per_lane = lambda p: jnp.tile(p.reshape(C_HEADS, C_HEAD_DIM).T, (1, 2 * batch))[:, None, :]
    o = _rwkv_scan(xs, per_lane(r_k), per_lane(ln_w), per_lane(ln_b))
    return _rwkv_out(_rwkv_unlayout(o, batch), proj, _heads_minor(g_up).astype(BF16), batch, seq)


def _merge_kernel(h_ref, wg_ref, wbr_ref, wc_ref, oa_ref, ob_ref, oc_ref, od_ref, o_ref, wgb_ref, wbrb_ref, wcb_ref):
    _cast_weights_once((wg_ref, wbr_ref, wc_ref), (wgb_ref, wbrb_ref, wcb_ref))
    h = h_ref[...]
    acc = None
    row = 0
    for i, b_ref in enumerate((oa_ref, ob_ref, oc_ref, od_ref)):
        width = b_ref.shape[1]
        w = wcb_ref[...] if b_ref is oc_ref else wbrb_ref[row:row + width, :]
        gate = jax.nn.sigmoid(jnp.dot(h, wgb_ref[i], preferred_element_type=F32))
        term = gate * jnp.dot(b_ref[...], w, preferred_element_type=F32)
        acc = term if acc is None else acc + term
        row += width
    o_ref[...] = acc.astype(o_ref.dtype)


def _merge(h, w_gate, w_branch, layer, w_branch_c, branches, tm=512, tn=512):
    m = h.shape[0]
    tm = min(tm, m)
    d_mix = w_branch.shape[-2]
    row = lambda a: pl.BlockSpec((tm, a.shape[1]), lambda j, i: (i, 0))
    return pl.pallas_call(
        _merge_kernel,
        out_shape=jax.ShapeDtypeStruct((m, D_MODEL), BF16),
        grid=(D_MODEL // tn, m // tm),
        in_specs=[
            row(h),
            _weight_spec((4, D_MODEL, tn), lambda j, i: (0, 0, j), layer),
            _weight_spec((d_mix, tn), lambda j, i: (0, j), layer),
            _weight_spec((w_branch_c.shape[0], tn), lambda j, i: (0, j)),
        ] + [row(b) for b in branches],
        out_specs=pl.BlockSpec((tm, tn), lambda j, i: (i, j)),
        scratch_shapes=[pltpu.VMEM((4, D_MODEL, tn), BF16), pltpu.VMEM((d_mix, tn), BF16), pltpu.VMEM((w_branch_c.shape[0], tn), BF16)],
        compiler_params=_cp("parallel", "arbitrary"),
        name="gated_merge",
    )(h, w_gate, w_branch, w_branch_c, *branches)


def _ffn_up_kernel(h_ref, wg_ref, wu_ref, o_ref, wgb_ref, wub_ref):
    _cast_weights_once((wg_ref, wu_ref), (wgb_ref, wub_ref))
    h = h_ref[...]
    g = jnp.dot(h, wgb_ref[...], preferred_element_type=F32)
    u = jnp.dot(h, wub_ref[...], preferred_element_type=F32)
    o_ref[...] = (g * jax.nn.sigmoid(g) * u).astype(o_ref.dtype)


def _ffn_up(h, w_gate, w_up, layer, tm=1024, tn=512):
    m, k = h.shape
    n = w_gate.shape[-1]
    tm = min(tm, m)
    return pl.pallas_call(
        _ffn_up_kernel,
        out_shape=jax.ShapeDtypeStruct((m, n), BF16),
        grid=(n // tn, m // tm),
        in_specs=[
            pl.BlockSpec((tm, k), lambda j, i: (i, 0)),
            _weight_spec((k, tn), lambda j, i: (0, j), layer),
            _weight_spec((k, tn), lambda j, i: (0, j), layer),
        ],
        out_specs=pl.BlockSpec((tm, tn), lambda j, i: (i, j)),
        scratch_shapes=[pltpu.VMEM((k, tn), BF16), pltpu.VMEM((k, tn), BF16)],
        compiler_params=_cp("parallel", "arbitrary"),
        name="ffn_up",
    )(h, w_gate, w_up)


def _pad_w_in(w):
    pad = ((0, 0), (0, C_LORA_PAD - C_LORA))
    lo = COL_CLO
    cf = [_heads_minor(w[:, COL_CF + i * C_WIDTH:COL_CF + (i + 1) * C_WIDTH]) for i in range(3)]
    return jnp.concatenate([
        w[:, :COL_CF],
        *cf,
        jnp.pad(w[:, lo:lo + C_LORA], pad),
        jnp.pad(w[:, lo + C_LORA:lo + 2 * C_LORA], pad),
        w[:, lo + 2 * C_LORA:],
    ], axis=1)


def kernel(x, norm_mix, w_in, a_q_norm, a_k_norm, b_conv_w, b_conv_b, b_filt_w1, b_filt_b1, b_filt_w2, b_filt_b2, b_filt_w3, b_filt_b3, b_filt_w4, b_filt_freq, b_skip, c_mu, c_w0, c_w_up, c_a0, c_a_up, c_g_up, c_k_k, c_k_a, c_r_k, c_ln_w, c_ln_b, d_lq1, d_lk1, d_lq2, d_lk2, d_subln, w_gate, w_branch, w_out, norm_ffn, w_ff_gate, w_ff_up, w_ff_down, norm_final):
    batch, seq, _ = x.shape
    m = batch * seq
    cos, sin = _rope_tables(seq)
    cos_half, sin_half, cos_int, sin_int = _dft_tables(seq)
    filt_feats = _filter_features(seq)
    x = x.reshape(m, D_MODEL)
    for l in range(DEPTH):
        h = _rmsnorm(x, norm_mix[l], BF16)
        proj = _matmul(h, _pad_w_in(w_in[l]), F32, 1024, D_IN_PAD // 4, name="in_proj")

        gains = jnp.concatenate([
            jnp.broadcast_to(a_q_norm[l], (A_HEADS, A_HEAD_DIM)),
            jnp.broadcast_to(a_k_norm[l], (A_KV_HEADS, A_HEAD_DIM)),
        ])[:, None, :]
        o_a = _attn_a(_qk_prep(proj, gains, cos, sin, seq), proj, batch, seq)

        k_re, k_im = _hyena_filter_spectrum(seq, filt_feats, cos_int, sin_int, b_filt_w1[l], b_filt_b1[l], b_filt_w2[l],
                                            b_filt_b2[l], b_filt_w3[l], b_filt_b3[l], b_filt_w4[l], b_filt_freq[l])
        o_b = _hyena(proj, b_conv_w[l], b_conv_b[l], b_skip[l], cos_half, sin_half, k_re, k_im, batch, seq)

        o_c = _rwkv(proj, c_mu[l], c_w0[l], c_w_up[l], c_a0[l], c_a_up[l], c_g_up[l], c_k_k[l], c_k_a[l], c_r_k[l],
                    c_ln_w[l], c_ln_b[l], batch, seq)

        lam_init = 0.8 - 0.6 * math.exp(-0.3 * l)
        lam_vecs = jnp.stack([d_lq1[l], d_lk1[l], d_lq2[l], d_lk2[l]])
        o_d = _attn_d(proj, lam_vecs, d_subln[l], lam_init, batch, seq)

        c_lo, c_hi = A_WIDTH + B_WIDTH, A_WIDTH + B_WIDTH + C_WIDTH
        w_br_c = _heads_minor(w_branch[l, c_lo:c_hi].T).T
        merged = _merge(h, w_gate, w_branch, l, w_br_c, (o_a, o_b, o_c, o_d))
        x, h2 = _matmul_res_norm(merged, w_out, l, x, norm_ffn[l])
        mid = _ffn_up(h2, w_ff_gate, w_ff_up, l)
        x = _matmul(mid, w_ff_down, F32, 1024, 512, residual=x, layer=l, name="ffn_down")
    return _rmsnorm(x, norm_final, F32).reshape(batch, seq, D_MODEL)
```

```python
import functools
import math

import jax
import jax.numpy as jnp
import numpy as np
from jax import lax
from jax.experimental import pallas as pl
from jax.experimental.pallas import tpu as pltpu

D_MODEL = 2048
DEPTH = 2
GRID_W = 64
NORM_EPS = 1e-6

A_HEADS = 8
A_KV_HEADS = 2
A_HEAD_DIM = 128
A_WIDTH = A_HEADS * A_HEAD_DIM
ROPE_THETA = 10000.0

B_WIDTH = 512
B_EMB_DIM = 33
B_FILTER_HIDDEN = 64
B_DECAY_TARGET = 1e-2
B_FAST_DECAY_PCT = 0.3
B_SLOW_DECAY_PCT = 1.5

C_HEADS = 8
C_HEAD_DIM = 64
C_WIDTH = C_HEADS * C_HEAD_DIM
C_LORA = 96
C_LORA_PAD = 128
C_GATE_LORA = 256
C_GN_EPS = 64e-5

D_HEADS = 4
D_HEAD_DIM = 64
D_V_DIM = 2 * D_HEAD_DIM
D_WIDTH = D_HEADS * D_V_DIM

FFN_HIDDEN = -(-8 * D_MODEL // (3 * 256)) * 256

COL_AQ = 0
COL_AK = COL_AQ + A_WIDTH
COL_AV = COL_AK + A_KV_HEADS * A_HEAD_DIM
COL_BU = COL_AV + A_KV_HEADS * A_HEAD_DIM
COL_CF = COL_BU + 3 * B_WIDTH
COL_CLO = COL_CF + 3 * C_WIDTH
COL_CG = COL_CLO + 2 * C_LORA_PAD
COL_DQ = COL_CG + C_GATE_LORA
COL_DK = COL_DQ + 2 * D_HEADS * D_HEAD_DIM
COL_DV = COL_DK + 2 * D_HEADS * D_HEAD_DIM
D_IN_PAD = COL_DV + D_WIDTH

VMEM_LIMIT_V7X = 56 * 1024 * 1024
F32 = jnp.float32
BF16 = jnp.bfloat16
HIGHEST = lax.Precision.HIGHEST
NT_DIMS = (((1,), (1,)), ((), ()))
LOG2_E = math.log2(math.e)


def _cp(*sem):
    return pltpu.CompilerParams(dimension_semantics=sem, vmem_limit_bytes=VMEM_LIMIT_V7X)


def _const_spec(shape):
    return pl.BlockSpec(shape, lambda *_: (0,) * len(shape), pipeline_mode=pl.Buffered(1))


SINGLE_BUFFER_BYTES = 8 * 1024 * 1024


def _weight_spec(shape, index_map, layer=None):
    mode = {"pipeline_mode": pl.Buffered(1)} if 4 * math.prod(shape) > SINGLE_BUFFER_BYTES else {}
    if layer is None:
        return pl.BlockSpec(shape, index_map, **mode)
    return pl.BlockSpec((None,) + tuple(shape), lambda *g: (layer,) + tuple(index_map(*g)), **mode)


def _rmsnorm_kernel(x_ref, g_ref, o_ref):
    x = x_ref[...]
    ms = jnp.mean(x * x, axis=-1, keepdims=True)
    o_ref[...] = (x * lax.rsqrt(ms + NORM_EPS) * g_ref[...]).astype(o_ref.dtype)


def _rmsnorm(x, g, out_dtype, tm=1024):
    m, d = x.shape
    tm = min(tm, m)
    return pl.pallas_call(
        _rmsnorm_kernel,
        out_shape=jax.ShapeDtypeStruct((m, d), out_dtype),
        grid=(m // tm,),
        in_specs=[pl.BlockSpec((tm, d), lambda i: (i, 0)), pl.BlockSpec((1, d), lambda i: (0, 0))],
        out_specs=pl.BlockSpec((tm, d), lambda i: (i, 0)),
        compiler_params=_cp("parallel"),
        name="rmsnorm",
    )(x, g.reshape(1, d))


def _mm_f32_kernel(a_ref, b_ref, o_ref):
    def split(x):
        hi = x.astype(BF16)
        return hi, (x - hi.astype(F32)).astype(BF16)

    (a_hi, a_lo), (b_hi, b_lo) = split(a_ref[...]), split(b_ref[...])
    dot = functools.partial(jnp.dot, preferred_element_type=F32)
    o_ref[...] = dot(a_hi, b_hi) + (dot(a_hi, b_lo) + dot(a_lo, b_hi))


def _matmul_f32(a, b, tm, tn, name):
    m, k = a.shape
    n = b.shape[1]
    tm, tn = min(tm, m), min(tn, n)
    return pl.pallas_call(
        _mm_f32_kernel,
        out_shape=jax.ShapeDtypeStruct((m, n), F32),
        grid=(m // tm, n // tn),
        in_specs=[pl.BlockSpec((tm, k), lambda i, j: (i, 0)), pl.BlockSpec((k, tn), lambda i, j: (0, j))],
        out_specs=pl.BlockSpec((tm, tn), lambda i, j: (i, j)),
        compiler_params=_cp("parallel", "arbitrary"),
        name=name,
    )(a, b)


def _cast_weights_once(w_refs, wb_refs):
    @pl.when(pl.program_id(1) == 0)
    def _():
        for w_ref, wb_ref in zip(w_refs, wb_refs):
            wb_ref[...] = w_ref[...].astype(BF16)


def _mm_kernel(a_ref, w_ref, o_ref, wb_ref):
    _cast_weights_once((w_ref,), (wb_ref,))
    o_ref[...] = jnp.dot(a_ref[...], wb_ref[...], preferred_element_type=F32).astype(o_ref.dtype)


def _mm_res_kernel(a_ref, w_ref, r_ref, o_ref, wb_ref):
    _cast_weights_once((w_ref,), (wb_ref,))
    o_ref[...] = r_ref[...] + jnp.dot(a_ref[...], wb_ref[...], preferred_element_type=F32)


def _matmul(a, w, out_dtype, tm, tn, residual=None, layer=None, name="matmul"):
    m, k = a.shape
    n = w.shape[-1]
    tm, tn = min(tm, m), min(tn, n)
    in_specs = [pl.BlockSpec((tm, k), lambda j, i: (i, 0)), _weight_spec((k, tn), lambda j, i: (0, j), layer)]
    args = [a, w]
    body = _mm_kernel
    if residual is not None:
        body = _mm_res_kernel
        in_specs.append(pl.BlockSpec((tm, tn), lambda j, i: (i, j)))
        args.append(residual)
    return pl.pallas_call(
        body,
        out_shape=jax.ShapeDtypeStruct((m, n), out_dtype),
        grid=(n // tn, m // tm),
        in_specs=in_specs,
        out_specs=pl.BlockSpec((tm, tn), lambda j, i: (i, j)),
        scratch_shapes=[pltpu.VMEM((k, tn), BF16)],
        compiler_params=_cp("parallel", "arbitrary"),
        name=name,
    )(*args)


def _mm_res_norm_kernel(a_ref, w_ref, r_ref, g_ref, x_ref, h_ref, wb_ref):
    @pl.when(pl.program_id(0) == 0)
    def _():
        wb_ref[...] = w_ref[...].astype(BF16)

    x = r_ref[...] + jnp.dot(a_ref[...], wb_ref[...], preferred_element_type=F32)
    x_ref[...] = x
    ms = jnp.mean(x * x, axis=-1, keepdims=True)
    h_ref[...] = (x * lax.rsqrt(ms + NORM_EPS) * g_ref[...]).astype(h_ref.dtype)


def _matmul_res_norm(a, w, layer, residual, gain, tm=512):
    m, k = a.shape
    n = w.shape[-1]
    tm = min(tm, m)
    rows = lambda width: pl.BlockSpec((tm, width), lambda i: (i, 0))
    return pl.pallas_call(
        _mm_res_norm_kernel,
        out_shape=[jax.ShapeDtypeStruct((m, n), F32), jax.ShapeDtypeStruct((m, n), BF16)],
        grid=(m // tm,),
        in_specs=[rows(k), _weight_spec((k, n), lambda i: (0, 0), layer), rows(n), pl.BlockSpec((1, n), lambda i: (0, 0))],
        out_specs=[rows(n), rows(n)],
        scratch_shapes=[pltpu.VMEM((k, n), BF16)],
        compiler_params=_cp("arbitrary"),
        name="out_proj_norm",
    )(a, w, residual, gain.reshape(1, n))


def _rope_tables(seq):
    rows = seq // GRID_W
    row_idx = jnp.repeat(jnp.arange(rows, dtype=F32), GRID_W)
    col_idx = jnp.tile(jnp.arange(GRID_W, dtype=F32), rows)
    axis_dim = A_HEAD_DIM // 2
    inv_freq = ROPE_THETA ** (-jnp.arange(0, axis_dim, 2, dtype=F32) / axis_dim)
    ang_r = row_idx[:, None] * inv_freq[None, :]
    ang_c = col_idx[:, None] * inv_freq[None, :]
    ang = jnp.concatenate([ang_r, ang_r, ang_c, ang_c], axis=-1)
    return jnp.cos(ang), jnp.sin(ang)


def _qk_prep_kernel(x_ref, g_ref, cos_ref, sin_ref, o_ref):
    cos = cos_ref[...]
    sin = sin_ref[...]
    lane = lax.broadcasted_iota(jnp.int32, cos.shape, 1)
    quarter = A_HEAD_DIM // 4
    first = (lane % (2 * quarter)) < quarter
    for h in range(A_HEADS + A_KV_HEADS):
        cols = slice(h * A_HEAD_DIM, (h + 1) * A_HEAD_DIM)
        x = x_ref[:, cols]
        xn = x * lax.rsqrt(jnp.mean(x * x, axis=-1, keepdims=True) + NORM_EPS) * g_ref[h]
        rot = jnp.where(first, -pltpu.roll(xn, A_HEAD_DIM - quarter, 1), pltpu.roll(xn, quarter, 1))
        y = xn * cos + rot * sin
        if h < A_HEADS:
            y = y * (A_HEAD_DIM**-0.5 * LOG2_E)
        o_ref[:, cols] = y.astype(o_ref.dtype)


def _qk_prep(proj, gains, cos, sin, seq, tm=512):
    m = proj.shape[0]
    tm = min(tm, seq)
    width = (A_HEADS + A_KV_HEADS) * A_HEAD_DIM
    nt = seq // tm
    return pl.pallas_call(
        _qk_prep_kernel,
        out_shape=jax.ShapeDtypeStruct((m, width), BF16),
        grid=(m // tm,),
        in_specs=[
            pl.BlockSpec((tm, width), lambda i: (i, 0)),
            pl.BlockSpec(gains.shape, lambda i: (0, 0, 0)),
            pl.BlockSpec((tm, A_HEAD_DIM), lambda i: (i % nt, 0)),
            pl.BlockSpec((tm, A_HEAD_DIM), lambda i: (i % nt, 0)),
        ],
        out_specs=pl.BlockSpec((tm, width), lambda i: (i, 0)),
        compiler_params=_cp("parallel"),
        name="a_qk_prep",
    )(proj, gains, cos, sin)


class _OnlineSoftmax:
    def __init__(self, rows, width):
        self.m = jnp.full((rows, 1), -jnp.inf, F32)
        self.l = jnp.zeros((rows, 1), F32)
        self.acc = jnp.zeros((rows, width), F32)

    def add(self, s, v):
        m_new = jnp.maximum(self.m, jnp.max(s, axis=-1, keepdims=True))
        p = jnp.exp2(s - m_new)
        alpha = jnp.exp2(self.m - m_new)
        self.l = alpha * self.l + jnp.sum(p, axis=-1, keepdims=True)
        self.acc = alpha * self.acc + jnp.dot(p.astype(BF16), v, preferred_element_type=F32)
        self.m = m_new

    def result(self):
        return self.acc / self.l


def _attn_a_kernel(q_ref, k_ref, v_ref, o_ref, *, kc):
    q = q_ref[...]
    sm = _OnlineSoftmax(q.shape[0], A_HEAD_DIM)
    for c in range(k_ref.shape[0] // kc):
        keys = slice(c * kc, (c + 1) * kc)
        s = lax.dot_general(q, k_ref[keys, :], NT_DIMS, preferred_element_type=F32)
        sm.add(s, v_ref[keys, :].astype(BF16))
    o_ref[...] = sm.result().astype(o_ref.dtype)


def _attn_a(qk, proj, batch, seq, tq=1024, kc=1024):
    tq = min(tq, seq)
    kc = min(kc, seq)
    nq = seq // tq
    group = A_HEADS // A_KV_HEADS
    k_blk = COL_AK // A_HEAD_DIM
    v_blk = COL_AV // A_HEAD_DIM
    return pl.pallas_call(
        functools.partial(_attn_a_kernel, kc=kc),
        out_shape=jax.ShapeDtypeStruct((batch * seq, A_WIDTH), BF16),
        grid=(batch, A_HEADS, nq),
        in_specs=[
            pl.BlockSpec((tq, A_HEAD_DIM), lambda b, h, i: (b * nq + i, h)),
            pl.BlockSpec((seq, A_HEAD_DIM), lambda b, h, i: (b, k_blk + h // group)),
            pl.BlockSpec((seq, A_HEAD_DIM), lambda b, h, i: (b, v_blk + h // group)),
        ],
        out_specs=pl.BlockSpec((tq, A_HEAD_DIM), lambda b, h, i: (b * nq + i, h)),
        compiler_params=_cp("parallel", "arbitrary", "arbitrary"),
        name="a_attention",
    )(qk, qk, proj)


def _attn_d_kernel(q_ref, k_ref, v_ref, slope_ref, lam_ref, g_ref, o_ref, *, tq, kc, lam_init):
    q = q_ref[...] * (D_HEAD_DIM**-0.5 * LOG2_E)
    lane = lax.broadcasted_iota(jnp.int32, q.shape, 1)
    q_maps = (jnp.where(lane < D_HEAD_DIM, q, 0.0).astype(BF16), jnp.where(lane >= D_HEAD_DIM, q, 0.0).astype(BF16))
    slope = slope_ref[0][:, 0:1] * LOG2_E
    rel = pl.program_id(2) * tq + lax.broadcasted_iota(jnp.int32, (tq, kc), 0) - lax.broadcasted_iota(jnp.int32, (tq, kc), 1)
    rel = slope * rel.astype(F32)
    maps = (_OnlineSoftmax(tq, D_V_DIM), _OnlineSoftmax(tq, D_V_DIM))
    for c in range(k_ref.shape[0] // kc):
        keys = slice(c * kc, (c + 1) * kc)
        k = k_ref[keys, :].astype(BF16)
        v = v_ref[keys, :].astype(BF16)
        bias = jnp.abs(rel - slope * float(c * kc))
        for q_map, sm in zip(q_maps, maps):
            sm.add(lax.dot_general(q_map, k, NT_DIMS, preferred_element_type=F32) - bias, v)
    lam_v = lam_ref[...]
    lam = (
        jnp.exp(jnp.sum(lam_v[0:1] * lam_v[1:2], axis=-1, keepdims=True))
        - jnp.exp(jnp.sum(lam_v[2:3] * lam_v[3:4], axis=-1, keepdims=True))
        + lam_init
    )
    o = maps[0].result() - lam * maps[1].result()
    o = o * lax.rsqrt(jnp.mean(o * o, axis=-1, keepdims=True) + NORM_EPS) * g_ref[...]
    o_ref[...] = (o * (1.0 - lam_init)).astype(o_ref.dtype)


def _attn_d(proj, lam_vecs, subln, lam_init, batch, seq, tq=1024, kc=512):
    tq = min(tq, seq)
    kc = min(kc, seq)
    nq = seq // tq
    slopes = 2.0 ** (-8.0 * np.arange(1, D_HEADS + 1, dtype=np.float32) / D_HEADS)
    slopes = jnp.asarray(np.broadcast_to(slopes[:, None, None], (D_HEADS, 1, 128)).astype(np.float32))
    q_blk, k_blk, v_blk = COL_DQ // D_V_DIM, COL_DK // D_V_DIM, COL_DV // D_V_DIM
    return pl.pallas_call(
        functools.partial(_attn_d_kernel, tq=tq, kc=kc, lam_init=lam_init),
        out_shape=jax.ShapeDtypeStruct((batch * seq, D_WIDTH), BF16),
        grid=(batch, D_HEADS, nq),
        in_specs=[
            pl.BlockSpec((tq, D_V_DIM), lambda b, h, i: (b * nq + i, q_blk + h)),
            pl.BlockSpec((seq, D_V_DIM), lambda b, h, i: (b, k_blk + h)),
            pl.BlockSpec((seq, D_V_DIM), lambda b, h, i: (b, v_blk + h)),
            pl.BlockSpec((1, 1, 128), lambda b, h, i: (h, 0, 0)),
            pl.BlockSpec((4, D_HEAD_DIM), lambda b, h, i: (0, 0)),
            pl.BlockSpec((1, D_V_DIM), lambda b, h, i: (0, 0)),
        ],
        out_specs=pl.BlockSpec((tq, D_V_DIM), lambda b, h, i: (b * nq + i, h)),
        compiler_params=_cp("parallel", "arbitrary", "arbitrary"),
        name="d_attention",
    )(proj, proj, proj, slopes, lam_vecs, subln.reshape(1, D_V_DIM))


DFT_ROW_SPLIT = 64


def _dft_tables(seq):
    n = 2 * seq
    blk = min(DFT_ROW_SPLIT, seq)
    col = jnp.arange(seq, dtype=jnp.int32)[None, :]
    hi = jnp.arange(seq // blk, dtype=jnp.int32)[:, None]
    lo = jnp.arange(blk, dtype=jnp.int32)[:, None]

    def cos_sin(index, period):
        ang = (index % period).astype(F32) * (2.0 * math.pi / period)
        return jnp.cos(ang), jnp.sin(ang)

    def tables(col_term, period):
        (ca, sa), (cb, sb) = cos_sin(2 * blk * hi * col_term, period), cos_sin((2 * lo + 1) * col_term, period)
        ca, sa, cb, sb = ca[:, None, :], sa[:, None, :], cb[None], sb[None]
        return (ca * cb - sa * sb).reshape(seq, seq), (sa * cb + ca * sb).reshape(seq, seq)

    cos_half, sin_half = tables(2 * col + 1, 4 * n)
    cos_int, sin_int = tables(col, 2 * n)
    return cos_half.astype(BF16), sin_half.astype(BF16), cos_int, sin_int


def _filter_features(seq):
    t = jnp.linspace(0.0, 1.0, seq, dtype=F32)[:, None]
    n_bands = (B_EMB_DIM - 1) // 2
    bands = jnp.linspace(1e-4, n_bands - 1, n_bands, dtype=F32)[None, :]
    ang = (2.0 * math.pi / seq) * jnp.arange(seq, dtype=F32)[:, None] * bands
    z = jnp.concatenate([t, jnp.cos(ang), -jnp.sin(ang)], axis=-1)
    z = jnp.pad(z, ((0, 0), (0, B_FILTER_HIDDEN - B_EMB_DIM)))
    max_decay = math.log(B_DECAY_TARGET) / B_FAST_DECAY_PCT
    min_decay = math.log(B_DECAY_TARGET) / B_SLOW_DECAY_PCT
    deltas = jnp.abs(jnp.linspace(min_decay, max_decay, B_WIDTH, dtype=F32))[None, :]
    return z, t, deltas


def _filter_kernel(z_ref, t_ref, dl_ref, w1_ref, b1_ref, w2_ref, b2_ref, w3_ref, b3_ref, w4_ref, fr_ref, hs_ref, hd_ref):
    fr = fr_ref[...]
    hid = jnp.sin(fr * (jnp.dot(z_ref[...], w1_ref[...], preferred_element_type=F32, precision=HIGHEST) + b1_ref[...]))
    hid = jnp.sin(fr * (jnp.dot(hid, w2_ref[...], preferred_element_type=F32, precision=HIGHEST) + b2_ref[...]))
    hid = jnp.sin(fr * (jnp.dot(hid, w3_ref[...], preferred_element_type=F32, precision=HIGHEST) + b3_ref[...]))
    h = jnp.dot(hid, w4_ref[...], preferred_element_type=F32, precision=HIGHEST)
    window = jnp.exp(-t_ref[...] * dl_ref[...])
    h_fwd = h[:, :B_WIDTH] * window
    h_bwd = h[:, B_WIDTH:] * window
    row = lax.broadcasted_iota(jnp.int32, h_bwd.shape, 0)
    h_bwd = jnp.where(row == 0, 0.0, h_bwd)
    norm = jnp.sum(jnp.abs(h_fwd), axis=0, keepdims=True) + jnp.sum(jnp.abs(h_bwd), axis=0, keepdims=True)
    seq = h.shape[0]
    inv_n = 1.0 / seq
    hs_ref[...] = (h_fwd + h_bwd) / norm * inv_n
    hd_ref[...] = (h_bwd - h_fwd) / norm * inv_n


def _hyena_filter_spectrum(seq, feats, cos_int, sin_int, w1, b1, w2, b2, w3, b3, w4, freq):
    z, t, deltas = feats
    w1p = jnp.pad(w1, ((0, B_FILTER_HIDDEN - B_EMB_DIM), (0, 0)))
    row = lambda v: v.reshape(1, -1)
    args = (z, t, deltas, w1p, row(b1), w2, row(b2), w3, row(b3), w4, row(freq))
    h_sum, h_diff = pl.pallas_call(
        _filter_kernel,
        out_shape=[jax.ShapeDtypeStruct((seq, B_WIDTH), F32)] * 2,
        in_specs=[pl.BlockSpec(a.shape, lambda: (0, 0)) for a in args],
        out_specs=[pl.BlockSpec((seq, B_WIDTH), lambda: (0, 0))] * 2,
        compiler_params=pltpu.CompilerParams(vmem_limit_bytes=VMEM_LIMIT_V7X),
        name="b_filter",
    )(*args)
    k_re = _matmul_f32(cos_int, h_sum, 256, B_WIDTH, name="b_filter_dft_re")
    k_im = _matmul_f32(sin_int, h_diff, 256, B_WIDTH, name="b_filter_dft_im")
    return k_re, k_im


def _hyena_kernel(v_ref, x1_ref, x0_ref, wv_ref, w1_ref, w0_ref, bv_ref, b1_ref, b0_ref, skip_ref, c_ref, s_ref, kre_ref, kim_ref, o_ref,
                  *, f_blk):
    seq = v_ref.shape[0]
    row = lax.broadcasted_iota(jnp.int32, v_ref.shape, 0)

    def conv3(u_ref, w_ref, b_ref):
        u = u_ref[...]
        w = w_ref[...]
        u_prev = jnp.where(row == 0, 0.0, pltpu.roll(u, 1, 0))
        u_next = jnp.where(row == seq - 1, 0.0, pltpu.roll(u, seq - 1, 0))
        return w[0:1] * u_prev + w[1:2] * u + w[2:3] * u_next + b_ref[...]

    z = conv3(v_ref, wv_ref, bv_ref) * conv3(x1_ref, w1_ref, b1_ref)
    zb = z.astype(BF16)
    y = z * skip_ref[...]
    for f0 in range(0, seq, f_blk):
        fs = slice(f0, f0 + f_blk)
        cz = jnp.dot(c_ref[fs, :], zb, preferred_element_type=F32)
        sz = jnp.dot(s_ref[fs, :], zb, preferred_element_type=F32)
        k_re = kre_ref[fs, :]
        k_im = kim_ref[fs, :]
        y_re = (cz * k_re + sz * k_im).astype(BF16)
        y_im = (cz * k_im - sz * k_re).astype(BF16)
        y = y + (jnp.dot(c_ref[:, fs], y_re, preferred_element_type=F32) - jnp.dot(s_ref[:, fs], y_im, preferred_element_type=F32))
    o_ref[...] = (y * conv3(x0_ref, w0_ref, b0_ref)).astype(o_ref.dtype)


def _hyena(proj, conv_w, conv_b, skip, cos_half, sin_half, k_re, k_im, batch, seq, cb=256, f_blk=1024):
    ncb = B_WIDTH // cb
    u_blk = COL_BU // cb
    f_blk = min(f_blk, seq)

    def u_spec(part):
        return pl.BlockSpec((seq, cb), lambda j, b: (b, u_blk + part * ncb + j))

    def w_spec(rows, part):
        return pl.BlockSpec((rows, cb), lambda j, b: (0, part * ncb + j))

    k_spec = pl.BlockSpec((seq, cb), lambda j, b: (0, j), pipeline_mode=pl.Buffered(1))
    return pl.pallas_call(
        functools.partial(_hyena_kernel, f_blk=f_blk),
        out_shape=jax.ShapeDtypeStruct((batch * seq, B_WIDTH), BF16),
        grid=(ncb, batch),
        in_specs=[
            u_spec(0), u_spec(1), u_spec(2),
            w_spec(3, 0), w_spec(3, 1), w_spec(3, 2),
            w_spec(1, 0), w_spec(1, 1), w_spec(1, 2),
            pl.BlockSpec((1, cb), lambda j, b: (0, j)),
            _const_spec((seq, seq)), _const_spec((seq, seq)),
            k_spec, k_spec,
        ],
        out_specs=pl.BlockSpec((seq, cb), lambda j, b: (b, j)),
        compiler_params=_cp("parallel", "arbitrary"),
        name="b_hyena",
    )(proj, proj, proj, conv_w, conv_w, conv_w, conv_b.reshape(1, -1), conv_b.reshape(1, -1), conv_b.reshape(1, -1),
      skip.reshape(1, -1), cos_half, sin_half, k_re, k_im)


def _rwkv_prep_kernel(cur_ref, lo_ref, pcur_ref, plo_ref, ncur_ref, nlo_ref, mu_ref, mulo_ref, w0_ref, wup_ref, a0_ref, aup_ref,
                      kk_ref, ka_ref, o_ref, *, chunk):
    first = pl.program_id(1) == 0
    last = pl.program_id(1) == pl.num_programs(1) - 1
    halo = pcur_ref.shape[0]
    tm = cur_ref.shape[0]
    row_in_chunk = lax.broadcasted_iota(jnp.int32, (tm, C_WIDTH), 0) % chunk

    def neighbours(x_ref, p_ref, n_ref):
        x = x_ref[...]
        row = lax.broadcasted_iota(jnp.int32, x.shape, 0)
        p_row = jnp.where(first, 0.0, p_ref[halo - 1:halo, :])
        n_row = jnp.where(last, 0.0, n_ref[0:1, :])
        prev = jnp.where(row == 0, p_row, pltpu.roll(x, 1, 0))
        nxt = jnp.where(row == tm - 1, n_row, pltpu.roll(x, tm - 1, 0))
        return x, (prev, nxt)

    cur, cur_sh = neighbours(cur_ref, pcur_ref, ncur_ref)
    lo, lo_sh = neighbours(lo_ref, plo_ref, nlo_ref)
    k_k = kk_ref[...]
    k_a = ka_ref[...]
    for d in range(2):
        f = cur + (cur_sh[d] - cur) * mu_ref[d]
        f_lo = lo + (lo_sh[d] - lo) * mulo_ref[d]
        r = f[:, :C_WIDTH]
        k = f[:, C_WIDTH:2 * C_WIDTH]
        v = f[:, 2 * C_WIDTH:]
        w_lo = f_lo[:, :C_LORA_PAD]
        a_lo = f_lo[:, C_LORA_PAD:]
        x = w0_ref[d] + jnp.dot(jnp.tanh(w_lo), wup_ref[d], preferred_element_type=F32, precision=HIGHEST)
        w = -(jnp.maximum(-x, 0.0) + jnp.log(1.0 + jnp.exp(-jnp.abs(x)))) - 0.5
        neg_log_decay = jnp.exp(w)
        log_g = neg_log_decay
        shift = 1
        while shift < chunk:
            if d == 0:
                moved = jnp.where(row_in_chunk >= shift, pltpu.roll(log_g, shift, 0), 0.0)
            else:
                moved = jnp.where(row_in_chunk < chunk - shift, pltpu.roll(log_g, tm - shift, 0), 0.0)
            log_g = log_g + moved
            shift *= 2
        g = jnp.exp(-log_g)
        g_inv = jnp.exp(log_g)
        g_before = jnp.exp(neg_log_decay - log_g)
        a = jax.nn.sigmoid(a0_ref[d] + jnp.dot(a_lo, aup_ref[d], preferred_element_type=F32, precision=HIGHEST))
        transposed = lambda val: val.T.reshape(C_HEAD_DIM, C_HEADS, tm)
        kk = transposed(k * k_k)
        kk = kk / jnp.maximum(jnp.sqrt(jnp.sum(kk * kk, axis=0, keepdims=True)), 1e-12)
        for q, val in enumerate((r * g, g, k * (1.0 + (a - 1.0) * k_a) * g_inv, v)):
            o_ref[d, q] = transposed(val)
        o_ref[d, 4] = -kk * transposed(g_before)
        o_ref[d, 5] = kk * transposed(a * g_inv)


def _rwkv_prep(proj, mu, mu_lo, w0, w_up, a0, a_up, k_k, k_a, batch, seq, chunk, tm=256):
    tm = min(tm, seq)
    nt = seq // tm
    halo = 8
    hb = tm // halo
    n_hblk = batch * seq // halo
    wide, narrow = 3 * C_WIDTH, 2 * C_LORA_PAD
    cur_blk, lo_blk = COL_CF // wide, COL_CLO // narrow

    def cur_map(blk):
        return lambda b, j: (b * nt + j, blk)

    def prev_map(blk):
        return lambda b, j: (jnp.maximum((b * nt + j) * hb - 1, 0), blk)

    def next_map(blk):
        return lambda b, j: (jnp.minimum((b * nt + j + 1) * hb, n_hblk - 1), blk)

    small = lambda a: pl.BlockSpec(a.shape, lambda b, j: (0,) * a.ndim)
    consts = (mu, mu_lo, w0, w_up, a0, a_up, k_k, k_a)
    return pl.pallas_call(
        functools.partial(_rwkv_prep_kernel, chunk=chunk),
        out_shape=jax.ShapeDtypeStruct((2, 6, C_HEAD_DIM, batch, C_HEADS, seq), F32),
        grid=(batch, nt),
        in_specs=[
            pl.BlockSpec((tm, wide), cur_map(cur_blk)),
            pl.BlockSpec((tm, narrow), cur_map(lo_blk)),
            pl.BlockSpec((halo, wide), prev_map(cur_blk)),
            pl.BlockSpec((halo, narrow), prev_map(lo_blk)),
            pl.BlockSpec((halo, wide), next_map(cur_blk)),
            pl.BlockSpec((halo, narrow), next_map(lo_blk)),
        ] + [small(a) for a in consts],
        out_specs=pl.BlockSpec((2, 6, C_HEAD_DIM, None, C_HEADS, tm), lambda b, j: (0, 0, 0, b, 0, j)),
        compiler_params=_cp("parallel", "arbitrary"),
        name="c_prep",
    )(proj, proj, proj, proj, proj, proj, *consts)


RELAYOUT_T = 128


def _rwkv_relayout_kernel(pf_ref, pb_ref, o_ref, *, tc):
    n = pf_ref.shape[0]
    t_blk = pf_ref.shape[-1]
    mirror = t_blk - 1 - lax.broadcasted_iota(jnp.int32, (pb_ref.shape[1] * pb_ref.shape[2], t_blk), 1)
    for k in range(n):
        bwd = jnp.take_along_axis(pb_ref[k].reshape(-1, t_blk), mirror, axis=1)
        cols = jnp.concatenate([pf_ref[k].reshape(-1, t_blk), bwd], axis=0).T
        for c in range(t_blk // tc):
            o_ref[c, k] = cols[c * tc:(c + 1) * tc]


def _rwkv_relayout(p, tc):
    _, nq, n, batch, heads, seq = p.shape
    t_blk = min(RELAYOUT_T, seq)
    nb = seq // t_blk
    lanes = 2 * batch * heads
    per_blk = t_blk // tc
    return pl.pallas_call(
        functools.partial(_rwkv_relayout_kernel, tc=tc),
        out_shape=jax.ShapeDtypeStruct((seq // tc, nq, n, tc, lanes), F32),
        grid=(nb, nq),
        in_specs=[
            pl.BlockSpec((None, None, n, batch, heads, t_blk), lambda c, q: (0, q, 0, 0, 0, c)),
            pl.BlockSpec((None, None, n, batch, heads, t_blk), lambda c, q: (1, q, 0, 0, 0, nb - 1 - c)),
        ],
        out_specs=pl.BlockSpec((per_blk, None, n, tc, lanes), lambda c, q: (c, q, 0, 0, 0)),
        compiler_params=_cp("parallel", "arbitrary"),
        name="c_relayout",
    )(p, p)


def _rwkv_scan_kernel(x_ref, rk_ref, lnw_ref, lnb_ref, out_ref, s_ref, o_ref, *, tc, k_chunk):
    n = s_ref.shape[0]
    lanes = s_ref.shape[2]

    @pl.when(pl.program_id(0) == 0)
    def _():
        s_ref[...] = jnp.zeros_like(s_ref)

    def row(q, k, t):
        return x_ref[pl.ds((q * n + k) * tc + t, 1), :]

    def step(t, carry):
        vv = x_ref[pl.ds(3 * n * tc + t, n, stride=tc), :]

        def sa_chunk(c, sa):
            base = pl.multiple_of(c * k_chunk, k_chunk)
            for j in range(k_chunk):
                sa = sa + s_ref[base + j] * row(4, base + j, t)
            return sa

        sa = lax.fori_loop(0, n // k_chunk, sa_chunk, jnp.zeros((n, lanes), F32))

        def update_chunk(c, o):
            base = pl.multiple_of(c * k_chunk, k_chunk)
            for j in range(k_chunk):
                k = base + j
                sk = s_ref[k] + (sa * row(5, k, t) + vv * row(2, k, t))
                s_ref[k] = sk
                o = o + sk * row(0, k, t)
            return o

        o_ref[pl.ds(t, n, stride=tc), :] = lax.fori_loop(0, n // k_chunk, update_chunk, jnp.zeros((n, lanes), F32))
        return carry

    lax.fori_loop(0, tc, step, 0)
    for k in range(n):
        s_ref[k] = s_ref[k] * x_ref[(n + k) * tc + tc - 1:(n + k) * tc + tc, :]

    quantity = lambda q: x_ref[q * n * tc:(q + 1) * n * tc, :].reshape(n, tc, lanes)
    o = o_ref[...].reshape(n, tc, lanes)
    mean = jnp.mean(o, axis=0, keepdims=True)
    var = jnp.mean(jnp.square(o - mean), axis=0, keepdims=True)
    o = (o - mean) * lax.rsqrt(var + C_GN_EPS) * lnw_ref[...] + lnb_ref[...]
    bonus = jnp.sum(quantity(0) * quantity(2) * rk_ref[...], axis=0, keepdims=True)
    out_ref[...] = o + bonus * quantity(3)


def _rwkv_scan(xs, rk, lnw, lnb, k_chunk=32):
    nc, nq, n, tc, lanes = xs.shape
    tile = lambda: pl.BlockSpec((n, 1, lanes), lambda i: (0, 0, 0))
    return pl.pallas_call(
        functools.partial(_rwkv_scan_kernel, tc=tc, k_chunk=min(k_chunk, n)),
        out_shape=jax.ShapeDtypeStruct((n, nc * tc, lanes), F32),
        grid=(nc,),
        in_specs=[pl.BlockSpec((nq * n * tc, lanes), lambda i: (i, 0)), tile(), tile(), tile()],
        out_specs=pl.BlockSpec((n, tc, lanes), lambda i: (0, i, 0)),
        scratch_shapes=[pltpu.VMEM((n, n, lanes), F32), pltpu.VMEM((n * tc, lanes), F32)],
        compiler_params=_cp("arbitrary"),
        name="c_scan",
    )(xs.reshape(nc * nq * n * tc, lanes), rk, lnw, lnb)


def _rwkv_unlayout_kernel(o_ref, q_ref):
    for v in range(o_ref.shape[0]):
        rows = o_ref[v].T
        q_ref[:, v] = rows.reshape(q_ref.shape[0], q_ref.shape[2], q_ref.shape[3])


def _rwkv_unlayout(o, batch):
    n, seq, lanes = o.shape
    t_blk = min(RELAYOUT_T, seq)
    heads = lanes // (2 * batch)
    return pl.pallas_call(
        _rwkv_unlayout_kernel,
        out_shape=jax.ShapeDtypeStruct((2 * batch, n, heads, seq), F32),
        grid=(seq // t_blk,),
        in_specs=[pl.BlockSpec((n, t_blk, lanes), lambda c: (0, c, 0))],
        out_specs=pl.BlockSpec((2 * batch, n, heads, t_blk), lambda c: (0, 0, 0, c)),
        compiler_params=_cp("parallel"),
        name="c_unlayout",
    )(o)


def _rwkv_out_kernel(qf_ref, qb_ref, glo_ref, gup_ref, y_ref):
    t_blk = qf_ref.shape[-1]
    fwd = qf_ref[...].reshape(-1, t_blk)
    bwd = qb_ref[...].reshape(-1, t_blk)
    mirror = t_blk - 1 - lax.broadcasted_iota(jnp.int32, bwd.shape, 1)
    o = (fwd + jnp.take_along_axis(bwd, mirror, axis=1)).T
    g = jnp.dot(jax.nn.sigmoid(glo_ref[...]).astype(BF16), gup_ref[...], preferred_element_type=F32)
    y_ref[...] = (o * g).astype(y_ref.dtype)


def _rwkv_out(q, proj, g_up, batch, seq):
    _, n, heads, _ = q.shape
    t_blk = min(RELAYOUT_T, seq)
    nt = seq // t_blk
    g_blk = COL_CG // C_GATE_LORA
    return pl.pallas_call(
        _rwkv_out_kernel,
        out_shape=jax.ShapeDtypeStruct((batch * seq, C_WIDTH), BF16),
        grid=(batch, nt),
        in_specs=[
            pl.BlockSpec((None, n, heads, t_blk), lambda b, j: (b, 0, 0, j)),
            pl.BlockSpec((None, n, heads, t_blk), lambda b, j: (batch + b, 0, 0, nt - 1 - j)),
            pl.BlockSpec((t_blk, C_GATE_LORA), lambda b, j: (b * nt + j, g_blk)),
            pl.BlockSpec((C_GATE_LORA, C_WIDTH), lambda b, j: (0, 0)),
        ],
        out_specs=pl.BlockSpec((t_blk, C_WIDTH), lambda b, j: (b * nt + j, 0)),
        compiler_params=_cp("parallel", "arbitrary"),
        name="c_out",
    )(q, q, proj, g_up)


SCAN_TC = 16


def _heads_minor(p):
    return p.reshape(p.shape[:-1] + (C_HEADS, C_HEAD_DIM)).swapaxes(-1, -2).reshape(p.shape)


def _rwkv(proj, mu, w0, w_up, a0, a_up, g_up, k_k, k_a, r_k, ln_w, ln_b, batch, seq):
    pad_lo = C_LORA_PAD - C_LORA
    split = 3 * C_WIDTH
    mu_wide = jnp.concatenate([_heads_minor(mu[:, i * C_WIDTH:(i + 1) * C_WIDTH]) for i in range(3)], axis=-1)[:, None, :]
    mu_lo = jnp.concatenate([
        jnp.pad(mu[:, split:split + C_LORA], ((0, 0), (0, pad_lo))),
        jnp.pad(mu[:, split + C_LORA:], ((0, 0), (0, pad_lo))),
    ], axis=-1)[:, None, :]
    w_up_p = jnp.pad(_heads_minor(w_up), ((0, 0), (0, pad_lo), (0, 0)))
    a_up_p = jnp.pad(_heads_minor(a_up), ((0, 0), (0, pad_lo), (0, 0)))
    feats = _rwkv_prep(proj, mu_wide, mu_lo, _heads_minor(w0)[:, None, :], w_up_p, _heads_minor(a0)[:, None, :], a_up_p,
                       _heads_minor(k_k).reshape(1, C_WIDTH), _heads_minor(k_a).reshape(1, C_WIDTH), batch, seq, min(SCAN_TC, seq))
    xs = _rwkv_relayout(feats, min(SCAN_TC, seq))
    per_lane = lambda p: jnp.tile(p.reshape(C_HEADS, C_HEAD_DIM).T, (1, 2 * batch))[:, None, :]
    o = _rwkv_scan(xs, per_lane(r_k), per_lane(ln_w), per_lane(ln_b))
    return _rwkv_out(_rwkv_unlayout(o, batch), proj, _heads_minor(g_up).astype(BF16), batch, seq)


def _merge_kernel(h_ref, wg_ref, wbr_ref, wc_ref, oa_ref, ob_ref, oc_ref, od_ref, o_ref, wgb_ref, wbrb_ref, wcb_ref):
    _cast_weights_once((wg_ref, wbr_ref, wc_ref), (wgb_ref, wbrb_ref, wcb_ref))
    h = h_ref[...]
    acc = None
    row = 0
    for i, b_ref in enumerate((oa_ref, ob_ref, oc_ref, od_ref)):
        width = b_ref.shape[1]
        w = wcb_ref[...] if b_ref is oc_ref else wbrb_ref[row:row + width, :]
        gate = jax.nn.sigmoid(jnp.dot(h, wgb_ref[i], preferred_element_type=F32))
        term = gate * jnp.dot(b_ref[...], w, preferred_element_type=F32)
        acc = term if acc is None else acc + term
        row += width
    o_ref[...] = acc.astype(o_ref.dtype)


def _merge(h, w_gate, w_branch, layer, w_branch_c, branches, tm=512, tn=512):
    m = h.shape[0]
    tm = min(tm, m)
    d_mix = w_branch.shape[-2]
    row = lambda a: pl.BlockSpec((tm, a.shape[1]), lambda j, i: (i, 0))
    return pl.pallas_call(
        _merge_kernel,
        out_shape=jax.ShapeDtypeStruct((m, D_MODEL), BF16),
        grid=(D_MODEL // tn, m // tm),
        in_specs=[
            row(h),
            _weight_spec((4, D_MODEL, tn), lambda j, i: (0, 0, j), layer),
            _weight_spec((d_mix, tn), lambda j, i: (0, j), layer),
            _weight_spec((w_branch_c.shape[0], tn), lambda j, i: (0, j)),
        ] + [row(b) for b in branches],
        out_specs=pl.BlockSpec((tm, tn), lambda j, i: (i, j)),
        scratch_shapes=[pltpu.VMEM((4, D_MODEL, tn), BF16), pltpu.VMEM((d_mix, tn), BF16), pltpu.VMEM((w_branch_c.shape[0], tn), BF16)],
        compiler_params=_cp("parallel", "arbitrary"),
        name="gated_merge",
    )(h, w_gate, w_branch, w_branch_c, *branches)


def _ffn_up_kernel(h_ref, wg_ref, wu_ref, o_ref, wgb_ref, wub_ref):
    _cast_weights_once((wg_ref, wu_ref), (wgb_ref, wub_ref))
    h = h_ref[...]
    g = jnp.dot(h, wgb_ref[...], preferred_element_type=F32)
    u = jnp.dot(h, wub_ref[...], preferred_element_type=F32)
    o_ref[...] = (g * jax.nn.sigmoid(g) * u).astype(o_ref.dtype)


def _ffn_up(h, w_gate, w_up, layer, tm=1024, tn=512):
    m, k = h.shape
    n = w_gate.shape[-1]
    tm = min(tm, m)
    return pl.pallas_call(
        _ffn_up_kernel,
        out_shape=jax.ShapeDtypeStruct((m, n), BF16),
        grid=(n // tn, m // tm),
        in_specs=[
            pl.BlockSpec((tm, k), lambda j, i: (i, 0)),
            _weight_spec((k, tn), lambda j, i: (0, j), layer),
            _weight_spec((k, tn), lambda j, i: (0, j), layer),
        ],
        out_specs=pl.BlockSpec((tm, tn), lambda j, i: (i, j)),
        scratch_shapes=[pltpu.VMEM((k, tn), BF16), pltpu.VMEM((k, tn), BF16)],
        compiler_params=_cp("parallel", "arbitrary"),
        name="ffn_up",
    )(h, w_gate, w_up)


def _pad_w_in(w):
    pad = ((0, 0), (0, C_LORA_PAD - C_LORA))
    lo = COL_CLO
    cf = [_heads_minor(w[:, COL_CF + i * C_WIDTH:COL_CF + (i + 1) * C_WIDTH]) for i in range(3)]
    return jnp.concatenate([
        w[:, :COL_CF],
        *cf,
        jnp.pad(w[:, lo:lo + C_LORA], pad),
        jnp.pad(w[:, lo + C_LORA:lo + 2 * C_LORA], pad),
        w[:, lo + 2 * C_LORA:],
    ], axis=1)


def kernel(x, norm_mix, w_in, a_q_norm, a_k_norm, b_conv_w, b_conv_b, b_filt_w1, b_filt_b1, b_filt_w2, b_filt_b2, b_filt_w3, b_filt_b3, b_filt_w4, b_filt_freq, b_skip, c_mu, c_w0, c_w_up, c_a0, c_a_up, c_g_up, c_k_k, c_k_a, c_r_k, c_ln_w, c_ln_b, d_lq1, d_lk1, d_lq2, d_lk2, d_subln, w_gate, w_branch, w_out, norm_ffn, w_ff_gate, w_ff_up, w_ff_down, norm_final):
    batch, seq, _ = x.shape
    m = batch * seq
    cos, sin = _rope_tables(seq)
    cos_half, sin_half, cos_int, sin_int = _dft_tables(seq)
    filt_feats = _filter_features(seq)
    x = x.reshape(m, D_MODEL)
    for l in range(DEPTH):
        h = _rmsnorm(x, norm_mix[l], BF16)
        proj = _matmul(h, _pad_w_in(w_in[l]), F32, 1024, D_IN_PAD // 4, name="in_proj")

        gains = jnp.concatenate([
            jnp.broadcast_to(a_q_norm[l], (A_HEADS, A_HEAD_DIM)),
            jnp.broadcast_to(a_k_norm[l], (A_KV_HEADS, A_HEAD_DIM)),
        ])[:, None, :]
        o_a = _attn_a(_qk_prep(proj, gains, cos, sin, seq), proj, batch, seq)

        k_re, k_im = _hyena_filter_spectrum(seq, filt_feats, cos_int, sin_int, b_filt_w1[l], b_filt_b1[l], b_filt_w2[l],
                                            b_filt_b2[l], b_filt_w3[l], b_filt_b3[l], b_filt_w4[l], b_filt_freq[l])
        o_b = _hyena(proj, b_conv_w[l], b_conv_b[l], b_skip[l], cos_half, sin_half, k_re, k_im, batch, seq)

        o_c = _rwkv(proj, c_mu[l], c_w0[l], c_w_up[l], c_a0[l], c_a_up[l], c_g_up[l], c_k_k[l], c_k_a[l], c_r_k[l],
                    c_ln_w[l], c_ln_b[l], batch, seq)

        lam_init = 0.8 - 0.6 * math.exp(-0.3 * l)
        lam_vecs = jnp.stack([d_lq1[l], d_lk1[l], d_lq2[l], d_lk2[l]])
        o_d = _attn_d(proj, lam_vecs, d_subln[l], lam_init, batch, seq)

        c_lo, c_hi = A_WIDTH + B_WIDTH, A_WIDTH + B_WIDTH + C_WIDTH
        w_br_c = w_branch[l, c_lo:c_hi].reshape(C_HEADS, C_HEAD_DIM, D_MODEL).swapaxes(0, 1).reshape(C_WIDTH, D_MODEL)
        merged = _merge(h, w_gate, w_branch, l, w_br_c, (o_a, o_b, o_c, o_d))
        x, h2 = _matmul_res_norm(merged, w_out, l, x, norm_ffn[l])
        mid = _ffn_up(h2, w_ff_gate, w_ff_up, l)
        x = _matmul(mid, w_ff_down, F32, 1024, 512, residual=x, layer=l, name="ffn_down")
    return _rmsnorm(x, norm_final, F32).reshape(batch, seq, D_MODEL)
```

```python
import functools
import math

import jax
import jax.numpy as jnp
import numpy as np
from jax import lax
from jax.experimental import pallas as pl
from jax.experimental.pallas import tpu as pltpu

D_MODEL = 2048
DEPTH = 2
GRID_W = 64
NORM_EPS = 1e-6

A_HEADS = 8
A_KV_HEADS = 2
A_HEAD_DIM = 128
A_WIDTH = A_HEADS * A_HEAD_DIM
ROPE_THETA = 10000.0

B_WIDTH = 512
B_EMB_DIM = 33
B_FILTER_HIDDEN = 64
B_DECAY_TARGET = 1e-2
B_FAST_DECAY_PCT = 0.3
B_SLOW_DECAY_PCT = 1.5

C_HEADS = 8
C_HEAD_DIM = 64
C_WIDTH = C_HEADS * C_HEAD_DIM
C_LORA = 96
C_LORA_PAD = 128
C_GATE_LORA = 256
C_GN_EPS = 64e-5

D_HEADS = 4
D_HEAD_DIM = 64
D_V_DIM = 2 * D_HEAD_DIM
D_WIDTH = D_HEADS * D_V_DIM

FFN_HIDDEN = -(-8 * D_MODEL // (3 * 256)) * 256

COL_AQ = 0
COL_AK = COL_AQ + A_WIDTH
COL_AV = COL_AK + A_KV_HEADS * A_HEAD_DIM
COL_BU = COL_AV + A_KV_HEADS * A_HEAD_DIM
COL_CF = COL_BU + 3 * B_WIDTH
COL_CLO = COL_CF + 3 * C_WIDTH
COL_CG = COL_CLO + 2 * C_LORA_PAD
COL_DQ = COL_CG + C_GATE_LORA
COL_DK = COL_DQ + 2 * D_HEADS * D_HEAD_DIM
COL_DV = COL_DK + 2 * D_HEADS * D_HEAD_DIM
D_IN_PAD = COL_DV + D_WIDTH

VMEM_LIMIT_V7X = 56 * 1024 * 1024
F32 = jnp.float32
BF16 = jnp.bfloat16
HIGHEST = lax.Precision.HIGHEST
NT_DIMS = (((1,), (1,)), ((), ()))
LOG2_E = math.log2(math.e)


def _cp(*sem):
    return pltpu.CompilerParams(dimension_semantics=sem, vmem_limit_bytes=VMEM_LIMIT_V7X)


def _const_spec(shape):
    return pl.BlockSpec(shape, lambda *_: (0,) * len(shape), pipeline_mode=pl.Buffered(1))


SINGLE_BUFFER_BYTES = 8 * 1024 * 1024


def _weight_spec(shape, index_map, layer=None):
    mode = {"pipeline_mode": pl.Buffered(1)} if 4 * math.prod(shape) > SINGLE_BUFFER_BYTES else {}
    if layer is None:
        return pl.BlockSpec(shape, index_map, **mode)
    return pl.BlockSpec((None,) + tuple(shape), lambda *g: (layer,) + tuple(index_map(*g)), **mode)


def _rmsnorm_kernel(x_ref, g_ref, o_ref):
    x = x_ref[...]
    ms = jnp.mean(x * x, axis=-1, keepdims=True)
    o_ref[...] = (x * lax.rsqrt(ms + NORM_EPS) * g_ref[...]).astype(o_ref.dtype)


def _rmsnorm(x, g, out_dtype, tm=1024):
    m, d = x.shape
    tm = min(tm, m)
    return pl.pallas_call(
        _rmsnorm_kernel,
        out_shape=jax.ShapeDtypeStruct((m, d), out_dtype),
        grid=(m // tm,),
        in_specs=[pl.BlockSpec((tm, d), lambda i: (i, 0)), pl.BlockSpec((1, d), lambda i: (0, 0))],
        out_specs=pl.BlockSpec((tm, d), lambda i: (i, 0)),
        compiler_params=_cp("parallel"),
        name="rmsnorm",
    )(x, g.reshape(1, d))


def _mm_f32_kernel(a_ref, b_ref, o_ref):
    def split(x):
        hi = x.astype(BF16)
        return hi, (x - hi.astype(F32)).astype(BF16)

    (a_hi, a_lo), (b_hi, b_lo) = split(a_ref[...]), split(b_ref[...])
    dot = functools.partial(jnp.dot, preferred_element_type=F32)
    o_ref[...] = dot(a_hi, b_hi) + (dot(a_hi, b_lo) + dot(a_lo, b_hi))


def _matmul_f32(a, b, tm, tn, name):
    m, k = a.shape
    n = b.shape[1]
    tm, tn = min(tm, m), min(tn, n)
    return pl.pallas_call(
        _mm_f32_kernel,
        out_shape=jax.ShapeDtypeStruct((m, n), F32),
        grid=(m // tm, n // tn),
        in_specs=[pl.BlockSpec((tm, k), lambda i, j: (i, 0)), pl.BlockSpec((k, tn), lambda i, j: (0, j))],
        out_specs=pl.BlockSpec((tm, tn), lambda i, j: (i, j)),
        compiler_params=_cp("parallel", "arbitrary"),
        name=name,
    )(a, b)


def _cast_weights_once(w_refs, wb_refs):
    @pl.when(pl.program_id(1) == 0)
    def _():
        for w_ref, wb_ref in zip(w_refs, wb_refs):
            wb_ref[...] = w_ref[...].astype(BF16)


def _mm_kernel(a_ref, w_ref, o_ref, wb_ref):
    _cast_weights_once((w_ref,), (wb_ref,))
    o_ref[...] = jnp.dot(a_ref[...], wb_ref[...], preferred_element_type=F32).astype(o_ref.dtype)


def _mm_res_kernel(a_ref, w_ref, r_ref, o_ref, wb_ref):
    _cast_weights_once((w_ref,), (wb_ref,))
    o_ref[...] = r_ref[...] + jnp.dot(a_ref[...], wb_ref[...], preferred_element_type=F32)


def _matmul(a, w, out_dtype, tm, tn, residual=None, layer=None, name="matmul"):
    m, k = a.shape
    n = w.shape[-1]
    tm, tn = min(tm, m), min(tn, n)
    in_specs = [pl.BlockSpec((tm, k), lambda j, i: (i, 0)), _weight_spec((k, tn), lambda j, i: (0, j), layer)]
    args = [a, w]
    body = _mm_kernel
    if residual is not None:
        body = _mm_res_kernel
        in_specs.append(pl.BlockSpec((tm, tn), lambda j, i: (i, j)))
        args.append(residual)
    return pl.pallas_call(
        body,
        out_shape=jax.ShapeDtypeStruct((m, n), out_dtype),
        grid=(n // tn, m // tm),
        in_specs=in_specs,
        out_specs=pl.BlockSpec((tm, tn), lambda j, i: (i, j)),
        scratch_shapes=[pltpu.VMEM((k, tn), BF16)],
        compiler_params=_cp("parallel", "arbitrary"),
        name=name,
    )(*args)


def _mm_res_norm_kernel(a_ref, w_ref, r_ref, g_ref, x_ref, h_ref, wb_ref):
    @pl.when(pl.program_id(0) == 0)
    def _():
        wb_ref[...] = w_ref[...].astype(BF16)

    x = r_ref[...] + jnp.dot(a_ref[...], wb_ref[...], preferred_element_type=F32)
    x_ref[...] = x
    ms = jnp.mean(x * x, axis=-1, keepdims=True)
    h_ref[...] = (x * lax.rsqrt(ms + NORM_EPS) * g_ref[...]).astype(h_ref.dtype)


def _matmul_res_norm(a, w, layer, residual, gain, tm=512):
    m, k = a.shape
    n = w.shape[-1]
    tm = min(tm, m)
    rows = lambda width: pl.BlockSpec((tm, width), lambda i: (i, 0))
    return pl.pallas_call(
        _mm_res_norm_kernel,
        out_shape=[jax.ShapeDtypeStruct((m, n), F32), jax.ShapeDtypeStruct((m, n), BF16)],
        grid=(m // tm,),
        in_specs=[rows(k), _weight_spec((k, n), lambda i: (0, 0), layer), rows(n), pl.BlockSpec((1, n), lambda i: (0, 0))],
        out_specs=[rows(n), rows(n)],
        scratch_shapes=[pltpu.VMEM((k, n), BF16)],
        compiler_params=_cp("arbitrary"),
        name="out_proj_norm",
    )(a, w, residual, gain.reshape(1, n))


def _rope_tables(seq):
    rows = seq // GRID_W
    row_idx = jnp.repeat(jnp.arange(rows, dtype=F32), GRID_W)
    col_idx = jnp.tile(jnp.arange(GRID_W, dtype=F32), rows)
    axis_dim = A_HEAD_DIM // 2
    inv_freq = ROPE_THETA ** (-jnp.arange(0, axis_dim, 2, dtype=F32) / axis_dim)
    ang_r = row_idx[:, None] * inv_freq[None, :]
    ang_c = col_idx[:, None] * inv_freq[None, :]
    ang = jnp.concatenate([ang_r, ang_r, ang_c, ang_c], axis=-1)
    return jnp.cos(ang), jnp.sin(ang)


def _qk_prep_kernel(x_ref, g_ref, cos_ref, sin_ref, o_ref):
    cos = cos_ref[...]
    sin = sin_ref[...]
    lane = lax.broadcasted_iota(jnp.int32, cos.shape, 1)
    quarter = A_HEAD_DIM // 4
    first = (lane % (2 * quarter)) < quarter
    for h in range(A_HEADS + A_KV_HEADS):
        cols = slice(h * A_HEAD_DIM, (h + 1) * A_HEAD_DIM)
        x = x_ref[:, cols]
        xn = x * lax.rsqrt(jnp.mean(x * x, axis=-1, keepdims=True) + NORM_EPS) * g_ref[h]
        rot = jnp.where(first, -pltpu.roll(xn, A_HEAD_DIM - quarter, 1), pltpu.roll(xn, quarter, 1))
        y = xn * cos + rot * sin
        if h < A_HEADS:
            y = y * (A_HEAD_DIM**-0.5 * LOG2_E)
        o_ref[:, cols] = y.astype(o_ref.dtype)


def _qk_prep(proj, gains, cos, sin, seq, tm=512):
    m = proj.shape[0]
    tm = min(tm, seq)
    width = (A_HEADS + A_KV_HEADS) * A_HEAD_DIM
    nt = seq // tm
    return pl.pallas_call(
        _qk_prep_kernel,
        out_shape=jax.ShapeDtypeStruct((m, width), BF16),
        grid=(m // tm,),
        in_specs=[
            pl.BlockSpec((tm, width), lambda i: (i, 0)),
            pl.BlockSpec(gains.shape, lambda i: (0, 0, 0)),
            pl.BlockSpec((tm, A_HEAD_DIM), lambda i: (i % nt, 0)),
            pl.BlockSpec((tm, A_HEAD_DIM), lambda i: (i % nt, 0)),
        ],
        out_specs=pl.BlockSpec((tm, width), lambda i: (i, 0)),
        compiler_params=_cp("parallel"),
        name="a_qk_prep",
    )(proj, gains, cos, sin)


class _OnlineSoftmax:
    def __init__(self, rows, width):
        self.m = jnp.full((rows, 1), -jnp.inf, F32)
        self.l = jnp.zeros((rows, 1), F32)
        self.acc = jnp.zeros((rows, width), F32)

    def add(self, s, v):
        m_new = jnp.maximum(self.m, jnp.max(s, axis=-1, keepdims=True))
        p = jnp.exp2(s - m_new)
        alpha = jnp.exp2(self.m - m_new)
        self.l = alpha * self.l + jnp.sum(p, axis=-1, keepdims=True)
        self.acc = alpha * self.acc + jnp.dot(p.astype(BF16), v, preferred_element_type=F32)
        self.m = m_new

    def result(self):
        return self.acc / self.l


def _attn_a_kernel(q_ref, k_ref, v_ref, o_ref, *, kc):
    q = q_ref[...]
    sm = _OnlineSoftmax(q.shape[0], A_HEAD_DIM)
    for c in range(k_ref.shape[0] // kc):
        keys = slice(c * kc, (c + 1) * kc)
        s = lax.dot_general(q, k_ref[keys, :], NT_DIMS, preferred_element_type=F32)
        sm.add(s, v_ref[keys, :].astype(BF16))
    o_ref[...] = sm.result().astype(o_ref.dtype)


def _attn_a(qk, proj, batch, seq, tq=1024, kc=1024):
    tq = min(tq, seq)
    kc = min(kc, seq)
    nq = seq // tq
    group = A_HEADS // A_KV_HEADS
    k_blk = COL_AK // A_HEAD_DIM
    v_blk = COL_AV // A_HEAD_DIM
    return pl.pallas_call(
        functools.partial(_attn_a_kernel, kc=kc),
        out_shape=jax.ShapeDtypeStruct((batch * seq, A_WIDTH), BF16),
        grid=(batch, A_HEADS, nq),
        in_specs=[
            pl.BlockSpec((tq, A_HEAD_DIM), lambda b, h, i: (b * nq + i, h)),
            pl.BlockSpec((seq, A_HEAD_DIM), lambda b, h, i: (b, k_blk + h // group)),
            pl.BlockSpec((seq, A_HEAD_DIM), lambda b, h, i: (b, v_blk + h // group)),
        ],
        out_specs=pl.BlockSpec((tq, A_HEAD_DIM), lambda b, h, i: (b * nq + i, h)),
        compiler_params=_cp("parallel", "arbitrary", "arbitrary"),
        name="a_attention",
    )(qk, qk, proj)


def _attn_d_kernel(q_ref, k_ref, v_ref, slope_ref, lam_ref, g_ref, o_ref, *, tq, kc, lam_init):
    q = q_ref[...] * (D_HEAD_DIM**-0.5 * LOG2_E)
    lane = lax.broadcasted_iota(jnp.int32, q.shape, 1)
    q_maps = (jnp.where(lane < D_HEAD_DIM, q, 0.0).astype(BF16), jnp.where(lane >= D_HEAD_DIM, q, 0.0).astype(BF16))
    slope = slope_ref[0][:, 0:1] * LOG2_E
    rel = pl.program_id(2) * tq + lax.broadcasted_iota(jnp.int32, (tq, kc), 0) - lax.broadcasted_iota(jnp.int32, (tq, kc), 1)
    rel = slope * rel.astype(F32)
    maps = (_OnlineSoftmax(tq, D_V_DIM), _OnlineSoftmax(tq, D_V_DIM))
    for c in range(k_ref.shape[0] // kc):
        keys = slice(c * kc, (c + 1) * kc)
        k = k_ref[keys, :].astype(BF16)
        v = v_ref[keys, :].astype(BF16)
        bias = jnp.abs(rel - slope * float(c * kc))
        for q_map, sm in zip(q_maps, maps):
            sm.add(lax.dot_general(q_map, k, NT_DIMS, preferred_element_type=F32) - bias, v)
    lam_v = lam_ref[...]
    lam = (
        jnp.exp(jnp.sum(lam_v[0:1] * lam_v[1:2], axis=-1, keepdims=True))
        - jnp.exp(jnp.sum(lam_v[2:3] * lam_v[3:4], axis=-1, keepdims=True))
        + lam_init
    )
    o = maps[0].result() - lam * maps[1].result()
    o = o * lax.rsqrt(jnp.mean(o * o, axis=-1, keepdims=True) + NORM_EPS) * g_ref[...]
    o_ref[...] = (o * (1.0 - lam_init)).astype(o_ref.dtype)


def _attn_d(proj, lam_vecs, subln, lam_init, batch, seq, tq=1024, kc=512):
    tq = min(tq, seq)
    kc = min(kc, seq)
    nq = seq // tq
    slopes = 2.0 ** (-8.0 * np.arange(1, D_HEADS + 1, dtype=np.float32) / D_HEADS)
    slopes = jnp.asarray(np.broadcast_to(slopes[:, None, None], (D_HEADS, 1, 128)).astype(np.float32))
    q_blk, k_blk, v_blk = COL_DQ // D_V_DIM, COL_DK // D_V_DIM, COL_DV // D_V_DIM
    return pl.pallas_call(
        functools.partial(_attn_d_kernel, tq=tq, kc=kc, lam_init=lam_init),
        out_shape=jax.ShapeDtypeStruct((batch * seq, D_WIDTH), BF16),
        grid=(batch, D_HEADS, nq),
        in_specs=[
            pl.BlockSpec((tq, D_V_DIM), lambda b, h, i: (b * nq + i, q_blk + h)),
            pl.BlockSpec((seq, D_V_DIM), lambda b, h, i: (b, k_blk + h)),
            pl.BlockSpec((seq, D_V_DIM), lambda b, h, i: (b, v_blk + h)),
            pl.BlockSpec((1, 1, 128), lambda b, h, i: (h, 0, 0)),
            pl.BlockSpec((4, D_HEAD_DIM), lambda b, h, i: (0, 0)),
            pl.BlockSpec((1, D_V_DIM), lambda b, h, i: (0, 0)),
        ],
        out_specs=pl.BlockSpec((tq, D_V_DIM), lambda b, h, i: (b * nq + i, h)),
        compiler_params=_cp("parallel", "arbitrary", "arbitrary"),
        name="d_attention",
    )(proj, proj, proj, slopes, lam_vecs, subln.reshape(1, D_V_DIM))


DFT_ROW_SPLIT = 64


def _dft_tables(seq):
    n = 2 * seq
    blk = min(DFT_ROW_SPLIT, seq)
    col = jnp.arange(seq, dtype=jnp.int32)[None, :]
    hi = jnp.arange(seq // blk, dtype=jnp.int32)[:, None]
    lo = jnp.arange(blk, dtype=jnp.int32)[:, None]

    def cos_sin(index, period):
        ang = (index % period).astype(F32) * (2.0 * math.pi / period)
        return jnp.cos(ang), jnp.sin(ang)

    def tables(col_term, period):
        (ca, sa), (cb, sb) = cos_sin(2 * blk * hi * col_term, period), cos_sin((2 * lo + 1) * col_term, period)
        ca, sa, cb, sb = ca[:, None, :], sa[:, None, :], cb[None], sb[None]
        return (ca * cb - sa * sb).reshape(seq, seq), (sa * cb + ca * sb).reshape(seq, seq)

    cos_half, sin_half = tables(2 * col + 1, 4 * n)
    cos_int, sin_int = tables(col, 2 * n)
    return cos_half.astype(BF16), sin_half.astype(BF16), cos_int, sin_int


def _filter_features(seq):
    t = jnp.linspace(0.0, 1.0, seq, dtype=F32)[:, None]
    n_bands = (B_EMB_DIM - 1) // 2
    bands = jnp.linspace(1e-4, n_bands - 1, n_bands, dtype=F32)[None, :]
    ang = (2.0 * math.pi / seq) * jnp.arange(seq, dtype=F32)[:, None] * bands
    z = jnp.concatenate([t, jnp.cos(ang), -jnp.sin(ang)], axis=-1)
    z = jnp.pad(z, ((0, 0), (0, B_FILTER_HIDDEN - B_EMB_DIM)))
    max_decay = math.log(B_DECAY_TARGET) / B_FAST_DECAY_PCT
    min_decay = math.log(B_DECAY_TARGET) / B_SLOW_DECAY_PCT
    deltas = jnp.abs(jnp.linspace(min_decay, max_decay, B_WIDTH, dtype=F32))[None, :]
    return z, t, deltas


def _filter_kernel(z_ref, t_ref, dl_ref, w1_ref, b1_ref, w2_ref, b2_ref, w3_ref, b3_ref, w4_ref, fr_ref, hs_ref, hd_ref):
    fr = fr_ref[...]
    hid = jnp.sin(fr * (jnp.dot(z_ref[...], w1_ref[...], preferred_element_type=F32, precision=HIGHEST) + b1_ref[...]))
    hid = jnp.sin(fr * (jnp.dot(hid, w2_ref[...], preferred_element_type=F32, precision=HIGHEST) + b2_ref[...]))
    hid = jnp.sin(fr * (jnp.dot(hid, w3_ref[...], preferred_element_type=F32, precision=HIGHEST) + b3_ref[...]))
    h = jnp.dot(hid, w4_ref[...], preferred_element_type=F32, precision=HIGHEST)
    window = jnp.exp(-t_ref[...] * dl_ref[...])
    h_fwd = h[:, :B_WIDTH] * window
    h_bwd = h[:, B_WIDTH:] * window
    row = lax.broadcasted_iota(jnp.int32, h_bwd.shape, 0)
    h_bwd = jnp.where(row == 0, 0.0, h_bwd)
    norm = jnp.sum(jnp.abs(h_fwd), axis=0, keepdims=True) + jnp.sum(jnp.abs(h_bwd), axis=0, keepdims=True)
    seq = h.shape[0]
    inv_n = 1.0 / seq
    hs_ref[...] = (h_fwd + h_bwd) / norm * inv_n
    hd_ref[...] = (h_bwd - h_fwd) / norm * inv_n


def _hyena_filter_spectrum(seq, feats, cos_int, sin_int, w1, b1, w2, b2, w3, b3, w4, freq):
    z, t, deltas = feats
    w1p = jnp.pad(w1, ((0, B_FILTER_HIDDEN - B_EMB_DIM), (0, 0)))
    row = lambda v: v.reshape(1, -1)
    args = (z, t, deltas, w1p, row(b1), w2, row(b2), w3, row(b3), w4, row(freq))
    h_sum, h_diff = pl.pallas_call(
        _filter_kernel,
        out_shape=[jax.ShapeDtypeStruct((seq, B_WIDTH), F32)] * 2,
        in_specs=[pl.BlockSpec(a.shape, lambda: (0, 0)) for a in args],
        out_specs=[pl.BlockSpec((seq, B_WIDTH), lambda: (0, 0))] * 2,
        compiler_params=pltpu.CompilerParams(vmem_limit_bytes=VMEM_LIMIT_V7X),
        name="b_filter",
    )(*args)
    k_re = _matmul_f32(cos_int, h_sum, 256, B_WIDTH, name="b_filter_dft_re")
    k_im = _matmul_f32(sin_int, h_diff, 256, B_WIDTH, name="b_filter_dft_im")
    return k_re, k_im


def _hyena_kernel(v_ref, x1_ref, x0_ref, wv_ref, w1_ref, w0_ref, bv_ref, b1_ref, b0_ref, skip_ref, c_ref, s_ref, kre_ref, kim_ref, o_ref,
                  *, f_blk):
    seq = v_ref.shape[0]
    row = lax.broadcasted_iota(jnp.int32, v_ref.shape, 0)

    def conv3(u_ref, w_ref, b_ref):
        u = u_ref[...]
        w = w_ref[...]
        u_prev = jnp.where(row == 0, 0.0, pltpu.roll(u, 1, 0))
        u_next = jnp.where(row == seq - 1, 0.0, pltpu.roll(u, seq - 1, 0))
        return w[0:1] * u_prev + w[1:2] * u + w[2:3] * u_next + b_ref[...]

    z = conv3(v_ref, wv_ref, bv_ref) * conv3(x1_ref, w1_ref, b1_ref)
    zb = z.astype(BF16)
    y = z * skip_ref[...]
    for f0 in range(0, seq, f_blk):
        fs = slice(f0, f0 + f_blk)
        cz = jnp.dot(c_ref[fs, :], zb, preferred_element_type=F32)
        sz = jnp.dot(s_ref[fs, :], zb, preferred_element_type=F32)
        k_re = kre_ref[fs, :]
        k_im = kim_ref[fs, :]
        y_re = (cz * k_re + sz * k_im).astype(BF16)
        y_im = (cz * k_im - sz * k_re).astype(BF16)
        y = y + (jnp.dot(c_ref[:, fs], y_re, preferred_element_type=F32) - jnp.dot(s_ref[:, fs], y_im, preferred_element_type=F32))
    o_ref[...] = (y * conv3(x0_ref, w0_ref, b0_ref)).astype(o_ref.dtype)


def _hyena(proj, conv_w, conv_b, skip, cos_half, sin_half, k_re, k_im, batch, seq, cb=256, f_blk=1024):
    ncb = B_WIDTH // cb
    u_blk = COL_BU // cb
    f_blk = min(f_blk, seq)

    def u_spec(part):
        return pl.BlockSpec((seq, cb), lambda j, b: (b, u_blk + part * ncb + j))

    def w_spec(rows, part):
        return pl.BlockSpec((rows, cb), lambda j, b: (0, part * ncb + j))

    k_spec = pl.BlockSpec((seq, cb), lambda j, b: (0, j), pipeline_mode=pl.Buffered(1))
    return pl.pallas_call(
        functools.partial(_hyena_kernel, f_blk=f_blk),
        out_shape=jax.ShapeDtypeStruct((batch * seq, B_WIDTH), BF16),
        grid=(ncb, batch),
        in_specs=[
            u_spec(0), u_spec(1), u_spec(2),
            w_spec(3, 0), w_spec(3, 1), w_spec(3, 2),
            w_spec(1, 0), w_spec(1, 1), w_spec(1, 2),
            pl.BlockSpec((1, cb), lambda j, b: (0, j)),
            _const_spec((seq, seq)), _const_spec((seq, seq)),
            k_spec, k_spec,
        ],
        out_specs=pl.BlockSpec((seq, cb), lambda j, b: (b, j)),
        compiler_params=_cp("parallel", "arbitrary"),
        name="b_hyena",
    )(proj, proj, proj, conv_w, conv_w, conv_w, conv_b.reshape(1, -1), conv_b.reshape(1, -1), conv_b.reshape(1, -1),
      skip.reshape(1, -1), cos_half, sin_half, k_re, k_im)


def _rwkv_prep_kernel(cur_ref, lo_ref, pcur_ref, plo_ref, ncur_ref, nlo_ref, mu_ref, mulo_ref, w0_ref, wup_ref, a0_ref, aup_ref,
                      kk_ref, ka_ref, o_ref, *, chunk):
    first = pl.program_id(1) == 0
    last = pl.program_id(1) == pl.num_programs(1) - 1
    halo = pcur_ref.shape[0]
    tm = cur_ref.shape[0]
    row_in_chunk = lax.broadcasted_iota(jnp.int32, (tm, C_WIDTH), 0) % chunk

    def neighbours(x_ref, p_ref, n_ref):
        x = x_ref[...]
        row = lax.broadcasted_iota(jnp.int32, x.shape, 0)
        p_row = jnp.where(first, 0.0, p_ref[halo - 1:halo, :])
        n_row = jnp.where(last, 0.0, n_ref[0:1, :])
        prev = jnp.where(row == 0, p_row, pltpu.roll(x, 1, 0))
        nxt = jnp.where(row == tm - 1, n_row, pltpu.roll(x, tm - 1, 0))
        return x, (prev, nxt)

    cur, cur_sh = neighbours(cur_ref, pcur_ref, ncur_ref)
    lo, lo_sh = neighbours(lo_ref, plo_ref, nlo_ref)
    k_k = kk_ref[...]
    k_a = ka_ref[...]
    for d in range(2):
        f = cur + (cur_sh[d] - cur) * mu_ref[d]
        f_lo = lo + (lo_sh[d] - lo) * mulo_ref[d]
        r = f[:, :C_WIDTH]
        k = f[:, C_WIDTH:2 * C_WIDTH]
        v = f[:, 2 * C_WIDTH:]
        w_lo = f_lo[:, :C_LORA_PAD]
        a_lo = f_lo[:, C_LORA_PAD:]
        x = w0_ref[d] + jnp.dot(jnp.tanh(w_lo), wup_ref[d], preferred_element_type=F32, precision=HIGHEST)
        w = -(jnp.maximum(-x, 0.0) + jnp.log(1.0 + jnp.exp(-jnp.abs(x)))) - 0.5
        neg_log_decay = jnp.exp(w)
        log_g = neg_log_decay
        shift = 1
        while shift < chunk:
            if d == 0:
                moved = jnp.where(row_in_chunk >= shift, pltpu.roll(log_g, shift, 0), 0.0)
            else:
                moved = jnp.where(row_in_chunk < chunk - shift, pltpu.roll(log_g, tm - shift, 0), 0.0)
            log_g = log_g + moved
            shift *= 2
        g = jnp.exp(-log_g)
        g_inv = jnp.exp(log_g)
        g_before = jnp.exp(neg_log_decay - log_g)
        a = jax.nn.sigmoid(a0_ref[d] + jnp.dot(a_lo, aup_ref[d], preferred_element_type=F32, precision=HIGHEST))
        transposed = lambda val: val.T.reshape(C_HEAD_DIM, C_HEADS, tm)
        kk = transposed(k * k_k)
        kk = kk / jnp.maximum(jnp.sqrt(jnp.sum(kk * kk, axis=0, keepdims=True)), 1e-12)
        for q, val in enumerate((r * g, g, k * (1.0 + (a - 1.0) * k_a) * g_inv, v)):
            o_ref[d, q] = transposed(val)
        o_ref[d, 4] = -kk * transposed(g_before)
        o_ref[d, 5] = kk * transposed(a * g_inv)


def _rwkv_prep(proj, mu, mu_lo, w0, w_up, a0, a_up, k_k, k_a, batch, seq, chunk, tm=256):
    tm = min(tm, seq)
    nt = seq // tm
    halo = 8
    hb = tm // halo
    n_hblk = batch * seq // halo
    wide, narrow = 3 * C_WIDTH, 2 * C_LORA_PAD
    cur_blk, lo_blk = COL_CF // wide, COL_CLO // narrow

    def cur_map(blk):
        return lambda b, j: (b * nt + j, blk)

    def prev_map(blk):
        return lambda b, j: (jnp.maximum((b * nt + j) * hb - 1, 0), blk)

    def next_map(blk):
        return lambda b, j: (jnp.minimum((b * nt + j + 1) * hb, n_hblk - 1), blk)

    small = lambda a: pl.BlockSpec(a.shape, lambda b, j: (0,) * a.ndim)
    consts = (mu, mu_lo, w0, w_up, a0, a_up, k_k, k_a)
    return pl.pallas_call(
        functools.partial(_rwkv_prep_kernel, chunk=chunk),
        out_shape=jax.ShapeDtypeStruct((2, 6, C_HEAD_DIM, batch, C_HEADS, seq), F32),
        grid=(batch, nt),
        in_specs=[
            pl.BlockSpec((tm, wide), cur_map(cur_blk)),
            pl.BlockSpec((tm, narrow), cur_map(lo_blk)),
            pl.BlockSpec((halo, wide), prev_map(cur_blk)),
            pl.BlockSpec((halo, narrow), prev_map(lo_blk)),
            pl.BlockSpec((halo, wide), next_map(cur_blk)),
            pl.BlockSpec((halo, narrow), next_map(lo_blk)),
        ] + [small(a) for a in consts],
        out_specs=pl.BlockSpec((2, 6, C_HEAD_DIM, None, C_HEADS, tm), lambda b, j: (0, 0, 0, b, 0, j)),
        compiler_params=_cp("parallel", "arbitrary"),
        name="c_prep",
    )(proj, proj, proj, proj, proj, proj, *consts)


RELAYOUT_T = 128


def _rwkv_relayout_kernel(pf_ref, pb_ref, o_ref, *, tc):
    n = pf_ref.shape[0]
    t_blk = pf_ref.shape[-1]
    mirror = t_blk - 1 - lax.broadcasted_iota(jnp.int32, (pb_ref.shape[1] * pb_ref.shape[2], t_blk), 1)
    for k in range(n):
        bwd = jnp.take_along_axis(pb_ref[k].reshape(-1, t_blk), mirror, axis=1)
        cols = jnp.concatenate([pf_ref[k].reshape(-1, t_blk), bwd], axis=0).T
        for c in range(t_blk // tc):
            o_ref[c, k] = cols[c * tc:(c + 1) * tc]


def _rwkv_relayout(p, tc):
    _, nq, n, batch, heads, seq = p.shape
    t_blk = min(RELAYOUT_T, seq)
    nb = seq // t_blk
    lanes = 2 * batch * heads
    per_blk = t_blk // tc
    return pl.pallas_call(
        functools.partial(_rwkv_relayout_kernel, tc=tc),
        out_shape=jax.ShapeDtypeStruct((seq // tc, nq, n, tc, lanes), F32),
        grid=(nb, nq),
        in_specs=[
            pl.BlockSpec((None, None, n, batch, heads, t_blk), lambda c, q: (0, q, 0, 0, 0, c)),
            pl.BlockSpec((None, None, n, batch, heads, t_blk), lambda c, q: (1, q, 0, 0, 0, nb - 1 - c)),
        ],
        out_specs=pl.BlockSpec((per_blk, None, n, tc, lanes), lambda c, q: (c, q, 0, 0, 0)),
        compiler_params=_cp("parallel", "arbitrary"),
        name="c_relayout",
    )(p, p)


def _rwkv_scan_kernel(x_ref, rk_ref, lnw_ref, lnb_ref, out_ref, s_ref, o_ref, *, tc, k_chunk):
    n = s_ref.shape[0]
    lanes = s_ref.shape[2]

    @pl.when(pl.program_id(0) == 0)
    def _():
        s_ref[...] = jnp.zeros_like(s_ref)

    def row(q, k, t):
        return x_ref[pl.ds((q * n + k) * tc + t, 1), :]

    def step(t, carry):
        vv = x_ref[pl.ds(3 * n * tc + t, n, stride=tc), :]

        def sa_chunk(c, sa):
            base = pl.multiple_of(c * k_chunk, k_chunk)
            for j in range(k_chunk):
                sa = sa + s_ref[base + j] * row(4, base + j, t)
            return sa

        sa = lax.fori_loop(0, n // k_chunk, sa_chunk, jnp.zeros((n, lanes), F32))

        def update_chunk(c, o):
            base = pl.multiple_of(c * k_chunk, k_chunk)
            for j in range(k_chunk):
                k = base + j
                sk = s_ref[k] + (sa * row(5, k, t) + vv * row(2, k, t))
                s_ref[k] = sk
                o = o + sk * row(0, k, t)
            return o

        o_ref[pl.ds(t, n, stride=tc), :] = lax.fori_loop(0, n // k_chunk, update_chunk, jnp.zeros((n, lanes), F32))
        return carry

    lax.fori_loop(0, tc, step, 0)
    for k in range(n):
        s_ref[k] = s_ref[k] * x_ref[(n + k) * tc + tc - 1:(n + k) * tc + tc, :]

    quantity = lambda q: x_ref[q * n * tc:(q + 1) * n * tc, :].reshape(n, tc, lanes)
    o = o_ref[...].reshape(n, tc, lanes)
    mean = jnp.mean(o, axis=0, keepdims=True)
    var = jnp.mean(jnp.square(o - mean), axis=0, keepdims=True)
    o = (o - mean) * lax.rsqrt(var + C_GN_EPS) * lnw_ref[...] + lnb_ref[...]
    bonus = jnp.sum(quantity(0) * quantity(2) * rk_ref[...], axis=0, keepdims=True)
    out_ref[...] = o + bonus * quantity(3)


def _rwkv_scan(xs, rk, lnw, lnb, k_chunk=32):
    nc, nq, n, tc, lanes = xs.shape
    tile = lambda: pl.BlockSpec((n, 1, lanes), lambda i: (0, 0, 0))
    return pl.pallas_call(
        functools.partial(_rwkv_scan_kernel, tc=tc, k_chunk=min(k_chunk, n)),
        out_shape=jax.ShapeDtypeStruct((n, nc * tc, lanes), F32),
        grid=(nc,),
        in_specs=[pl.BlockSpec((nq * n * tc, lanes), lambda i: (i, 0)), tile(), tile(), tile()],
        out_specs=pl.BlockSpec((n, tc, lanes), lambda i: (0, i, 0)),
        scratch_shapes=[pltpu.VMEM((n, n, lanes), F32), pltpu.VMEM((n * tc, lanes), F32)],
        compiler_params=_cp("arbitrary"),
        name="c_scan",
    )(xs.reshape(nc * nq * n * tc, lanes), rk, lnw, lnb)


def _rwkv_unlayout_kernel(o_ref, q_ref):
    for v in range(o_ref.shape[0]):
        rows = o_ref[v].T
        q_ref[:, v] = rows.reshape(q_ref.shape[0], q_ref.shape[2], q_ref.shape[3])


def _rwkv_unlayout(o, batch):
    n, seq, lanes = o.shape
    t_blk = min(RELAYOUT_T, seq)
    heads = lanes // (2 * batch)
    return pl.pallas_call(
        _rwkv_unlayout_kernel,
        out_shape=jax.ShapeDtypeStruct((2 * batch, n, heads, seq), F32),
        grid=(seq // t_blk,),
        in_specs=[pl.BlockSpec((n, t_blk, lanes), lambda c: (0, c, 0))],
        out_specs=pl.BlockSpec((2 * batch, n, heads, t_blk), lambda c: (0, 0, 0, c)),
        compiler_params=_cp("parallel"),
        name="c_unlayout",
    )(o)


def _rwkv_out_kernel(qf_ref, qb_ref, glo_ref, gup_ref, y_ref):
    t_blk = qf_ref.shape[-1]
    lane_tile = min(RELAYOUT_T, t_blk)
    tiles = t_blk // lane_tile
    mirror = lane_tile - 1 - lax.broadcasted_iota(jnp.int32, (qf_ref.shape[0] * qf_ref.shape[1], lane_tile), 1)
    g = jnp.dot(jax.nn.sigmoid(glo_ref[...]).astype(BF16), gup_ref[...], preferred_element_type=F32)
    for i in range(tiles):
        cols = slice(i * lane_tile, (i + 1) * lane_tile)
        back = slice((tiles - 1 - i) * lane_tile, (tiles - i) * lane_tile)
        fwd = qf_ref[:, :, cols].reshape(-1, lane_tile)
        bwd = qb_ref[:, :, back].reshape(-1, lane_tile)
        o = (fwd + jnp.take_along_axis(bwd, mirror, axis=1)).T
        y_ref[cols, :] = (o * g[cols, :]).astype(y_ref.dtype)


OUT_T = 256


def _rwkv_out(q, proj, g_up, batch, seq):
    _, n, heads, _ = q.shape
    t_blk = min(OUT_T, seq)
    nt = seq // t_blk
    g_blk = COL_CG // C_GATE_LORA
    return pl.pallas_call(
        _rwkv_out_kernel,
        out_shape=jax.ShapeDtypeStruct((batch * seq, C_WIDTH), BF16),
        grid=(batch, nt),
        in_specs=[
            pl.BlockSpec((None, n, heads, t_blk), lambda b, j: (b, 0, 0, j)),
            pl.BlockSpec((None, n, heads, t_blk), lambda b, j: (batch + b, 0, 0, nt - 1 - j)),
            pl.BlockSpec((t_blk, C_GATE_LORA), lambda b, j: (b * nt + j, g_blk)),
            pl.BlockSpec((C_GATE_LORA, C_WIDTH), lambda b, j: (0, 0)),
        ],
        out_specs=pl.BlockSpec((t_blk, C_WIDTH), lambda b, j: (b * nt + j, 0)),
        compiler_params=_cp("parallel", "arbitrary"),
        name="c_out",
    )(q, q, proj, g_up)


SCAN_TC = 16


def _heads_minor(p):
    return p.reshape(p.shape[:-1] + (C_HEADS, C_HEAD_DIM)).swapaxes(-1, -2).reshape(p.shape)


def _rwkv(proj, mu, w0, w_up, a0, a_up, g_up, k_k, k_a, r_k, ln_w, ln_b, batch, seq):
    pad_lo = C_LORA_PAD - C_LORA
    split = 3 * C_WIDTH
    mu_wide = jnp.concatenate([_heads_minor(mu[:, i * C_WIDTH:(i + 1) * C_WIDTH]) for i in range(3)], axis=-1)[:, None, :]
    mu_lo = jnp.concatenate([
        jnp.pad(mu[:, split:split + C_LORA], ((0, 0), (0, pad_lo))),
        jnp.pad(mu[:, split + C_LORA:], ((0, 0), (0, pad_lo))),
    ], axis=-1)[:, None, :]
    w_up_p = jnp.pad(_heads_minor(w_up), ((0, 0), (0, pad_lo), (0, 0)))
    a_up_p = jnp.pad(_heads_minor(a_up), ((0, 0), (0, pad_lo), (0, 0)))
    feats = _rwkv_prep(proj, mu_wide, mu_lo, _heads_minor(w0)[:, None, :], w_up_p, _heads_minor(a0)[:, None, :], a_up_p,
                       _heads_minor(k_k).reshape(1, C_WIDTH), _heads_minor(k_a).reshape(1, C_WIDTH), batch, seq, min(SCAN_TC, seq))
    xs = _rwkv_relayout(feats, min(SCAN_TC, seq))
    per_lane = lambda p: jnp.tile(p.reshape(C_HEADS, C_HEAD_DIM).T, (1, 2 * batch))[:, None, :]
    o = _rwkv_scan(xs, per_lane(r_k), per_lane(ln_w), per_lane(ln_b))
    return _rwkv_out(_rwkv_unlayout(o, batch), proj, _heads_minor(g_up).astype(BF16), batch, seq)


def _merge_kernel(h_ref, wg_ref, wbr_ref, wc_ref, oa_ref, ob_ref, oc_ref, od_ref, o_ref, wgb_ref, wbrb_ref, wcb_ref):
    _cast_weights_once((wg_ref, wbr_ref, wc_ref), (wgb_ref, wbrb_ref, wcb_ref))
    h = h_ref[...]
    acc = None
    row = 0
    for i, b_ref in enumerate((oa_ref, ob_ref, oc_ref, od_ref)):
        width = b_ref.shape[1]
        w = wcb_ref[...] if b_ref is oc_ref else wbrb_ref[row:row + width, :]
        gate = jax.nn.sigmoid(jnp.dot(h, wgb_ref[i], preferred_element_type=F32))
        term = gate * jnp.dot(b_ref[...], w, preferred_element_type=F32)
        acc = term if acc is None else acc + term
        row += width
    o_ref[...] = acc.astype(o_ref.dtype)


def _merge(h, w_gate, w_branch, layer, w_branch_c, branches, tm=512, tn=512):
    m = h.shape[0]
    tm = min(tm, m)
    d_mix = w_branch.shape[-2]
    row = lambda a: pl.BlockSpec((tm, a.shape[1]), lambda j, i: (i, 0))
    return pl.pallas_call(
        _merge_kernel,
        out_shape=jax.ShapeDtypeStruct((m, D_MODEL), BF16),
        grid=(D_MODEL // tn, m // tm),
        in_specs=[
            row(h),
            _weight_spec((4, D_MODEL, tn), lambda j, i: (0, 0, j), layer),
            _weight_spec((d_mix, tn), lambda j, i: (0, j), layer),
            _weight_spec((w_branch_c.shape[0], tn), lambda j, i: (0, j)),
        ] + [row(b) for b in branches],
        out_specs=pl.BlockSpec((tm, tn), lambda j, i: (i, j)),
        scratch_shapes=[pltpu.VMEM((4, D_MODEL, tn), BF16), pltpu.VMEM((d_mix, tn), BF16), pltpu.VMEM((w_branch_c.shape[0], tn), BF16)],
        compiler_params=_cp("parallel", "arbitrary"),
        name="gated_merge",
    )(h, w_gate, w_branch, w_branch_c, *branches)


def _ffn_up_kernel(h_ref, wg_ref, wu_ref, o_ref, wgb_ref, wub_ref):
    _cast_weights_once((wg_ref, wu_ref), (wgb_ref, wub_ref))
    h = h_ref[...]
    g = jnp.dot(h, wgb_ref[...], preferred_element_type=F32)
    u = jnp.dot(h, wub_ref[...], preferred_element_type=F32)
    o_ref[...] = (g * jax.nn.sigmoid(g) * u).astype(o_ref.dtype)


def _ffn_up(h, w_gate, w_up, layer, tm=1024, tn=512):
    m, k = h.shape
    n = w_gate.shape[-1]
    tm = min(tm, m)
    return pl.pallas_call(
        _ffn_up_kernel,
        out_shape=jax.ShapeDtypeStruct((m, n), BF16),
        grid=(n // tn, m // tm),
        in_specs=[
            pl.BlockSpec((tm, k), lambda j, i: (i, 0)),
            _weight_spec((k, tn), lambda j, i: (0, j), layer),
            _weight_spec((k, tn), lambda j, i: (0, j), layer),
        ],
        out_specs=pl.BlockSpec((tm, tn), lambda j, i: (i, j)),
        scratch_shapes=[pltpu.VMEM((k, tn), BF16), pltpu.VMEM((k, tn), BF16)],
        compiler_params=_cp("parallel", "arbitrary"),
        name="ffn_up",
    )(h, w_gate, w_up)


def _pad_w_in(w):
    pad = ((0, 0), (0, C_LORA_PAD - C_LORA))
    lo = COL_CLO
    cf = [_heads_minor(w[:, COL_CF + i * C_WIDTH:COL_CF + (i + 1) * C_WIDTH]) for i in range(3)]
    return jnp.concatenate([
        w[:, :COL_CF],
        *cf,
        jnp.pad(w[:, lo:lo + C_LORA], pad),
        jnp.pad(w[:, lo + C_LORA:lo + 2 * C_LORA], pad),
        w[:, lo + 2 * C_LORA:],
    ], axis=1)


def kernel(x, norm_mix, w_in, a_q_norm, a_k_norm, b_conv_w, b_conv_b, b_filt_w1, b_filt_b1, b_filt_w2, b_filt_b2, b_filt_w3, b_filt_b3, b_filt_w4, b_filt_freq, b_skip, c_mu, c_w0, c_w_up, c_a0, c_a_up, c_g_up, c_k_k, c_k_a, c_r_k, c_ln_w, c_ln_b, d_lq1, d_lk1, d_lq2, d_lk2, d_subln, w_gate, w_branch, w_out, norm_ffn, w_ff_gate, w_ff_up, w_ff_down, norm_final):
    batch, seq, _ = x.shape
    m = batch * seq
    cos, sin = _rope_tables(seq)
    cos_half, sin_half, cos_int, sin_int = _dft_tables(seq)
    filt_feats = _filter_features(seq)
    x = x.reshape(m, D_MODEL)
    for l in range(DEPTH):
        h = _rmsnorm(x, norm_mix[l], BF16)
        proj = _matmul(h, _pad_w_in(w_in[l]), F32, 1024, D_IN_PAD // 4, name="in_proj")

        gains = jnp.concatenate([
            jnp.broadcast_to(a_q_norm[l], (A_HEADS, A_HEAD_DIM)),
            jnp.broadcast_to(a_k_norm[l], (A_KV_HEADS, A_HEAD_DIM)),
        ])[:, None, :]
        o_a = _attn_a(_qk_prep(proj, gains, cos, sin, seq), proj, batch, seq)

        k_re, k_im = _hyena_filter_spectrum(seq, filt_feats, cos_int, sin_int, b_filt_w1[l], b_filt_b1[l], b_filt_w2[l],
                                            b_filt_b2[l], b_filt_w3[l], b_filt_b3[l], b_filt_w4[l], b_filt_freq[l])
        o_b = _hyena(proj, b_conv_w[l], b_conv_b[l], b_skip[l], cos_half, sin_half, k_re, k_im, batch, seq)

        o_c = _rwkv(proj, c_mu[l], c_w0[l], c_w_up[l], c_a0[l], c_a_up[l], c_g_up[l], c_k_k[l], c_k_a[l], c_r_k[l],
                    c_ln_w[l], c_ln_b[l], batch, seq)

        lam_init = 0.8 - 0.6 * math.exp(-0.3 * l)
        lam_vecs = jnp.stack([d_lq1[l], d_lk1[l], d_lq2[l], d_lk2[l]])
        o_d = _attn_d(proj, lam_vecs, d_subln[l], lam_init, batch, seq)

        c_lo, c_hi = A_WIDTH + B_WIDTH, A_WIDTH + B_WIDTH + C_WIDTH
        w_br_c = _heads_minor(w_branch[l, c_lo:c_hi].T).T
        merged = _merge(h, w_gate, w_branch, l, w_br_c, (o_a, o_b, o_c, o_d))
        x, h2 = _matmul_res_norm(merged, w_out, l, x, norm_ffn[l])
        mid = _ffn_up(h2, w_ff_gate, w_ff_up, l)
        x = _matmul(mid, w_ff_down, F32, 1024, 512, residual=x, layer=l, name="ffn_down")
    return _rmsnorm(x, norm_final, F32).reshape(batch, seq, D_MODEL)
```

```python
import functools
import math

import jax
import jax.numpy as jnp
import numpy as np
from jax import lax
from jax.experimental import pallas as pl
from jax.experimental.pallas import tpu as pltpu

D_MODEL = 2048
DEPTH = 2
GRID_W = 64
NORM_EPS = 1e-6

A_HEADS = 8
A_KV_HEADS = 2
A_HEAD_DIM = 128
A_WIDTH = A_HEADS * A_HEAD_DIM
ROPE_THETA = 10000.0

B_WIDTH = 512
B_EMB_DIM = 33
B_FILTER_HIDDEN = 64
B_DECAY_TARGET = 1e-2
B_FAST_DECAY_PCT = 0.3
B_SLOW_DECAY_PCT = 1.5

C_HEADS = 8
C_HEAD_DIM = 64
C_WIDTH = C_HEADS * C_HEAD_DIM
C_LORA = 96
C_LORA_PAD = 128
C_GATE_LORA = 256
C_GN_EPS = 64e-5

D_HEADS = 4
D_HEAD_DIM = 64
D_V_DIM = 2 * D_HEAD_DIM
D_WIDTH = D_HEADS * D_V_DIM

FFN_HIDDEN = -(-8 * D_MODEL // (3 * 256)) * 256

COL_AQ = 0
COL_AK = COL_AQ + A_WIDTH
COL_AV = COL_AK + A_KV_HEADS * A_HEAD_DIM
COL_BU = COL_AV + A_KV_HEADS * A_HEAD_DIM
COL_CF = COL_BU + 3 * B_WIDTH
COL_CLO = COL_CF + 3 * C_WIDTH
COL_CG = COL_CLO + 2 * C_LORA_PAD
COL_DQ = COL_CG + C_GATE_LORA
COL_DK = COL_DQ + 2 * D_HEADS * D_HEAD_DIM
COL_DV = COL_DK + 2 * D_HEADS * D_HEAD_DIM
D_IN_PAD = COL_DV + D_WIDTH

VMEM_LIMIT_V7X = 56 * 1024 * 1024
F32 = jnp.float32
BF16 = jnp.bfloat16
HIGHEST = lax.Precision.HIGHEST
NT_DIMS = (((1,), (1,)), ((), ()))
LOG2_E = math.log2(math.e)


def _cp(*sem):
    return pltpu.CompilerParams(dimension_semantics=sem, vmem_limit_bytes=VMEM_LIMIT_V7X)


def _const_spec(shape):
    return pl.BlockSpec(shape, lambda *_: (0,) * len(shape), pipeline_mode=pl.Buffered(1))


SINGLE_BUFFER_BYTES = 8 * 1024 * 1024


def _weight_spec(shape, index_map, layer=None):
    mode = {"pipeline_mode": pl.Buffered(1)} if 4 * math.prod(shape) > SINGLE_BUFFER_BYTES else {}
    if layer is None:
        return pl.BlockSpec(shape, index_map, **mode)
    return pl.BlockSpec((None,) + tuple(shape), lambda *g: (layer,) + tuple(index_map(*g)), **mode)


def _rmsnorm_kernel(x_ref, g_ref, o_ref):
    x = x_ref[...]
    ms = jnp.mean(x * x, axis=-1, keepdims=True)
    o_ref[...] = (x * lax.rsqrt(ms + NORM_EPS) * g_ref[...]).astype(o_ref.dtype)


def _rmsnorm(x, g, out_dtype, tm=1024):
    m, d = x.shape
    tm = min(tm, m)
    return pl.pallas_call(
        _rmsnorm_kernel,
        out_shape=jax.ShapeDtypeStruct((m, d), out_dtype),
        grid=(m // tm,),
        in_specs=[pl.BlockSpec((tm, d), lambda i: (i, 0)), pl.BlockSpec((1, d), lambda i: (0, 0))],
        out_specs=pl.BlockSpec((tm, d), lambda i: (i, 0)),
        compiler_params=_cp("parallel"),
        name="rmsnorm",
    )(x, g.reshape(1, d))


def _mm_f32_kernel(a_ref, b_ref, o_ref):
    def split(x):
        hi = x.astype(BF16)
        return hi, (x - hi.astype(F32)).astype(BF16)

    (a_hi, a_lo), (b_hi, b_lo) = split(a_ref[...]), split(b_ref[...])
    dot = functools.partial(jnp.dot, preferred_element_type=F32)
    o_ref[...] = dot(a_hi, b_hi) + (dot(a_hi, b_lo) + dot(a_lo, b_hi))


def _matmul_f32(a, b, tm, tn, name):
    m, k = a.shape
    n = b.shape[1]
    tm, tn = min(tm, m), min(tn, n)
    return pl.pallas_call(
        _mm_f32_kernel,
        out_shape=jax.ShapeDtypeStruct((m, n), F32),
        grid=(m // tm, n // tn),
        in_specs=[pl.BlockSpec((tm, k), lambda i, j: (i, 0)), pl.BlockSpec((k, tn), lambda i, j: (0, j))],
        out_specs=pl.BlockSpec((tm, tn), lambda i, j: (i, j)),
        compiler_params=_cp("parallel", "arbitrary"),
        name=name,
    )(a, b)


def _cast_weights_once(w_refs, wb_refs):
    @pl.when(pl.program_id(1) == 0)
    def _():
        for w_ref, wb_ref in zip(w_refs, wb_refs):
            wb_ref[...] = w_ref[...].astype(BF16)


def _mm_kernel(a_ref, w_ref, o_ref, wb_ref):
    _cast_weights_once((w_ref,), (wb_ref,))
    o_ref[...] = jnp.dot(a_ref[...], wb_ref[...], preferred_element_type=F32).astype(o_ref.dtype)


def _mm_res_kernel(a_ref, w_ref, r_ref, o_ref, wb_ref):
    _cast_weights_once((w_ref,), (wb_ref,))
    o_ref[...] = r_ref[...] + jnp.dot(a_ref[...], wb_ref[...], preferred_element_type=F32)


def _matmul(a, w, out_dtype, tm, tn, residual=None, layer=None, name="matmul"):
    m, k = a.shape
    n = w.shape[-1]
    tm, tn = min(tm, m), min(tn, n)
    in_specs = [pl.BlockSpec((tm, k), lambda j, i: (i, 0)), _weight_spec((k, tn), lambda j, i: (0, j), layer)]
    args = [a, w]
    body = _mm_kernel
    if residual is not None:
        body = _mm_res_kernel
        in_specs.append(pl.BlockSpec((tm, tn), lambda j, i: (i, j)))
        args.append(residual)
    return pl.pallas_call(
        body,
        out_shape=jax.ShapeDtypeStruct((m, n), out_dtype),
        grid=(n // tn, m // tm),
        in_specs=in_specs,
        out_specs=pl.BlockSpec((tm, tn), lambda j, i: (i, j)),
        scratch_shapes=[pltpu.VMEM((k, tn), BF16)],
        compiler_params=_cp("parallel", "arbitrary"),
        name=name,
    )(*args)


def _mm_res_norm_kernel(a_ref, w_ref, r_ref, g_ref, x_ref, h_ref, wb_ref):
    @pl.when(pl.program_id(0) == 0)
    def _():
        wb_ref[...] = w_ref[...].astype(BF16)

    x = r_ref[...] + jnp.dot(a_ref[...], wb_ref[...], preferred_element_type=F32)
    x_ref[...] = x
    ms = jnp.mean(x * x, axis=-1, keepdims=True)
    h_ref[...] = (x * lax.rsqrt(ms + NORM_EPS) * g_ref[...]).astype(h_ref.dtype)


def _matmul_res_norm(a, w, layer, residual, gain, tm=512):
    m, k = a.shape
    n = w.shape[-1]
    tm = min(tm, m)
    rows = lambda width: pl.BlockSpec((tm, width), lambda i: (i, 0))
    return pl.pallas_call(
        _mm_res_norm_kernel,
        out_shape=[jax.ShapeDtypeStruct((m, n), F32), jax.ShapeDtypeStruct((m, n), BF16)],
        grid=(m // tm,),
        in_specs=[rows(k), _weight_spec((k, n), lambda i: (0, 0), layer), rows(n), pl.BlockSpec((1, n), lambda i: (0, 0))],
        out_specs=[rows(n), rows(n)],
        scratch_shapes=[pltpu.VMEM((k, n), BF16)],
        compiler_params=_cp("arbitrary"),
        name="out_proj_norm",
    )(a, w, residual, gain.reshape(1, n))


def _rope_tables(seq):
    rows = seq // GRID_W
    row_idx = jnp.repeat(jnp.arange(rows, dtype=F32), GRID_W)
    col_idx = jnp.tile(jnp.arange(GRID_W, dtype=F32), rows)
    axis_dim = A_HEAD_DIM // 2
    inv_freq = ROPE_THETA ** (-jnp.arange(0, axis_dim, 2, dtype=F32) / axis_dim)
    ang_r = row_idx[:, None] * inv_freq[None, :]
    ang_c = col_idx[:, None] * inv_freq[None, :]
    ang = jnp.concatenate([ang_r, ang_r, ang_c, ang_c], axis=-1)
    return jnp.cos(ang), jnp.sin(ang)


def _qk_prep_kernel(x_ref, g_ref, cos_ref, sin_ref, o_ref):
    cos = cos_ref[...]
    sin = sin_ref[...]
    lane = lax.broadcasted_iota(jnp.int32, cos.shape, 1)
    quarter = A_HEAD_DIM // 4
    first = (lane % (2 * quarter)) < quarter
    for h in range(A_HEADS + A_KV_HEADS):
        cols = slice(h * A_HEAD_DIM, (h + 1) * A_HEAD_DIM)
        x = x_ref[:, cols]
        xn = x * lax.rsqrt(jnp.mean(x * x, axis=-1, keepdims=True) + NORM_EPS) * g_ref[h]
        rot = jnp.where(first, -pltpu.roll(xn, A_HEAD_DIM - quarter, 1), pltpu.roll(xn, quarter, 1))
        y = xn * cos + rot * sin
        if h < A_HEADS:
            y = y * (A_HEAD_DIM**-0.5 * LOG2_E)
        o_ref[:, cols] = y.astype(o_ref.dtype)


def _qk_prep(proj, gains, cos, sin, seq, tm=1024):
    m = proj.shape[0]
    tm = min(tm, seq)
    width = (A_HEADS + A_KV_HEADS) * A_HEAD_DIM
    nt = seq // tm
    return pl.pallas_call(
        _qk_prep_kernel,
        out_shape=jax.ShapeDtypeStruct((m, width), BF16),
        grid=(m // tm,),
        in_specs=[
            pl.BlockSpec((tm, width), lambda i: (i, 0)),
            pl.BlockSpec(gains.shape, lambda i: (0, 0, 0)),
            pl.BlockSpec((tm, A_HEAD_DIM), lambda i: (i % nt, 0)),
            pl.BlockSpec((tm, A_HEAD_DIM), lambda i: (i % nt, 0)),
        ],
        out_specs=pl.BlockSpec((tm, width), lambda i: (i, 0)),
        compiler_params=_cp("parallel"),
        name="a_qk_prep",
    )(proj, gains, cos, sin)


class _OnlineSoftmax:
    def __init__(self, rows, width):
        self.m = jnp.full((rows, 1), -jnp.inf, F32)
        self.l = jnp.zeros((rows, 1), F32)
        self.acc = jnp.zeros((rows, width), F32)

    def add(self, s, v):
        m_new = jnp.maximum(self.m, jnp.max(s, axis=-1, keepdims=True))
        p = jnp.exp2(s - m_new)
        alpha = jnp.exp2(self.m - m_new)
        self.l = alpha * self.l + jnp.sum(p, axis=-1, keepdims=True)
        self.acc = alpha * self.acc + jnp.dot(p.astype(BF16), v, preferred_element_type=F32)
        self.m = m_new

    def result(self):
        return self.acc / self.l


def _attn_a_kernel(q_ref, k_ref, v_ref, o_ref, *, kc):
    q = q_ref[...]
    sm = _OnlineSoftmax(q.shape[0], A_HEAD_DIM)
    for c in range(k_ref.shape[0] // kc):
        keys = slice(c * kc, (c + 1) * kc)
        s = lax.dot_general(q, k_ref[keys, :], NT_DIMS, preferred_element_type=F32)
        sm.add(s, v_ref[keys, :].astype(BF16))
    o_ref[...] = sm.result().astype(o_ref.dtype)


def _attn_a(qk, proj, batch, seq, tq=1024, kc=1024):
    tq = min(tq, seq)
    kc = min(kc, seq)
    nq = seq // tq
    group = A_HEADS // A_KV_HEADS
    k_blk = COL_AK // A_HEAD_DIM
    v_blk = COL_AV // A_HEAD_DIM
    return pl.pallas_call(
        functools.partial(_attn_a_kernel, kc=kc),
        out_shape=jax.ShapeDtypeStruct((batch * seq, A_WIDTH), BF16),
        grid=(batch, A_HEADS, nq),
        in_specs=[
            pl.BlockSpec((tq, A_HEAD_DIM), lambda b, h, i: (b * nq + i, h)),
            pl.BlockSpec((seq, A_HEAD_DIM), lambda b, h, i: (b, k_blk + h // group)),
            pl.BlockSpec((seq, A_HEAD_DIM), lambda b, h, i: (b, v_blk + h // group)),
        ],
        out_specs=pl.BlockSpec((tq, A_HEAD_DIM), lambda b, h, i: (b * nq + i, h)),
        compiler_params=_cp("parallel", "arbitrary", "arbitrary"),
        name="a_attention",
    )(qk, qk, proj)


def _attn_d_kernel(q_ref, k_ref, v_ref, slope_ref, lam_ref, g_ref, o_ref, *, tq, kc, lam_init):
    q = q_ref[...] * (D_HEAD_DIM**-0.5 * LOG2_E)
    lane = lax.broadcasted_iota(jnp.int32, q.shape, 1)
    q_maps = (jnp.where(lane < D_HEAD_DIM, q, 0.0).astype(BF16), jnp.where(lane >= D_HEAD_DIM, q, 0.0).astype(BF16))
    slope = slope_ref[0][:, 0:1] * LOG2_E
    rel = pl.program_id(2) * tq + lax.broadcasted_iota(jnp.int32, (tq, kc), 0) - lax.broadcasted_iota(jnp.int32, (tq, kc), 1)
    rel = slope * rel.astype(F32)
    maps = (_OnlineSoftmax(tq, D_V_DIM), _OnlineSoftmax(tq, D_V_DIM))
    for c in range(k_ref.shape[0] // kc):
        keys = slice(c * kc, (c + 1) * kc)
        k = k_ref[keys, :].astype(BF16)
        v = v_ref[keys, :].astype(BF16)
        bias = jnp.abs(rel - slope * float(c * kc))
        for q_map, sm in zip(q_maps, maps):
            sm.add(lax.dot_general(q_map, k, NT_DIMS, preferred_element_type=F32) - bias, v)
    lam_v = lam_ref[...]
    lam = (
        jnp.exp(jnp.sum(lam_v[0:1] * lam_v[1:2], axis=-1, keepdims=True))
        - jnp.exp(jnp.sum(lam_v[2:3] * lam_v[3:4], axis=-1, keepdims=True))
        + lam_init
    )
    o = maps[0].result() - lam * maps[1].result()
    o = o * lax.rsqrt(jnp.mean(o * o, axis=-1, keepdims=True) + NORM_EPS) * g_ref[...]
    o_ref[...] = (o * (1.0 - lam_init)).astype(o_ref.dtype)


def _attn_d(proj, lam_vecs, subln, lam_init, batch, seq, tq=1024, kc=512):
    tq = min(tq, seq)
    kc = min(kc, seq)
    nq = seq // tq
    slopes = 2.0 ** (-8.0 * np.arange(1, D_HEADS + 1, dtype=np.float32) / D_HEADS)
    slopes = jnp.asarray(np.broadcast_to(slopes[:, None, None], (D_HEADS, 1, 128)).astype(np.float32))
    q_blk, k_blk, v_blk = COL_DQ // D_V_DIM, COL_DK // D_V_DIM, COL_DV // D_V_DIM
    return pl.pallas_call(
        functools.partial(_attn_d_kernel, tq=tq, kc=kc, lam_init=lam_init),
        out_shape=jax.ShapeDtypeStruct((batch * seq, D_WIDTH), BF16),
        grid=(batch, D_HEADS, nq),
        in_specs=[
            pl.BlockSpec((tq, D_V_DIM), lambda b, h, i: (b * nq + i, q_blk + h)),
            pl.BlockSpec((seq, D_V_DIM), lambda b, h, i: (b, k_blk + h)),
            pl.BlockSpec((seq, D_V_DIM), lambda b, h, i: (b, v_blk + h)),
            pl.BlockSpec((1, 1, 128), lambda b, h, i: (h, 0, 0)),
            pl.BlockSpec((4, D_HEAD_DIM), lambda b, h, i: (0, 0)),
            pl.BlockSpec((1, D_V_DIM), lambda b, h, i: (0, 0)),
        ],
        out_specs=pl.BlockSpec((tq, D_V_DIM), lambda b, h, i: (b * nq + i, h)),
        compiler_params=_cp("parallel", "arbitrary", "arbitrary"),
        name="d_attention",
    )(proj, proj, proj, slopes, lam_vecs, subln.reshape(1, D_V_DIM))


DFT_ROW_SPLIT = 64


def _dft_tables(seq):
    n = 2 * seq
    blk = min(DFT_ROW_SPLIT, seq)
    col = jnp.arange(seq, dtype=jnp.int32)[None, :]
    hi = jnp.arange(seq // blk, dtype=jnp.int32)[:, None]
    lo = jnp.arange(blk, dtype=jnp.int32)[:, None]

    def cos_sin(index, period):
        ang = (index % period).astype(F32) * (2.0 * math.pi / period)
        return jnp.cos(ang), jnp.sin(ang)

    def tables(col_term, period):
        (ca, sa), (cb, sb) = cos_sin(2 * blk * hi * col_term, period), cos_sin((2 * lo + 1) * col_term, period)
        ca, sa, cb, sb = ca[:, None, :], sa[:, None, :], cb[None], sb[None]
        return (ca * cb - sa * sb).reshape(seq, seq), (sa * cb + ca * sb).reshape(seq, seq)

    cos_half, sin_half = tables(2 * col + 1, 4 * n)
    cos_int, sin_int = tables(col, 2 * n)
    return cos_half.astype(BF16), sin_half.astype(BF16), cos_int, sin_int


def _filter_features(seq):
    t = jnp.linspace(0.0, 1.0, seq, dtype=F32)[:, None]
    n_bands = (B_EMB_DIM - 1) // 2
    bands = jnp.linspace(1e-4, n_bands - 1, n_bands, dtype=F32)[None, :]
    ang = (2.0 * math.pi / seq) * jnp.arange(seq, dtype=F32)[:, None] * bands
    z = jnp.concatenate([t, jnp.cos(ang), -jnp.sin(ang)], axis=-1)
    z = jnp.pad(z, ((0, 0), (0, B_FILTER_HIDDEN - B_EMB_DIM)))
    max_decay = math.log(B_DECAY_TARGET) / B_FAST_DECAY_PCT
    min_decay = math.log(B_DECAY_TARGET) / B_SLOW_DECAY_PCT
    deltas = jnp.abs(jnp.linspace(min_decay, max_decay, B_WIDTH, dtype=F32))[None, :]
    return z, t, deltas


def _filter_kernel(z_ref, t_ref, dl_ref, w1_ref, b1_ref, w2_ref, b2_ref, w3_ref, b3_ref, w4_ref, fr_ref, hs_ref, hd_ref):
    fr = fr_ref[...]
    hid = jnp.sin(fr * (jnp.dot(z_ref[...], w1_ref[...], preferred_element_type=F32, precision=HIGHEST) + b1_ref[...]))
    hid = jnp.sin(fr * (jnp.dot(hid, w2_ref[...], preferred_element_type=F32, precision=HIGHEST) + b2_ref[...]))
    hid = jnp.sin(fr * (jnp.dot(hid, w3_ref[...], preferred_element_type=F32, precision=HIGHEST) + b3_ref[...]))
    h = jnp.dot(hid, w4_ref[...], preferred_element_type=F32, precision=HIGHEST)
    window = jnp.exp(-t_ref[...] * dl_ref[...])
    h_fwd = h[:, :B_WIDTH] * window
    h_bwd = h[:, B_WIDTH:] * window
    row = lax.broadcasted_iota(jnp.int32, h_bwd.shape, 0)
    h_bwd = jnp.where(row == 0, 0.0, h_bwd)
    norm = jnp.sum(jnp.abs(h_fwd), axis=0, keepdims=True) + jnp.sum(jnp.abs(h_bwd), axis=0, keepdims=True)
    seq = h.shape[0]
    inv_n = 1.0 / seq
    hs_ref[...] = (h_fwd + h_bwd) / norm * inv_n
    hd_ref[...] = (h_bwd - h_fwd) / norm * inv_n


def _hyena_filter_spectrum(seq, feats, cos_int, sin_int, w1, b1, w2, b2, w3, b3, w4, freq):
    z, t, deltas = feats
    w1p = jnp.pad(w1, ((0, B_FILTER_HIDDEN - B_EMB_DIM), (0, 0)))
    row = lambda v: v.reshape(1, -1)
    args = (z, t, deltas, w1p, row(b1), w2, row(b2), w3, row(b3), w4, row(freq))
    h_sum, h_diff = pl.pallas_call(
        _filter_kernel,
        out_shape=[jax.ShapeDtypeStruct((seq, B_WIDTH), F32)] * 2,
        in_specs=[pl.BlockSpec(a.shape, lambda: (0, 0)) for a in args],
        out_specs=[pl.BlockSpec((seq, B_WIDTH), lambda: (0, 0))] * 2,
        compiler_params=pltpu.CompilerParams(vmem_limit_bytes=VMEM_LIMIT_V7X),
        name="b_filter",
    )(*args)
    k_re = _matmul_f32(cos_int, h_sum, 256, B_WIDTH, name="b_filter_dft_re")
    k_im = _matmul_f32(sin_int, h_diff, 256, B_WIDTH, name="b_filter_dft_im")
    return k_re, k_im


def _hyena_kernel(v_ref, x1_ref, x0_ref, wv_ref, w1_ref, w0_ref, bv_ref, b1_ref, b0_ref, skip_ref, c_ref, s_ref, kre_ref, kim_ref, o_ref,
                  *, f_blk):
    seq = v_ref.shape[0]
    row = lax.broadcasted_iota(jnp.int32, v_ref.shape, 0)

    def conv3(u_ref, w_ref, b_ref):
        u = u_ref[...]
        w = w_ref[...]
        u_prev = jnp.where(row == 0, 0.0, pltpu.roll(u, 1, 0))
        u_next = jnp.where(row == seq - 1, 0.0, pltpu.roll(u, seq - 1, 0))
        return w[0:1] * u_prev + w[1:2] * u + w[2:3] * u_next + b_ref[...]

    z = conv3(v_ref, wv_ref, bv_ref) * conv3(x1_ref, w1_ref, b1_ref)
    zb = z.astype(BF16)
    y = z * skip_ref[...]
    for f0 in range(0, seq, f_blk):
        fs = slice(f0, f0 + f_blk)
        cz = jnp.dot(c_ref[fs, :], zb, preferred_element_type=F32)
        sz = jnp.dot(s_ref[fs, :], zb, preferred_element_type=F32)
        k_re = kre_ref[fs, :]
        k_im = kim_ref[fs, :]
        y_re = (cz * k_re + sz * k_im).astype(BF16)
        y_im = (cz * k_im - sz * k_re).astype(BF16)
        y = y + (jnp.dot(c_ref[:, fs], y_re, preferred_element_type=F32) - jnp.dot(s_ref[:, fs], y_im, preferred_element_type=F32))
    o_ref[...] = (y * conv3(x0_ref, w0_ref, b0_ref)).astype(o_ref.dtype)


def _hyena(proj, conv_w, conv_b, skip, cos_half, sin_half, k_re, k_im, batch, seq, cb=256, f_blk=1024):
    ncb = B_WIDTH // cb
    u_blk = COL_BU // cb
    f_blk = min(f_blk, seq)

    def u_spec(part):
        return pl.BlockSpec((seq, cb), lambda j, b: (b, u_blk + part * ncb + j))

    def w_spec(rows, part):
        return pl.BlockSpec((rows, cb), lambda j, b: (0, part * ncb + j))

    k_spec = pl.BlockSpec((seq, cb), lambda j, b: (0, j), pipeline_mode=pl.Buffered(1))
    return pl.pallas_call(
        functools.partial(_hyena_kernel, f_blk=f_blk),
        out_shape=jax.ShapeDtypeStruct((batch * seq, B_WIDTH), BF16),
        grid=(ncb, batch),
        in_specs=[
            u_spec(0), u_spec(1), u_spec(2),
            w_spec(3, 0), w_spec(3, 1), w_spec(3, 2),
            w_spec(1, 0), w_spec(1, 1), w_spec(1, 2),
            pl.BlockSpec((1, cb), lambda j, b: (0, j)),
            _const_spec((seq, seq)), _const_spec((seq, seq)),
            k_spec, k_spec,
        ],
        out_specs=pl.BlockSpec((seq, cb), lambda j, b: (b, j)),
        compiler_params=_cp("parallel", "arbitrary"),
        name="b_hyena",
    )(proj, proj, proj, conv_w, conv_w, conv_w, conv_b.reshape(1, -1), conv_b.reshape(1, -1), conv_b.reshape(1, -1),
      skip.reshape(1, -1), cos_half, sin_half, k_re, k_im)


def _rwkv_prep_kernel(cur_ref, lo_ref, pcur_ref, plo_ref, ncur_ref, nlo_ref, mu_ref, mulo_ref, w0_ref, wup_ref, a0_ref, aup_ref,
                      kk_ref, ka_ref, o_ref, *, chunk):
    first = pl.program_id(1) == 0
    last = pl.program_id(1) == pl.num_programs(1) - 1
    halo = pcur_ref.shape[0]
    tm = cur_ref.shape[0]
    row_in_chunk = lax.broadcasted_iota(jnp.int32, (tm, C_WIDTH), 0) % chunk

    def neighbours(x_ref, p_ref, n_ref):
        x = x_ref[...]
        row = lax.broadcasted_iota(jnp.int32, x.shape, 0)
        p_row = jnp.where(first, 0.0, p_ref[halo - 1:halo, :])
        n_row = jnp.where(last, 0.0, n_ref[0:1, :])
        prev = jnp.where(row == 0, p_row, pltpu.roll(x, 1, 0))
        nxt = jnp.where(row == tm - 1, n_row, pltpu.roll(x, tm - 1, 0))
        return x, (prev, nxt)

    cur, cur_sh = neighbours(cur_ref, pcur_ref, ncur_ref)
    lo, lo_sh = neighbours(lo_ref, plo_ref, nlo_ref)
    k_k = kk_ref[...]
    k_a = ka_ref[...]
    for d in range(2):
        f = cur + (cur_sh[d] - cur) * mu_ref[d]
        f_lo = lo + (lo_sh[d] - lo) * mulo_ref[d]
        r = f[:, :C_WIDTH]
        k = f[:, C_WIDTH:2 * C_WIDTH]
        v = f[:, 2 * C_WIDTH:]
        w_lo = f_lo[:, :C_LORA_PAD]
        a_lo = f_lo[:, C_LORA_PAD:]
        x = w0_ref[d] + jnp.dot(jnp.tanh(w_lo), wup_ref[d], preferred_element_type=F32, precision=HIGHEST)
        w = -(jnp.maximum(-x, 0.0) + jnp.log(1.0 + jnp.exp(-jnp.abs(x)))) - 0.5
        neg_log_decay = jnp.exp(w)
        log_g = neg_log_decay
        shift = 1
        while shift < chunk:
            if d == 0:
                moved = jnp.where(row_in_chunk >= shift, pltpu.roll(log_g, shift, 0), 0.0)
            else:
                moved = jnp.where(row_in_chunk < chunk - shift, pltpu.roll(log_g, tm - shift, 0), 0.0)
            log_g = log_g + moved
            shift *= 2
        g = jnp.exp(-log_g)
        g_inv = jnp.exp(log_g)
        g_before = jnp.exp(neg_log_decay - log_g)
        a = jax.nn.sigmoid(a0_ref[d] + jnp.dot(a_lo, aup_ref[d], preferred_element_type=F32, precision=HIGHEST))
        transposed = lambda val: val.T.reshape(C_HEAD_DIM, C_HEADS, tm)
        kk = transposed(k * k_k)
        kk = kk / jnp.maximum(jnp.sqrt(jnp.sum(kk * kk, axis=0, keepdims=True)), 1e-12)
        for q, val in enumerate((r * g, g, k * (1.0 + (a - 1.0) * k_a) * g_inv, v)):
            o_ref[d, q] = transposed(val)
        o_ref[d, 4] = -kk * transposed(g_before)
        o_ref[d, 5] = kk * transposed(a * g_inv)


def _rwkv_prep(proj, mu, mu_lo, w0, w_up, a0, a_up, k_k, k_a, batch, seq, chunk, tm=256):
    tm = min(tm, seq)
    nt = seq // tm
    halo = 8
    hb = tm // halo
    n_hblk = batch * seq // halo
    wide, narrow = 3 * C_WIDTH, 2 * C_LORA_PAD
    cur_blk, lo_blk = COL_CF // wide, COL_CLO // narrow

    def cur_map(blk):
        return lambda b, j: (b * nt + j, blk)

    def prev_map(blk):
        return lambda b, j: (jnp.maximum((b * nt + j) * hb - 1, 0), blk)

    def next_map(blk):
        return lambda b, j: (jnp.minimum((b * nt + j + 1) * hb, n_hblk - 1), blk)

    small = lambda a: pl.BlockSpec(a.shape, lambda b, j: (0,) * a.ndim)
    consts = (mu, mu_lo, w0, w_up, a0, a_up, k_k, k_a)
    return pl.pallas_call(
        functools.partial(_rwkv_prep_kernel, chunk=chunk),
        out_shape=jax.ShapeDtypeStruct((2, 6, C_HEAD_DIM, batch, C_HEADS, seq), F32),
        grid=(batch, nt),
        in_specs=[
            pl.BlockSpec((tm, wide), cur_map(cur_blk)),
            pl.BlockSpec((tm, narrow), cur_map(lo_blk)),
            pl.BlockSpec((halo, wide), prev_map(cur_blk)),
            pl.BlockSpec((halo, narrow), prev_map(lo_blk)),
            pl.BlockSpec((halo, wide), next_map(cur_blk)),
            pl.BlockSpec((halo, narrow), next_map(lo_blk)),
        ] + [small(a) for a in consts],
        out_specs=pl.BlockSpec((2, 6, C_HEAD_DIM, None, C_HEADS, tm), lambda b, j: (0, 0, 0, b, 0, j)),
        compiler_params=_cp("parallel", "arbitrary"),
        name="c_prep",
    )(proj, proj, proj, proj, proj, proj, *consts)


RELAYOUT_T = 128


def _rwkv_relayout_kernel(pf_ref, pb_ref, o_ref, *, tc):
    n = pf_ref.shape[0]
    t_blk = pf_ref.shape[-1]
    mirror = t_blk - 1 - lax.broadcasted_iota(jnp.int32, (pb_ref.shape[1] * pb_ref.shape[2], t_blk), 1)
    for k in range(n):
        bwd = jnp.take_along_axis(pb_ref[k].reshape(-1, t_blk), mirror, axis=1)
        cols = jnp.concatenate([pf_ref[k].reshape(-1, t_blk), bwd], axis=0).T
        for c in range(t_blk // tc):
            o_ref[c, k] = cols[c * tc:(c + 1) * tc]


def _rwkv_relayout(p, tc):
    _, nq, n, batch, heads, seq = p.shape
    t_blk = min(RELAYOUT_T, seq)
    nb = seq // t_blk
    lanes = 2 * batch * heads
    per_blk = t_blk // tc
    return pl.pallas_call(
        functools.partial(_rwkv_relayout_kernel, tc=tc),
        out_shape=jax.ShapeDtypeStruct((seq // tc, nq, n, tc, lanes), F32),
        grid=(nb, nq),
        in_specs=[
            pl.BlockSpec((None, None, n, batch, heads, t_blk), lambda c, q: (0, q, 0, 0, 0, c)),
            pl.BlockSpec((None, None, n, batch, heads, t_blk), lambda c, q: (1, q, 0, 0, 0, nb - 1 - c)),
        ],
        out_specs=pl.BlockSpec((per_blk, None, n, tc, lanes), lambda c, q: (c, q, 0, 0, 0)),
        compiler_params=_cp("parallel", "arbitrary"),
        name="c_relayout",
    )(p, p)


def _rwkv_scan_kernel(x_ref, rk_ref, lnw_ref, lnb_ref, out_ref, s_ref, o_ref, *, tc, k_chunk):
    n = s_ref.shape[0]
    lanes = s_ref.shape[2]

    @pl.when(pl.program_id(0) == 0)
    def _():
        s_ref[...] = jnp.zeros_like(s_ref)

    def row(q, k, t):
        return x_ref[pl.ds((q * n + k) * tc + t, 1), :]

    def step(t, carry):
        vv = x_ref[pl.ds(3 * n * tc + t, n, stride=tc), :]

        sa = s_ref[0] * row(4, 0, t)
        for j in range(1, n):
            sa = sa + s_ref[j] * row(4, j, t)

        def update_chunk(c, o):
            base = pl.multiple_of(c * k_chunk, k_chunk)
            for j in range(k_chunk):
                k = base + j
                sk = s_ref[k] + (sa * row(5, k, t) + vv * row(2, k, t))
                s_ref[k] = sk
                o = o + sk * row(0, k, t)
            return o

        o_ref[pl.ds(t, n, stride=tc), :] = lax.fori_loop(0, n // k_chunk, update_chunk, jnp.zeros((n, lanes), F32))
        return carry

    lax.fori_loop(0, tc, step, 0)
    for k in range(n):
        s_ref[k] = s_ref[k] * x_ref[(n + k) * tc + tc - 1:(n + k) * tc + tc, :]

    quantity = lambda q: x_ref[q * n * tc:(q + 1) * n * tc, :].reshape(n, tc, lanes)
    o = o_ref[...].reshape(n, tc, lanes)
    mean = jnp.mean(o, axis=0, keepdims=True)
    var = jnp.mean(jnp.square(o - mean), axis=0, keepdims=True)
    o = (o - mean) * lax.rsqrt(var + C_GN_EPS) * lnw_ref[...] + lnb_ref[...]
    bonus = jnp.sum(quantity(0) * quantity(2) * rk_ref[...], axis=0, keepdims=True)
    out_ref[...] = o + bonus * quantity(3)


def _rwkv_scan(xs, rk, lnw, lnb, k_chunk=32):
    nc, nq, n, tc, lanes = xs.shape
    tile = lambda: pl.BlockSpec((n, 1, lanes), lambda i: (0, 0, 0))
    return pl.pallas_call(
        functools.partial(_rwkv_scan_kernel, tc=tc, k_chunk=min(k_chunk, n)),
        out_shape=jax.ShapeDtypeStruct((n, nc * tc, lanes), F32),
        grid=(nc,),
        in_specs=[pl.BlockSpec((nq * n * tc, lanes), lambda i: (i, 0)), tile(), tile(), tile()],
        out_specs=pl.BlockSpec((n, tc, lanes), lambda i: (0, i, 0)),
        scratch_shapes=[pltpu.VMEM((n, n, lanes), F32), pltpu.VMEM((n * tc, lanes), F32)],
        compiler_params=_cp("arbitrary"),
        name="c_scan",
    )(xs.reshape(nc * nq * n * tc, lanes), rk, lnw, lnb)


def _rwkv_unlayout_kernel(o_ref, q_ref):
    for v in range(o_ref.shape[0]):
        rows = o_ref[v].T
        q_ref[:, v] = rows.reshape(q_ref.shape[0], q_ref.shape[2], q_ref.shape[3])


def _rwkv_unlayout(o, batch):
    n, seq, lanes = o.shape
    t_blk = min(RELAYOUT_T, seq)
    heads = lanes // (2 * batch)
    return pl.pallas_call(
        _rwkv_unlayout_kernel,
        out_shape=jax.ShapeDtypeStruct((2 * batch, n, heads, seq), F32),
        grid=(seq // t_blk,),
        in_specs=[pl.BlockSpec((n, t_blk, lanes), lambda c: (0, c, 0))],
        out_specs=pl.BlockSpec((2 * batch, n, heads, t_blk), lambda c: (0, 0, 0, c)),
        compiler_params=_cp("parallel"),
        name="c_unlayout",
    )(o)


def _rwkv_out_kernel(qf_ref, qb_ref, glo_ref, gup_ref, y_ref):
    t_blk = qf_ref.shape[-1]
    lane_tile = min(RELAYOUT_T, t_blk)
    tiles = t_blk // lane_tile
    mirror = lane_tile - 1 - lax.broadcasted_iota(jnp.int32, (qf_ref.shape[0] * qf_ref.shape[1], lane_tile), 1)
    g = jnp.dot(jax.nn.sigmoid(glo_ref[...]).astype(BF16), gup_ref[...], preferred_element_type=F32)
    for i in range(tiles):
        cols = slice(i * lane_tile, (i + 1) * lane_tile)
        back = slice((tiles - 1 - i) * lane_tile, (tiles - i) * lane_tile)
        fwd = qf_ref[:, :, cols].reshape(-1, lane_tile)
        bwd = qb_ref[:, :, back].reshape(-1, lane_tile)
        o = (fwd + jnp.take_along_axis(bwd, mirror, axis=1)).T
        y_ref[cols, :] = (o * g[cols, :]).astype(y_ref.dtype)


OUT_T = 512


def _rwkv_out(q, proj, g_up, batch, seq):
    _, n, heads, _ = q.shape
    t_blk = min(OUT_T, seq)
    nt = seq // t_blk
    g_blk = COL_CG // C_GATE_LORA
    return pl.pallas_call(
        _rwkv_out_kernel,
        out_shape=jax.ShapeDtypeStruct((batch * seq, C_WIDTH), BF16),
        grid=(batch, nt),
        in_specs=[
            pl.BlockSpec((None, n, heads, t_blk), lambda b, j: (b, 0, 0, j)),
            pl.BlockSpec((None, n, heads, t_blk), lambda b, j: (batch + b, 0, 0, nt - 1 - j)),
            pl.BlockSpec((t_blk, C_GATE_LORA), lambda b, j: (b * nt + j, g_blk)),
            pl.BlockSpec((C_GATE_LORA, C_WIDTH), lambda b, j: (0, 0)),
        ],
        out_specs=pl.BlockSpec((t_blk, C_WIDTH), lambda b, j: (b * nt + j, 0)),
        compiler_params=_cp("parallel", "arbitrary"),
        name="c_out",
    )(q, q, proj, g_up)


SCAN_TC = 16


def _heads_minor(p):
    return p.reshape(p.shape[:-1] + (C_HEADS, C_HEAD_DIM)).swapaxes(-1, -2).reshape(p.shape)


def _rwkv(proj, mu, w0, w_up, a0, a_up, g_up, k_k, k_a, r_k, ln_w, ln_b, batch, seq):
    pad_lo = C_LORA_PAD - C_LORA
    split = 3 * C_WIDTH
    mu_wide = jnp.concatenate([_heads_minor(mu[:, i * C_WIDTH:(i + 1) * C_WIDTH]) for i in range(3)], axis=-1)[:, None, :]
    mu_lo = jnp.concatenate([
        jnp.pad(mu[:, split:split + C_LORA], ((0, 0), (0, pad_lo))),
        jnp.pad(mu[:, split + C_LORA:], ((0, 0), (0, pad_lo))),
    ], axis=-1)[:, None, :]
    w_up_p = jnp.pad(_heads_minor(w_up), ((0, 0), (0, pad_lo), (0, 0)))
    a_up_p = jnp.pad(_heads_minor(a_up), ((0, 0), (0, pad_lo), (0, 0)))
    feats = _rwkv_prep(proj, mu_wide, mu_lo, _heads_minor(w0)[:, None, :], w_up_p, _heads_minor(a0)[:, None, :], a_up_p,
                       _heads_minor(k_k).reshape(1, C_WIDTH), _heads_minor(k_a).reshape(1, C_WIDTH), batch, seq, min(SCAN_TC, seq))
    xs = _rwkv_relayout(feats, min(SCAN_TC, seq))
    per_lane = lambda p: jnp.tile(p.reshape(C_HEADS, C_HEAD_DIM).T, (1, 2 * batch))[:, None, :]
    o = _rwkv_scan(xs, per_lane(r_k), per_lane(ln_w), per_lane(ln_b))
    return _rwkv_out(_rwkv_unlayout(o, batch), proj, _heads_minor(g_up).astype(BF16), batch, seq)


def _merge_kernel(h_ref, wg_ref, wbr_ref, wc_ref, oa_ref, ob_ref, oc_ref, od_ref, o_ref, wgb_ref, wbrb_ref, wcb_ref):
    _cast_weights_once((wg_ref, wbr_ref, wc_ref), (wgb_ref, wbrb_ref, wcb_ref))
    h = h_ref[...]
    acc = None
    row = 0
    for i, b_ref in enumerate((oa_ref, ob_ref, oc_ref, od_ref)):
        width = b_ref.shape[1]
        w = wcb_ref[...] if b_ref is oc_ref else wbrb_ref[row:row + width, :]
        gate = jax.nn.sigmoid(jnp.dot(h, wgb_ref[i], preferred_element_type=F32))
        term = gate * jnp.dot(b_ref[...], w, preferred_element_type=F32)
        acc = term if acc is None else acc + term
        row += width
    o_ref[...] = acc.astype(o_ref.dtype)


def _merge(h, w_gate, w_branch, layer, w_branch_c, branches, tm=512, tn=512):
    m = h.shape[0]
    tm = min(tm, m)
    d_mix = w_branch.shape[-2]
    row = lambda a: pl.BlockSpec((tm, a.shape[1]), lambda j, i: (i, 0))
    return pl.pallas_call(
        _merge_kernel,
        out_shape=jax.ShapeDtypeStruct((m, D_MODEL), BF16),
        grid=(D_MODEL // tn, m // tm),
        in_specs=[
            row(h),
            _weight_spec((4, D_MODEL, tn), lambda j, i: (0, 0, j), layer),
            _weight_spec((d_mix, tn), lambda j, i: (0, j), layer),
            _weight_spec((w_branch_c.shape[0], tn), lambda j, i: (0, j)),
        ] + [row(b) for b in branches],
        out_specs=pl.BlockSpec((tm, tn), lambda j, i: (i, j)),
        scratch_shapes=[pltpu.VMEM((4, D_MODEL, tn), BF16), pltpu.VMEM((d_mix, tn), BF16), pltpu.VMEM((w_branch_c.shape[0], tn), BF16)],
        compiler_params=_cp("parallel", "arbitrary"),
        name="gated_merge",
    )(h, w_gate, w_branch, w_branch_c, *branches)


def _ffn_up_kernel(h_ref, wg_ref, wu_ref, o_ref, wgb_ref, wub_ref):
    _cast_weights_once((wg_ref, wu_ref), (wgb_ref, wub_ref))
    h = h_ref[...]
    g = jnp.dot(h, wgb_ref[...], preferred_element_type=F32)
    u = jnp.dot(h, wub_ref[...], preferred_element_type=F32)
    o_ref[...] = (g * jax.nn.sigmoid(g) * u).astype(o_ref.dtype)


def _ffn_up(h, w_gate, w_up, layer, tm=1024, tn=512):
    m, k = h.shape
    n = w_gate.shape[-1]
    tm = min(tm, m)
    return pl.pallas_call(
        _ffn_up_kernel,
        out_shape=jax.ShapeDtypeStruct((m, n), BF16),
        grid=(n // tn, m // tm),
        in_specs=[
            pl.BlockSpec((tm, k), lambda j, i: (i, 0)),
            _weight_spec((k, tn), lambda j, i: (0, j), layer),
            _weight_spec((k, tn), lambda j, i: (0, j), layer),
        ],
        out_specs=pl.BlockSpec((tm, tn), lambda j, i: (i, j)),
        scratch_shapes=[pltpu.VMEM((k, tn), BF16), pltpu.VMEM((k, tn), BF16)],
        compiler_params=_cp("parallel", "arbitrary"),
        name="ffn_up",
    )(h, w_gate, w_up)


def _pad_w_in(w):
    pad = ((0, 0), (0, C_LORA_PAD - C_LORA))
    lo = COL_CLO
    cf = [_heads_minor(w[:, COL_CF + i * C_WIDTH:COL_CF + (i + 1) * C_WIDTH]) for i in range(3)]
    return jnp.concatenate([
        w[:, :COL_CF],
        *cf,
        jnp.pad(w[:, lo:lo + C_LORA], pad),
        jnp.pad(w[:, lo + C_LORA:lo + 2 * C_LORA], pad),
        w[:, lo + 2 * C_LORA:],
    ], axis=1)


def kernel(x, norm_mix, w_in, a_q_norm, a_k_norm, b_conv_w, b_conv_b, b_filt_w1, b_filt_b1, b_filt_w2, b_filt_b2, b_filt_w3, b_filt_b3, b_filt_w4, b_filt_freq, b_skip, c_mu, c_w0, c_w_up, c_a0, c_a_up, c_g_up, c_k_k, c_k_a, c_r_k, c_ln_w, c_ln_b, d_lq1, d_lk1, d_lq2, d_lk2, d_subln, w_gate, w_branch, w_out, norm_ffn, w_ff_gate, w_ff_up, w_ff_down, norm_final):
    batch, seq, _ = x.shape
    m = batch * seq
    cos, sin = _rope_tables(seq)
    cos_half, sin_half, cos_int, sin_int = _dft_tables(seq)
    filt_feats = _filter_features(seq)
    x = x.reshape(m, D_MODEL)
    for l in range(DEPTH):
        h = _rmsnorm(x, norm_mix[l], BF16)
        proj = _matmul(h, _pad_w_in(w_in[l]), F32, 1024, D_IN_PAD // 4, name="in_proj")

        gains = jnp.concatenate([
            jnp.broadcast_to(a_q_norm[l], (A_HEADS, A_HEAD_DIM)),
            jnp.broadcast_to(a_k_norm[l], (A_KV_HEADS, A_HEAD_DIM)),
        ])[:, None, :]
        o_a = _attn_a(_qk_prep(proj, gains, cos, sin, seq), proj, batch, seq)

        k_re, k_im = _hyena_filter_spectrum(seq, filt_feats, cos_int, sin_int, b_filt_w1[l], b_filt_b1[l], b_filt_w2[l],
                                            b_filt_b2[l], b_filt_w3[l], b_filt_b3[l], b_filt_w4[l], b_filt_freq[l])
        o_b = _hyena(proj, b_conv_w[l], b_conv_b[l], b_skip[l], cos_half, sin_half, k_re, k_im, batch, seq)

        o_c = _rwkv(proj, c_mu[l], c_w0[l], c_w_up[l], c_a0[l], c_a_up[l], c_g_up[l], c_k_k[l], c_k_a[l], c_r_k[l],
                    c_ln_w[l], c_ln_b[l], batch, seq)

        lam_init = 0.8 - 0.6 * math.exp(-0.3 * l)
        lam_vecs = jnp.stack([d_lq1[l], d_lk1[l], d_lq2[l], d_lk2[l]])
        o_d = _attn_d(proj, lam_vecs, d_subln[l], lam_init, batch, seq)

        c_lo, c_hi = A_WIDTH + B_WIDTH, A_WIDTH + B_WIDTH + C_WIDTH
        w_br_c = _heads_minor(w_branch[l, c_lo:c_hi].T).T
        merged = _merge(h, w_gate, w_branch, l, w_br_c, (o_a, o_b, o_c, o_d))
        x, h2 = _matmul_res_norm(merged, w_out, l, x, norm_ffn[l])
        mid = _ffn_up(h2, w_ff_gate, w_ff_up, l)
        x = _matmul(mid, w_ff_down, F32, 1024, 512, residual=x, layer=l, name="ffn_down")
    return _rmsnorm(x, norm_final, F32).reshape(batch, seq, D_MODEL)
```
